```python
import math
import jax, jax.numpy as jnp
from jax import lax
import numpy as np

D_MODEL = 1024
BATCH = 8
SEQ = 4096
DEPTH = 2
DEC_BATCH = 32
DEC_SEQ = 1
PAST_LEN = 16384
PAGE_SIZE = 128

D_MIX = 2 * D_MODEL
HEAD_DIM = 64
SSD_WIDTH = D_MIX // 2
SSD_HEADS = SSD_WIDTH // HEAD_DIM
SSD_GROUPS = 2
SSD_REP = SSD_HEADS // SSD_GROUPS
SSD_STATE = 128
SSD_CONV = 4
SSD_CHUNK = 256
CONV_DIM = SSD_WIDTH + 2 * SSD_GROUPS * SSD_STATE
NSA_WIDTH = D_MIX - SSD_WIDTH
NSA_HEADS = NSA_WIDTH // HEAD_DIM
NSA_KV_HEADS = 2
NSA_REP = NSA_HEADS // NSA_KV_HEADS
CMP_BLOCK = 32
CMP_STRIDE = 16
CMP_SPAN = CMP_BLOCK // CMP_STRIDE
CMP_HID = 2 * HEAD_DIM
SLC_BLOCK = 64
SLC_TOP = 16
WINDOW = 512
Q_BLOCK = 64
REL_BUCKETS = 32
REL_MAX_DIST = 128
NORM_EPS = 1e-6
KV_COLS = 2 * NSA_KV_HEADS * HEAD_DIM
IN_COLS = SSD_WIDTH + CONV_DIM + SSD_HEADS + NSA_WIDTH + 3 * KV_COLS + 3 * NSA_HEADS + NSA_WIDTH
NEG = -1e30

kernel_name = 'hybrid_ssd_nsa_adaln_step'


def rms_norm(x, g):
    xf = x.astype(jnp.float32)
    y = xf * lax.rsqrt(jnp.mean(xf * xf, -1, keepdims=True) + NORM_EPS)
    return (y * g.astype(jnp.float32)).astype(x.dtype)


def gated_rms_norm(y, z, w, groups):
    u = (y * jax.nn.silu(z)).astype(jnp.float32)
    sh = u.shape
    u = u.reshape(sh[:-1] + (groups, sh[-1] // groups))
    u = u * lax.rsqrt(jnp.mean(u * u, -1, keepdims=True) + NORM_EPS)
    return (u.reshape(sh) * w.astype(jnp.float32)).astype(y.dtype)


def modulate_in(x, c, norm_g, ada_w, ada_b):
    mod = jax.nn.silu(c) @ ada_w + ada_b
    shift, scale, gate = jnp.split(mod, 3, axis=-1)
    h = rms_norm(x, norm_g) * (1 + scale[:, None]) + shift[:, None]
    return h, gate[:, None]


def split_in(u):
    sizes = (SSD_WIDTH, CONV_DIM, SSD_HEADS, NSA_WIDTH, KV_COLS, KV_COLS, KV_COLS, 3 * NSA_HEADS, NSA_WIDTH)
    offs = [int(o) for o in np.cumsum(sizes)[:-1]]
    z_s, xbc, dt, q, kvc, kvs, kvw, gl, z_a = jnp.split(u, offs, axis=-1)
    b, L = u.shape[:2]
    kv = lambda t: t.reshape(b, L, 2, NSA_KV_HEADS, HEAD_DIM)
    return (z_s, xbc, dt, q.reshape(b, L, NSA_HEADS, HEAD_DIM), kv(kvc), kv(kvs), kv(kvw),
            gl.reshape(b, L, 3, NSA_HEADS), z_a)


def rel_bucket(dist):
    n = jnp.maximum(dist, 0)
    max_exact = REL_BUCKETS // 2
    nf = jnp.maximum(n, 1).astype(jnp.float32)
    large = max_exact + (jnp.log(nf / max_exact) / math.log(REL_MAX_DIST / max_exact)
                         * (REL_BUCKETS - max_exact)).astype(jnp.int32)
    large = jnp.minimum(large, REL_BUCKETS - 1)
    return jnp.where(n < max_exact, n, large)


def causal_conv(u_ext, w, bias):
    L = u_ext.shape[1] - (SSD_CONV - 1)
    out = bias + u_ext[:, 0:L] * w[0]
    for k in range(1, SSD_CONV):
        out = out + u_ext[:, k:k + L] * w[k]
    return jax.nn.silu(out)


def segsum(a):
    cs = jnp.cumsum(a, axis=-1)
    T = a.shape[-1]
    diff = cs[..., :, None] - cs[..., None, :]
    return jnp.where(jnp.tril(jnp.ones((T, T), bool)), diff, -jnp.inf)


def ssd_scan(x, dt, A, Bm, Cm, h0):
    f32 = jnp.float32
    b, L, H, P = x.shape
    G, R, N = SSD_GROUPS, SSD_REP, SSD_STATE
    Q = min(SSD_CHUNK, L)
    nc = -(-L // Q)
    pad = nc * Q - L
    xdt = jnp.pad(x.astype(f32) * dt[..., None], ((0, 0), (0, pad), (0, 0), (0, 0))).reshape(b, nc, Q, G, R, P)
    a = jnp.pad(dt * A, ((0, 0), (0, pad), (0, 0))).reshape(b, nc, Q, G, R).transpose(0, 3, 4, 1, 2)
    Bc = jnp.pad(Bm.astype(f32), ((0, 0), (0, pad), (0, 0), (0, 0))).reshape(b, nc, Q, G, N)
    Cc = jnp.pad(Cm.astype(f32), ((0, 0), (0, pad), (0, 0), (0, 0))).reshape(b, nc, Q, G, N)
    a_cs = jnp.cumsum(a, axis=-1)
    Lmat = jnp.exp(segsum(a))
    CB = jnp.einsum('bclgn,bcsgn->bgcls', Cc, Bc)
    y_diag = jnp.einsum('bgrcls,bcsgrp->bclgrp', CB[:, :, None] * Lmat, xdt)
    decay_states = jnp.exp(a_cs[..., -1:] - a_cs)
    states = jnp.einsum('bclgn,bgrcl,bclgrp->bcgrpn', Bc, decay_states, xdt)
    states = jnp.concatenate([h0.astype(f32).reshape(b, 1, G, R, P, N), states], axis=1)
    tot = jnp.pad(a_cs[..., -1], ((0, 0), (0, 0), (0, 0), (1, 0)))
    decay_chunk = jnp.exp(segsum(tot))
    new_states = jnp.einsum('bgrzc,bcgrpn->bzgrpn', decay_chunk, states)
    y_off = jnp.einsum('bclgn,bcgrpn,bgrcl->bclgrp', Cc, new_states[:, :-1], jnp.exp(a_cs))
    y = (y_diag + y_off).reshape(b, nc * Q, H, P)[:, :L]
    return y, new_states[:, -1].reshape(b, H, P, N)


def ssd_mixer(xbc_ext, dt_raw, conv_w, conv_b, dt_bias, a_log, d_skip, h0):
    f32 = jnp.float32
    u = causal_conv(xbc_ext, conv_w, conv_b)
    b, L = u.shape[:2]
    gn = SSD_GROUPS * SSD_STATE
    xs = u[..., :SSD_WIDTH].reshape(b, L, SSD_HEADS, HEAD_DIM)
    Bm = u[..., SSD_WIDTH:SSD_WIDTH + gn].reshape(b, L, SSD_GROUPS, SSD_STATE)
    Cm = u[..., SSD_WIDTH + gn:].reshape(b, L, SSD_GROUPS, SSD_STATE)
    dt = jax.nn.softplus(dt_raw.astype(f32) + dt_bias.astype(f32))
    A = -jnp.exp(a_log.astype(f32))
    y, h = ssd_scan(xs, dt, A, Bm, Cm, h0)
    y = y + xs.astype(f32) * d_skip.astype(f32)[:, None]
    return y.reshape(b, L, SSD_WIDTH).astype(xbc_ext.dtype), h


def compress_kv(kv, pe, w1, w2):
    b, L = kv.shape[:2]
    n_cmp = (L - CMP_BLOCK) // CMP_STRIDE + 1
    n_seg = n_cmp + CMP_SPAN - 1
    seg = kv[:, :n_seg * CMP_STRIDE].reshape(b, n_seg, CMP_STRIDE, 2, NSA_KV_HEADS, HEAD_DIM)
    w1s = w1.reshape(2, CMP_SPAN, CMP_STRIDE, HEAD_DIM, CMP_HID)
    part = jnp.einsum('bnokgd,kjodf->bjnkgf', seg, w1s)
    pre = part[:, 0, 0:n_cmp]
    for j in range(1, CMP_SPAN):
        pre = pre + part[:, j, j:j + n_cmp]
    pre = pre + jnp.einsum('kld,kldf->kf', pe, w1)[:, None, :]
    out = jnp.einsum('bnkgf,kfd->bnkgd', jax.nn.silu(pre), w2)
    return out, jnp.arange(n_cmp) * CMP_STRIDE + CMP_BLOCK - 1


def to_blocks(kv):
    b, L = kv.shape[:2]
    n = -(-L // SLC_BLOCK)
    kv = jnp.pad(kv, ((0, 0), (0, n * SLC_BLOCK - L), (0, 0), (0, 0), (0, 0)))
    return kv.reshape((b, n, SLC_BLOCK) + kv.shape[2:])


def block_overlap(n_c, n_s):
    cs = jnp.arange(n_c)[:, None] * CMP_STRIDE
    ss = jnp.arange(n_s)[None, :] * SLC_BLOCK
    return ((cs < ss + SLC_BLOCK) & (cs + CMP_BLOCK > ss)).astype(jnp.float32)


def nsa_attend(q, qpos, gates, kv_c, cpos, kv_s, kv_w, wpos, rel_bias):
    f32 = jnp.float32
    b, Lq = q.shape[:2]
    G, R, NB = NSA_KV_HEADS, NSA_REP, REL_BUCKETS
    n_c, n_s = kv_c.shape[1], kv_s.shape[1]
    tab = rel_bias.astype(f32)
    qg = q.astype(f32).reshape(b, Lq, G, R, HEAD_DIM) * (HEAD_DIM ** -0.5)

    def dense_bias(kpos):
        bk = rel_bucket(qpos[:, None] - kpos[None, :])
        return tab[bk].reshape(Lq, kpos.shape[0], G, R).transpose(2, 3, 0, 1)

    kc = kv_c[:, :, 0].astype(f32)
    vc = kv_c[:, :, 1].astype(f32)
    mc = cpos[None, :] <= qpos[:, None]
    lc = jnp.einsum('bqgrd,bkgd->bgrqk', qg, kc) + dense_bias(cpos)
    pc = jax.nn.softmax(jnp.where(mc, lc, NEG), axis=-1) * jnp.any(mc, -1)[:, None]
    o_cmp = jnp.einsum('bgrqk,bkgd->bqgrd', pc, vc)
    imp = jnp.einsum('bgrqk,kj->bqgj', pc, block_overlap(n_c, n_s))
    blk = jnp.arange(n_s)[None, :]
    cur = (qpos // SLC_BLOCK)[:, None]
    forced = (blk == 0) | (blk == cur) | (blk == cur - 1)
    elig = blk * SLC_BLOCK <= qpos[:, None]
    imp = jnp.where(forced[None, :, None], 1e6, jnp.where(elig[None, :, None], imp, -1e6))
    _, sel = lax.top_k(imp, min(SLC_TOP, n_s))
    bi = jnp.arange(b)[:, None, None, None]
    gi = jnp.arange(G)[None, None, :, None]
    g_kv = kv_s[bi, sel, :, :, gi].astype(f32)
    kpos_s = sel[..., None] * SLC_BLOCK + jnp.arange(SLC_BLOCK)
    qp5 = qpos[None, :, None, None, None]
    ms = kpos_s <= qp5
    tab_g = tab.reshape(NB, G, R).transpose(1, 0, 2)
    bias_s = tab_g[gi[..., None], rel_bucket(qp5 - kpos_s)]
    ls = jnp.einsum('bqgrd,bqgnld->bqgrnl', qg, g_kv[..., 0, :]) + bias_s.transpose(0, 1, 2, 5, 3, 4)
    ls = jnp.where(ms[:, :, :, None], ls, NEG)
    ps = jax.nn.softmax(ls.reshape(b, Lq, G, R, -1), axis=-1).reshape(ls.shape)
    o_slc = jnp.einsum('bqgrnl,bqgnld->bqgrd', ps, g_kv[..., 1, :])
    kw = kv_w[:, :, 0].astype(f32)
    vw = kv_w[:, :, 1].astype(f32)
    dw = qpos[:, None] - wpos[None, :]
    mw = (dw >= 0) & (dw < WINDOW) & (wpos[None, :] >= 0)
    lw = jnp.einsum('bqgrd,bkgd->bgrqk', qg, kw) + dense_bias(wpos)
    pw = jax.nn.softmax(jnp.where(mw, lw, NEG), axis=-1)
    o_win = jnp.einsum('bgrqk,bkgd->bqgrd', pw, vw)
    g = jax.nn.sigmoid(gates.astype(f32)).reshape(b, Lq, 3, G, R, 1)
    o = g[:, :, 0] * o_cmp + g[:, :, 1] * o_slc + g[:, :, 2] * o_win
    return o.reshape(b, Lq, NSA_WIDTH).astype(q.dtype)


def nsa_prompt(q, gates, kvc, kvs, kvw, cmp_pe, cmp_w1, cmp_w2, rel_bias):
    b, L = q.shape[:2]
    kv_c, cpos = compress_kv(kvc, cmp_pe, cmp_w1, cmp_w2)
    kv_s = to_blocks(kvs)
    kvw_pad = jnp.pad(kvw, ((0, 0), (WINDOW, 0), (0, 0), (0, 0), (0, 0)))

    def one_block(i):
        s0 = i * Q_BLOCK
        qpos = s0 + jnp.arange(Q_BLOCK)
        qb = lax.dynamic_slice_in_dim(q, s0, Q_BLOCK, 1)
        gb = lax.dynamic_slice_in_dim(gates, s0, Q_BLOCK, 1)
        kw = lax.dynamic_slice_in_dim(kvw_pad, s0, WINDOW + Q_BLOCK, 1)
        wpos = s0 - WINDOW + jnp.arange(WINDOW + Q_BLOCK)
        return nsa_attend(qb, qpos, gb, kv_c, cpos, kv_s, kw, wpos, rel_bias)

    out = lax.map(one_block, jnp.arange(L // Q_BLOCK))
    return out.transpose(1, 0, 2, 3).reshape(b, L, NSA_WIDTH)


def nsa_sample(q, gates, kvc, kvs, kvw, past_c, past_s, win_buf, cmp_pe, cmp_w1, cmp_w2, rel_bias):
    b, Lq = q.shape[:2]
    row = (2, NSA_KV_HEADS, HEAD_DIM)
    full_c = jnp.concatenate([past_c.reshape((b, -1) + row).astype(kvc.dtype), kvc], axis=1)
    full_s = jnp.concatenate([past_s.reshape((b, -1) + row).astype(kvs.dtype), kvs], axis=1)
    kv_c, cpos = compress_kv(full_c, cmp_pe, cmp_w1, cmp_w2)
    kv_s = to_blocks(full_s)
    w_buf = win_buf.shape[1]
    kw = jnp.concatenate([win_buf.astype(kvw.dtype), kvw], axis=1)
    wpos = PAST_LEN - w_buf + jnp.arange(w_buf + Lq)
    qpos = PAST_LEN + jnp.arange(Lq)
    o = nsa_attend(q, qpos, gates, kv_c, cpos, kv_s, kw, wpos, rel_bias)
    return o, kw[:, -w_buf:]


def layer_out(x, gate, y_ssd, z_ssd, y_nsa, z_att, ssd_norm_w, nsa_norm_w, w_out):
    m = jnp.concatenate([gated_rms_norm(y_ssd, z_ssd, ssd_norm_w, SSD_GROUPS),
                         gated_rms_norm(y_nsa, z_att, nsa_norm_w, NSA_KV_HEADS)], axis=-1)
    return x + gate * (m @ w_out)


def setup_inputs(seed: int = 0) -> dict:
    key = jax.random.key(seed)
    ks = list(jax.random.split(key, 40))
    f32 = jnp.float32

    def nrm(shape, s):
        return s * jax.random.normal(ks.pop(), shape, f32)

    n_pages = PAST_LEN // PAGE_SIZE
    n_pool = (5 * DEC_BATCH * n_pages + 3) // 4
    w_buf = min(WINDOW, PAST_LEN)
    kv_row = (2, NSA_KV_HEADS, HEAD_DIM)
    page_table = jax.random.permutation(ks.pop(), n_pool)[:DEC_BATCH * n_pages].reshape(DEC_BATCH, n_pages).astype(jnp.int32)
    dt0 = jnp.exp(jax.random.uniform(ks.pop(), (DEPTH, SSD_HEADS), f32, math.log(1e-3), math.log(1e-1)))
    a_init = jax.random.uniform(ks.pop(), (DEPTH, SSD_HEADS), f32, 1.0, 16.0)
    return {
        'x_prompt': nrm((BATCH, SEQ, D_MODEL), 1.0),
        'x_sample': nrm((DEC_BATCH, DEC_SEQ, D_MODEL), 1.0),
        'cache_cmp_kv': nrm((DEPTH, n_pool, PAGE_SIZE) + kv_row, 1.0),
        'cache_slc_kv': nrm((DEPTH, n_pool, PAGE_SIZE) + kv_row, 1.0),
        'state_win_kv': nrm((DEPTH, DEC_BATCH, w_buf) + kv_row, 1.0),
        'state_conv': nrm((DEPTH, DEC_BATCH, SSD_CONV - 1, CONV_DIM), 1.0),
        'state_ssm': nrm((DEPTH, DEC_BATCH, SSD_HEADS, HEAD_DIM, SSD_STATE), 0.5),
        'page_table': page_table,
        'c_prompt': nrm((BATCH, D_MODEL), 1.0),
        'c_sample': nrm((DEC_BATCH, D_MODEL), 1.0),
        'norm_g': 1.0 + nrm((DEPTH, D_MODEL), 0.02),
        'ada_w': nrm((DEPTH, D_MODEL, 3 * D_MODEL), 0.5 * D_MODEL ** -0.5),
        'ada_b': nrm((DEPTH, 3 * D_MODEL), 0.1),
        'w_in': nrm((DEPTH, D_MODEL, IN_COLS), D_MODEL ** -0.5),
        'conv_w': nrm((DEPTH, SSD_CONV, CONV_DIM), SSD_CONV ** -0.5),
        'conv_b': nrm((DEPTH, CONV_DIM), 0.02),
        'dt_bias': dt0 + jnp.log(-jnp.expm1(-dt0)),
        'a_log': jnp.log(a_init),
        'd_skip': 1.0 + nrm((DEPTH, SSD_HEADS), 0.1),
        'ssd_norm_w': 1.0 + nrm((DEPTH, SSD_WIDTH), 0.02),
        'cmp_pe': nrm((DEPTH, 2, CMP_BLOCK, HEAD_DIM), 0.1),
        'cmp_w1': nrm((DEPTH, 2, CMP_BLOCK, HEAD_DIM, CMP_HID), (CMP_BLOCK * HEAD_DIM) ** -0.5),
        'cmp_w2': nrm((DEPTH, 2, CMP_HID, HEAD_DIM), CMP_HID ** -0.5),
        'nsa_norm_w': 1.0 + nrm((DEPTH, NSA_WIDTH), 0.02),
        'w_out': nrm((DEPTH, D_MIX, D_MODEL), D_MIX ** -0.5),
        'rel_bias': nrm((REL_BUCKETS, NSA_HEADS), 0.2),
        'final_norm_g': 1.0 + nrm((D_MODEL,), 0.02),
    }


def reference(x_prompt, x_sample, cache_cmp_kv, cache_slc_kv, state_win_kv, state_conv, state_ssm,
              page_table, c_prompt, c_sample, norm_g, ada_w, ada_b, w_in, conv_w, conv_b, dt_bias,
              a_log, d_skip, ssd_norm_w, cmp_pe, cmp_w1, cmp_w2, nsa_norm_w, w_out, rel_bias,
              final_norm_g):
    xp, xs = x_prompt, x_sample
    n_prompt, L_p = xp.shape[:2]
    pc_l, ps_l, pw_l, pconv_l, pssm_l = [], [], [], [], []
    sc_l, ss_l, sw_l, sconv_l, sssm_l = [], [], [], [], []
    for l in range(DEPTH):
        hp, gp = modulate_in(xp, c_prompt, norm_g[l], ada_w[l], ada_b[l])
        z_s, xbc, dt_r, q, kvc, kvs, kvw, gl, z_a = split_in(hp @ w_in[l])
        xbc_ext = jnp.pad(xbc, ((0, 0), (SSD_CONV - 1, 0), (0, 0)))
        h0 = jnp.zeros((n_prompt, SSD_HEADS, HEAD_DIM, SSD_STATE), jnp.float32)
        y_ssd, h_fin = ssd_mixer(xbc_ext, dt_r, conv_w[l], conv_b[l], dt_bias[l], a_log[l], d_skip[l], h0)
        y_nsa = nsa_prompt(q, gl, kvc, kvs, kvw, cmp_pe[l], cmp_w1[l], cmp_w2[l], rel_bias)
        xp = layer_out(xp, gp, y_ssd, z_s, y_nsa, z_a, ssd_norm_w[l], nsa_norm_w[l], w_out[l])
        pc_l.append(kvc)
        ps_l.append(kvs)
        pw_l.append(kvw[:, -min(WINDOW, L_p):])
        pconv_l.append(xbc[:, -(SSD_CONV - 1):])
        pssm_l.append(h_fin)
        hs, gs = modulate_in(xs, c_sample, norm_g[l], ada_w[l], ada_b[l])
        z_s2, xbc2, dt_r2, q2, kvc2, kvs2, kvw2, gl2, z_a2 = split_in(hs @ w_in[l])
        xbc_ext2 = jnp.concatenate([state_conv[l].astype(xbc2.dtype), xbc2], axis=1)
        y_ssd2, h2 = ssd_mixer(xbc_ext2, dt_r2, conv_w[l], conv_b[l], dt_bias[l], a_log[l], d_skip[l], state_ssm[l])
        y_nsa2, win2 = nsa_sample(q2, gl2, kvc2, kvs2, kvw2, cache_cmp_kv[l, page_table],
                                  cache_slc_kv[l, page_table], state_win_kv[l],
                                  cmp_pe[l], cmp_w1[l], cmp_w2[l], rel_bias)
        xs = layer_out(xs, gs, y_ssd2, z_s2, y_nsa2, z_a2, ssd_norm_w[l], nsa_norm_w[l], w_out[l])
        sc_l.append(kvc2)
        ss_l.append(kvs2)
        sw_l.append(win2)
        sconv_l.append(xbc_ext2[:, -(SSD_CONV - 1):])
        sssm_l.append(h2)
    y_prompt = rms_norm(xp, final_norm_g)
    y_sample = rms_norm(xs, final_norm_g)
    return (y_prompt, y_sample,
            jnp.stack(pc_l), jnp.stack(ps_l), jnp.stack(pw_l), jnp.stack(pconv_l), jnp.stack(pssm_l),
            jnp.stack(sc_l), jnp.stack(ss_l), jnp.stack(sw_l), jnp.stack(sconv_l), jnp.stack(sssm_l))
```

```python
import functools
import math

import numpy as np
import jax
import jax.numpy as jnp
from jax import lax
from jax.experimental import pallas as pl
from jax.experimental.pallas import tpu as pltpu

F32 = jnp.float32
BF16 = jnp.bfloat16
HIGHEST = lax.Precision.HIGHEST

D_MODEL = 1024
HEAD_DIM = 64
SSD_WIDTH = 1024
SSD_HEADS = 16
SSD_GROUPS = 2
SSD_STATE = 128
SSD_CONV = 4
SSD_CHUNK = 256
CONV_DIM = SSD_WIDTH + 2 * SSD_GROUPS * SSD_STATE
NSA_WIDTH = 1024
NSA_HEADS = 16
NSA_KV_HEADS = 2
NSA_REP = NSA_HEADS // NSA_KV_HEADS
CMP_BLOCK = 32
CMP_STRIDE = 16
CMP_HID = 2 * HEAD_DIM
SLC_BLOCK = 64
SLC_TOP = 16
WINDOW = 512
Q_BLOCK = 64
REL_BUCKETS = 32
REL_MAX_DIST = 128
NORM_EPS = 1e-6
KV_COLS = 2 * NSA_KV_HEADS * HEAD_DIM
PAGE_SIZE = 128
NEG = -1e30

LANE = 128
HALF = LANE // 2
GROUP_W = NSA_REP * HEAD_DIM
ROWS = NSA_REP * Q_BLOCK
SLC_NEAR = 4 * SLC_BLOCK
SLC_PAD = SLC_NEAR - SLC_BLOCK
WIN_W = WINDOW + 2 * Q_BLOCK
WIN_PAD = WIN_W - Q_BLOCK
CMP_PAGES = 16
VMEM_LIMIT = 48 * 1024 * 1024

_SEG_NAMES = ("z_s", "xbc", "dt", "q", "kvc", "kvs", "kvw", "gl", "z_a")
_SEG_SIZES = (SSD_WIDTH, CONV_DIM, SSD_HEADS, NSA_WIDTH, KV_COLS, KV_COLS, KV_COLS, 3 * NSA_HEADS, NSA_WIDTH)
_SEG_PAD = tuple(-(-s // LANE) * LANE for s in _SEG_SIZES)
_SEG_OFF = tuple(int(o) for o in np.cumsum((0,) + _SEG_PAD[:-1]))
IN_PAD = int(sum(_SEG_PAD))


def _sigmoid(x):
    return 1.0 / (1.0 + jnp.exp(-x))


def _silu(x):
    return x * _sigmoid(x)


def _dot32(a, b):
    return jnp.dot(a, b, precision=HIGHEST, preferred_element_type=F32)


def _dot_nt(a, b):
    return lax.dot_general(a, b, (((1,), (1,)), ((), ())), preferred_element_type=F32)


def _bucket_table():
    n = np.arange(REL_MAX_DIST + 1)
    max_exact = REL_BUCKETS // 2
    nf = np.maximum(n, 1).astype(np.float32)
    large = max_exact + (np.log(nf / np.float32(max_exact)) / np.float32(math.log(REL_MAX_DIST / max_exact))
                         * np.float32(REL_BUCKETS - max_exact)).astype(np.int32)
    large = np.minimum(large, REL_BUCKETS - 1)
    return np.where(n < max_exact, n, large).astype(np.int32)


_BUCKETS = _bucket_table()


def _bias_of_dist(rel_bias, dist):
    idx = _BUCKETS[np.clip(dist, 0, REL_MAX_DIST)]
    out = jnp.take(rel_bias.astype(F32), jnp.asarray(idx.reshape(-1)), axis=0)
    return out.T.reshape((NSA_HEADS,) + dist.shape)


def _mod_kernel(c_ref, w_ref, b_ref, o_ref):
    o_ref[...] = _dot32(_silu(c_ref[...]), w_ref[...]) + b_ref[...]


def _modulation(c, w, b):
    m, d = c.shape
    n = w.shape[1]
    tn = 512
    return pl.pallas_call(
        _mod_kernel,
        out_shape=jax.ShapeDtypeStruct((m, n), F32),
        grid=(n // tn,),
        in_specs=[pl.BlockSpec((m, d), lambda j: (0, 0)),
                  pl.BlockSpec((d, tn), lambda j: (0, j)),
                  pl.BlockSpec((1, tn), lambda j: (0, j))],
        out_specs=pl.BlockSpec((m, tn), lambda j: (0, j)),
        name="adaln_mod",
    )(c, w, b.reshape(1, n))


def _inproj_kernel(x_ref, g_ref, sc_ref, sh_ref, w_ref, *out_refs, per_row):
    x = x_ref[...]
    xn = x * lax.rsqrt(jnp.mean(x * x, axis=-1, keepdims=True) + NORM_EPS)
    sc = sc_ref[...] if per_row else sc_ref[0]
    sh = sh_ref[...] if per_row else sh_ref[0]
    h = ((xn * g_ref[...]) * (1.0 + sc) + sh).astype(BF16)
    for name, off, width, ref in zip(_SEG_NAMES, _SEG_OFF, _SEG_PAD, out_refs):
        r = jnp.dot(h, w_ref[:, off:off + width], preferred_element_type=F32)
        if name == "q":
            r = r * (HEAD_DIM ** -0.5)
        ref[...] = r


def _in_projection(x2d, g, scale, shift, w_pad, rows_per_batch):
    m = x2d.shape[0]
    per_row = rows_per_batch == 1
    tm = m if per_row else 256
    if per_row:
        mod_spec = pl.BlockSpec((tm, D_MODEL), lambda i: (0, 0))
        sc, sh = scale, shift
    else:
        mod_spec = pl.BlockSpec((1, 1, D_MODEL), lambda i: ((i * tm) // rows_per_batch, 0, 0))
        sc, sh = scale[:, None, :], shift[:, None, :]
    outs = tuple(jax.ShapeDtypeStruct((m, w), F32) for w in _SEG_PAD)
    return pl.pallas_call(
        functools.partial(_inproj_kernel, per_row=per_row),
        out_shape=outs,
        grid=(m // tm,),
        in_specs=[pl.BlockSpec((tm, D_MODEL), lambda i: (i, 0)),
                  pl.BlockSpec((1, D_MODEL), lambda i: (0, 0)),
                  mod_spec, mod_spec,
                  pl.BlockSpec((D_MODEL, IN_PAD), lambda i: (0, 0))],
        out_specs=tuple(pl.BlockSpec((tm, w), lambda i: (i, 0)) for w in _SEG_PAD),
        compiler_params=pltpu.CompilerParams(vmem_limit_bytes=VMEM_LIMIT),
        name="in_projection",
    )(x2d, g.reshape(1, D_MODEL), sc, sh, w_pad)


def _softplus(x):
    return jnp.maximum(x, 0.0) + jnp.log(1.0 + jnp.exp(-jnp.abs(x)))


def _ssd_kernel(xbc_ref, dt_ref, cw_ref, cb_ref, dtb_ref, alog_ref, dsk_ref, e_ref, tril_ref,
                y_ref, hfin_ref, xe_sc, st_sc):
    c = pl.program_id(1)
    q = SSD_CHUNK
    n_pairs = SSD_HEADS // 2

    @pl.when(c == 0)
    def _():
        xe_sc[0:8, :] = jnp.zeros((8, CONV_DIM), F32)
        st_sc[...] = jnp.zeros(st_sc.shape, F32)

    xe_sc[8:8 + q, :] = xbc_ref[0]
    acc = cb_ref[...] + cw_ref[0:1, :] * xe_sc[5:5 + q, :]
    for k in range(1, SSD_CONV):
        acc = acc + cw_ref[k:k + 1, :] * xe_sc[5 + k:5 + k + q, :]
    u = _silu(acc)
    xe_sc[0:8, :] = xe_sc[q:q + 8, :]

    xs = u[:, :SSD_WIDTH]
    gn = SSD_GROUPS * SSD_STATE
    bm = u[:, SSD_WIDTH:SSD_WIDTH + gn]
    cm = u[:, SSD_WIDTH + gn:]

    dt = _softplus(dt_ref[0] + dtb_ref[...])
    a = dt * (-jnp.exp(alog_ref[...]))
    cs = _dot32(tril_ref[...], a)
    cs_t = cs.T
    cs_last = cs[q - 1:q, :]
    e = e_ref[...]
    dt_e = _dot32(dt, e)
    w_e = _dot32(dt * jnp.exp(cs_last - cs), e)
    ecs_e = _dot32(jnp.exp(cs), e)
    tot_e = _dot32(jnp.broadcast_to(jnp.exp(cs_last), (8, LANE)), e)[0:1, :]
    xdt = (xs * dt_e).astype(BF16)
    xw = (xs * w_e).astype(BF16)

    li = lax.broadcasted_iota(jnp.int32, (q, q), 0)
    si = lax.broadcasted_iota(jnp.int32, (q, q), 1)
    tri = li >= si
    lane = lax.broadcasted_iota(jnp.int32, (q, LANE), 1)

    for g in range(SSD_GROUPS):
        cg = cm[:, g * SSD_STATE:(g + 1) * SSD_STATE].astype(BF16)
        bg = bm[:, g * SSD_STATE:(g + 1) * SSD_STATE]
        cb = _dot_nt(cg, bg.astype(BF16))
        bg_t = bg.T.astype(BF16)
        for jp in range(n_pairs // SSD_GROUPS):
            j = g * (n_pairs // SSD_GROUPS) + jp
            sl = slice(j * LANE, (j + 1) * LANE)
            xdt_p = xdt[:, sl]
            ys = []
            for hh in (2 * j, 2 * j + 1):
                diff = cs[:, hh:hh + 1] - cs_t[hh:hh + 1, :]
                lmat = jnp.exp(jnp.where(tri, diff, NEG))
                ys.append(jnp.dot((cb * lmat).astype(BF16), xdt_p, preferred_element_type=F32))
            y_diag = jnp.where(lane < HALF, ys[0], ys[1])
            st = st_sc[j]
            y_off = jnp.dot(cg, st.astype(BF16), preferred_element_type=F32) * ecs_e[:, sl]
            y_ref[0, :, sl] = y_diag + y_off + xs[:, sl] * dsk_ref[:, sl]
            new = jnp.dot(bg_t, xw[:, sl], preferred_element_type=F32)
            st_sc[j] = st * tot_e[:, sl] + new

    @pl.when(c == pl.num_programs(1) - 1)
    def _():
        for j in range(n_pairs):
            hfin_ref[0, j * LANE:(j + 1) * LANE, :] = st_sc[j].T


def _head_expand():
    e = np.zeros((LANE, SSD_WIDTH), np.float32)
    for h in range(SSD_HEADS):
        e[h, h * HEAD_DIM:(h + 1) * HEAD_DIM] = 1.0
    return jnp.asarray(e)


def _pad_lanes(v):
    return jnp.pad(v.astype(F32), (0, LANE - v.shape[0])).reshape(1, LANE)


def _ssd_prompt(xbc, dt, conv_w, conv_b, dt_bias, a_log, d_skip):
    b, l, _ = xbc.shape
    nc = l // SSD_CHUNK
    full = lambda shape: pl.BlockSpec(shape, lambda i, c: (0,) * len(shape))
    y, hfin = pl.pallas_call(
        _ssd_kernel,
        out_shape=(jax.ShapeDtypeStruct((b, l, SSD_WIDTH), F32),
                   jax.ShapeDtypeStruct((b, SSD_HEADS * HEAD_DIM, SSD_STATE), F32)),
        grid=(b, nc),
        in_specs=[pl.BlockSpec((1, SSD_CHUNK, CONV_DIM), lambda i, c: (i, c, 0)),
                  pl.BlockSpec((1, SSD_CHUNK, LANE), lambda i, c: (i, c, 0)),
                  full((SSD_CONV, CONV_DIM)), full((1, CONV_DIM)), full((1, LANE)), full((1, LANE)),
                  full((1, SSD_WIDTH)), full((LANE, SSD_WIDTH)), full((SSD_CHUNK, SSD_CHUNK))],
        out_specs=(pl.BlockSpec((1, SSD_CHUNK, SSD_WIDTH), lambda i, c: (i, c, 0)),
                   pl.BlockSpec((1, SSD_HEADS * HEAD_DIM, SSD_STATE), lambda i, c: (i, 0, 0))),
        scratch_shapes=[pltpu.VMEM((SSD_CHUNK + 8, CONV_DIM), F32),
                        pltpu.VMEM((SSD_HEADS // 2, SSD_STATE, LANE), F32)],
        compiler_params=pltpu.CompilerParams(dimension_semantics=("arbitrary", "arbitrary"),
                                             vmem_limit_bytes=VMEM_LIMIT),
        name="ssd_prompt",
    )(xbc, dt, conv_w, conv_b.reshape(1, CONV_DIM), _pad_lanes(dt_bias), _pad_lanes(a_log),
      jnp.repeat(d_skip.astype(F32), HEAD_DIM).reshape(1, SSD_WIDTH), _head_expand(),
      jnp.asarray(np.tril(np.ones((SSD_CHUNK, SSD_CHUNK), np.float32))))
    return y, hfin.reshape(b, SSD_HEADS, HEAD_DIM, SSD_STATE)


def _ssd_step_kernel(xbc_ref, c0_ref, c1_ref, c2_ref, dt_ref, h0_ref, cw_ref, cb_ref, dtb_ref, alog_ref,
                     dsk_ref, e_ref, y_ref, hout_ref, xt_sc, dect_sc, bc_sc, yt_sc, xs_sc):
    b = pl.program_id(0)
    nb = xbc_ref.shape[0]
    rows = SSD_HEADS * HEAD_DIM
    gn = SSD_GROUPS * SSD_STATE

    @pl.when(b == 0)
    def _():
        acc = (cb_ref[...] + cw_ref[0:1, :] * c0_ref[...] + cw_ref[1:2, :] * c1_ref[...]
               + cw_ref[2:3, :] * c2_ref[...] + cw_ref[3:4, :] * xbc_ref[...])
        u = _silu(acc)
        xs = u[:, :SSD_WIDTH]
        dt = _softplus(dt_ref[...] + dtb_ref[...])
        dec = jnp.exp(dt * (-jnp.exp(alog_ref[...])))
        e = e_ref[...]
        xdt = xs * _dot32(dt, e)
        dec_e = _dot32(dec, e)
        pad = jnp.zeros((LANE - nb, SSD_WIDTH), F32)
        xt_sc[...] = jnp.concatenate([xdt, pad], axis=0).T
        dect_sc[...] = jnp.concatenate([dec_e, pad], axis=0).T
        bc_sc[...] = u[:, SSD_WIDTH:]
        xs_sc[...] = xs
        yt_sc[...] = jnp.zeros(yt_sc.shape, F32)

    ri = lax.broadcasted_iota(jnp.int32, (LANE, LANE), 0)
    onehot = jnp.where(ri == b, 1.0, 0.0)
    xcol = _dot32(xt_sc[...], onehot)
    dcol = _dot32(dect_sc[...], onehot)
    bc = bc_sc[pl.ds(b, 1), :]
    row = lax.broadcasted_iota(jnp.int32, (rows, SSD_STATE), 0)
    first = row < rows // SSD_GROUPS
    b_full = jnp.where(first, bc[:, 0:SSD_STATE], bc[:, SSD_STATE:gn])
    c_full = jnp.where(first, bc[:, gn:gn + SSD_STATE], bc[:, gn + SSD_STATE:])
    new = dcol * h0_ref[0] + xcol * b_full
    hout_ref[0] = new
    ycol = _dot32(new * c_full, jnp.ones((SSD_STATE, LANE), F32))
    lane = lax.broadcasted_iota(jnp.int32, (rows, LANE), 1)
    yt_sc[...] = jnp.where(lane == b, ycol, yt_sc[...])

    @pl.when(b == nb - 1)
    def _():
        y_ref[...] = yt_sc[...].T[0:nb, :] + xs_sc[...] * dsk_ref[...]


def _ssd_step(xbc, conv_state, dt, h0, conv_w, conv_b, dt_bias, a_log, d_skip):
    nb = xbc.shape[0]
    rows = SSD_HEADS * HEAD_DIM
    full = lambda shape: pl.BlockSpec(shape, lambda i: (0,) * len(shape))
    y, hout = pl.pallas_call(
        _ssd_step_kernel,
        out_shape=(jax.ShapeDtypeStruct((nb, SSD_WIDTH), F32),
                   jax.ShapeDtypeStruct((nb, rows, SSD_STATE), F32)),
        grid=(nb,),
        in_specs=[full((nb, CONV_DIM)), full((nb, CONV_DIM)), full((nb, CONV_DIM)), full((nb, CONV_DIM)),
                  full((nb, LANE)),
                  pl.BlockSpec((1, rows, SSD_STATE), lambda i: (i, 0, 0)),
                  full((SSD_CONV, CONV_DIM)), full((1, CONV_DIM)), full((1, LANE)), full((1, LANE)),
                  full((1, SSD_WIDTH)), full((LANE, SSD_WIDTH))],
        out_specs=(full((nb, SSD_WIDTH)),
                   pl.BlockSpec((1, rows, SSD_STATE), lambda i: (i, 0, 0))),
        scratch_shapes=[pltpu.VMEM((rows, LANE), F32), pltpu.VMEM((rows, LANE), F32),
                        pltpu.VMEM((nb, 2 * SSD_GROUPS * SSD_STATE), F32),
                        pltpu.VMEM((rows, LANE), F32), pltpu.VMEM((nb, SSD_WIDTH), F32)],
        compiler_params=pltpu.CompilerParams(dimension_semantics=("arbitrary",)),
        name="ssd_step",
    )(xbc, conv_state[:, 0], conv_state[:, 1], conv_state[:, 2], dt, h0.reshape(nb, rows, SSD_STATE),
      conv_w, conv_b.reshape(1, CONV_DIM), _pad_lanes(dt_bias), _pad_lanes(a_log),
      jnp.repeat(d_skip.astype(F32), HEAD_DIM).reshape(1, SSD_WIDTH), _head_expand())
    return y, hout.reshape(nb, SSD_HEADS, HEAD_DIM, SSD_STATE)


def _compress_kernel(pt_ref, *refs):
    pages = (refs[:CMP_PAGES], refs[CMP_PAGES:2 * CMP_PAGES])
    w1_ref, pe_ref, w2_ref, out_ref, sh_sc, pe_sc = refs[2 * CMP_PAGES:]
    s = pl.program_id(1)
    segs = PAGE_SIZE // CMP_STRIDE
    rows = CMP_PAGES * segs
    hid2 = NSA_KV_HEADS * CMP_HID

    @pl.when(s == 0)
    def _():
        sh_sc[:, 0:8, :] = jnp.zeros((2, 8, hid2), F32)
        for k in range(2):
            t = jnp.zeros((8, 2 * hid2), F32)
            for o in range(CMP_STRIDE):
                t = t + jnp.dot(pe_ref[o, k].astype(BF16), w1_ref[o, k], preferred_element_type=F32)
            pe_sc[k] = jnp.broadcast_to(t[0:1, 0:hid2] + t[1:2, hid2:], (8, hid2))

    for k in range(2):
        acc = jnp.zeros((rows, 2 * hid2), F32)
        for o in range(CMP_STRIDE):
            xo = jnp.concatenate([p[0, pl.ds(o, segs, stride=CMP_STRIDE), :] for p in pages[k]], axis=0)
            acc = acc + jnp.dot(xo.astype(BF16), w1_ref[o, k], preferred_element_type=F32)
        sh_sc[k, 8:8 + rows, :] = acc[:, 0:hid2]
        pre = acc[:, hid2:] + sh_sc[k, 7:7 + rows, :] + pe_sc[k, 0:1, :]
        sh_sc[k, 0:8, :] = sh_sc[k, rows:rows + 8, :]
        out_ref[0, :, k * LANE:(k + 1) * LANE] = jnp.dot(_silu(pre).astype(BF16), w2_ref[k],
                                                         preferred_element_type=F32)


def _compress_weights(cmp_pe, cmp_w1, cmp_w2):
    span = CMP_BLOCK // CMP_STRIDE
    w1s = cmp_w1.astype(F32).reshape(2, span, CMP_STRIDE, HEAD_DIM, CMP_HID)
    z = jnp.zeros((2, span, CMP_STRIDE, HEAD_DIM, CMP_HID), F32)
    top = jnp.concatenate([w1s, z], axis=-1)
    bot = jnp.concatenate([z, w1s], axis=-1)
    bd = jnp.concatenate([top, bot], axis=-2)
    w1 = jnp.transpose(bd, (2, 0, 3, 1, 4)).reshape(CMP_STRIDE, 2, LANE, span * 2 * CMP_HID).astype(BF16)
    pe = cmp_pe.astype(F32).reshape(2, span, CMP_STRIDE, HEAD_DIM)
    pe = jnp.transpose(pe, (2, 0, 1, 3))
    pe = jnp.concatenate([pe, pe], axis=-1)
    pe = jnp.pad(pe, ((0, 0), (0, 0), (0, 8 - span), (0, 0)))
    w2 = cmp_w2.astype(F32)
    z2 = jnp.zeros_like(w2)
    w2bd = jnp.concatenate([jnp.concatenate([w2, z2], axis=-1), jnp.concatenate([z2, w2], axis=-1)],
                           axis=-2).astype(BF16)
    return w1, pe, w2bd


def _compress(pages_arr, page_ids, cweights):
    nb, n_pages = page_ids.shape
    steps = n_pages // CMP_PAGES
    segs = PAGE_SIZE // CMP_STRIDE
    rows = CMP_PAGES * segs
    w1, pe, w2bd = cweights
    hid2 = NSA_KV_HEADS * CMP_HID

    def page_spec(i, k):
        return pl.BlockSpec((1, PAGE_SIZE, LANE),
                            lambda b, s, pt: (pt[(b * steps + s) * CMP_PAGES + i], 0, k))

    full = lambda shape: pl.BlockSpec(shape, lambda b, s, pt: (0,) * len(shape))
    return pl.pallas_call(
        _compress_kernel,
        out_shape=jax.ShapeDtypeStruct((nb, n_pages * segs, KV_COLS), F32),
        grid_spec=pltpu.PrefetchScalarGridSpec(
            num_scalar_prefetch=1,
            grid=(nb, steps),
            in_specs=[page_spec(i, k) for k in range(2) for i in range(CMP_PAGES)]
            + [full(w1.shape), full(pe.shape), full(w2bd.shape)],
            out_specs=pl.BlockSpec((1, rows, KV_COLS), lambda b, s, pt: (b, s, 0)),
            scratch_shapes=[pltpu.VMEM((2, rows + 8, hid2), F32), pltpu.VMEM((2, 8, hid2), F32)]),
        compiler_params=pltpu.CompilerParams(dimension_semantics=("arbitrary", "arbitrary"),
                                             vmem_limit_bytes=VMEM_LIMIT),
        name="nsa_compress",
    )(page_ids.reshape(-1), *([pages_arr] * (2 * CMP_PAGES)), w1, pe, w2bd)


def _overlap_matrix(n_rows, n_blocks, n_cols):
    m = np.arange(n_rows)[:, None]
    j = np.arange(n_cols)[None, :]
    cs = (m - 1) * CMP_STRIDE
    ov = (m >= 1) & (j < n_blocks) & (cs < j * SLC_BLOCK + SLC_BLOCK) & (cs + CMP_BLOCK > j * SLC_BLOCK)
    return jnp.asarray(ov.astype(np.float32))


def _stack_q(q, g):
    lane = lax.broadcasted_iota(jnp.int32, (Q_BLOCK, LANE), 1)
    keep = (lane < HALF) if g == 0 else (lane >= HALF)
    parts = []
    for jp in range(NSA_REP // 2):
        j = g * (NSA_REP // 2) + jp
        slab = q[:, j * LANE:(j + 1) * LANE]
        rolled = pltpu.roll(slab, HALF, 1)
        first, second = (slab, rolled) if g == 0 else (rolled, slab)
        parts.append(jnp.where(keep, first, 0.0))
        parts.append(jnp.where(keep, second, 0.0))
    return jnp.concatenate(parts, axis=0).astype(BF16)


def _unstack_o(acc, g):
    lane = lax.broadcasted_iota(jnp.int32, (Q_BLOCK, LANE), 1)
    outs = []
    for jp in range(NSA_REP // 2):
        a = acc[(2 * jp) * Q_BLOCK:(2 * jp + 1) * Q_BLOCK]
        b = acc[(2 * jp + 1) * Q_BLOCK:(2 * jp + 2) * Q_BLOCK]
        if g == 0:
            outs.append(jnp.where(lane < HALF, a, pltpu.roll(b, HALF, 1)))
        else:
            outs.append(jnp.where(lane < HALF, pltpu.roll(a, HALF, 1), b))
    return jnp.concatenate(outs, axis=1)


def _tile8(x):
    return jnp.concatenate([x] * NSA_REP, axis=0)


def _rank_select(imp, n_valid, rank_of):
    jrow = lax.broadcasted_iota(jnp.int32, imp.shape, 0)
    rank = jnp.zeros(imp.shape, F32)
    for k in range(n_valid):
        rk = imp[k:k + 1, :]
        ahead = (rk > imp) | ((rk == imp) & (jrow > k))
        rank = rank + jnp.where(ahead, 1.0, 0.0)
    return rank


def _cmp_kernel(q_ref, kv_ref, pb_ref, ov_ref, o_ref, sel_ref, *, n_keys, n_blocks):
    i = pl.program_id(1)
    q = q_ref[0]
    kv = kv_ref[0]
    kc = kv[:, 0:LANE].astype(BF16)
    vc = kv[:, LANE:].astype(BF16)
    qi = lax.broadcasted_iota(jnp.int32, (Q_BLOCK, n_keys), 0)
    mi = lax.broadcasted_iota(jnp.int32, (Q_BLOCK, n_keys), 1)
    qpos = Q_BLOCK * i + qi
    valid8 = _tile8((mi >= 1) & (CMP_STRIDE * mi + CMP_STRIDE - 1 <= qpos))
    rowvalid8 = _tile8(jnp.where(qpos[:, 0:1] >= CMP_BLOCK - 1, 1.0, 0.0))
    shift = (4 * i + 4) % n_keys
    jj = lax.broadcasted_iota(jnp.int32, (LANE, LANE), 1)
    for g in range(NSA_KV_HEADS):
        s = _dot_nt(_stack_q(q, g), kc)
        bias = jnp.concatenate([pltpu.roll(pb_ref[g * NSA_REP + r], shift, 1) for r in range(NSA_REP)], axis=0)
        s = jnp.where(valid8, s + bias, NEG)
        p = jnp.exp(s - jnp.max(s, axis=-1, keepdims=True))
        pc = p / jnp.sum(p, axis=-1, keepdims=True) * rowvalid8
        o_ref[0, :, g * GROUP_W:(g + 1) * GROUP_W] = _unstack_o(
            jnp.dot(pc.astype(BF16), vc, preferred_element_type=F32), g)
        ps = pc[0:Q_BLOCK]
        for r in range(1, NSA_REP):
            ps = ps + pc[r * Q_BLOCK:(r + 1) * Q_BLOCK]
        imp = _dot32(jnp.concatenate([ps, ps], axis=0), ov_ref[...])
        forced = (jj == 0) | (jj == i) | (jj == i - 1)
        imp = jnp.where(forced, 1e6, jnp.where(jj <= i, imp, -1e6))
        imp = jnp.where(jj < n_blocks, imp, -2e6)
        rank = _rank_select(imp.T, n_blocks, None)
        sel = jnp.where(rank < SLC_TOP, 1.0, 0.0).T
        sel_ref[0, g] = sel[0:Q_BLOCK].astype(BF16)


def _cmp_prompt(q, kvc_cmp, pattern, ov):
    b, l, _ = q.shape
    n_keys = kvc_cmp.shape[1]
    n_blocks = l // SLC_BLOCK
    return pl.pallas_call(
        functools.partial(_cmp_kernel, n_keys=n_keys, n_blocks=n_blocks),
        out_shape=(jax.ShapeDtypeStruct((b, l, NSA_WIDTH), F32),
                   jax.ShapeDtypeStruct((b, NSA_KV_HEADS, l, LANE), BF16)),
        grid=(b, l // Q_BLOCK),
        in_specs=[pl.BlockSpec((1, Q_BLOCK, NSA_WIDTH), lambda bi, i: (bi, i, 0)),
                  pl.BlockSpec((1, n_keys, KV_COLS), lambda bi, i: (bi, 0, 0)),
                  pl.BlockSpec(pattern.shape, lambda bi, i: (0, 0, 0)),
                  pl.BlockSpec(ov.shape, lambda bi, i: (0, 0))],
        out_specs=(pl.BlockSpec((1, Q_BLOCK, NSA_WIDTH), lambda bi, i: (bi, i, 0)),
                   pl.BlockSpec((1, NSA_KV_HEADS, Q_BLOCK, LANE), lambda bi, i: (bi, 0, i, 0))),
        compiler_params=pltpu.CompilerParams(vmem_limit_bytes=VMEM_LIMIT),
        name="nsa_cmp_prompt",
    )(q, kvc_cmp, pattern, ov)


def _flash_update(s, v, m, l, acc):
    m_new = jnp.maximum(m, jnp.max(s, axis=-1, keepdims=True))
    alpha = jnp.exp(m - m_new)
    p = jnp.exp(s - m_new)
    l = alpha * l + jnp.sum(p, axis=-1, keepdims=True)
    acc = alpha * acc + jnp.dot(p.astype(BF16), v, preferred_element_type=F32)
    return m_new, l, acc


def _slc_kernel(q_ref, sel_ref, k_ref, v_ref, tn_ref, o_ref):
    i = pl.program_id(1)
    q = q_ref[0]
    n_far = (jnp.maximum(i - 3, 0) + 3) // 4
    jj = lax.broadcasted_iota(jnp.int32, (Q_BLOCK, LANE), 1)
    ej = lax.broadcasted_iota(jnp.int32, (LANE, SLC_NEAR), 0)
    ec = lax.broadcasted_iota(jnp.int32, (LANE, SLC_NEAR), 1) // SLC_BLOCK
    qi = lax.broadcasted_iota(jnp.int32, (Q_BLOCK, SLC_NEAR), 0)
    ci = lax.broadcasted_iota(jnp.int32, (Q_BLOCK, SLC_NEAR), 1)
    for g in range(NSA_KV_HEADS):
        qs = _stack_q(q, g)
        selg = sel_ref[0, g]
        sel_far = jnp.where(jj < i - 3, selg, jnp.zeros_like(selg))

        def far_tile(t, carry):
            m, l, acc = carry
            start = pl.multiple_of(SLC_PAD + SLC_NEAR * t, SLC_BLOCK)
            k = k_ref[0, pl.ds(start, SLC_NEAR), :]
            v = v_ref[0, pl.ds(start, SLC_NEAR), :]
            expand = jnp.where(ej == 4 * t + ec, 1.0, 0.0).astype(BF16)
            hit = jnp.dot(sel_far, expand, preferred_element_type=F32)
            s = _dot_nt(qs, k) + _tile8(jnp.where(hit > 0.5, 0.0, NEG))
            return _flash_update(s, v, m, l, acc)

        init = (jnp.full((ROWS, 1), NEG, F32), jnp.zeros((ROWS, 1), F32), jnp.zeros((ROWS, LANE), F32))
        m, l, acc = lax.fori_loop(0, n_far, far_tile, init)

        start = pl.multiple_of(Q_BLOCK * i, Q_BLOCK)
        k = k_ref[0, pl.ds(start, SLC_NEAR), :]
        v = v_ref[0, pl.ds(start, SLC_NEAR), :]
        expand = jnp.where(ej == i - 3 + ec, 1.0, 0.0).astype(BF16)
        hit = jnp.dot(selg, expand, preferred_element_type=F32)
        rel = ci - SLC_PAD
        ok = (hit > 0.5) & (Q_BLOCK * i + rel >= 0) & (rel <= qi)
        bias = tn_ref[g * NSA_REP:(g + 1) * NSA_REP].reshape(ROWS, SLC_NEAR)
        s = jnp.where(_tile8(ok), _dot_nt(qs, k) + bias, NEG)
        m, l, acc = _flash_update(s, v, m, l, acc)
        o_ref[0, :, g * GROUP_W:(g + 1) * GROUP_W] = _unstack_o(acc / l, g)


def _slc_prompt(q, sel, k_pad, v_pad, tn):
    b, l, _ = q.shape
    lp = k_pad.shape[1]
    return pl.pallas_call(
        _slc_kernel,
        out_shape=jax.ShapeDtypeStruct((b, l, NSA_WIDTH), F32),
        grid=(b, l // Q_BLOCK),
        in_specs=[pl.BlockSpec((1, Q_BLOCK, NSA_WIDTH), lambda bi, i: (bi, i, 0)),
                  pl.BlockSpec((1, NSA_KV_HEADS, Q_BLOCK, LANE), lambda bi, i: (bi, 0, i, 0)),
                  pl.BlockSpec((1, lp, LANE), lambda bi, i: (bi, 0, 0)),
                  pl.BlockSpec((1, lp, LANE), lambda bi, i: (bi, 0, 0)),
                  pl.BlockSpec(tn.shape, lambda bi, i: (0, 0, 0))],
        out_specs=pl.BlockSpec((1, Q_BLOCK, NSA_WIDTH), lambda bi, i: (bi, i, 0)),
        compiler_params=pltpu.CompilerParams(vmem_limit_bytes=VMEM_LIMIT),
        name="nsa_slc_prompt",
    )(q, sel, k_pad, v_pad, tn)


def _win_kernel(q_ref, k_ref, v_ref, tw_ref, o_ref):
    i = pl.program_id(1)
    q = q_ref[0]
    start = pl.multiple_of(Q_BLOCK * i, Q_BLOCK)
    k = k_ref[0, pl.ds(start, WIN_W), :]
    v = v_ref[0, pl.ds(start, WIN_W), :]
    qi = lax.broadcasted_iota(jnp.int32, (Q_BLOCK, WIN_W), 0)
    ci = lax.broadcasted_iota(jnp.int32, (Q_BLOCK, WIN_W), 1)
    dw = qi + WIN_PAD - ci
    ok8 = _tile8((dw >= 0) & (dw < WINDOW) & (Q_BLOCK * i + ci - WIN_PAD >= 0))
    for g in range(NSA_KV_HEADS):
        bias = tw_ref[g * NSA_REP:(g + 1) * NSA_REP].reshape(ROWS, WIN_W)
        s = jnp.where(ok8, _dot_nt(_stack_q(q, g), k) + bias, NEG)
        p = jnp.exp(s - jnp.max(s, axis=-1, keepdims=True))
        acc = jnp.dot(p.astype(BF16), v, preferred_element_type=F32) / jnp.sum(p, axis=-1, keepdims=True)
        o_ref[0, :, g * GROUP_W:(g + 1) * GROUP_W] = _unstack_o(acc, g)


def _win_prompt(q, k_pad, v_pad, tw):
    b, l, _ = q.shape
    lp = k_pad.shape[1]
    return pl.pallas_call(
        _win_kernel,
        out_shape=jax.ShapeDtypeStruct((b, l, NSA_WIDTH), F32),
        grid=(b, l // Q_BLOCK),
        in_specs=[pl.BlockSpec((1, Q_BLOCK, NSA_WIDTH), lambda bi, i: (bi, i, 0)),
                  pl.BlockSpec((1, lp, LANE), lambda bi, i: (bi, 0, 0)),
                  pl.BlockSpec((1, lp, LANE), lambda bi, i: (bi, 0, 0)),
                  pl.BlockSpec(tw.shape, lambda bi, i: (0, 0, 0))],
        out_specs=pl.BlockSpec((1, Q_BLOCK, NSA_WIDTH), lambda bi, i: (bi, i, 0)),
        compiler_params=pltpu.CompilerParams(vmem_limit_bytes=VMEM_LIMIT),
        name="nsa_win_prompt",
    )(q, k_pad, v_pad, tw)


def _dense1_kernel(q_ref, k_ref, v_ref, bm_ref, o_ref, ps_ref):
    k = k_ref[0].astype(BF16)
    v = v_ref[0].astype(BF16)
    for g in range(NSA_KV_HEADS):
        rows = slice(g * NSA_REP, (g + 1) * NSA_REP)
        s = _dot_nt(q_ref[0, rows, :], k) + bm_ref[rows, :]
        p = jnp.exp(s - jnp.max(s, axis=-1, keepdims=True))
        pc = p / jnp.sum(p, axis=-1, keepdims=True)
        o_ref[0, rows, :] = jnp.dot(pc.astype(BF16), v, preferred_element_type=F32)
        ps_ref[0, g:g + 1, :] = jnp.sum(pc, axis=0, keepdims=True)


def _dense1(q_pad, kv, biasmask):
    nb, n, _ = kv.shape
    return pl.pallas_call(
        _dense1_kernel,
        out_shape=(jax.ShapeDtypeStruct((nb, NSA_HEADS, LANE), F32),
                   jax.ShapeDtypeStruct((nb, NSA_KV_HEADS, n), F32)),
        grid=(nb,),
        in_specs=[pl.BlockSpec((1, NSA_HEADS, LANE), lambda b: (b, 0, 0)),
                  pl.BlockSpec((1, n, LANE), lambda b: (b, 0, 0)),
                  pl.BlockSpec((1, n, LANE), lambda b: (b, 0, 1)),
                  pl.BlockSpec((NSA_HEADS, n), lambda b: (0, 0))],
        out_specs=(pl.BlockSpec((1, NSA_HEADS, LANE), lambda b: (b, 0, 0)),
                   pl.BlockSpec((1, NSA_KV_HEADS, n), lambda b: (b, 0, 0))),
        name="nsa_dense_sample",
    )(q_pad, kv, kv, biasmask)


def _take_group_half(o_pad):
    nb = o_pad.shape[0]
    o = o_pad.reshape(nb, NSA_KV_HEADS, NSA_REP, NSA_KV_HEADS, HEAD_DIM)
    o = jnp.stack([o[:, g, :, g, :] for g in range(NSA_KV_HEADS)], axis=1)
    return o.reshape(nb, NSA_WIDTH)


def _topk_kernel(ps_ref, ov_ref, idx_ref, imp_sc, *, n_blocks, cur):
    n_rows = ps_ref.shape[0]
    n_cols = ov_ref.shape[1]
    ps = jnp.concatenate([ps_ref[...], jnp.zeros((LANE - n_rows, ps_ref.shape[1]), F32)], axis=0)
    imp = _dot32(ps, ov_ref[...])
    jj = lax.broadcasted_iota(jnp.int32, (LANE, n_cols), 1)
    forced = (jj == 0) | (jj == cur) | (jj == cur - 1)
    imp = jnp.where(forced, 1e6, jnp.where(jj <= cur, imp, -1e6))
    imp = jnp.where(jj < n_blocks, imp, -2e6)
    imp_t = imp.T
    imp_sc[...] = imp_t
    jrow = lax.broadcasted_iota(jnp.int32, (n_cols, LANE), 0)

    def body(k, rank):
        rk = imp_sc[pl.ds(k, 1), :]
        ahead = (rk > imp_t) | ((rk == imp_t) & (jrow > k))
        return rank + jnp.where(ahead, 1.0, 0.0)

    rank = lax.fori_loop(0, n_blocks, body, jnp.zeros((n_cols, LANE), F32))
    jf = jrow.astype(F32)
    rows = [jnp.sum(jnp.where(rank == float(r), jf, 0.0), axis=0, keepdims=True) for r in range(SLC_TOP)]
    idx_ref[...] = jnp.concatenate(rows, axis=0).astype(jnp.int32)


def _topk_sample(psum, ov, n_blocks, cur):
    n_rows, n_keys = psum.shape
    n_cols = ov.shape[1]
    return pl.pallas_call(
        functools.partial(_topk_kernel, n_blocks=n_blocks, cur=cur),
        out_shape=jax.ShapeDtypeStruct((SLC_TOP, LANE), jnp.int32),
        grid=(1,),
        in_specs=[pl.BlockSpec((n_rows, n_keys), lambda i: (0, 0)),
                  pl.BlockSpec(ov.shape, lambda i: (0, 0))],
        out_specs=pl.BlockSpec((SLC_TOP, LANE), lambda i: (0, 0)),
        scratch_shapes=[pltpu.VMEM((n_cols, LANE), F32)],
        name="nsa_topk_sample",
    )(psum, ov)


def _slc1_kernel(phys_ref, jsel_ref, q_ref, blk_ref, new_ref, bb_ref, o_ref, m_sc, l_sc, acc_sc, *, cur, past):
    b = pl.program_id(0)
    g = pl.program_id(1)
    n = pl.program_id(2)
    j = jsel_ref[(b * NSA_KV_HEADS + g) * SLC_TOP + n]

    @pl.when(n == 0)
    def _():
        m_sc[...] = jnp.full(m_sc.shape, NEG, F32)
        l_sc[...] = jnp.zeros(l_sc.shape, F32)
        acc_sc[...] = jnp.zeros(acc_sc.shape, F32)

    row = lax.broadcasted_iota(jnp.int32, (SLC_BLOCK, KV_COLS), 0)
    fresh = jnp.where(row == 0, new_ref[0], 0.0)
    blk = jnp.where(j == cur, fresh, blk_ref[0])
    k = blk[:, 0:LANE].astype(BF16)
    v = blk[:, LANE:].astype(BF16)
    bidx = jnp.clip(j - (cur - 3), 0, 3)
    bias = bb_ref[bidx, pl.ds(pl.multiple_of(g * NSA_REP, NSA_REP), NSA_REP), :]
    ci = lax.broadcasted_iota(jnp.int32, (NSA_REP, SLC_BLOCK), 1)
    s = jnp.where(j * SLC_BLOCK + ci <= past, _dot_nt(q_ref[0], k) + bias, NEG)
    m, l, acc = _flash_update(s, v, m_sc[...], l_sc[...], acc_sc[...])
    m_sc[...] = m
    l_sc[...] = l
    acc_sc[...] = acc

    @pl.when(n == SLC_TOP - 1)
    def _():
        o_ref[0] = acc / l


def _slc_sample(q_pad, cache_blocks, phys, jsel, new_rows, biasblk, cur, past):
    nb = q_pad.shape[0]
    idx = lambda b, g, n: (b * NSA_KV_HEADS + g) * SLC_TOP + n
    return pl.pallas_call(
        functools.partial(_slc1_kernel, cur=cur, past=past),
        out_shape=jax.ShapeDtypeStruct((nb, NSA_HEADS, LANE), F32),
        grid_spec=pltpu.PrefetchScalarGridSpec(
            num_scalar_prefetch=2,
            grid=(nb, NSA_KV_HEADS, SLC_TOP),
            in_specs=[pl.BlockSpec((1, NSA_REP, LANE), lambda b, g, n, ph, js: (b, g, 0)),
                      pl.BlockSpec((1, SLC_BLOCK, KV_COLS), lambda b, g, n, ph, js: (ph[idx(b, g, n)], 0, 0)),
                      pl.BlockSpec((1, 1, KV_COLS), lambda b, g, n, ph, js: (b, 0, 0)),
                      pl.BlockSpec(biasblk.shape, lambda b, g, n, ph, js: (0, 0, 0))],
            out_specs=pl.BlockSpec((1, NSA_REP, LANE), lambda b, g, n, ph, js: (b, g, 0)),
            scratch_shapes=[pltpu.VMEM((NSA_REP, 1), F32), pltpu.VMEM((NSA_REP, 1), F32),
                            pltpu.VMEM((NSA_REP, LANE), F32)]),
        compiler_params=pltpu.CompilerParams(dimension_semantics=("arbitrary", "arbitrary", "arbitrary")),
        name="nsa_slc_sample",
    )(phys, jsel, q_pad, cache_blocks, new_rows, biasblk)


def _out_kernel(x_ref, gate_ref, yssd_ref, zs_ref, oc_ref, os_ref, ow_ref, gl_ref, za_ref,
                nw1_ref, nw2_ref, w_ref, eg_ref, fg_ref, o_ref, *, per_row, final):
    gates = _sigmoid(gl_ref[...])
    y_nsa = (_dot32(gates, eg_ref[0]) * oc_ref[...] + _dot32(gates, eg_ref[1]) * os_ref[...]
             + _dot32(gates, eg_ref[2]) * ow_ref[...])

    def gated_norm(y, z, w):
        u = y * _silu(z)
        half = u.shape[1] // 2
        parts = []
        for g in range(2):
            ug = u[:, g * half:(g + 1) * half]
            parts.append(ug * lax.rsqrt(jnp.mean(ug * ug, axis=-1, keepdims=True) + NORM_EPS))
        return (jnp.concatenate(parts, axis=1) * w).astype(BF16)

    m1 = gated_norm(yssd_ref[...], zs_ref[...], nw1_ref[...])
    m2 = gated_norm(y_nsa, za_ref[...], nw2_ref[...])
    proj = (jnp.dot(m1, w_ref[0:SSD_WIDTH, :], preferred_element_type=F32)
            + jnp.dot(m2, w_ref[SSD_WIDTH:, :], preferred_element_type=F32))
    gate = gate_ref[...] if per_row else gate_ref[0]
    out = x_ref[...] + gate * proj
    if final:
        out = out * lax.rsqrt(jnp.mean(out * out, axis=-1, keepdims=True) + NORM_EPS) * fg_ref[...]
    o_ref[...] = out


def _gate_expand():
    e = np.zeros((3, LANE, NSA_WIDTH), np.float32)
    for br in range(3):
        for h in range(NSA_HEADS):
            e[br, br * NSA_HEADS + h, h * HEAD_DIM:(h + 1) * HEAD_DIM] = 1.0
    return jnp.asarray(e)


def _layer_out(x2d, gate, y_ssd, z_s, o_cmp, o_slc, o_win, gl, z_a, nw1, nw2, w_out_bf, final_g,
               rows_per_batch, final):
    m = x2d.shape[0]
    per_row = rows_per_batch == 1
    tm = m if per_row else 256
    if per_row:
        gate_spec = pl.BlockSpec((tm, D_MODEL), lambda i: (0, 0))
        gt = gate
    else:
        gate_spec = pl.BlockSpec((1, 1, D_MODEL), lambda i: ((i * tm) // rows_per_batch, 0, 0))
        gt = gate[:, None, :]
    row = lambda w: pl.BlockSpec((tm, w), lambda i: (i, 0))
    full = lambda shape: pl.BlockSpec(shape, lambda i: (0,) * len(shape))
    eg = _gate_expand()
    return pl.pallas_call(
        functools.partial(_out_kernel, per_row=per_row, final=final),
        out_shape=jax.ShapeDtypeStruct((m, D_MODEL), F32),
        grid=(m // tm,),
        in_specs=[row(D_MODEL), gate_spec, row(SSD_WIDTH), row(SSD_WIDTH), row(NSA_WIDTH), row(NSA_WIDTH),
                  row(NSA_WIDTH), row(LANE), row(NSA_WIDTH), full((1, SSD_WIDTH)), full((1, NSA_WIDTH)),
                  full(w_out_bf.shape), full(eg.shape), full((1, D_MODEL))],
        out_specs=row(D_MODEL),
        compiler_params=pltpu.CompilerParams(vmem_limit_bytes=VMEM_LIMIT),
        name="layer_out",
    )(x2d, gt, y_ssd, z_s, o_cmp, o_slc, o_win, gl, z_a, nw1.reshape(1, SSD_WIDTH), nw2.reshape(1, NSA_WIDTH),
      w_out_bf, eg, final_g.reshape(1, D_MODEL))


def _pad_in_weights(w_in):
    cols = []
    off = 0
    for size, width in zip(_SEG_SIZES, _SEG_PAD):
        seg = w_in[:, off:off + size]
        cols.append(jnp.pad(seg, ((0, 0), (0, width - size))))
        off += size
    return jnp.concatenate(cols, axis=1).astype(BF16)


def _front_pad_bf16(kv, rows):
    return jnp.pad(kv, ((0, 0), (rows, 0), (0, 0))).astype(BF16)


def kernel(x_prompt, x_sample, cache_cmp_kv, cache_slc_kv, state_win_kv, state_conv, state_ssm, page_table,
           c_prompt, c_sample, norm_g, ada_w, ada_b, w_in, conv_w, conv_b, dt_bias, a_log, d_skip,
           ssd_norm_w, cmp_pe, cmp_w1, cmp_w2, nsa_norm_w, w_out, rel_bias, final_norm_g):
    nbp, lp, _ = x_prompt.shape
    nbs = x_sample.shape[0]
    depth = w_in.shape[0]
    n_pool = cache_cmp_kv.shape[1]
    n_pages = page_table.shape[1]
    past = n_pages * PAGE_SIZE
    w_buf = state_win_kv.shape[2]
    kv_row = (2, NSA_KV_HEADS, HEAD_DIM)

    n_ck = lp // CMP_STRIDE
    n_sb = lp // SLC_BLOCK
    qi = np.arange(Q_BLOCK)[:, None]
    c0 = n_ck - 4
    pattern = _bias_of_dist(rel_bias, qi - CMP_STRIDE * (np.arange(n_ck)[None, :] - c0) - (CMP_STRIDE - 1))
    far = rel_bias.astype(F32)[REL_BUCKETS - 1][:, None, None]
    tn = _bias_of_dist(rel_bias, qi - np.arange(SLC_NEAR)[None, :] + SLC_PAD) - far
    tw = _bias_of_dist(rel_bias, qi - np.arange(WIN_W)[None, :] + WIN_PAD)
    ov_p = _overlap_matrix(n_ck, n_sb, LANE)

    n_cs = past // CMP_STRIDE
    cur = past // SLC_BLOCK
    n_blk_s = cur + 1
    n_cols_s = -(-n_blk_s // LANE) * LANE
    ov_s = _overlap_matrix(n_cs, n_blk_s, n_cols_s)
    m_s = np.arange(n_cs)
    bm_c = jnp.where(jnp.asarray(m_s >= 1)[None, :],
                     _bias_of_dist(rel_bias, past - (CMP_STRIDE * m_s + CMP_STRIDE - 1)), NEG)
    n_w = -(-(w_buf + 1) // LANE) * LANE
    iw = np.arange(n_w)
    dw = w_buf - iw
    ok_w = (iw <= w_buf) & (dw >= 0) & (dw < WINDOW) & (past - w_buf + iw >= 0)
    bm_w = jnp.where(jnp.asarray(ok_w)[None, :], _bias_of_dist(rel_bias, dw), NEG)
    jb = (cur - 3 + np.arange(4))[:, None]
    biasblk = _bias_of_dist(rel_bias, past - SLC_BLOCK * jb - np.arange(SLC_BLOCK)[None, :])
    biasblk = jnp.transpose(biasblk, (1, 0, 2))

    cmp_pages = cache_cmp_kv.reshape(depth * n_pool, PAGE_SIZE, KV_COLS)
    slc_blocks = cache_slc_kv.reshape(depth * n_pool * (PAGE_SIZE // SLC_BLOCK), SLC_BLOCK, KV_COLS)
    prompt_pages = jnp.arange(nbp * (lp // PAGE_SIZE), dtype=jnp.int32).reshape(nbp, lp // PAGE_SIZE)

    c_all = jnp.concatenate([c_prompt, c_sample], axis=0)
    xp = x_prompt.reshape(nbp * lp, D_MODEL)
    xs = x_sample.reshape(nbs, D_MODEL)
    outs = {k: [] for k in ("pc", "ps", "pw", "pconv", "pssm", "sc", "ss", "sw", "sconv", "sssm")}

    for l in range(depth):
        final = l == depth - 1
        w_pad = _pad_in_weights(w_in[l])
        w_out_bf = w_out[l].astype(BF16)
        cweights = _compress_weights(cmp_pe[l], cmp_w1[l], cmp_w2[l])
        mod = _modulation(c_all, ada_w[l], ada_b[l])
        shift, scale, gate = mod[:, :D_MODEL], mod[:, D_MODEL:2 * D_MODEL], mod[:, 2 * D_MODEL:]

        z_s, xbc, dt, q, kvc, kvs, kvw, gl, z_a = _in_projection(xp, norm_g[l], scale[:nbp], shift[:nbp], w_pad, lp)
        xbc3 = xbc.reshape(nbp, lp, CONV_DIM)
        y_ssd, h_fin = _ssd_prompt(xbc3, dt.reshape(nbp, lp, LANE), conv_w[l], conv_b[l], dt_bias[l], a_log[l],
                                   d_skip[l])
        q3 = q.reshape(nbp, lp, NSA_WIDTH)
        kvc3, kvs3, kvw3 = (t.reshape(nbp, lp, KV_COLS) for t in (kvc, kvs, kvw))
        kc = _compress(kvc.reshape(nbp * (lp // PAGE_SIZE), PAGE_SIZE, KV_COLS), prompt_pages, cweights)
        o_cmp, sel = _cmp_prompt(q3, kc, pattern, ov_p)
        ks_pad = _front_pad_bf16(kvs3, SLC_PAD)
        o_slc = _slc_prompt(q3, sel, ks_pad[:, :, :LANE], ks_pad[:, :, LANE:], tn)
        kw_pad = _front_pad_bf16(kvw3, WIN_PAD)
        o_win = _win_prompt(q3, kw_pad[:, :, :LANE], kw_pad[:, :, LANE:], tw)
        xp = _layer_out(xp, gate[:nbp], y_ssd.reshape(nbp * lp, SSD_WIDTH), z_s,
                        o_cmp.reshape(nbp * lp, NSA_WIDTH), o_slc.reshape(nbp * lp, NSA_WIDTH),
                        o_win.reshape(nbp * lp, NSA_WIDTH), gl, z_a, ssd_norm_w[l], nsa_norm_w[l], w_out_bf,
                        final_norm_g, lp, final)
        outs["pc"].append(kvc3.reshape((nbp, lp) + kv_row))
        outs["ps"].append(kvs3.reshape((nbp, lp) + kv_row))
        outs["pw"].append(kvw3[:, -min(WINDOW, lp):].reshape((nbp, min(WINDOW, lp)) + kv_row))
        outs["pconv"].append(xbc3[:, -(SSD_CONV - 1):])
        outs["pssm"].append(h_fin)

        z_s2, xbc2, dt2, q2, kvc2, kvs2, kvw2, gl2, z_a2 = _in_projection(
            xs, norm_g[l], scale[nbp:], shift[nbp:], w_pad, 1)
        y_ssd2, h2 = _ssd_step(xbc2, state_conv[l], dt2, state_ssm[l], conv_w[l], conv_b[l], dt_bias[l],
                               a_log[l], d_skip[l])
        qh = q2.reshape(nbs, NSA_KV_HEADS, NSA_REP, HEAD_DIM)
        zq = jnp.zeros((nbs, NSA_REP, HEAD_DIM), F32)
        q_pad = jnp.stack([jnp.concatenate([qh[:, 0], zq], axis=-1), jnp.concatenate([zq, qh[:, 1]], axis=-1)],
                          axis=1).reshape(nbs, NSA_HEADS, LANE).astype(BF16)
        kc2 = _compress(cmp_pages, page_table + l * n_pool, cweights)
        oc2, psum = _dense1(q_pad, kc2, bm_c)
        sel_idx = _topk_sample(psum.reshape(nbs * NSA_KV_HEADS, n_cs), ov_s, n_blk_s, cur)
        jsel = sel_idx[:, :nbs * NSA_KV_HEADS].T.reshape(nbs, NSA_KV_HEADS, SLC_TOP)
        jc = jnp.minimum(jsel, cur - 1)
        page = jnp.take_along_axis(page_table, (jc // 2).reshape(nbs, -1), axis=1).reshape(jsel.shape)
        phys = jnp.where(jsel < cur, (page + l * n_pool) * 2 + jc % 2, 0).astype(jnp.int32)
        os2 = _slc_sample(q_pad, slc_blocks, phys.reshape(-1), jsel.reshape(-1), kvs2.reshape(nbs, 1, KV_COLS),
                          biasblk, cur, past)
        kw_full = jnp.concatenate([state_win_kv[l].reshape(nbs, w_buf, KV_COLS), kvw2[:, None, :]], axis=1)
        kw_in = jnp.pad(kw_full, ((0, 0), (0, n_w - (w_buf + 1)), (0, 0)))
        ow2, _ = _dense1(q_pad, kw_in, bm_w)
        xs = _layer_out(xs, gate[nbp:], y_ssd2, z_s2, _take_group_half(oc2), _take_group_half(os2),
                        _take_group_half(ow2), gl2, z_a2, ssd_norm_w[l], nsa_norm_w[l], w_out_bf,
                        final_norm_g, 1, final)
        outs["sc"].append(kvc2.reshape((nbs, 1) + kv_row))
        outs["ss"].append(kvs2.reshape((nbs, 1) + kv_row))
        outs["sw"].append(kw_full[:, -w_buf:].reshape((nbs, w_buf) + kv_row))
        outs["sconv"].append(jnp.concatenate([state_conv[l][:, 1:], xbc2[:, None, :]], axis=1))
        outs["sssm"].append(h2)

    st = lambda k: jnp.stack(outs[k])
    return (xp.reshape(nbp, lp, D_MODEL), xs.reshape(nbs, 1, D_MODEL),
            st("pc"), st("ps"), st("pw"), st("pconv"), st("pssm"),
            st("sc"), st("ss"), st("sw"), st("sconv"), st("sssm"))
```

```python
import functools
import math

import numpy as np
import jax
import jax.numpy as jnp
from jax import lax
from jax.experimental import pallas as pl
from jax.experimental.pallas import tpu as pltpu

F32 = jnp.float32
BF16 = jnp.bfloat16
HIGHEST = lax.Precision.HIGHEST

D_MODEL = 1024
HEAD_DIM = 64
SSD_WIDTH = 1024
SSD_HEADS = 16
SSD_GROUPS = 2
SSD_STATE = 128
SSD_CONV = 4
SSD_CHUNK = 256
CONV_DIM = SSD_WIDTH + 2 * SSD_GROUPS * SSD_STATE
NSA_WIDTH = 1024
NSA_HEADS = 16
NSA_KV_HEADS = 2
NSA_REP = NSA_HEADS // NSA_KV_HEADS
CMP_BLOCK = 32
CMP_STRIDE = 16
CMP_HID = 2 * HEAD_DIM
SLC_BLOCK = 64
SLC_TOP = 16
WINDOW = 512
Q_BLOCK = 64
REL_BUCKETS = 32
REL_MAX_DIST = 128
NORM_EPS = 1e-6
KV_COLS = 2 * NSA_KV_HEADS * HEAD_DIM
PAGE_SIZE = 128
NEG = -1e30

LANE = 128
HALF = LANE // 2
GROUP_W = NSA_REP * HEAD_DIM
ROWS = NSA_REP * Q_BLOCK
SLC_NEAR = 4 * SLC_BLOCK
SLC_PAD = SLC_NEAR
SLC_NEARW = SLC_NEAR + LANE
WIN_W = WINDOW + 2 * Q_BLOCK
WIN_PAD = WIN_W - Q_BLOCK
CMP_PAGES = 16
VMEM_LIMIT = 48 * 1024 * 1024

_SEG_NAMES = ("z_s", "xbc", "dt", "q", "kvc", "kvs", "kvw", "gl", "z_a")
_SEG_SIZES = (SSD_WIDTH, CONV_DIM, SSD_HEADS, NSA_WIDTH, KV_COLS, KV_COLS, KV_COLS, 3 * NSA_HEADS, NSA_WIDTH)
_SEG_PAD = tuple(-(-s // LANE) * LANE for s in _SEG_SIZES)
_SEG_OFF = tuple(int(o) for o in np.cumsum((0,) + _SEG_PAD[:-1]))
IN_PAD = int(sum(_SEG_PAD))


def _sigmoid(x):
    return 1.0 / (1.0 + jnp.exp(-x))


def _silu(x):
    return x * _sigmoid(x)


def _dot32(a, b):
    return jnp.dot(a, b, precision=HIGHEST, preferred_element_type=F32)


def _dot_nt(a, b):
    return lax.dot_general(a, b, (((1,), (1,)), ((), ())), preferred_element_type=F32)


def _bucket_table():
    n = np.arange(REL_MAX_DIST + 1)
    max_exact = REL_BUCKETS // 2
    nf = np.maximum(n, 1).astype(np.float32)
    large = max_exact + (np.log(nf / np.float32(max_exact)) / np.float32(math.log(REL_MAX_DIST / max_exact))
                         * np.float32(REL_BUCKETS - max_exact)).astype(np.int32)
    large = np.minimum(large, REL_BUCKETS - 1)
    return np.where(n < max_exact, n, large).astype(np.int32)


_BUCKETS = _bucket_table()


def _bias_of_dist(rel_bias, dist):
    idx = _BUCKETS[np.clip(dist, 0, REL_MAX_DIST)]
    out = jnp.take(rel_bias.astype(F32), jnp.asarray(idx.reshape(-1)), axis=0)
    return out.T.reshape((NSA_HEADS,) + dist.shape)


def _mod_kernel(c_ref, w_ref, b_ref, o_ref):
    o_ref[...] = _dot32(_silu(c_ref[...]), w_ref[...]) + b_ref[...]


def _modulation(c, w, b):
    m, d = c.shape
    n = w.shape[1]
    tn = 512
    return pl.pallas_call(
        _mod_kernel,
        out_shape=jax.ShapeDtypeStruct((m, n), F32),
        grid=(n // tn,),
        in_specs=[pl.BlockSpec((m, d), lambda j: (0, 0)),
                  pl.BlockSpec((d, tn), lambda j: (0, j)),
                  pl.BlockSpec((1, tn), lambda j: (0, j))],
        out_specs=pl.BlockSpec((m, tn), lambda j: (0, j)),
        name="adaln_mod",
    )(c, w, b.reshape(1, n))


def _inproj_kernel(x_ref, g_ref, sc_ref, sh_ref, w_ref, *out_refs, per_row):
    x = x_ref[...]
    xn = x * lax.rsqrt(jnp.mean(x * x, axis=-1, keepdims=True) + NORM_EPS)
    sc = sc_ref[...] if per_row else sc_ref[0]
    sh = sh_ref[...] if per_row else sh_ref[0]
    h = ((xn * g_ref[...]) * (1.0 + sc) + sh).astype(BF16)
    for name, off, width, ref in zip(_SEG_NAMES, _SEG_OFF, _SEG_PAD, out_refs):
        r = jnp.dot(h, w_ref[:, off:off + width], preferred_element_type=F32)
        if name == "q":
            r = r * (HEAD_DIM ** -0.5)
        ref[...] = r


def _in_projection(x2d, g, scale, shift, w_pad, rows_per_batch):
    m = x2d.shape[0]
    per_row = rows_per_batch == 1
    tm = m if per_row else 256
    if per_row:
        mod_spec = pl.BlockSpec((tm, D_MODEL), lambda i: (0, 0))
        sc, sh = scale, shift
    else:
        mod_spec = pl.BlockSpec((1, 1, D_MODEL), lambda i: ((i * tm) // rows_per_batch, 0, 0))
        sc, sh = scale[:, None, :], shift[:, None, :]
    outs = tuple(jax.ShapeDtypeStruct((m, w), F32) for w in _SEG_PAD)
    return pl.pallas_call(
        functools.partial(_inproj_kernel, per_row=per_row),
        out_shape=outs,
        grid=(m // tm,),
        in_specs=[pl.BlockSpec((tm, D_MODEL), lambda i: (i, 0)),
                  pl.BlockSpec((1, D_MODEL), lambda i: (0, 0)),
                  mod_spec, mod_spec,
                  pl.BlockSpec((D_MODEL, IN_PAD), lambda i: (0, 0))],
        out_specs=tuple(pl.BlockSpec((tm, w), lambda i: (i, 0)) for w in _SEG_PAD),
        compiler_params=pltpu.CompilerParams(vmem_limit_bytes=VMEM_LIMIT),
        name="in_projection",
    )(x2d, g.reshape(1, D_MODEL), sc, sh, w_pad)


def _softplus(x):
    return jnp.maximum(x, 0.0) + jnp.log(1.0 + jnp.exp(-jnp.abs(x)))


def _ssd_kernel(xbc_ref, dt_ref, cw_ref, cb_ref, dtb_ref, alog_ref, dsk_ref, e_ref, tril_ref,
                y_ref, hfin_ref, xe_sc, st_sc):
    c = pl.program_id(1)
    q = SSD_CHUNK
    n_pairs = SSD_HEADS // 2

    @pl.when(c == 0)
    def _():
        xe_sc[0:8, :] = jnp.zeros((8, CONV_DIM), F32)
        st_sc[...] = jnp.zeros(st_sc.shape, F32)

    xe_sc[8:8 + q, :] = xbc_ref[0]
    acc = cb_ref[...] + cw_ref[0:1, :] * xe_sc[5:5 + q, :]
    for k in range(1, SSD_CONV):
        acc = acc + cw_ref[k:k + 1, :] * xe_sc[5 + k:5 + k + q, :]
    u = _silu(acc)
    xe_sc[0:8, :] = xe_sc[q:q + 8, :]

    xs = u[:, :SSD_WIDTH]
    gn = SSD_GROUPS * SSD_STATE
    bm = u[:, SSD_WIDTH:SSD_WIDTH + gn]
    cm = u[:, SSD_WIDTH + gn:]

    dt = _softplus(dt_ref[0] + dtb_ref[...])
    a = dt * (-jnp.exp(alog_ref[...]))
    cs = _dot32(tril_ref[...], a)
    cs_t = cs.T
    cs_last = cs[q - 1:q, :]
    e = e_ref[...]
    dt_e = _dot32(dt, e)
    w_e = _dot32(dt * jnp.exp(cs_last - cs), e)
    ecs_e = _dot32(jnp.exp(cs), e)
    tot_e = _dot32(jnp.broadcast_to(jnp.exp(cs_last), (8, LANE)), e)[0:1, :]
    xdt = (xs * dt_e).astype(BF16)
    xw = (xs * w_e).astype(BF16)

    li = lax.broadcasted_iota(jnp.int32, (q, q), 0)
    si = lax.broadcasted_iota(jnp.int32, (q, q), 1)
    tri = li >= si
    lane = lax.broadcasted_iota(jnp.int32, (q, LANE), 1)

    for g in range(SSD_GROUPS):
        cg = cm[:, g * SSD_STATE:(g + 1) * SSD_STATE].astype(BF16)
        bg = bm[:, g * SSD_STATE:(g + 1) * SSD_STATE]
        cb = _dot_nt(cg, bg.astype(BF16))
        bg_t = bg.T.astype(BF16)
        for jp in range(n_pairs // SSD_GROUPS):
            j = g * (n_pairs // SSD_GROUPS) + jp
            sl = slice(j * LANE, (j + 1) * LANE)
            xdt_p = xdt[:, sl]
            ys = []
            for hh in (2 * j, 2 * j + 1):
                diff = cs[:, hh:hh + 1] - cs_t[hh:hh + 1, :]
                lmat = jnp.exp(jnp.where(tri, diff, NEG))
                ys.append(jnp.dot((cb * lmat).astype(BF16), xdt_p, preferred_element_type=F32))
            y_diag = jnp.where(lane < HALF, ys[0], ys[1])
            st = st_sc[j]
            y_off = jnp.dot(cg, st.astype(BF16), preferred_element_type=F32) * ecs_e[:, sl]
            y_ref[0, :, sl] = y_diag + y_off + xs[:, sl] * dsk_ref[:, sl]
            new = jnp.dot(bg_t, xw[:, sl], preferred_element_type=F32)
            st_sc[j] = st * tot_e[:, sl] + new

    @pl.when(c == pl.num_programs(1) - 1)
    def _():
        for j in range(n_pairs):
            hfin_ref[0, j * LANE:(j + 1) * LANE, :] = st_sc[j].T


def _head_expand():
    e = np.zeros((LANE, SSD_WIDTH), np.float32)
    for h in range(SSD_HEADS):
        e[h, h * HEAD_DIM:(h + 1) * HEAD_DIM] = 1.0
    return jnp.asarray(e)


def _pad_lanes(v):
    return jnp.pad(v.astype(F32), (0, LANE - v.shape[0])).reshape(1, LANE)


def _ssd_prompt(xbc, dt, conv_w, conv_b, dt_bias, a_log, d_skip):
    b, l, _ = xbc.shape
    nc = l // SSD_CHUNK
    full = lambda shape: pl.BlockSpec(shape, lambda i, c: (0,) * len(shape))
    y, hfin = pl.pallas_call(
        _ssd_kernel,
        out_shape=(jax.ShapeDtypeStruct((b, l, SSD_WIDTH), F32),
                   jax.ShapeDtypeStruct((b, SSD_HEADS * HEAD_DIM, SSD_STATE), F32)),
        grid=(b, nc),
        in_specs=[pl.BlockSpec((1, SSD_CHUNK, CONV_DIM), lambda i, c: (i, c, 0)),
                  pl.BlockSpec((1, SSD_CHUNK, LANE), lambda i, c: (i, c, 0)),
                  full((SSD_CONV, CONV_DIM)), full((1, CONV_DIM)), full((1, LANE)), full((1, LANE)),
                  full((1, SSD_WIDTH)), full((LANE, SSD_WIDTH)), full((SSD_CHUNK, SSD_CHUNK))],
        out_specs=(pl.BlockSpec((1, SSD_CHUNK, SSD_WIDTH), lambda i, c: (i, c, 0)),
                   pl.BlockSpec((1, SSD_HEADS * HEAD_DIM, SSD_STATE), lambda i, c: (i, 0, 0))),
        scratch_shapes=[pltpu.VMEM((SSD_CHUNK + 8, CONV_DIM), F32),
                        pltpu.VMEM((SSD_HEADS // 2, SSD_STATE, LANE), F32)],
        compiler_params=pltpu.CompilerParams(dimension_semantics=("arbitrary", "arbitrary"),
                                             vmem_limit_bytes=VMEM_LIMIT),
        name="ssd_prompt",
    )(xbc, dt, conv_w, conv_b.reshape(1, CONV_DIM), _pad_lanes(dt_bias), _pad_lanes(a_log),
      jnp.repeat(d_skip.astype(F32), HEAD_DIM).reshape(1, SSD_WIDTH), _head_expand(),
      jnp.asarray(np.tril(np.ones((SSD_CHUNK, SSD_CHUNK), np.float32))))
    return y, hfin.reshape(b, SSD_HEADS, HEAD_DIM, SSD_STATE)


def _ssd_step_kernel(xbc_ref, c0_ref, c1_ref, c2_ref, dt_ref, h0_ref, cw_ref, cb_ref, dtb_ref, alog_ref,
                     dsk_ref, e_ref, y_ref, hout_ref, xt_sc, dect_sc, bc_sc, yt_sc, xs_sc):
    b = pl.program_id(0)
    nb = xbc_ref.shape[0]
    rows = SSD_HEADS * HEAD_DIM
    gn = SSD_GROUPS * SSD_STATE

    @pl.when(b == 0)
    def _():
        acc = (cb_ref[...] + cw_ref[0:1, :] * c0_ref[...] + cw_ref[1:2, :] * c1_ref[...]
               + cw_ref[2:3, :] * c2_ref[...] + cw_ref[3:4, :] * xbc_ref[...])
        u = _silu(acc)
        xs = u[:, :SSD_WIDTH]
        dt = _softplus(dt_ref[...] + dtb_ref[...])
        dec = jnp.exp(dt * (-jnp.exp(alog_ref[...])))
        e = e_ref[...]
        xdt = xs * _dot32(dt, e)
        dec_e = _dot32(dec, e)
        pad = jnp.zeros((LANE - nb, SSD_WIDTH), F32)
        xt_sc[...] = jnp.concatenate([xdt, pad], axis=0).T
        dect_sc[...] = jnp.concatenate([dec_e, pad], axis=0).T
        bc_sc[...] = u[:, SSD_WIDTH:]
        xs_sc[...] = xs
        yt_sc[...] = jnp.zeros(yt_sc.shape, F32)

    ri = lax.broadcasted_iota(jnp.int32, (LANE, LANE), 0)
    onehot = jnp.where(ri == b, 1.0, 0.0)
    xcol = _dot32(xt_sc[...], onehot)
    dcol = _dot32(dect_sc[...], onehot)
    bc = bc_sc[pl.ds(b, 1), :]
    row = lax.broadcasted_iota(jnp.int32, (rows, SSD_STATE), 0)
    first = row < rows // SSD_GROUPS
    b_full = jnp.where(first, bc[:, 0:SSD_STATE], bc[:, SSD_STATE:gn])
    c_full = jnp.where(first, bc[:, gn:gn + SSD_STATE], bc[:, gn + SSD_STATE:])
    new = dcol * h0_ref[0] + xcol * b_full
    hout_ref[0] = new
    ycol = _dot32(new * c_full, jnp.ones((SSD_STATE, LANE), F32))
    lane = lax.broadcasted_iota(jnp.int32, (rows, LANE), 1)
    yt_sc[...] = jnp.where(lane == b, ycol, yt_sc[...])

    @pl.when(b == nb - 1)
    def _():
        y_ref[...] = yt_sc[...].T[0:nb, :] + xs_sc[...] * dsk_ref[...]


def _ssd_step(xbc, conv_state, dt, h0, conv_w, conv_b, dt_bias, a_log, d_skip):
    nb = xbc.shape[0]
    rows = SSD_HEADS * HEAD_DIM
    full = lambda shape: pl.BlockSpec(shape, lambda i: (0,) * len(shape))
    y, hout = pl.pallas_call(
        _ssd_step_kernel,
        out_shape=(jax.ShapeDtypeStruct((nb, SSD_WIDTH), F32),
                   jax.ShapeDtypeStruct((nb, rows, SSD_STATE), F32)),
        grid=(nb,),
        in_specs=[full((nb, CONV_DIM)), full((nb, CONV_DIM)), full((nb, CONV_DIM)), full((nb, CONV_DIM)),
                  full((nb, LANE)),
                  pl.BlockSpec((1, rows, SSD_STATE), lambda i: (i, 0, 0)),
                  full((SSD_CONV, CONV_DIM)), full((1, CONV_DIM)), full((1, LANE)), full((1, LANE)),
                  full((1, SSD_WIDTH)), full((LANE, SSD_WIDTH))],
        out_specs=(full((nb, SSD_WIDTH)),
                   pl.BlockSpec((1, rows, SSD_STATE), lambda i: (i, 0, 0))),
        scratch_shapes=[pltpu.VMEM((rows, LANE), F32), pltpu.VMEM((rows, LANE), F32),
                        pltpu.VMEM((nb, 2 * SSD_GROUPS * SSD_STATE), F32),
                        pltpu.VMEM((rows, LANE), F32), pltpu.VMEM((nb, SSD_WIDTH), F32)],
        compiler_params=pltpu.CompilerParams(dimension_semantics=("arbitrary",)),
        name="ssd_step",
    )(xbc, conv_state[:, 0], conv_state[:, 1], conv_state[:, 2], dt, h0.reshape(nb, rows, SSD_STATE),
      conv_w, conv_b.reshape(1, CONV_DIM), _pad_lanes(dt_bias), _pad_lanes(a_log),
      jnp.repeat(d_skip.astype(F32), HEAD_DIM).reshape(1, SSD_WIDTH), _head_expand())
    return y, hout.reshape(nb, SSD_HEADS, HEAD_DIM, SSD_STATE)


def _compress_kernel(pt_ref, *refs, transposed):
    if transposed:
        w1_ref, pe_ref, w2_ref, out_ref, sh_sc, pe_sc, xs_sc = refs[CMP_PAGES:]
        for i in range(CMP_PAGES):
            for k in range(2):
                xs_sc[i, k] = refs[i][0, k * LANE:(k + 1) * LANE, :].T
        pages = tuple(tuple(xs_sc.at[i, k] for i in range(CMP_PAGES)) for k in range(2))
    else:
        w1_ref, pe_ref, w2_ref, out_ref, sh_sc, pe_sc = refs[2 * CMP_PAGES:]
        pages = tuple(tuple(refs[k * CMP_PAGES + i].at[0] for i in range(CMP_PAGES)) for k in range(2))
    s = pl.program_id(1)
    segs = PAGE_SIZE // CMP_STRIDE
    rows = CMP_PAGES * segs
    hid2 = NSA_KV_HEADS * CMP_HID

    @pl.when(s == 0)
    def _():
        sh_sc[:, 0:8, :] = jnp.zeros((2, 8, hid2), F32)
        for k in range(2):
            t = jnp.zeros((8, 2 * hid2), F32)
            for o in range(CMP_STRIDE):
                t = t + jnp.dot(pe_ref[o, k].astype(BF16), w1_ref[o, k], preferred_element_type=F32)
            pe_sc[k] = jnp.broadcast_to(t[0:1, 0:hid2] + t[1:2, hid2:], (8, hid2))

    for k in range(2):
        acc = jnp.zeros((rows, 2 * hid2), F32)
        for o in range(CMP_STRIDE):
            xo = jnp.concatenate([p[pl.ds(o, segs, stride=CMP_STRIDE), :] for p in pages[k]], axis=0)
            acc = acc + jnp.dot(xo.astype(BF16), w1_ref[o, k], preferred_element_type=F32)
        sh_sc[k, 8:8 + rows, :] = acc[:, 0:hid2]
        pre = acc[:, hid2:] + sh_sc[k, 7:7 + rows, :] + pe_sc[k, 0:1, :]
        sh_sc[k, 0:8, :] = sh_sc[k, rows:rows + 8, :]
        out_ref[0, :, k * LANE:(k + 1) * LANE] = jnp.dot(_silu(pre).astype(BF16), w2_ref[k],
                                                         preferred_element_type=F32)


def _compress_weights(cmp_pe, cmp_w1, cmp_w2):
    span = CMP_BLOCK // CMP_STRIDE
    w1s = cmp_w1.astype(F32).reshape(2, span, CMP_STRIDE, HEAD_DIM, CMP_HID)
    z = jnp.zeros((2, span, CMP_STRIDE, HEAD_DIM, CMP_HID), F32)
    top = jnp.concatenate([w1s, z], axis=-1)
    bot = jnp.concatenate([z, w1s], axis=-1)
    bd = jnp.concatenate([top, bot], axis=-2)
    w1 = jnp.transpose(bd, (2, 0, 3, 1, 4)).reshape(CMP_STRIDE, 2, LANE, span * 2 * CMP_HID).astype(BF16)
    pe = cmp_pe.astype(F32).reshape(2, span, CMP_STRIDE, HEAD_DIM)
    pe = jnp.transpose(pe, (2, 0, 1, 3))
    pe = jnp.concatenate([pe, pe], axis=-1)
    pe = jnp.pad(pe, ((0, 0), (0, 0), (0, 8 - span), (0, 0)))
    w2 = cmp_w2.astype(F32)
    z2 = jnp.zeros_like(w2)
    w2bd = jnp.concatenate([jnp.concatenate([w2, z2], axis=-1), jnp.concatenate([z2, w2], axis=-1)],
                           axis=-2).astype(BF16)
    return w1, pe, w2bd


def _compress(pages_arr, page_ids, cweights, transposed):
    nb, n_pages = page_ids.shape
    steps = n_pages // CMP_PAGES
    segs = PAGE_SIZE // CMP_STRIDE
    rows = CMP_PAGES * segs
    w1, pe, w2bd = cweights
    hid2 = NSA_KV_HEADS * CMP_HID
    page_of = lambda b, s, pt, i: pt[(b * steps + s) * CMP_PAGES + i]
    scratch = [pltpu.VMEM((2, rows + 8, hid2), F32), pltpu.VMEM((2, 8, hid2), F32)]
    if transposed:
        page_specs = [pl.BlockSpec((1, KV_COLS, PAGE_SIZE), lambda b, s, pt, i=i: (page_of(b, s, pt, i), 0, 0))
                      for i in range(CMP_PAGES)]
        scratch.append(pltpu.VMEM((CMP_PAGES, 2, PAGE_SIZE, LANE), F32))
    else:
        page_specs = [pl.BlockSpec((1, PAGE_SIZE, LANE), lambda b, s, pt, i=i, k=k: (page_of(b, s, pt, i), 0, k))
                      for k in range(2) for i in range(CMP_PAGES)]
    full = lambda shape: pl.BlockSpec(shape, lambda b, s, pt: (0,) * len(shape))
    return pl.pallas_call(
        functools.partial(_compress_kernel, transposed=transposed),
        out_shape=jax.ShapeDtypeStruct((nb, n_pages * segs, KV_COLS), F32),
        grid_spec=pltpu.PrefetchScalarGridSpec(
            num_scalar_prefetch=1,
            grid=(nb, steps),
            in_specs=page_specs + [full(w1.shape), full(pe.shape), full(w2bd.shape)],
            out_specs=pl.BlockSpec((1, rows, KV_COLS), lambda b, s, pt: (b, s, 0)),
            scratch_shapes=scratch),
        compiler_params=pltpu.CompilerParams(dimension_semantics=("arbitrary", "arbitrary"),
                                             vmem_limit_bytes=VMEM_LIMIT),
        name="nsa_compress",
    )(page_ids.reshape(-1), *([pages_arr] * len(page_specs)), w1, pe, w2bd)


def _overlap_matrix(n_rows, n_blocks, n_cols):
    m = np.arange(n_rows)[:, None]
    j = np.arange(n_cols)[None, :]
    cs = (m - 1) * CMP_STRIDE
    ov = (m >= 1) & (j < n_blocks) & (cs < j * SLC_BLOCK + SLC_BLOCK) & (cs + CMP_BLOCK > j * SLC_BLOCK)
    return jnp.asarray(ov.astype(np.float32))


def _stack_q(q, g):
    return _stack_q_f32(q, g).astype(BF16)


def _stack_q_f32(q, g):
    lane = lax.broadcasted_iota(jnp.int32, (Q_BLOCK, LANE), 1)
    keep = (lane < HALF) if g == 0 else (lane >= HALF)
    parts = []
    for jp in range(NSA_REP // 2):
        j = g * (NSA_REP // 2) + jp
        slab = q[:, j * LANE:(j + 1) * LANE]
        rolled = pltpu.roll(slab, HALF, 1)
        first, second = (slab, rolled) if g == 0 else (rolled, slab)
        parts.append(jnp.where(keep, first, 0.0))
        parts.append(jnp.where(keep, second, 0.0))
    return jnp.concatenate(parts, axis=0)


def _unstack_o(acc, g):
    lane = lax.broadcasted_iota(jnp.int32, (Q_BLOCK, LANE), 1)
    outs = []
    for jp in range(NSA_REP // 2):
        a = acc[(2 * jp) * Q_BLOCK:(2 * jp + 1) * Q_BLOCK]
        b = acc[(2 * jp + 1) * Q_BLOCK:(2 * jp + 2) * Q_BLOCK]
        if g == 0:
            outs.append(jnp.where(lane < HALF, a, pltpu.roll(b, HALF, 1)))
        else:
            outs.append(jnp.where(lane < HALF, pltpu.roll(a, HALF, 1), b))
    return jnp.concatenate(outs, axis=1)


def _tile8(x):
    return jnp.concatenate([x] * NSA_REP, axis=0)


def _rank_select(imp, n_valid, rank_of):
    jrow = lax.broadcasted_iota(jnp.int32, imp.shape, 0)
    rank = jnp.zeros(imp.shape, F32)
    for k in range(n_valid):
        rk = imp[k:k + 1, :]
        ahead = (rk > imp) | ((rk == imp) & (jrow > k))
        rank = rank + jnp.where(ahead, 1.0, 0.0)
    return rank


def _cmp_kernel(q_ref, kv_ref, pb_ref, ov_ref, o_ref, sel_ref, *, n_keys, n_blocks):
    i = pl.program_id(1)
    q = q_ref[0]
    kv = kv_ref[0]
    kc = kv[:, 0:LANE].astype(BF16)
    vc = kv[:, LANE:].astype(BF16)
    qi = lax.broadcasted_iota(jnp.int32, (Q_BLOCK, n_keys), 0)
    mi = lax.broadcasted_iota(jnp.int32, (Q_BLOCK, n_keys), 1)
    qpos = Q_BLOCK * i + qi
    valid8 = _tile8((mi >= 1) & (CMP_STRIDE * mi + CMP_STRIDE - 1 <= qpos))
    rowvalid8 = _tile8(jnp.where(qpos[:, 0:1] >= CMP_BLOCK - 1, 1.0, 0.0))
    shift = (4 * i + 4) % n_keys
    jj = lax.broadcasted_iota(jnp.int32, (LANE, LANE), 1)
    sel_t = []
    for g in range(NSA_KV_HEADS):
        s = _dot_nt(_stack_q(q, g), kc)
        bias = jnp.concatenate([pltpu.roll(pb_ref[g * NSA_REP + r], shift, 1) for r in range(NSA_REP)], axis=0)
        s = jnp.where(valid8, s + bias, NEG)
        p = jnp.exp(s - jnp.max(s, axis=-1, keepdims=True))
        pc = p / jnp.sum(p, axis=-1, keepdims=True) * rowvalid8
        o_ref[0, :, g * GROUP_W:(g + 1) * GROUP_W] = _unstack_o(
            jnp.dot(pc.astype(BF16), vc, preferred_element_type=F32), g)
        ps = pc[0:Q_BLOCK]
        for r in range(1, NSA_REP):
            ps = ps + pc[r * Q_BLOCK:(r + 1) * Q_BLOCK]
        imp = _dot32(jnp.concatenate([ps, ps], axis=0), ov_ref[...])
        forced = (jj == 0) | (jj == i) | (jj == i - 1)
        imp = jnp.where(forced, 1e6, jnp.where(jj <= i, imp, -1e6))
        imp = jnp.where(jj < n_blocks, imp, -2e6)
        rank = _rank_select(imp.T, n_blocks, None)
        sel_t.append(jnp.where(rank < SLC_TOP, 1.0, 0.0))
    sel_ref[0, 0] = jnp.where(jj < HALF, sel_t[0], sel_t[1]).astype(BF16)


def _cmp_prompt(q, kvc_cmp, pattern, ov):
    b, l, _ = q.shape
    n_keys = kvc_cmp.shape[1]
    n_blocks = l // SLC_BLOCK
    return pl.pallas_call(
        functools.partial(_cmp_kernel, n_keys=n_keys, n_blocks=n_blocks),
        out_shape=(jax.ShapeDtypeStruct((b, l, NSA_WIDTH), F32),
                   jax.ShapeDtypeStruct((b, l // Q_BLOCK, LANE, LANE), BF16)),
        grid=(b, l // Q_BLOCK),
        in_specs=[pl.BlockSpec((1, Q_BLOCK, NSA_WIDTH), lambda bi, i: (bi, i, 0)),
                  pl.BlockSpec((1, n_keys, KV_COLS), lambda bi, i: (bi, 0, 0)),
                  pl.BlockSpec(pattern.shape, lambda bi, i: (0, 0, 0)),
                  pl.BlockSpec(ov.shape, lambda bi, i: (0, 0))],
        out_specs=(pl.BlockSpec((1, Q_BLOCK, NSA_WIDTH), lambda bi, i: (bi, i, 0)),
                   pl.BlockSpec((1, 1, LANE, LANE), lambda bi, i: (bi, i, 0, 0))),
        compiler_params=pltpu.CompilerParams(vmem_limit_bytes=VMEM_LIMIT),
        name="nsa_cmp_prompt",
    )(q, kvc_cmp, pattern, ov)


def _flash_update_t(st, vt, m, l, acc):
    m_new = jnp.maximum(m, jnp.max(st, axis=0, keepdims=True))
    alpha = jnp.exp(m - m_new)
    p = jnp.exp(st - m_new)
    l = alpha * l + jnp.sum(p, axis=0, keepdims=True)
    acc = alpha * acc + jnp.dot(vt, p.astype(BF16), preferred_element_type=F32)
    return m_new, l, acc


def _slc_kernel(q_ref, sel_ref, k_ref, vt_ref, tnt_ref, tile_ref, o_ref, far_sc, near_sc):
    i = pl.program_id(1)
    q = q_ref[0]
    n_far = (jnp.maximum(i - 3, 0) + 3) // 4
    sel = sel_ref[0, 0]
    jrow = lax.broadcasted_iota(jnp.int32, (LANE, ROWS), 0)
    a = ((i + 1) // 2) * LANE
    delta = Q_BLOCK * (i + 1) - a
    row = lax.broadcasted_iota(jnp.int32, (SLC_NEARW, ROWS), 0)
    qi = lax.broadcasted_iota(jnp.int32, (SLC_NEARW, ROWS), 1) % Q_BLOCK
    rel = row - delta - (SLC_NEAR - Q_BLOCK)
    near_ok = (rel >= -(SLC_NEAR - Q_BLOCK)) & (rel <= qi) & (Q_BLOCK * i + rel >= 0)
    blk0 = a // SLC_BLOCK - SLC_PAD // SLC_BLOCK
    groups = range(NSA_KV_HEADS)
    qs_t = [_stack_q_f32(q, g).T.astype(BF16) for g in groups]
    for g in groups:
        hit = jnp.dot(sel, tile_ref[g], preferred_element_type=F32) > 0.5
        near_sc[g] = jnp.where(hit, 0.0, NEG)
        far_sc[g] = jnp.where(hit & (jrow < i - 3), 0.0, NEG)

    def far_tile(t, carry):
        start = pl.multiple_of(SLC_PAD + SLC_NEAR * t, SLC_NEAR)
        k = k_ref[0, pl.ds(start, SLC_NEAR), :]
        vt = vt_ref[0, :, pl.ds(start, SLC_NEAR)]
        out = []
        for g in groups:
            st = jnp.dot(k, qs_t[g], preferred_element_type=F32)
            st = jnp.concatenate(
                [st[jb * SLC_BLOCK:(jb + 1) * SLC_BLOCK] + far_sc[g, pl.ds(4 * t + jb, 1), :]
                 for jb in range(SLC_NEAR // SLC_BLOCK)], axis=0)
            out.append(_flash_update_t(st, vt, *carry[g]))
        return tuple(out)

    init = (jnp.full((1, ROWS), NEG, F32), jnp.zeros((1, ROWS), F32), jnp.zeros((LANE, ROWS), F32))
    carry = lax.fori_loop(0, n_far, far_tile, (init, init))

    start = pl.multiple_of(a, LANE)
    k = k_ref[0, pl.ds(start, SLC_NEARW), :]
    vt = vt_ref[0, :, pl.ds(start, SLC_NEARW)]
    for g in groups:
        st = jnp.dot(k, qs_t[g], preferred_element_type=F32) + tnt_ref[delta // Q_BLOCK, g]
        st = jnp.concatenate(
            [st[jb * SLC_BLOCK:(jb + 1) * SLC_BLOCK] + near_sc[g, pl.ds(jnp.clip(blk0 + jb, 0, LANE - 1), 1), :]
             for jb in range(SLC_NEARW // SLC_BLOCK)], axis=0)
        m, l, acc = _flash_update_t(jnp.where(near_ok, st, NEG), vt, *carry[g])
        o_ref[0, :, g * GROUP_W:(g + 1) * GROUP_W] = _unstack_o((acc / l).T, g)


def _slc_prompt(q, sel, k_pad, vt_pad, tnt, tile):
    b, l, _ = q.shape
    lp = k_pad.shape[1]
    return pl.pallas_call(
        _slc_kernel,
        out_shape=jax.ShapeDtypeStruct((b, l, NSA_WIDTH), F32),
        grid=(b, l // Q_BLOCK),
        in_specs=[pl.BlockSpec((1, Q_BLOCK, NSA_WIDTH), lambda bi, i: (bi, i, 0)),
                  pl.BlockSpec((1, 1, LANE, LANE), lambda bi, i: (bi, i, 0, 0)),
                  pl.BlockSpec((1, lp, LANE), lambda bi, i: (bi, 0, 0)),
                  pl.BlockSpec((1, LANE, lp), lambda bi, i: (bi, 0, 0)),
                  pl.BlockSpec(tnt.shape, lambda bi, i: (0, 0, 0, 0)),
                  pl.BlockSpec(tile.shape, lambda bi, i: (0, 0, 0))],
        out_specs=pl.BlockSpec((1, Q_BLOCK, NSA_WIDTH), lambda bi, i: (bi, i, 0)),
        scratch_shapes=[pltpu.VMEM((NSA_KV_HEADS, LANE, ROWS), F32), pltpu.VMEM((NSA_KV_HEADS, LANE, ROWS), F32)],
        compiler_params=pltpu.CompilerParams(vmem_limit_bytes=VMEM_LIMIT),
        name="nsa_slc_prompt",
    )(q, sel, k_pad, vt_pad, tnt, tile)


def _group_tile():
    t = np.zeros((NSA_KV_HEADS, LANE, ROWS), np.float32)
    for g in range(NSA_KV_HEADS):
        for r in range(NSA_REP):
            for qq in range(Q_BLOCK):
                t[g, g * Q_BLOCK + qq, r * Q_BLOCK + qq] = 1.0
    return jnp.asarray(t, BF16)


def _win_kernel(q_ref, k_ref, v_ref, tw_ref, o_ref):
    i = pl.program_id(1)
    q = q_ref[0]
    start = pl.multiple_of(Q_BLOCK * i, Q_BLOCK)
    k = k_ref[0, pl.ds(start, WIN_W), :]
    v = v_ref[0, pl.ds(start, WIN_W), :]
    qi = lax.broadcasted_iota(jnp.int32, (Q_BLOCK, WIN_W), 0)
    ci = lax.broadcasted_iota(jnp.int32, (Q_BLOCK, WIN_W), 1)
    dw = qi + WIN_PAD - ci
    ok8 = _tile8((dw >= 0) & (dw < WINDOW) & (Q_BLOCK * i + ci - WIN_PAD >= 0))
    for g in range(NSA_KV_HEADS):
        bias = tw_ref[g * NSA_REP:(g + 1) * NSA_REP].reshape(ROWS, WIN_W)
        s = jnp.where(ok8, _dot_nt(_stack_q(q, g), k) + bias, NEG)
        p = jnp.exp(s - jnp.max(s, axis=-1, keepdims=True))
        acc = jnp.dot(p.astype(BF16), v, preferred_element_type=F32) / jnp.sum(p, axis=-1, keepdims=True)
        o_ref[0, :, g * GROUP_W:(g + 1) * GROUP_W] = _unstack_o(acc, g)


def _win_prompt(q, k_pad, v_pad, tw):
    b, l, _ = q.shape
    lp = k_pad.shape[1]
    return pl.pallas_call(
        _win_kernel,
        out_shape=jax.ShapeDtypeStruct((b, l, NSA_WIDTH), F32),
        grid=(b, l // Q_BLOCK),
        in_specs=[pl.BlockSpec((1, Q_BLOCK, NSA_WIDTH), lambda bi, i: (bi, i, 0)),
                  pl.BlockSpec((1, lp, LANE), lambda bi, i: (bi, 0, 0)),
                  pl.BlockSpec((1, lp, LANE), lambda bi, i: (bi, 0, 0)),
                  pl.BlockSpec(tw.shape, lambda bi, i: (0, 0, 0))],
        out_specs=pl.BlockSpec((1, Q_BLOCK, NSA_WIDTH), lambda bi, i: (bi, i, 0)),
        compiler_params=pltpu.CompilerParams(vmem_limit_bytes=VMEM_LIMIT),
        name="nsa_win_prompt",
    )(q, k_pad, v_pad, tw)


def _dense1_kernel(q_ref, k_ref, v_ref, bm_ref, o_ref, ps_ref):
    k = k_ref[0].astype(BF16)
    v = v_ref[0].astype(BF16)
    for g in range(NSA_KV_HEADS):
        rows = slice(g * NSA_REP, (g + 1) * NSA_REP)
        s = _dot_nt(q_ref[0, rows, :], k) + bm_ref[rows, :]
        p = jnp.exp(s - jnp.max(s, axis=-1, keepdims=True))
        pc = p / jnp.sum(p, axis=-1, keepdims=True)
        o_ref[0, rows, :] = jnp.dot(pc.astype(BF16), v, preferred_element_type=F32)
        ps_ref[0, g:g + 1, :] = jnp.sum(pc, axis=0, keepdims=True)


def _dense1(q_pad, kv, biasmask):
    nb, n, _ = kv.shape
    return pl.pallas_call(
        _dense1_kernel,
        out_shape=(jax.ShapeDtypeStruct((nb, NSA_HEADS, LANE), F32),
                   jax.ShapeDtypeStruct((nb, NSA_KV_HEADS, n), F32)),
        grid=(nb,),
        in_specs=[pl.BlockSpec((1, NSA_HEADS, LANE), lambda b: (b, 0, 0)),
                  pl.BlockSpec((1, n, LANE), lambda b: (b, 0, 0)),
                  pl.BlockSpec((1, n, LANE), lambda b: (b, 0, 1)),
                  pl.BlockSpec((NSA_HEADS, n), lambda b: (0, 0))],
        out_specs=(pl.BlockSpec((1, NSA_HEADS, LANE), lambda b: (b, 0, 0)),
                   pl.BlockSpec((1, NSA_KV_HEADS, n), lambda b: (b, 0, 0))),
        name="nsa_dense_sample",
    )(q_pad, kv, kv, biasmask)


def _take_group_half(o_pad):
    nb = o_pad.shape[0]
    o = o_pad.reshape(nb, NSA_KV_HEADS, NSA_REP, NSA_KV_HEADS, HEAD_DIM)
    o = jnp.stack([o[:, g, :, g, :] for g in range(NSA_KV_HEADS)], axis=1)
    return o.reshape(nb, NSA_WIDTH)


def _topk_kernel(ps_ref, ov_ref, idx_ref, imp_sc, *, n_blocks, cur):
    n_rows = ps_ref.shape[0]
    n_cols = ov_ref.shape[1]
    ps = jnp.concatenate([ps_ref[...], jnp.zeros((LANE - n_rows, ps_ref.shape[1]), F32)], axis=0)
    imp = _dot32(ps, ov_ref[...])
    jj = lax.broadcasted_iota(jnp.int32, (LANE, n_cols), 1)
    forced = (jj == 0) | (jj == cur) | (jj == cur - 1)
    imp = jnp.where(forced, 1e6, jnp.where(jj <= cur, imp, -1e6))
    imp = jnp.where(jj < n_blocks, imp, -2e6)
    imp_t = imp.T
    imp_sc[...] = imp_t
    jrow = lax.broadcasted_iota(jnp.int32, (n_cols, LANE), 0)

    def body(k, rank):
        rk = imp_sc[pl.ds(k, 1), :]
        ahead = (rk > imp_t) | ((rk == imp_t) & (jrow > k))
        return rank + jnp.where(ahead, 1.0, 0.0)

    rank = lax.fori_loop(0, n_blocks, body, jnp.zeros((n_cols, LANE), F32))
    jf = jrow.astype(F32)
    rows = [jnp.sum(jnp.where(rank == float(r), jf, 0.0), axis=0, keepdims=True) for r in range(SLC_TOP)]
    idx_ref[...] = jnp.concatenate(rows, axis=0).astype(jnp.int32)


def _topk_sample(psum, ov, n_blocks, cur):
    n_rows, n_keys = psum.shape
    n_cols = ov.shape[1]
    return pl.pallas_call(
        functools.partial(_topk_kernel, n_blocks=n_blocks, cur=cur),
        out_shape=jax.ShapeDtypeStruct((SLC_TOP, LANE), jnp.int32),
        grid=(1,),
        in_specs=[pl.BlockSpec((n_rows, n_keys), lambda i: (0, 0)),
                  pl.BlockSpec(ov.shape, lambda i: (0, 0))],
        out_specs=pl.BlockSpec((SLC_TOP, LANE), lambda i: (0, 0)),
        scratch_shapes=[pltpu.VMEM((n_cols, LANE), F32)],
        name="nsa_topk_sample",
    )(psum, ov)


def _slc1_kernel(phys_ref, jsel_ref, q_ref, *refs, cur, past):
    pages = refs[:SLC_TOP]
    new_ref, bb_ref, o_ref = refs[SLC_TOP:]
    b = pl.program_id(0)
    g = pl.program_id(1)
    goff = pl.multiple_of(g * HEAD_DIM, HEAD_DIM)
    q = q_ref[0]
    lane = lax.broadcasted_iota(jnp.int32, (NSA_REP, PAGE_SIZE), 1)
    first = lax.broadcasted_iota(jnp.int32, (HEAD_DIM, PAGE_SIZE), 1) == 0
    new_k = jnp.where(first, new_ref[0, pl.ds(goff, HEAD_DIM), :], 0.0)
    new_v = jnp.where(first, new_ref[0, pl.ds(LANE + goff, HEAD_DIM), :], 0.0)
    scores, values = [], []
    for n in range(SLC_TOP):
        j = jsel_ref[(b * NSA_KV_HEADS + g) * SLC_TOP + n]
        kt = jnp.where(j == cur, new_k, pages[n][0, pl.ds(goff, HEAD_DIM), :])
        vt = jnp.where(j == cur, new_v, pages[n][0, pl.ds(LANE + goff, HEAD_DIM), :])
        s = jnp.dot(q, kt.astype(BF16), preferred_element_type=F32)
        ok = (lane // SLC_BLOCK == j % 2) & ((j // 2) * PAGE_SIZE + lane <= past)
        bias = bb_ref[jnp.clip(j - (cur - 3), 0, 3), pl.ds(pl.multiple_of(g * NSA_REP, NSA_REP), NSA_REP), :]
        scores.append(jnp.where(ok, s + bias, NEG))
        values.append(vt.astype(BF16))
    s_all = jnp.concatenate(scores, axis=1)
    p = jnp.exp(s_all - jnp.max(s_all, axis=-1, keepdims=True))
    acc = jnp.zeros((NSA_REP, HEAD_DIM), F32)
    for n in range(SLC_TOP):
        acc = acc + _dot_nt(p[:, n * PAGE_SIZE:(n + 1) * PAGE_SIZE].astype(BF16), values[n])
    o_ref[0] = acc / jnp.sum(p, axis=-1, keepdims=True)


def _slc_sample(q, cache_pages_t, phys, jsel, new_cols, biasblk, cur, past):
    nb = q.shape[0]
    idx = lambda b, g, n: (b * NSA_KV_HEADS + g) * SLC_TOP + n
    page_specs = [pl.BlockSpec((1, KV_COLS, PAGE_SIZE), lambda b, g, ph, js, n=n: (ph[idx(b, g, n)], 0, 0))
                  for n in range(SLC_TOP)]
    return pl.pallas_call(
        functools.partial(_slc1_kernel, cur=cur, past=past),
        out_shape=jax.ShapeDtypeStruct((nb, NSA_HEADS, HEAD_DIM), F32),
        grid_spec=pltpu.PrefetchScalarGridSpec(
            num_scalar_prefetch=2,
            grid=(nb, NSA_KV_HEADS),
            in_specs=[pl.BlockSpec((1, NSA_REP, HEAD_DIM), lambda b, g, ph, js: (b, g, 0))] + page_specs
            + [pl.BlockSpec((1, KV_COLS, 1), lambda b, g, ph, js: (b, 0, 0)),
               pl.BlockSpec(biasblk.shape, lambda b, g, ph, js: (0, 0, 0))],
            out_specs=pl.BlockSpec((1, NSA_REP, HEAD_DIM), lambda b, g, ph, js: (b, g, 0))),
        name="nsa_slc_sample",
    )(phys, jsel, q, *([cache_pages_t] * SLC_TOP), new_cols, biasblk)


def _out_kernel(x_ref, gate_ref, yssd_ref, zs_ref, oc_ref, os_ref, ow_ref, gl_ref, za_ref,
                nw1_ref, nw2_ref, w_ref, eg_ref, fg_ref, o_ref, *, per_row, final):
    gates = _sigmoid(gl_ref[...])
    y_nsa = (_dot32(gates, eg_ref[0]) * oc_ref[...] + _dot32(gates, eg_ref[1]) * os_ref[...]
             + _dot32(gates, eg_ref[2]) * ow_ref[...])

    def gated_norm(y, z, w):
        u = y * _silu(z)
        half = u.shape[1] // 2
        parts = []
        for g in range(2):
            ug = u[:, g * half:(g + 1) * half]
            parts.append(ug * lax.rsqrt(jnp.mean(ug * ug, axis=-1, keepdims=True) + NORM_EPS))
        return (jnp.concatenate(parts, axis=1) * w).astype(BF16)

    m1 = gated_norm(yssd_ref[...], zs_ref[...], nw1_ref[...])
    m2 = gated_norm(y_nsa, za_ref[...], nw2_ref[...])
    proj = (jnp.dot(m1, w_ref[0:SSD_WIDTH, :], preferred_element_type=F32)
            + jnp.dot(m2, w_ref[SSD_WIDTH:, :], preferred_element_type=F32))
    gate = gate_ref[...] if per_row else gate_ref[0]
    out = x_ref[...] + gate * proj
    if final:
        out = out * lax.rsqrt(jnp.mean(out * out, axis=-1, keepdims=True) + NORM_EPS) * fg_ref[...]
    o_ref[...] = out


def _gate_expand():
    e = np.zeros((3, LANE, NSA_WIDTH), np.float32)
    for br in range(3):
        for h in range(NSA_HEADS):
            e[br, br * NSA_HEADS + h, h * HEAD_DIM:(h + 1) * HEAD_DIM] = 1.0
    return jnp.asarray(e)


def _layer_out(x2d, gate, y_ssd, z_s, o_cmp, o_slc, o_win, gl, z_a, nw1, nw2, w_out_bf, final_g,
               rows_per_batch, final):
    m = x2d.shape[0]
    per_row = rows_per_batch == 1
    tm = m if per_row else 256
    if per_row:
        gate_spec = pl.BlockSpec((tm, D_MODEL), lambda i: (0, 0))
        gt = gate
    else:
        gate_spec = pl.BlockSpec((1, 1, D_MODEL), lambda i: ((i * tm) // rows_per_batch, 0, 0))
        gt = gate[:, None, :]
    row = lambda w: pl.BlockSpec((tm, w), lambda i: (i, 0))
    full = lambda shape: pl.BlockSpec(shape, lambda i: (0,) * len(shape))
    eg = _gate_expand()
    return pl.pallas_call(
        functools.partial(_out_kernel, per_row=per_row, final=final),
        out_shape=jax.ShapeDtypeStruct((m, D_MODEL), F32),
        grid=(m // tm,),
        in_specs=[row(D_MODEL), gate_spec, row(SSD_WIDTH), row(SSD_WIDTH), row(NSA_WIDTH), row(NSA_WIDTH),
                  row(NSA_WIDTH), row(LANE), row(NSA_WIDTH), full((1, SSD_WIDTH)), full((1, NSA_WIDTH)),
                  full(w_out_bf.shape), full(eg.shape), full((1, D_MODEL))],
        out_specs=row(D_MODEL),
        compiler_params=pltpu.CompilerParams(vmem_limit_bytes=VMEM_LIMIT),
        name="layer_out",
    )(x2d, gt, y_ssd, z_s, o_cmp, o_slc, o_win, gl, z_a, nw1.reshape(1, SSD_WIDTH), nw2.reshape(1, NSA_WIDTH),
      w_out_bf, eg, final_g.reshape(1, D_MODEL))


def _pad_in_weights(w_in):
    cols = []
    off = 0
    for size, width in zip(_SEG_SIZES, _SEG_PAD):
        seg = w_in[:, off:off + size]
        cols.append(jnp.pad(seg, ((0, 0), (0, width - size))))
        off += size
    return jnp.concatenate(cols, axis=1).astype(BF16)


def _front_pad_bf16(kv, rows):
    return jnp.pad(kv, ((0, 0), (rows, 0), (0, 0))).astype(BF16)


def kernel(x_prompt, x_sample, cache_cmp_kv, cache_slc_kv, state_win_kv, state_conv, state_ssm, page_table,
           c_prompt, c_sample, norm_g, ada_w, ada_b, w_in, conv_w, conv_b, dt_bias, a_log, d_skip,
           ssd_norm_w, cmp_pe, cmp_w1, cmp_w2, nsa_norm_w, w_out, rel_bias, final_norm_g):
    nbp, lp, _ = x_prompt.shape
    nbs = x_sample.shape[0]
    depth = w_in.shape[0]
    n_pool = cache_cmp_kv.shape[1]
    n_pages = page_table.shape[1]
    past = n_pages * PAGE_SIZE
    w_buf = state_win_kv.shape[2]
    kv_row = (2, NSA_KV_HEADS, HEAD_DIM)

    n_ck = lp // CMP_STRIDE
    n_sb = lp // SLC_BLOCK
    qi = np.arange(Q_BLOCK)[:, None]
    c0 = n_ck - 4
    pattern = _bias_of_dist(rel_bias, qi - CMP_STRIDE * (np.arange(n_ck)[None, :] - c0) - (CMP_STRIDE - 1))
    far = rel_bias.astype(F32)[REL_BUCKETS - 1][:, None, None]
    par = np.arange(2)[:, None, None]
    dist_n = (np.arange(Q_BLOCK)[None, None, :] - np.arange(SLC_NEARW)[None, :, None] + Q_BLOCK * par
              + (SLC_NEAR - Q_BLOCK))
    tnt = (_bias_of_dist(rel_bias, dist_n) - far[..., None]).reshape(NSA_KV_HEADS, NSA_REP, 2, SLC_NEARW, Q_BLOCK)
    tnt = jnp.transpose(tnt, (2, 0, 3, 1, 4)).reshape(2, NSA_KV_HEADS, SLC_NEARW, ROWS)
    tile = _group_tile()
    tw = _bias_of_dist(rel_bias, qi - np.arange(WIN_W)[None, :] + WIN_PAD)
    ov_p = _overlap_matrix(n_ck, n_sb, LANE)

    n_cs = past // CMP_STRIDE
    cur = past // SLC_BLOCK
    n_blk_s = cur + 1
    n_cols_s = -(-n_blk_s // LANE) * LANE
    ov_s = _overlap_matrix(n_cs, n_blk_s, n_cols_s)
    m_s = np.arange(n_cs)
    bm_c = jnp.where(jnp.asarray(m_s >= 1)[None, :],
                     _bias_of_dist(rel_bias, past - (CMP_STRIDE * m_s + CMP_STRIDE - 1)), NEG)
    n_w = -(-(w_buf + 1) // LANE) * LANE
    iw = np.arange(n_w)
    dw = w_buf - iw
    ok_w = (iw <= w_buf) & (dw >= 0) & (dw < WINDOW) & (past - w_buf + iw >= 0)
    bm_w = jnp.where(jnp.asarray(ok_w)[None, :], _bias_of_dist(rel_bias, dw), NEG)
    jb = (cur - 3 + np.arange(4))[:, None]
    biasblk = _bias_of_dist(rel_bias, past - SLC_BLOCK * jb - np.arange(SLC_BLOCK)[None, :])
    biasblk = jnp.transpose(biasblk, (1, 0, 2))
    biasblk = jnp.concatenate([biasblk, biasblk], axis=-1)

    pages_t = lambda c: jnp.transpose(c, (0, 1, 3, 4, 5, 2)).reshape(depth * n_pool, KV_COLS, PAGE_SIZE)
    cmp_pages_t = pages_t(cache_cmp_kv)
    slc_pages_t = pages_t(cache_slc_kv)
    prompt_pages = jnp.arange(nbp * (lp // PAGE_SIZE), dtype=jnp.int32).reshape(nbp, lp // PAGE_SIZE)

    c_all = jnp.concatenate([c_prompt, c_sample], axis=0)
    xp = x_prompt.reshape(nbp * lp, D_MODEL)
    xs = x_sample.reshape(nbs, D_MODEL)
    outs = {k: [] for k in ("pc", "ps", "pw", "pconv", "pssm", "sc", "ss", "sw", "sconv", "sssm")}

    for l in range(depth):
        final = l == depth - 1
        w_pad = _pad_in_weights(w_in[l])
        w_out_bf = w_out[l].astype(BF16)
        cweights = _compress_weights(cmp_pe[l], cmp_w1[l], cmp_w2[l])
        mod = _modulation(c_all, ada_w[l], ada_b[l])
        shift, scale, gate = mod[:, :D_MODEL], mod[:, D_MODEL:2 * D_MODEL], mod[:, 2 * D_MODEL:]

        z_s, xbc, dt, q, kvc, kvs, kvw, gl, z_a = _in_projection(xp, norm_g[l], scale[:nbp], shift[:nbp], w_pad, lp)
        xbc3 = xbc.reshape(nbp, lp, CONV_DIM)
        y_ssd, h_fin = _ssd_prompt(xbc3, dt.reshape(nbp, lp, LANE), conv_w[l], conv_b[l], dt_bias[l], a_log[l],
                                   d_skip[l])
        q3 = q.reshape(nbp, lp, NSA_WIDTH)
        kvc3, kvs3, kvw3 = (t.reshape(nbp, lp, KV_COLS) for t in (kvc, kvs, kvw))
        kc = _compress(kvc.reshape(nbp * (lp // PAGE_SIZE), PAGE_SIZE, KV_COLS), prompt_pages, cweights, False)
        o_cmp, sel = _cmp_prompt(q3, kc, pattern, ov_p)
        ks_pad = jnp.pad(kvs3, ((0, 0), (SLC_PAD, SLC_NEARW - SLC_PAD), (0, 0))).astype(BF16)
        o_slc = _slc_prompt(q3, sel, ks_pad[:, :, :LANE], jnp.transpose(ks_pad[:, :, LANE:], (0, 2, 1)), tnt, tile)
        kw_pad = _front_pad_bf16(kvw3, WIN_PAD)
        o_win = _win_prompt(q3, kw_pad[:, :, :LANE], kw_pad[:, :, LANE:], tw)
        xp = _layer_out(xp, gate[:nbp], y_ssd.reshape(nbp * lp, SSD_WIDTH), z_s,
                        o_cmp.reshape(nbp * lp, NSA_WIDTH), o_slc.reshape(nbp * lp, NSA_WIDTH),
                        o_win.reshape(nbp * lp, NSA_WIDTH), gl, z_a, ssd_norm_w[l], nsa_norm_w[l], w_out_bf,
                        final_norm_g, lp, final)
        outs["pc"].append(kvc3.reshape((nbp, lp) + kv_row))
        outs["ps"].append(kvs3.reshape((nbp, lp) + kv_row))
        outs["pw"].append(kvw3[:, -min(WINDOW, lp):].reshape((nbp, min(WINDOW, lp)) + kv_row))
        outs["pconv"].append(xbc3[:, -(SSD_CONV - 1):])
        outs["pssm"].append(h_fin)

        z_s2, xbc2, dt2, q2, kvc2, kvs2, kvw2, gl2, z_a2 = _in_projection(
            xs, norm_g[l], scale[nbp:], shift[nbp:], w_pad, 1)
        y_ssd2, h2 = _ssd_step(xbc2, state_conv[l], dt2, state_ssm[l], conv_w[l], conv_b[l], dt_bias[l],
                               a_log[l], d_skip[l])
        qh = q2.reshape(nbs, NSA_KV_HEADS, NSA_REP, HEAD_DIM)
        zq = jnp.zeros((nbs, NSA_REP, HEAD_DIM), F32)
        q_pad = jnp.stack([jnp.concatenate([qh[:, 0], zq], axis=-1), jnp.concatenate([zq, qh[:, 1]], axis=-1)],
                          axis=1).reshape(nbs, NSA_HEADS, LANE).astype(BF16)
        kc2 = _compress(cmp_pages_t, page_table + l * n_pool, cweights, True)
        oc2, psum = _dense1(q_pad, kc2, bm_c)
        sel_idx = _topk_sample(psum.reshape(nbs * NSA_KV_HEADS, n_cs), ov_s, n_blk_s, cur)
        jsel = sel_idx[:, :nbs * NSA_KV_HEADS].T.reshape(nbs, NSA_KV_HEADS, SLC_TOP)
        jc = jnp.minimum(jsel, cur - 1)
        page = jnp.take_along_axis(page_table, (jc // 2).reshape(nbs, -1), axis=1).reshape(jsel.shape)
        phys = jnp.where(jsel < cur, page + l * n_pool, 0).astype(jnp.int32)
        os2 = _slc_sample(q2.reshape(nbs, NSA_HEADS, HEAD_DIM).astype(BF16), slc_pages_t, phys.reshape(-1),
                          jsel.reshape(-1), kvs2.reshape(nbs, KV_COLS, 1), biasblk, cur, past)
        kw_full = jnp.concatenate([state_win_kv[l].reshape(nbs, w_buf, KV_COLS), kvw2[:, None, :]], axis=1)
        kw_in = jnp.pad(kw_full, ((0, 0), (0, n_w - (w_buf + 1)), (0, 0)))
        ow2, _ = _dense1(q_pad, kw_in, bm_w)
        xs = _layer_out(xs, gate[nbp:], y_ssd2, z_s2, _take_group_half(oc2), os2.reshape(nbs, NSA_WIDTH),
                        _take_group_half(ow2), gl2, z_a2, ssd_norm_w[l], nsa_norm_w[l], w_out_bf,
                        final_norm_g, 1, final)
        outs["sc"].append(kvc2.reshape((nbs, 1) + kv_row))
        outs["ss"].append(kvs2.reshape((nbs, 1) + kv_row))
        outs["sw"].append(kw_full[:, -w_buf:].reshape((nbs, w_buf) + kv_row))
        outs["sconv"].append(jnp.concatenate([state_conv[l][:, 1:], xbc2[:, None, :]], axis=1))
        outs["sssm"].append(h2)

    st = lambda k: jnp.stack(outs[k])
    return (xp.reshape(nbp, lp, D_MODEL), xs.reshape(nbs, 1, D_MODEL),
            st("pc"), st("ps"), st("pw"), st("pconv"), st("pssm"),
            st("sc"), st("ss"), st("sw"), st("sconv"), st("sssm"))
```

```python
import functools
import math

import numpy as np
import jax
import jax.numpy as jnp
from jax import lax
from jax.experimental import pallas as pl
from jax.experimental.pallas import tpu as pltpu

F32 = jnp.float32
BF16 = jnp.bfloat16
HIGHEST = lax.Precision.HIGHEST

D_MODEL = 1024
HEAD_DIM = 64
SSD_WIDTH = 1024
SSD_HEADS = 16
SSD_GROUPS = 2
SSD_STATE = 128
SSD_CONV = 4
SSD_CHUNK = 256
CONV_DIM = SSD_WIDTH + 2 * SSD_GROUPS * SSD_STATE
NSA_WIDTH = 1024
NSA_HEADS = 16
NSA_KV_HEADS = 2
NSA_REP = NSA_HEADS // NSA_KV_HEADS
CMP_BLOCK = 32
CMP_STRIDE = 16
CMP_HID = 2 * HEAD_DIM
SLC_BLOCK = 64
SLC_TOP = 16
WINDOW = 512
Q_BLOCK = 64
REL_BUCKETS = 32
REL_MAX_DIST = 128
NORM_EPS = 1e-6
KV_COLS = 2 * NSA_KV_HEADS * HEAD_DIM
PAGE_SIZE = 128
NEG = -1e30
LOG2E = 1.4426950408889634

LANE = 128
HALF = LANE // 2
GROUP_W = NSA_REP * HEAD_DIM
ROWS = NSA_REP * Q_BLOCK
SLC_NEAR = 4 * SLC_BLOCK
SLC_PAD = SLC_NEAR
SLC_NEARW = SLC_NEAR + LANE
SLC_BACK = 2 * SLC_NEAR
WIN_W = WINDOW + 2 * Q_BLOCK
WIN_PAD = WIN_W - Q_BLOCK
CMP_PAGES = 16
VMEM_LIMIT = 48 * 1024 * 1024

_SEG_NAMES = ("z_s", "xbc", "dt", "q", "kvc", "kvs", "kvw", "gl", "z_a")
_SEG_SIZES = (SSD_WIDTH, CONV_DIM, SSD_HEADS, NSA_WIDTH, KV_COLS, KV_COLS, KV_COLS, 3 * NSA_HEADS, NSA_WIDTH)
_SEG_PAD = tuple(-(-s // LANE) * LANE for s in _SEG_SIZES)
_SEG_OFF = tuple(int(o) for o in np.cumsum((0,) + _SEG_PAD[:-1]))
IN_PAD = int(sum(_SEG_PAD))


def _sigmoid(x):
    return 1.0 / (1.0 + jnp.exp(-x))


def _silu(x):
    return x * _sigmoid(x)


def _dot32(a, b):
    return jnp.dot(a, b, precision=HIGHEST, preferred_element_type=F32)


def _dot_nt(a, b):
    return lax.dot_general(a, b, (((1,), (1,)), ((), ())), preferred_element_type=F32)


def _bucket_table():
    n = np.arange(REL_MAX_DIST + 1)
    max_exact = REL_BUCKETS // 2
    nf = np.maximum(n, 1).astype(np.float32)
    large = max_exact + (np.log(nf / np.float32(max_exact)) / np.float32(math.log(REL_MAX_DIST / max_exact))
                         * np.float32(REL_BUCKETS - max_exact)).astype(np.int32)
    large = np.minimum(large, REL_BUCKETS - 1)
    return np.where(n < max_exact, n, large).astype(np.int32)


_BUCKETS = _bucket_table()


def _bias_of_dist(rel_bias, dist):
    idx = _BUCKETS[np.clip(dist, 0, REL_MAX_DIST)]
    out = jnp.take(rel_bias.astype(F32), jnp.asarray(idx.reshape(-1)), axis=0)
    return out.T.reshape((NSA_HEADS,) + dist.shape)


def _mod_kernel(c_ref, w_ref, b_ref, o_ref):
    o_ref[...] = _dot32(_silu(c_ref[...]), w_ref[...]) + b_ref[...]


def _modulation(c, w, b):
    m, d = c.shape
    n = w.shape[1]
    tn = 512
    return pl.pallas_call(
        _mod_kernel,
        out_shape=jax.ShapeDtypeStruct((m, n), F32),
        grid=(n // tn,),
        in_specs=[pl.BlockSpec((m, d), lambda j: (0, 0)),
                  pl.BlockSpec((d, tn), lambda j: (0, j)),
                  pl.BlockSpec((1, tn), lambda j: (0, j))],
        out_specs=pl.BlockSpec((m, tn), lambda j: (0, j)),
        name="adaln_mod",
    )(c, w, b.reshape(1, n))


def _inproj_kernel(x_ref, g_ref, sc_ref, sh_ref, w_ref, *out_refs, per_row):
    x = x_ref[...]
    xn = x * lax.rsqrt(jnp.mean(x * x, axis=-1, keepdims=True) + NORM_EPS)
    sc = sc_ref[...] if per_row else sc_ref[0]
    sh = sh_ref[...] if per_row else sh_ref[0]
    h = ((xn * g_ref[...]) * (1.0 + sc) + sh).astype(BF16)
    for name, off, width, ref in zip(_SEG_NAMES, _SEG_OFF, _SEG_PAD, out_refs):
        r = jnp.dot(h, w_ref[:, off:off + width], preferred_element_type=F32)
        if name == "q":
            r = r * (HEAD_DIM ** -0.5)
        ref[...] = r


def _in_projection(x2d, g, scale, shift, w_pad, rows_per_batch):
    m = x2d.shape[0]
    per_row = rows_per_batch == 1
    tm = m if per_row else 256
    if per_row:
        mod_spec = pl.BlockSpec((tm, D_MODEL), lambda i: (0, 0))
        sc, sh = scale, shift
    else:
        mod_spec = pl.BlockSpec((1, 1, D_MODEL), lambda i: ((i * tm) // rows_per_batch, 0, 0))
        sc, sh = scale[:, None, :], shift[:, None, :]
    outs = tuple(jax.ShapeDtypeStruct((m, w), F32) for w in _SEG_PAD)
    return pl.pallas_call(
        functools.partial(_inproj_kernel, per_row=per_row),
        out_shape=outs,
        grid=(m // tm,),
        in_specs=[pl.BlockSpec((tm, D_MODEL), lambda i: (i, 0)),
                  pl.BlockSpec((1, D_MODEL), lambda i: (0, 0)),
                  mod_spec, mod_spec,
                  pl.BlockSpec((D_MODEL, IN_PAD), lambda i: (0, 0))],
        out_specs=tuple(pl.BlockSpec((tm, w), lambda i: (i, 0)) for w in _SEG_PAD),
        compiler_params=pltpu.CompilerParams(vmem_limit_bytes=VMEM_LIMIT),
        name="in_projection",
    )(x2d, g.reshape(1, D_MODEL), sc, sh, w_pad)


def _softplus(x):
    return jnp.maximum(x, 0.0) + jnp.log(1.0 + jnp.exp(-jnp.abs(x)))


def _ssd_kernel(xbc_ref, dt_ref, cw_ref, cb_ref, dtb_ref, alog_ref, dsk_ref, e_ref, tril_ref,
                y_ref, hfin_ref, xe_sc, st_sc):
    c = pl.program_id(1)
    q = SSD_CHUNK
    n_pairs = SSD_HEADS // 2

    @pl.when(c == 0)
    def _():
        xe_sc[0:8, :] = jnp.zeros((8, CONV_DIM), F32)
        st_sc[...] = jnp.zeros(st_sc.shape, F32)

    xe_sc[8:8 + q, :] = xbc_ref[0]
    acc = cb_ref[...] + cw_ref[0:1, :] * xe_sc[5:5 + q, :]
    for k in range(1, SSD_CONV):
        acc = acc + cw_ref[k:k + 1, :] * xe_sc[5 + k:5 + k + q, :]
    u = _silu(acc)
    xe_sc[0:8, :] = xe_sc[q:q + 8, :]

    xs = u[:, :SSD_WIDTH]
    gn = SSD_GROUPS * SSD_STATE
    bm = u[:, SSD_WIDTH:SSD_WIDTH + gn]
    cm = u[:, SSD_WIDTH + gn:]

    dt = _softplus(dt_ref[0] + dtb_ref[...])
    a = dt * (-jnp.exp(alog_ref[...]))
    cs = _dot32(tril_ref[...], a)
    cs_t = cs.T
    cs_last = cs[q - 1:q, :]
    e = e_ref[...]
    dt_e = _dot32(dt, e)
    w_e = _dot32(dt * jnp.exp(cs_last - cs), e)
    ecs_e = _dot32(jnp.exp(cs), e)
    tot_e = _dot32(jnp.broadcast_to(jnp.exp(cs_last), (8, LANE)), e)[0:1, :]
    xdt = (xs * dt_e).astype(BF16)
    xw = (xs * w_e).astype(BF16)

    li = lax.broadcasted_iota(jnp.int32, (q, q), 0)
    si = lax.broadcasted_iota(jnp.int32, (q, q), 1)
    tri = li >= si
    lane = lax.broadcasted_iota(jnp.int32, (q, LANE), 1)

    for g in range(SSD_GROUPS):
        cg = cm[:, g * SSD_STATE:(g + 1) * SSD_STATE].astype(BF16)
        bg = bm[:, g * SSD_STATE:(g + 1) * SSD_STATE]
        cb = _dot_nt(cg, bg.astype(BF16))
        bg_t = bg.T.astype(BF16)
        for jp in range(n_pairs // SSD_GROUPS):
            j = g * (n_pairs // SSD_GROUPS) + jp
            sl = slice(j * LANE, (j + 1) * LANE)
            xdt_p = xdt[:, sl]
            ys = []
            for hh in (2 * j, 2 * j + 1):
                diff = cs[:, hh:hh + 1] - cs_t[hh:hh + 1, :]
                lmat = jnp.exp(jnp.where(tri, diff, NEG))
                ys.append(jnp.dot((cb * lmat).astype(BF16), xdt_p, preferred_element_type=F32))
            y_diag = jnp.where(lane < HALF, ys[0], ys[1])
            st = st_sc[j]
            y_off = jnp.dot(cg, st.astype(BF16), preferred_element_type=F32) * ecs_e[:, sl]
            y_ref[0, :, sl] = y_diag + y_off + xs[:, sl] * dsk_ref[:, sl]
            new = jnp.dot(bg_t, xw[:, sl], preferred_element_type=F32)
            st_sc[j] = st * tot_e[:, sl] + new

    @pl.when(c == pl.num_programs(1) - 1)
    def _():
        for j in range(n_pairs):
            hfin_ref[0, j * LANE:(j + 1) * LANE, :] = st_sc[j].T


def _head_expand():
    e = np.zeros((LANE, SSD_WIDTH), np.float32)
    for h in range(SSD_HEADS):
        e[h, h * HEAD_DIM:(h + 1) * HEAD_DIM] = 1.0
    return jnp.asarray(e)


def _pad_lanes(v):
    return jnp.pad(v.astype(F32), (0, LANE - v.shape[0])).reshape(1, LANE)


def _ssd_prompt(xbc, dt, conv_w, conv_b, dt_bias, a_log, d_skip):
    b, l, _ = xbc.shape
    nc = l // SSD_CHUNK
    full = lambda shape: pl.BlockSpec(shape, lambda i, c: (0,) * len(shape))
    y, hfin = pl.pallas_call(
        _ssd_kernel,
        out_shape=(jax.ShapeDtypeStruct((b, l, SSD_WIDTH), F32),
                   jax.ShapeDtypeStruct((b, SSD_HEADS * HEAD_DIM, SSD_STATE), F32)),
        grid=(b, nc),
        in_specs=[pl.BlockSpec((1, SSD_CHUNK, CONV_DIM), lambda i, c: (i, c, 0)),
                  pl.BlockSpec((1, SSD_CHUNK, LANE), lambda i, c: (i, c, 0)),
                  full((SSD_CONV, CONV_DIM)), full((1, CONV_DIM)), full((1, LANE)), full((1, LANE)),
                  full((1, SSD_WIDTH)), full((LANE, SSD_WIDTH)), full((SSD_CHUNK, SSD_CHUNK))],
        out_specs=(pl.BlockSpec((1, SSD_CHUNK, SSD_WIDTH), lambda i, c: (i, c, 0)),
                   pl.BlockSpec((1, SSD_HEADS * HEAD_DIM, SSD_STATE), lambda i, c: (i, 0, 0))),
        scratch_shapes=[pltpu.VMEM((SSD_CHUNK + 8, CONV_DIM), F32),
                        pltpu.VMEM((SSD_HEADS // 2, SSD_STATE, LANE), F32)],
        compiler_params=pltpu.CompilerParams(dimension_semantics=("arbitrary", "arbitrary"),
                                             vmem_limit_bytes=VMEM_LIMIT),
        name="ssd_prompt",
    )(xbc, dt, conv_w, conv_b.reshape(1, CONV_DIM), _pad_lanes(dt_bias), _pad_lanes(a_log),
      jnp.repeat(d_skip.astype(F32), HEAD_DIM).reshape(1, SSD_WIDTH), _head_expand(),
      jnp.asarray(np.tril(np.ones((SSD_CHUNK, SSD_CHUNK), np.float32))))
    return y, hfin.reshape(b, SSD_HEADS, HEAD_DIM, SSD_STATE)


def _ssd_step_kernel(xbc_ref, c0_ref, c1_ref, c2_ref, dt_ref, h0_ref, cw_ref, cb_ref, dtb_ref, alog_ref,
                     dsk_ref, e_ref, y_ref, hout_ref, xt_sc, dect_sc, bc_sc, yt_sc, xs_sc):
    b = pl.program_id(0)
    nb = xbc_ref.shape[0]
    rows = SSD_HEADS * HEAD_DIM
    gn = SSD_GROUPS * SSD_STATE

    @pl.when(b == 0)
    def _():
        acc = (cb_ref[...] + cw_ref[0:1, :] * c0_ref[...] + cw_ref[1:2, :] * c1_ref[...]
               + cw_ref[2:3, :] * c2_ref[...] + cw_ref[3:4, :] * xbc_ref[...])
        u = _silu(acc)
        xs = u[:, :SSD_WIDTH]
        dt = _softplus(dt_ref[...] + dtb_ref[...])
        dec = jnp.exp(dt * (-jnp.exp(alog_ref[...])))
        e = e_ref[...]
        xdt = xs * _dot32(dt, e)
        dec_e = _dot32(dec, e)
        pad = jnp.zeros((LANE - nb, SSD_WIDTH), F32)
        xt_sc[...] = jnp.concatenate([xdt, pad], axis=0).T
        dect_sc[...] = jnp.concatenate([dec_e, pad], axis=0).T
        bc_sc[...] = u[:, SSD_WIDTH:]
        xs_sc[...] = xs
        yt_sc[...] = jnp.zeros(yt_sc.shape, F32)

    ri = lax.broadcasted_iota(jnp.int32, (LANE, LANE), 0)
    onehot = jnp.where(ri == b, 1.0, 0.0)
    xcol = _dot32(xt_sc[...], onehot)
    dcol = _dot32(dect_sc[...], onehot)
    bc = bc_sc[pl.ds(b, 1), :]
    row = lax.broadcasted_iota(jnp.int32, (rows, SSD_STATE), 0)
    first = row < rows // SSD_GROUPS
    b_full = jnp.where(first, bc[:, 0:SSD_STATE], bc[:, SSD_STATE:gn])
    c_full = jnp.where(first, bc[:, gn:gn + SSD_STATE], bc[:, gn + SSD_STATE:])
    new = dcol * h0_ref[0] + xcol * b_full
    hout_ref[0] = new
    ycol = _dot32(new * c_full, jnp.ones((SSD_STATE, LANE), F32))
    lane = lax.broadcasted_iota(jnp.int32, (rows, LANE), 1)
    yt_sc[...] = jnp.where(lane == b, ycol, yt_sc[...])

    @pl.when(b == nb - 1)
    def _():
        y_ref[...] = yt_sc[...].T[0:nb, :] + xs_sc[...] * dsk_ref[...]


def _ssd_step(xbc, conv_state, dt, h0, conv_w, conv_b, dt_bias, a_log, d_skip):
    nb = xbc.shape[0]
    rows = SSD_HEADS * HEAD_DIM
    full = lambda shape: pl.BlockSpec(shape, lambda i: (0,) * len(shape))
    y, hout = pl.pallas_call(
        _ssd_step_kernel,
        out_shape=(jax.ShapeDtypeStruct((nb, SSD_WIDTH), F32),
                   jax.ShapeDtypeStruct((nb, rows, SSD_STATE), F32)),
        grid=(nb,),
        in_specs=[full((nb, CONV_DIM)), full((nb, CONV_DIM)), full((nb, CONV_DIM)), full((nb, CONV_DIM)),
                  full((nb, LANE)),
                  pl.BlockSpec((1, rows, SSD_STATE), lambda i: (i, 0, 0)),
                  full((SSD_CONV, CONV_DIM)), full((1, CONV_DIM)), full((1, LANE)), full((1, LANE)),
                  full((1, SSD_WIDTH)), full((LANE, SSD_WIDTH))],
        out_specs=(full((nb, SSD_WIDTH)),
                   pl.BlockSpec((1, rows, SSD_STATE), lambda i: (i, 0, 0))),
        scratch_shapes=[pltpu.VMEM((rows, LANE), F32), pltpu.VMEM((rows, LANE), F32),
                        pltpu.VMEM((nb, 2 * SSD_GROUPS * SSD_STATE), F32),
                        pltpu.VMEM((rows, LANE), F32), pltpu.VMEM((nb, SSD_WIDTH), F32)],
        compiler_params=pltpu.CompilerParams(dimension_semantics=("arbitrary",)),
        name="ssd_step",
    )(xbc, conv_state[:, 0], conv_state[:, 1], conv_state[:, 2], dt, h0.reshape(nb, rows, SSD_STATE),
      conv_w, conv_b.reshape(1, CONV_DIM), _pad_lanes(dt_bias), _pad_lanes(a_log),
      jnp.repeat(d_skip.astype(F32), HEAD_DIM).reshape(1, SSD_WIDTH), _head_expand())
    return y, hout.reshape(nb, SSD_HEADS, HEAD_DIM, SSD_STATE)


def _compress_kernel(pt_ref, *refs, transposed):
    if transposed:
        w1_ref, pe_ref, w2_ref, out_ref, sh_sc, pe_sc, xs_sc = refs[CMP_PAGES:]
        for i in range(CMP_PAGES):
            for k in range(2):
                xs_sc[i, k] = refs[i][0, k * LANE:(k + 1) * LANE, :].T
        pages = tuple(tuple(xs_sc.at[i, k] for i in range(CMP_PAGES)) for k in range(2))
    else:
        w1_ref, pe_ref, w2_ref, out_ref, sh_sc, pe_sc = refs[2 * CMP_PAGES:]
        pages = tuple(tuple(refs[k * CMP_PAGES + i].at[0] for i in range(CMP_PAGES)) for k in range(2))
    s = pl.program_id(1)
    segs = PAGE_SIZE // CMP_STRIDE
    rows = CMP_PAGES * segs
    hid2 = NSA_KV_HEADS * CMP_HID

    @pl.when(s == 0)
    def _():
        sh_sc[:, 0:8, :] = jnp.zeros((2, 8, hid2), F32)
        for k in range(2):
            t = jnp.zeros((8, 2 * hid2), F32)
            for o in range(CMP_STRIDE):
                t = t + jnp.dot(pe_ref[o, k].astype(BF16), w1_ref[o, k], preferred_element_type=F32)
            pe_sc[k] = jnp.broadcast_to(t[0:1, 0:hid2] + t[1:2, hid2:], (8, hid2))

    for k in range(2):
        acc = jnp.zeros((rows, 2 * hid2), F32)
        for o in range(CMP_STRIDE):
            xo = jnp.concatenate([p[pl.ds(o, segs, stride=CMP_STRIDE), :] for p in pages[k]], axis=0)
            acc = acc + jnp.dot(xo.astype(BF16), w1_ref[o, k], preferred_element_type=F32)
        sh_sc[k, 8:8 + rows, :] = acc[:, 0:hid2]
        pre = acc[:, hid2:] + sh_sc[k, 7:7 + rows, :] + pe_sc[k, 0:1, :]
        sh_sc[k, 0:8, :] = sh_sc[k, rows:rows + 8, :]
        out_ref[0, :, k * LANE:(k + 1) * LANE] = jnp.dot(_silu(pre).astype(BF16), w2_ref[k],
                                                         preferred_element_type=F32)


def _compress_weights(cmp_pe, cmp_w1, cmp_w2):
    span = CMP_BLOCK // CMP_STRIDE
    w1s = cmp_w1.astype(F32).reshape(2, span, CMP_STRIDE, HEAD_DIM, CMP_HID)
    z = jnp.zeros((2, span, CMP_STRIDE, HEAD_DIM, CMP_HID), F32)
    top = jnp.concatenate([w1s, z], axis=-1)
    bot = jnp.concatenate([z, w1s], axis=-1)
    bd = jnp.concatenate([top, bot], axis=-2)
    w1 = jnp.transpose(bd, (2, 0, 3, 1, 4)).reshape(CMP_STRIDE, 2, LANE, span * 2 * CMP_HID).astype(BF16)
    pe = cmp_pe.astype(F32).reshape(2, span, CMP_STRIDE, HEAD_DIM)
    pe = jnp.transpose(pe, (2, 0, 1, 3))
    pe = jnp.concatenate([pe, pe], axis=-1)
    pe = jnp.pad(pe, ((0, 0), (0, 0), (0, 8 - span), (0, 0)))
    w2 = cmp_w2.astype(F32)
    z2 = jnp.zeros_like(w2)
    w2bd = jnp.concatenate([jnp.concatenate([w2, z2], axis=-1), jnp.concatenate([z2, w2], axis=-1)],
                           axis=-2).astype(BF16)
    return w1, pe, w2bd


def _compress(pages_arr, page_ids, cweights, transposed):
    nb, n_pages = page_ids.shape
    steps = n_pages // CMP_PAGES
    segs = PAGE_SIZE // CMP_STRIDE
    rows = CMP_PAGES * segs
    w1, pe, w2bd = cweights
    hid2 = NSA_KV_HEADS * CMP_HID
    page_of = lambda b, s, pt, i: pt[(b * steps + s) * CMP_PAGES + i]
    scratch = [pltpu.VMEM((2, rows + 8, hid2), F32), pltpu.VMEM((2, 8, hid2), F32)]
    if transposed:
        page_specs = [pl.BlockSpec((1, KV_COLS, PAGE_SIZE), lambda b, s, pt, i=i: (page_of(b, s, pt, i), 0, 0))
                      for i in range(CMP_PAGES)]
        scratch.append(pltpu.VMEM((CMP_PAGES, 2, PAGE_SIZE, LANE), F32))
    else:
        page_specs = [pl.BlockSpec((1, PAGE_SIZE, LANE), lambda b, s, pt, i=i, k=k: (page_of(b, s, pt, i), 0, k))
                      for k in range(2) for i in range(CMP_PAGES)]
    full = lambda shape: pl.BlockSpec(shape, lambda b, s, pt: (0,) * len(shape))
    return pl.pallas_call(
        functools.partial(_compress_kernel, transposed=transposed),
        out_shape=jax.ShapeDtypeStruct((nb, n_pages * segs, KV_COLS), F32),
        grid_spec=pltpu.PrefetchScalarGridSpec(
            num_scalar_prefetch=1,
            grid=(nb, steps),
            in_specs=page_specs + [full(w1.shape), full(pe.shape), full(w2bd.shape)],
            out_specs=pl.BlockSpec((1, rows, KV_COLS), lambda b, s, pt: (b, s, 0)),
            scratch_shapes=scratch),
        compiler_params=pltpu.CompilerParams(dimension_semantics=("arbitrary", "arbitrary"),
                                             vmem_limit_bytes=VMEM_LIMIT),
        name="nsa_compress",
    )(page_ids.reshape(-1), *([pages_arr] * len(page_specs)), w1, pe, w2bd)


def _overlap_matrix(n_rows, n_blocks, n_cols):
    m = np.arange(n_rows)[:, None]
    j = np.arange(n_cols)[None, :]
    cs = (m - 1) * CMP_STRIDE
    ov = (m >= 1) & (j < n_blocks) & (cs < j * SLC_BLOCK + SLC_BLOCK) & (cs + CMP_BLOCK > j * SLC_BLOCK)
    return jnp.asarray(ov.astype(np.float32))


def _stack_q(q, g):
    return _stack_q_f32(q, g).astype(BF16)


def _stack_q_f32(q, g):
    lane = lax.broadcasted_iota(jnp.int32, (Q_BLOCK, LANE), 1)
    keep = (lane < HALF) if g == 0 else (lane >= HALF)
    parts = []
    for jp in range(NSA_REP // 2):
        j = g * (NSA_REP // 2) + jp
        slab = q[:, j * LANE:(j + 1) * LANE]
        rolled = pltpu.roll(slab, HALF, 1)
        first, second = (slab, rolled) if g == 0 else (rolled, slab)
        parts.append(jnp.where(keep, first, 0.0))
        parts.append(jnp.where(keep, second, 0.0))
    return jnp.concatenate(parts, axis=0)


def _unstack_o(acc, g):
    lane = lax.broadcasted_iota(jnp.int32, (Q_BLOCK, LANE), 1)
    outs = []
    for jp in range(NSA_REP // 2):
        a = acc[(2 * jp) * Q_BLOCK:(2 * jp + 1) * Q_BLOCK]
        b = acc[(2 * jp + 1) * Q_BLOCK:(2 * jp + 2) * Q_BLOCK]
        if g == 0:
            outs.append(jnp.where(lane < HALF, a, pltpu.roll(b, HALF, 1)))
        else:
            outs.append(jnp.where(lane < HALF, pltpu.roll(a, HALF, 1), b))
    return jnp.concatenate(outs, axis=1)


def _tile8(x):
    return jnp.concatenate([x] * NSA_REP, axis=0)


def _rank_rows(imp, n_valid):
    sub = 8
    chunks = [imp[c:c + sub] for c in range(0, imp.shape[0], sub)]
    ranks = [jnp.zeros(ch.shape, F32) for ch in chunks]
    jrow = lax.broadcasted_iota(jnp.int32, chunks[0].shape, 0)
    for k in range(n_valid):
        rk = imp[k:k + 1, :]
        for c, ch in enumerate(chunks):
            if c * sub > k:
                ahead = rk >= ch
            elif c * sub + sub - 1 < k:
                ahead = rk > ch
            else:
                ahead = (rk > ch) | ((rk == ch) & (jrow + c * sub > k))
            ranks[c] = ranks[c] + jnp.where(ahead, 1.0, 0.0)
    return jnp.concatenate(ranks, axis=0)


def _cmp_kernel(q_ref, kv_ref, pb_ref, ov_ref, o_ref, sel_ref, bias_sc, *, n_keys, n_blocks):
    i = pl.program_id(0)

    @pl.when(pl.program_id(1) == 0)
    def _():
        for h in range(NSA_HEADS):
            bias_sc[h] = pltpu.roll(pb_ref[h], (4 * i + 4) % n_keys, 1)

    q = q_ref[0]
    kv = kv_ref[0]
    kc = kv[:, 0:LANE].astype(BF16)
    vc = kv[:, LANE:].astype(BF16)
    qi = lax.broadcasted_iota(jnp.int32, (Q_BLOCK, n_keys), 0)
    mi = lax.broadcasted_iota(jnp.int32, (Q_BLOCK, n_keys), 1)
    qpos = Q_BLOCK * i + qi
    valid8 = _tile8((mi >= 1) & (CMP_STRIDE * mi + CMP_STRIDE - 1 <= qpos))
    rowvalid8 = _tile8(jnp.where(qpos[:, 0:1] >= CMP_BLOCK - 1, 1.0, 0.0))
    jj = lax.broadcasted_iota(jnp.int32, (LANE, LANE), 1)
    psum = []
    for g in range(NSA_KV_HEADS):
        s = _dot_nt(_stack_q(q, g), kc)
        s = jnp.where(valid8, s + bias_sc[g * NSA_REP:(g + 1) * NSA_REP].reshape(ROWS, n_keys), NEG)
        p = jnp.exp(s - jnp.max(s, axis=-1, keepdims=True))
        pc = p / jnp.sum(p, axis=-1, keepdims=True) * rowvalid8
        o_ref[0, :, g * GROUP_W:(g + 1) * GROUP_W] = _unstack_o(
            jnp.dot(pc.astype(BF16), vc, preferred_element_type=F32), g)
        ps = pc[0:Q_BLOCK]
        for r in range(1, NSA_REP):
            ps = ps + pc[r * Q_BLOCK:(r + 1) * Q_BLOCK]
        psum.append(ps)
    ps = jnp.concatenate(psum, axis=0)
    hi = ps.astype(BF16)
    lo = (ps - hi.astype(F32)).astype(BF16)
    ov = ov_ref[...].astype(BF16)
    imp = (jnp.dot(hi, ov, preferred_element_type=F32)
           + jnp.dot(lo, ov, preferred_element_type=F32))
    forced = (jj == 0) | (jj == i) | (jj == i - 1)
    imp = jnp.where(forced, 1e6, jnp.where(jj <= i, imp, -1e6))
    imp = jnp.where(jj < n_blocks, imp, -2e6)
    rank = _rank_rows(imp.T, n_blocks)
    sel_ref[0, 0] = jnp.where(rank < SLC_TOP, 1.0, 0.0).astype(BF16)


def _cmp_prompt(q, kvc_cmp, pattern, ov):
    b, l, _ = q.shape
    n_keys = kvc_cmp.shape[1]
    n_blocks = l // SLC_BLOCK
    return pl.pallas_call(
        functools.partial(_cmp_kernel, n_keys=n_keys, n_blocks=n_blocks),
        out_shape=(jax.ShapeDtypeStruct((b, l, NSA_WIDTH), F32),
                   jax.ShapeDtypeStruct((b, l // Q_BLOCK, LANE, LANE), BF16)),
        grid=(l // Q_BLOCK, b),
        in_specs=[pl.BlockSpec((1, Q_BLOCK, NSA_WIDTH), lambda i, bi: (bi, i, 0)),
                  pl.BlockSpec((1, n_keys, KV_COLS), lambda i, bi: (bi, 0, 0)),
                  pl.BlockSpec(pattern.shape, lambda i, bi: (0, 0, 0)),
                  pl.BlockSpec(ov.shape, lambda i, bi: (0, 0))],
        out_specs=(pl.BlockSpec((1, Q_BLOCK, NSA_WIDTH), lambda i, bi: (bi, i, 0)),
                   pl.BlockSpec((1, 1, LANE, LANE), lambda i, bi: (bi, i, 0, 0))),
        scratch_shapes=[pltpu.VMEM(pattern.shape, F32)],
        compiler_params=pltpu.CompilerParams(dimension_semantics=("arbitrary", "arbitrary"),
                                             vmem_limit_bytes=VMEM_LIMIT),
        name="nsa_cmp_prompt",
    )(q, kvc_cmp, pattern, ov)


def _flash_update_t(st, vt, m, acc):
    m_new = jnp.maximum(m, jnp.max(st, axis=0, keepdims=True))
    p = jnp.exp2(st - m_new)
    acc = jnp.exp2(m - m_new) * acc + jnp.dot(vt, p.astype(BF16), preferred_element_type=F32)
    return m_new, acc


def _slc_kernel(q_ref, sel_ref, kx_ref, vt_ref, tnt_ref, tile_ref, o_ref, far_sc, near_sc, sta_sc, stb_sc):
    i = pl.program_id(1)
    q = q_ref[0] * LOG2E
    n_far = (jnp.maximum(i - 3, 0) + 3) // 4
    sel = sel_ref[0, 0]
    jrow = lax.broadcasted_iota(jnp.int32, (LANE, ROWS), 0)
    a = ((i + 1) // 2) * LANE
    delta = Q_BLOCK * (i + 1) - a
    row = lax.broadcasted_iota(jnp.int32, (SLC_NEARW, ROWS), 0)
    qi = lax.broadcasted_iota(jnp.int32, (SLC_NEARW, ROWS), 1) % Q_BLOCK
    rel = row - delta - (SLC_NEAR - Q_BLOCK)
    near_ok = (rel >= -(SLC_NEAR - Q_BLOCK)) & (rel <= qi) & (Q_BLOCK * i + rel >= 0)
    groups = range(NSA_KV_HEADS)
    for g in groups:
        qs_t = _stack_q_f32(q, g).T.astype(BF16)
        hit = jnp.dot(sel, tile_ref[g], preferred_element_type=F32) > 0.5
        near_sc[g] = jnp.concatenate([qs_t, jnp.where(hit, 0.0, NEG).astype(BF16)], axis=0)
        far_sc[g] = jnp.concatenate([qs_t, jnp.where(hit & (jrow < i - 3), 0.0, NEG).astype(BF16)], axis=0)

    def far_scores(t, dst):
        start = pl.multiple_of(SLC_PAD + SLC_NEAR * t, SLC_NEAR)
        kx = kx_ref[0, pl.ds(start, SLC_NEAR), :]
        for g in groups:
            dst[g] = jnp.dot(kx, far_sc[g], preferred_element_type=F32)

    def far_softmax(t, src, carry):
        start = pl.multiple_of(SLC_PAD + SLC_NEAR * t, SLC_NEAR)
        return tuple(_flash_update_t(src[g], vt_ref[0, g, :, pl.ds(start, SLC_NEAR)], *carry[g]) for g in groups)

    def far_pair(u, carry):
        far_scores(2 * u + 1, stb_sc)
        carry = far_softmax(2 * u, sta_sc, carry)
        far_scores(2 * u + 2, sta_sc)
        return far_softmax(2 * u + 1, stb_sc, carry)

    far_scores(0, sta_sc)
    init = (jnp.full((1, ROWS), NEG, F32), jnp.zeros((LANE, ROWS), F32))
    carry = lax.fori_loop(0, (n_far + 1) // 2, far_pair, (init, init))

    start = pl.multiple_of(a, LANE)
    kx = kx_ref[0, pl.ds(start, SLC_NEARW), :]
    lane_t = lax.broadcasted_iota(jnp.int32, (ROWS, LANE), 1)
    for g in groups:
        st = jnp.dot(kx, near_sc[g], preferred_element_type=F32) + tnt_ref[delta // Q_BLOCK, g]
        _, acc = _flash_update_t(jnp.where(near_ok, st, NEG), vt_ref[0, g, :, pl.ds(start, SLC_NEARW)], *carry[g])
        acc_t = acc.T
        den = jnp.where(lane_t < HALF, pltpu.roll(acc_t, HALF, 1), 1.0)
        o_ref[0, :, g * GROUP_W:(g + 1) * GROUP_W] = _unstack_o(acc_t / den, 0)


def _slc_prompt(q, sel, kx_pad, vt_pad, tnt, tile):
    b, l, _ = q.shape
    lp = kx_pad.shape[1]
    return pl.pallas_call(
        _slc_kernel,
        out_shape=jax.ShapeDtypeStruct((b, l, NSA_WIDTH), F32),
        grid=(b, l // Q_BLOCK),
        in_specs=[pl.BlockSpec((1, Q_BLOCK, NSA_WIDTH), lambda bi, i: (bi, i, 0)),
                  pl.BlockSpec((1, 1, LANE, LANE), lambda bi, i: (bi, i, 0, 0)),
                  pl.BlockSpec((1, lp, 2 * LANE), lambda bi, i: (bi, 0, 0)),
                  pl.BlockSpec((1, NSA_KV_HEADS, LANE, lp), lambda bi, i: (bi, 0, 0, 0)),
                  pl.BlockSpec(tnt.shape, lambda bi, i: (0, 0, 0, 0)),
                  pl.BlockSpec(tile.shape, lambda bi, i: (0, 0, 0))],
        out_specs=pl.BlockSpec((1, Q_BLOCK, NSA_WIDTH), lambda bi, i: (bi, i, 0)),
        scratch_shapes=[pltpu.VMEM((NSA_KV_HEADS, 2 * LANE, ROWS), BF16),
                        pltpu.VMEM((NSA_KV_HEADS, 2 * LANE, ROWS), BF16),
                        pltpu.VMEM((NSA_KV_HEADS, SLC_NEAR, ROWS), F32),
                        pltpu.VMEM((NSA_KV_HEADS, SLC_NEAR, ROWS), F32)],
        compiler_params=pltpu.CompilerParams(vmem_limit_bytes=VMEM_LIMIT),
        name="nsa_slc_prompt",
    )(q, sel, kx_pad, vt_pad, tnt, tile)


def _group_tile():
    t = np.zeros((NSA_KV_HEADS, LANE, ROWS), np.float32)
    for g in range(NSA_KV_HEADS):
        for r in range(NSA_REP):
            for qq in range(Q_BLOCK):
                t[g, g * Q_BLOCK + qq, r * Q_BLOCK + qq] = 1.0
    return jnp.asarray(t, BF16)


def _win_kernel(q_ref, k_ref, v_ref, tw_ref, o_ref):
    i = pl.program_id(1)
    q = q_ref[0]
    start = pl.multiple_of(Q_BLOCK * i, Q_BLOCK)
    k = k_ref[0, pl.ds(start, WIN_W), :]
    v = v_ref[0, pl.ds(start, WIN_W), :]
    qi = lax.broadcasted_iota(jnp.int32, (Q_BLOCK, WIN_W), 0)
    ci = lax.broadcasted_iota(jnp.int32, (Q_BLOCK, WIN_W), 1)
    dw = qi + WIN_PAD - ci
    ok8 = _tile8((dw >= 0) & (dw < WINDOW) & (Q_BLOCK * i + ci - WIN_PAD >= 0))
    for g in range(NSA_KV_HEADS):
        bias = tw_ref[g * NSA_REP:(g + 1) * NSA_REP].reshape(ROWS, WIN_W)
        s = jnp.where(ok8, _dot_nt(_stack_q(q, g), k) + bias, NEG)
        p = jnp.exp(s - jnp.max(s, axis=-1, keepdims=True))
        acc = jnp.dot(p.astype(BF16), v, preferred_element_type=F32) / jnp.sum(p, axis=-1, keepdims=True)
        o_ref[0, :, g * GROUP_W:(g + 1) * GROUP_W] = _unstack_o(acc, g)


def _win_prompt(q, k_pad, v_pad, tw):
    b, l, _ = q.shape
    lp = k_pad.shape[1]
    return pl.pallas_call(
        _win_kernel,
        out_shape=jax.ShapeDtypeStruct((b, l, NSA_WIDTH), F32),
        grid=(b, l // Q_BLOCK),
        in_specs=[pl.BlockSpec((1, Q_BLOCK, NSA_WIDTH), lambda bi, i: (bi, i, 0)),
                  pl.BlockSpec((1, lp, LANE), lambda bi, i: (bi, 0, 0)),
                  pl.BlockSpec((1, lp, LANE), lambda bi, i: (bi, 0, 0)),
                  pl.BlockSpec(tw.shape, lambda bi, i: (0, 0, 0))],
        out_specs=pl.BlockSpec((1, Q_BLOCK, NSA_WIDTH), lambda bi, i: (bi, i, 0)),
        compiler_params=pltpu.CompilerParams(vmem_limit_bytes=VMEM_LIMIT),
        name="nsa_win_prompt",
    )(q, k_pad, v_pad, tw)


def _dense1_kernel(q_ref, k_ref, v_ref, bm_ref, o_ref, ps_ref):
    k = k_ref[0].astype(BF16)
    v = v_ref[0].astype(BF16)
    for g in range(NSA_KV_HEADS):
        rows = slice(g * NSA_REP, (g + 1) * NSA_REP)
        s = _dot_nt(q_ref[0, rows, :], k) + bm_ref[rows, :]
        p = jnp.exp(s - jnp.max(s, axis=-1, keepdims=True))
        pc = p / jnp.sum(p, axis=-1, keepdims=True)
        o_ref[0, rows, :] = jnp.dot(pc.astype(BF16), v, preferred_element_type=F32)
        ps_ref[0, g:g + 1, :] = jnp.sum(pc, axis=0, keepdims=True)


def _dense1(q_pad, kv, biasmask):
    nb, n, _ = kv.shape
    return pl.pallas_call(
        _dense1_kernel,
        out_shape=(jax.ShapeDtypeStruct((nb, NSA_HEADS, LANE), F32),
                   jax.ShapeDtypeStruct((nb, NSA_KV_HEADS, n), F32)),
        grid=(nb,),
        in_specs=[pl.BlockSpec((1, NSA_HEADS, LANE), lambda b: (b, 0, 0)),
                  pl.BlockSpec((1, n, LANE), lambda b: (b, 0, 0)),
                  pl.BlockSpec((1, n, LANE), lambda b: (b, 0, 1)),
                  pl.BlockSpec((NSA_HEADS, n), lambda b: (0, 0))],
        out_specs=(pl.BlockSpec((1, NSA_HEADS, LANE), lambda b: (b, 0, 0)),
                   pl.BlockSpec((1, NSA_KV_HEADS, n), lambda b: (b, 0, 0))),
        name="nsa_dense_sample",
    )(q_pad, kv, kv, biasmask)


def _take_group_half(o_pad):
    nb = o_pad.shape[0]
    o = o_pad.reshape(nb, NSA_KV_HEADS, NSA_REP, NSA_KV_HEADS, HEAD_DIM)
    o = jnp.stack([o[:, g, :, g, :] for g in range(NSA_KV_HEADS)], axis=1)
    return o.reshape(nb, NSA_WIDTH)


def _topk_kernel(ps_ref, ov_ref, idx_ref, imp_sc, *, n_blocks, cur):
    n_rows = ps_ref.shape[0]
    n_cols = ov_ref.shape[1]
    ps = jnp.concatenate([ps_ref[...], jnp.zeros((LANE - n_rows, ps_ref.shape[1]), F32)], axis=0)
    imp = _dot32(ps, ov_ref[...])
    jj = lax.broadcasted_iota(jnp.int32, (LANE, n_cols), 1)
    forced = (jj == 0) | (jj == cur) | (jj == cur - 1)
    imp = jnp.where(forced, 1e6, jnp.where(jj <= cur, imp, -1e6))
    imp = jnp.where(jj < n_blocks, imp, -2e6)
    imp_t = imp.T
    imp_sc[...] = imp_t
    jrow = lax.broadcasted_iota(jnp.int32, (n_cols, LANE), 0)

    def body(k, rank):
        rk = imp_sc[pl.ds(k, 1), :]
        ahead = (rk > imp_t) | ((rk == imp_t) & (jrow > k))
        return rank + jnp.where(ahead, 1.0, 0.0)

    rank = lax.fori_loop(0, n_blocks, body, jnp.zeros((n_cols, LANE), F32))
    jf = jrow.astype(F32)
    rows = [jnp.sum(jnp.where(rank == float(r), jf, 0.0), axis=0, keepdims=True) for r in range(SLC_TOP)]
    idx_ref[...] = jnp.concatenate(rows, axis=0).astype(jnp.int32)


def _topk_sample(psum, ov, n_blocks, cur):
    n_rows, n_keys = psum.shape
    n_cols = ov.shape[1]
    return pl.pallas_call(
        functools.partial(_topk_kernel, n_blocks=n_blocks, cur=cur),
        out_shape=jax.ShapeDtypeStruct((SLC_TOP, LANE), jnp.int32),
        grid=(1,),
        in_specs=[pl.BlockSpec((n_rows, n_keys), lambda i: (0, 0)),
                  pl.BlockSpec(ov.shape, lambda i: (0, 0))],
        out_specs=pl.BlockSpec((SLC_TOP, LANE), lambda i: (0, 0)),
        scratch_shapes=[pltpu.VMEM((n_cols, LANE), F32)],
        name="nsa_topk_sample",
    )(psum, ov)


def _slc1_kernel(phys_ref, jsel_ref, q_ref, *refs, cur, past):
    pages = refs[:SLC_TOP]
    new_ref, bb_ref, o_ref = refs[SLC_TOP:]
    b = pl.program_id(0)
    g = pl.program_id(1)
    goff = pl.multiple_of(g * HEAD_DIM, HEAD_DIM)
    q = q_ref[0]
    lane = lax.broadcasted_iota(jnp.int32, (NSA_REP, PAGE_SIZE), 1)
    first = lax.broadcasted_iota(jnp.int32, (HEAD_DIM, PAGE_SIZE), 1) == 0
    new_k = jnp.where(first, new_ref[0, pl.ds(goff, HEAD_DIM), :], 0.0)
    new_v = jnp.where(first, new_ref[0, pl.ds(LANE + goff, HEAD_DIM), :], 0.0)
    scores, values = [], []
    for n in range(SLC_TOP):
        j = jsel_ref[(b * NSA_KV_HEADS + g) * SLC_TOP + n]
        kt = jnp.where(j == cur, new_k, pages[n][0, pl.ds(goff, HEAD_DIM), :])
        vt = jnp.where(j == cur, new_v, pages[n][0, pl.ds(LANE + goff, HEAD_DIM), :])
        s = jnp.dot(q, kt.astype(BF16), preferred_element_type=F32)
        ok = (lane // SLC_BLOCK == j % 2) & ((j // 2) * PAGE_SIZE + lane <= past)
        bias = bb_ref[jnp.clip(j - (cur - 3), 0, 3), pl.ds(pl.multiple_of(g * NSA_REP, NSA_REP), NSA_REP), :]
        scores.append(jnp.where(ok, s + bias, NEG))
        values.append(vt.astype(BF16))
    s_all = jnp.concatenate(scores, axis=1)
    p = jnp.exp(s_all - jnp.max(s_all, axis=-1, keepdims=True))
    acc = jnp.zeros((NSA_REP, HEAD_DIM), F32)
    for n in range(SLC_TOP):
        acc = acc + _dot_nt(p[:, n * PAGE_SIZE:(n + 1) * PAGE_SIZE].astype(BF16), values[n])
    o_ref[0] = acc / jnp.sum(p, axis=-1, keepdims=True)


def _slc_sample(q, cache_pages_t, phys, jsel, new_cols, biasblk, cur, past):
    nb = q.shape[0]
    idx = lambda b, g, n: (b * NSA_KV_HEADS + g) * SLC_TOP + n
    page_specs = [pl.BlockSpec((1, KV_COLS, PAGE_SIZE), lambda b, g, ph, js, n=n: (ph[idx(b, g, n)], 0, 0))
                  for n in range(SLC_TOP)]
    return pl.pallas_call(
        functools.partial(_slc1_kernel, cur=cur, past=past),
        out_shape=jax.ShapeDtypeStruct((nb, NSA_HEADS, HEAD_DIM), F32),
        grid_spec=pltpu.PrefetchScalarGridSpec(
            num_scalar_prefetch=2,
            grid=(nb, NSA_KV_HEADS),
            in_specs=[pl.BlockSpec((1, NSA_REP, HEAD_DIM), lambda b, g, ph, js: (b, g, 0))] + page_specs
            + [pl.BlockSpec((1, KV_COLS, 1), lambda b, g, ph, js: (b, 0, 0)),
               pl.BlockSpec(biasblk.shape, lambda b, g, ph, js: (0, 0, 0))],
            out_specs=pl.BlockSpec((1, NSA_REP, HEAD_DIM), lambda b, g, ph, js: (b, g, 0))),
        name="nsa_slc_sample",
    )(phys, jsel, q, *([cache_pages_t] * SLC_TOP), new_cols, biasblk)


def _out_kernel(x_ref, gate_ref, yssd_ref, zs_ref, oc_ref, os_ref, ow_ref, gl_ref, za_ref,
                nw1_ref, nw2_ref, w_ref, eg_ref, fg_ref, o_ref, *, per_row, final):
    gates = _sigmoid(gl_ref[...])
    y_nsa = (_dot32(gates, eg_ref[0]) * oc_ref[...] + _dot32(gates, eg_ref[1]) * os_ref[...]
             + _dot32(gates, eg_ref[2]) * ow_ref[...])

    def gated_norm(y, z, w):
        u = y * _silu(z)
        half = u.shape[1] // 2
        parts = []
        for g in range(2):
            ug = u[:, g * half:(g + 1) * half]
            parts.append(ug * lax.rsqrt(jnp.mean(ug * ug, axis=-1, keepdims=True) + NORM_EPS))
        return (jnp.concatenate(parts, axis=1) * w).astype(BF16)

    m1 = gated_norm(yssd_ref[...], zs_ref[...], nw1_ref[...])
    m2 = gated_norm(y_nsa, za_ref[...], nw2_ref[...])
    proj = (jnp.dot(m1, w_ref[0:SSD_WIDTH, :], preferred_element_type=F32)
            + jnp.dot(m2, w_ref[SSD_WIDTH:, :], preferred_element_type=F32))
    gate = gate_ref[...] if per_row else gate_ref[0]
    out = x_ref[...] + gate * proj
    if final:
        out = out * lax.rsqrt(jnp.mean(out * out, axis=-1, keepdims=True) + NORM_EPS) * fg_ref[...]
    o_ref[...] = out


def _gate_expand():
    e = np.zeros((3, LANE, NSA_WIDTH), np.float32)
    for br in range(3):
        for h in range(NSA_HEADS):
            e[br, br * NSA_HEADS + h, h * HEAD_DIM:(h + 1) * HEAD_DIM] = 1.0
    return jnp.asarray(e)


def _layer_out(x2d, gate, y_ssd, z_s, o_cmp, o_slc, o_win, gl, z_a, nw1, nw2, w_out_bf, final_g,
               rows_per_batch, final):
    m = x2d.shape[0]
    per_row = rows_per_batch == 1
    tm = m if per_row else 256
    if per_row:
        gate_spec = pl.BlockSpec((tm, D_MODEL), lambda i: (0, 0))
        gt = gate
    else:
        gate_spec = pl.BlockSpec((1, 1, D_MODEL), lambda i: ((i * tm) // rows_per_batch, 0, 0))
        gt = gate[:, None, :]
    row = lambda w: pl.BlockSpec((tm, w), lambda i: (i, 0))
    full = lambda shape: pl.BlockSpec(shape, lambda i: (0,) * len(shape))
    eg = _gate_expand()
    return pl.pallas_call(
        functools.partial(_out_kernel, per_row=per_row, final=final),
        out_shape=jax.ShapeDtypeStruct((m, D_MODEL), F32),
        grid=(m // tm,),
        in_specs=[row(D_MODEL), gate_spec, row(SSD_WIDTH), row(SSD_WIDTH), row(NSA_WIDTH), row(NSA_WIDTH),
                  row(NSA_WIDTH), row(LANE), row(NSA_WIDTH), full((1, SSD_WIDTH)), full((1, NSA_WIDTH)),
                  full(w_out_bf.shape), full(eg.shape), full((1, D_MODEL))],
        out_specs=row(D_MODEL),
        compiler_params=pltpu.CompilerParams(vmem_limit_bytes=VMEM_LIMIT),
        name="layer_out",
    )(x2d, gt, y_ssd, z_s, o_cmp, o_slc, o_win, gl, z_a, nw1.reshape(1, SSD_WIDTH), nw2.reshape(1, NSA_WIDTH),
      w_out_bf, eg, final_g.reshape(1, D_MODEL))


def _pad_in_weights(w_in):
    cols = []
    off = 0
    for size, width in zip(_SEG_SIZES, _SEG_PAD):
        seg = w_in[:, off:off + size]
        cols.append(jnp.pad(seg, ((0, 0), (0, width - size))))
        off += size
    return jnp.concatenate(cols, axis=1).astype(BF16)


def _front_pad_bf16(kv, rows):
    return jnp.pad(kv, ((0, 0), (rows, 0), (0, 0))).astype(BF16)


def kernel(x_prompt, x_sample, cache_cmp_kv, cache_slc_kv, state_win_kv, state_conv, state_ssm, page_table,
           c_prompt, c_sample, norm_g, ada_w, ada_b, w_in, conv_w, conv_b, dt_bias, a_log, d_skip,
           ssd_norm_w, cmp_pe, cmp_w1, cmp_w2, nsa_norm_w, w_out, rel_bias, final_norm_g):
    nbp, lp, _ = x_prompt.shape
    nbs = x_sample.shape[0]
    depth = w_in.shape[0]
    n_pool = cache_cmp_kv.shape[1]
    n_pages = page_table.shape[1]
    past = n_pages * PAGE_SIZE
    w_buf = state_win_kv.shape[2]
    kv_row = (2, NSA_KV_HEADS, HEAD_DIM)

    n_ck = lp // CMP_STRIDE
    n_sb = lp // SLC_BLOCK
    qi = np.arange(Q_BLOCK)[:, None]
    c0 = n_ck - 4
    pattern = _bias_of_dist(rel_bias, qi - CMP_STRIDE * (np.arange(n_ck)[None, :] - c0) - (CMP_STRIDE - 1))
    far = rel_bias.astype(F32)[REL_BUCKETS - 1][:, None, None]
    par = np.arange(2)[:, None, None]
    dist_n = (np.arange(Q_BLOCK)[None, None, :] - np.arange(SLC_NEARW)[None, :, None] + Q_BLOCK * par
              + (SLC_NEAR - Q_BLOCK))
    tnt = (_bias_of_dist(rel_bias, dist_n) - far[..., None]).reshape(NSA_KV_HEADS, NSA_REP, 2, SLC_NEARW, Q_BLOCK)
    tnt = jnp.transpose(tnt, (2, 0, 3, 1, 4)).reshape(2, NSA_KV_HEADS, SLC_NEARW, ROWS) * LOG2E
    tile = _group_tile()
    key_row = np.arange(SLC_PAD + lp + SLC_BACK) - SLC_PAD
    blk_onehot = jnp.asarray((key_row[:, None] >= 0)
                             & (key_row[:, None] // SLC_BLOCK == np.arange(LANE)[None, :]), BF16)
    tw = _bias_of_dist(rel_bias, qi - np.arange(WIN_W)[None, :] + WIN_PAD)
    ov_p = _overlap_matrix(n_ck, n_sb, LANE)

    n_cs = past // CMP_STRIDE
    cur = past // SLC_BLOCK
    n_blk_s = cur + 1
    n_cols_s = -(-n_blk_s // LANE) * LANE
    ov_s = _overlap_matrix(n_cs, n_blk_s, n_cols_s)
    m_s = np.arange(n_cs)
    bm_c = jnp.where(jnp.asarray(m_s >= 1)[None, :],
                     _bias_of_dist(rel_bias, past - (CMP_STRIDE * m_s + CMP_STRIDE - 1)), NEG)
    n_w = -(-(w_buf + 1) // LANE) * LANE
    iw = np.arange(n_w)
    dw = w_buf - iw
    ok_w = (iw <= w_buf) & (dw >= 0) & (dw < WINDOW) & (past - w_buf + iw >= 0)
    bm_w = jnp.where(jnp.asarray(ok_w)[None, :], _bias_of_dist(rel_bias, dw), NEG)
    jb = (cur - 3 + np.arange(4))[:, None]
    biasblk = _bias_of_dist(rel_bias, past - SLC_BLOCK * jb - np.arange(SLC_BLOCK)[None, :])
    biasblk = jnp.transpose(biasblk, (1, 0, 2))
    biasblk = jnp.concatenate([biasblk, biasblk], axis=-1)

    pages_t = lambda c: jnp.transpose(c, (0, 1, 3, 4, 5, 2)).reshape(depth * n_pool, KV_COLS, PAGE_SIZE)
    cmp_pages_t = pages_t(cache_cmp_kv)
    slc_pages_t = pages_t(cache_slc_kv)
    prompt_pages = jnp.arange(nbp * (lp // PAGE_SIZE), dtype=jnp.int32).reshape(nbp, lp // PAGE_SIZE)

    c_all = jnp.concatenate([c_prompt, c_sample], axis=0)
    xp = x_prompt.reshape(nbp * lp, D_MODEL)
    xs = x_sample.reshape(nbs, D_MODEL)
    outs = {k: [] for k in ("pc", "ps", "pw", "pconv", "pssm", "sc", "ss", "sw", "sconv", "sssm")}

    for l in range(depth):
        final = l == depth - 1
        w_pad = _pad_in_weights(w_in[l])
        w_out_bf = w_out[l].astype(BF16)
        cweights = _compress_weights(cmp_pe[l], cmp_w1[l], cmp_w2[l])
        mod = _modulation(c_all, ada_w[l], ada_b[l])
        shift, scale, gate = mod[:, :D_MODEL], mod[:, D_MODEL:2 * D_MODEL], mod[:, 2 * D_MODEL:]

        z_s, xbc, dt, q, kvc, kvs, kvw, gl, z_a = _in_projection(xp, norm_g[l], scale[:nbp], shift[:nbp], w_pad, lp)
        xbc3 = xbc.reshape(nbp, lp, CONV_DIM)
        y_ssd, h_fin = _ssd_prompt(xbc3, dt.reshape(nbp, lp, LANE), conv_w[l], conv_b[l], dt_bias[l], a_log[l],
                                   d_skip[l])
        q3 = q.reshape(nbp, lp, NSA_WIDTH)
        kvc3, kvs3, kvw3 = (t.reshape(nbp, lp, KV_COLS) for t in (kvc, kvs, kvw))
        kc = _compress(kvc.reshape(nbp * (lp // PAGE_SIZE), PAGE_SIZE, KV_COLS), prompt_pages, cweights, False)
        o_cmp, sel = _cmp_prompt(q3, kc, pattern, ov_p)
        ks_pad = jnp.pad(kvs3, ((0, 0), (SLC_PAD, SLC_BACK), (0, 0))).astype(BF16)
        kx_pad = jnp.concatenate([ks_pad[:, :, :LANE], jnp.broadcast_to(blk_onehot, (nbp,) + blk_onehot.shape)],
                                 axis=-1)
        vt = jnp.transpose(ks_pad[:, :, LANE:].reshape(nbp, -1, NSA_KV_HEADS, HEAD_DIM), (0, 2, 3, 1))
        vt_pad = jnp.concatenate([vt, jnp.ones_like(vt)], axis=2)
        o_slc = _slc_prompt(q3, sel, kx_pad, vt_pad, tnt, tile)
        kw_pad = _front_pad_bf16(kvw3, WIN_PAD)
        o_win = _win_prompt(q3, kw_pad[:, :, :LANE], kw_pad[:, :, LANE:], tw)
        xp = _layer_out(xp, gate[:nbp], y_ssd.reshape(nbp * lp, SSD_WIDTH), z_s,
                        o_cmp.reshape(nbp * lp, NSA_WIDTH), o_slc.reshape(nbp * lp, NSA_WIDTH),
                        o_win.reshape(nbp * lp, NSA_WIDTH), gl, z_a, ssd_norm_w[l], nsa_norm_w[l], w_out_bf,
                        final_norm_g, lp, final)
        outs["pc"].append(kvc3.reshape((nbp, lp) + kv_row))
        outs["ps"].append(kvs3.reshape((nbp, lp) + kv_row))
        outs["pw"].append(kvw3[:, -min(WINDOW, lp):].reshape((nbp, min(WINDOW, lp)) + kv_row))
        outs["pconv"].append(xbc3[:, -(SSD_CONV - 1):])
        outs["pssm"].append(h_fin)

        z_s2, xbc2, dt2, q2, kvc2, kvs2, kvw2, gl2, z_a2 = _in_projection(
            xs, norm_g[l], scale[nbp:], shift[nbp:], w_pad, 1)
        y_ssd2, h2 = _ssd_step(xbc2, state_conv[l], dt2, state_ssm[l], conv_w[l], conv_b[l], dt_bias[l],
                               a_log[l], d_skip[l])
        qh = q2.reshape(nbs, NSA_KV_HEADS, NSA_REP, HEAD_DIM)
        zq = jnp.zeros((nbs, NSA_REP, HEAD_DIM), F32)
        q_pad = jnp.stack([jnp.concatenate([qh[:, 0], zq], axis=-1), jnp.concatenate([zq, qh[:, 1]], axis=-1)],
                          axis=1).reshape(nbs, NSA_HEADS, LANE).astype(BF16)
        kc2 = _compress(cmp_pages_t, page_table + l * n_pool, cweights, True)
        oc2, psum = _dense1(q_pad, kc2, bm_c)
        sel_idx = _topk_sample(psum.reshape(nbs * NSA_KV_HEADS, n_cs), ov_s, n_blk_s, cur)
        jsel = sel_idx[:, :nbs * NSA_KV_HEADS].T.reshape(nbs, NSA_KV_HEADS, SLC_TOP)
        jc = jnp.minimum(jsel, cur - 1)
        page = jnp.take_along_axis(page_table, (jc // 2).reshape(nbs, -1), axis=1).reshape(jsel.shape)
        phys = jnp.where(jsel < cur, page + l * n_pool, 0).astype(jnp.int32)
        os2 = _slc_sample(q2.reshape(nbs, NSA_HEADS, HEAD_DIM).astype(BF16), slc_pages_t, phys.reshape(-1),
                          jsel.reshape(-1), kvs2.reshape(nbs, KV_COLS, 1), biasblk, cur, past)
        kw_full = jnp.concatenate([state_win_kv[l].reshape(nbs, w_buf, KV_COLS), kvw2[:, None, :]], axis=1)
        kw_in = jnp.pad(kw_full, ((0, 0), (0, n_w - (w_buf + 1)), (0, 0)))
        ow2, _ = _dense1(q_pad, kw_in, bm_w)
        xs = _layer_out(xs, gate[nbp:], y_ssd2, z_s2, _take_group_half(oc2), os2.reshape(nbs, NSA_WIDTH),
                        _take_group_half(ow2), gl2, z_a2, ssd_norm_w[l], nsa_norm_w[l], w_out_bf,
                        final_norm_g, 1, final)
        outs["sc"].append(kvc2.reshape((nbs, 1) + kv_row))
        outs["ss"].append(kvs2.reshape((nbs, 1) + kv_row))
        outs["sw"].append(kw_full[:, -w_buf:].reshape((nbs, w_buf) + kv_row))
        outs["sconv"].append(jnp.concatenate([state_conv[l][:, 1:], xbc2[:, None, :]], axis=1))
        outs["sssm"].append(h2)

    st = lambda k: jnp.stack(outs[k])
    return (xp.reshape(nbp, lp, D_MODEL), xs.reshape(nbs, 1, D_MODEL),
            st("pc"), st("ps"), st("pw"), st("pconv"), st("pssm"),
            st("sc"), st("ss"), st("sw"), st("sconv"), st("sssm"))
```

```python
import functools
import math

import numpy as np
import jax
import jax.numpy as jnp
from jax import lax
from jax.experimental import pallas as pl
from jax.experimental.pallas import tpu as pltpu

F32 = jnp.float32
BF16 = jnp.bfloat16
HIGHEST = lax.Precision.HIGHEST

D_MODEL = 1024
HEAD_DIM = 64
SSD_WIDTH = 1024
SSD_HEADS = 16
SSD_GROUPS = 2
SSD_STATE = 128
SSD_CONV = 4
SSD_CHUNK = 256
CONV_DIM = SSD_WIDTH + 2 * SSD_GROUPS * SSD_STATE
NSA_WIDTH = 1024
NSA_HEADS = 16
NSA_KV_HEADS = 2
NSA_REP = NSA_HEADS // NSA_KV_HEADS
CMP_BLOCK = 32
CMP_STRIDE = 16
CMP_HID = 2 * HEAD_DIM
SLC_BLOCK = 64
SLC_TOP = 16
WINDOW = 512
Q_BLOCK = 64
REL_BUCKETS = 32
REL_MAX_DIST = 128
NORM_EPS = 1e-6
KV_COLS = 2 * NSA_KV_HEADS * HEAD_DIM
PAGE_SIZE = 128
NEG = -1e30
LOG2E = 1.4426950408889634

LANE = 128
HALF = LANE // 2
GROUP_W = NSA_REP * HEAD_DIM
ROWS = NSA_REP * Q_BLOCK
SLC_NEAR = 4 * SLC_BLOCK
SLC_PAD = SLC_NEAR
SLC_NEARW = SLC_NEAR + LANE
SLC_BACK = 2 * SLC_NEAR
WIN_W = WINDOW + 2 * Q_BLOCK
WIN_PAD = WIN_W - Q_BLOCK
CMP_PAGES_MAX = 32
VMEM_LIMIT = 48 * 1024 * 1024

_SEG_NAMES = ("z_s", "xbc", "dt", "q", "kvc", "kvs", "kvw", "gl", "z_a")
_SEG_SIZES = (SSD_WIDTH, CONV_DIM, SSD_HEADS, NSA_WIDTH, KV_COLS, KV_COLS, KV_COLS, 3 * NSA_HEADS, NSA_WIDTH)
_SEG_PAD = tuple(-(-s // LANE) * LANE for s in _SEG_SIZES)
_SEG_OFF = tuple(int(o) for o in np.cumsum((0,) + _SEG_PAD[:-1]))
IN_PAD = int(sum(_SEG_PAD))


def _sigmoid(x):
    return 1.0 / (1.0 + jnp.exp(-x))


def _silu(x):
    return x * _sigmoid(x)


def _dot32(a, b):
    return jnp.dot(a, b, precision=HIGHEST, preferred_element_type=F32)


def _split_bf16(x, terms):
    parts = []
    for _ in range(terms):
        p = x.astype(BF16)
        parts.append(p)
        x = x - p.astype(F32)
    return parts


def _dot_sel(x, sel_bf16, terms):
    return sum(jnp.dot(p, sel_bf16, preferred_element_type=F32) for p in _split_bf16(x, terms))


def _sel_dot(sel_bf16, x, terms):
    return sum(jnp.dot(sel_bf16, p, preferred_element_type=F32) for p in _split_bf16(x, terms))


def _dot_nt(a, b):
    return lax.dot_general(a, b, (((1,), (1,)), ((), ())), preferred_element_type=F32)


def _bucket_table():
    n = np.arange(REL_MAX_DIST + 1)
    max_exact = REL_BUCKETS // 2
    nf = np.maximum(n, 1).astype(np.float32)
    large = max_exact + (np.log(nf / np.float32(max_exact)) / np.float32(math.log(REL_MAX_DIST / max_exact))
                         * np.float32(REL_BUCKETS - max_exact)).astype(np.int32)
    large = np.minimum(large, REL_BUCKETS - 1)
    return np.where(n < max_exact, n, large).astype(np.int32)


_BUCKETS = _bucket_table()


def _bias_of_dist(rel_bias, dist):
    idx = _BUCKETS[np.clip(dist, 0, REL_MAX_DIST)]
    out = jnp.take(rel_bias.astype(F32), jnp.asarray(idx.reshape(-1)), axis=0)
    return out.T.reshape((NSA_HEADS,) + dist.shape)


def _mod_kernel(c_ref, w_ref, b_ref, o_ref):
    o_ref[...] = _dot32(_silu(c_ref[...]), w_ref[...]) + b_ref[...]


def _modulation(c, w, b):
    m, d = c.shape
    n = w.shape[1]
    tn = 512
    return pl.pallas_call(
        _mod_kernel,
        out_shape=jax.ShapeDtypeStruct((m, n), F32),
        grid=(n // tn,),
        in_specs=[pl.BlockSpec((m, d), lambda j: (0, 0)),
                  pl.BlockSpec((d, tn), lambda j: (0, j)),
                  pl.BlockSpec((1, tn), lambda j: (0, j))],
        out_specs=pl.BlockSpec((m, tn), lambda j: (0, j)),
        name="adaln_mod",
    )(c, w, b.reshape(1, n))


def _inproj_kernel(x_ref, g_ref, sc_ref, sh_ref, w_ref, *out_refs, per_row):
    x = x_ref[...]
    xn = x * lax.rsqrt(jnp.mean(x * x, axis=-1, keepdims=True) + NORM_EPS)
    sc = sc_ref[...] if per_row else sc_ref[0]
    sh = sh_ref[...] if per_row else sh_ref[0]
    h = ((xn * g_ref[...]) * (1.0 + sc) + sh).astype(BF16)
    for name, off, width, ref in zip(_SEG_NAMES, _SEG_OFF, _SEG_PAD, out_refs):
        r = jnp.dot(h, w_ref[:, off:off + width], preferred_element_type=F32)
        if name == "q":
            r = r * (HEAD_DIM ** -0.5)
        ref[...] = r


def _in_projection(x2d, g, scale, shift, w_pad, rows_per_batch):
    m = x2d.shape[0]
    per_row = rows_per_batch == 1
    tm = m if per_row else 256
    if per_row:
        mod_spec = pl.BlockSpec((tm, D_MODEL), lambda i: (0, 0))
        sc, sh = scale, shift
    else:
        mod_spec = pl.BlockSpec((1, 1, D_MODEL), lambda i: ((i * tm) // rows_per_batch, 0, 0))
        sc, sh = scale[:, None, :], shift[:, None, :]
    outs = tuple(jax.ShapeDtypeStruct((m, w), F32) for w in _SEG_PAD)
    return pl.pallas_call(
        functools.partial(_inproj_kernel, per_row=per_row),
        out_shape=outs,
        grid=(m // tm,),
        in_specs=[pl.BlockSpec((tm, D_MODEL), lambda i: (i, 0)),
                  pl.BlockSpec((1, D_MODEL), lambda i: (0, 0)),
                  mod_spec, mod_spec,
                  pl.BlockSpec((D_MODEL, IN_PAD), lambda i: (0, 0))],
        out_specs=tuple(pl.BlockSpec((tm, w), lambda i: (i, 0)) for w in _SEG_PAD),
        compiler_params=pltpu.CompilerParams(vmem_limit_bytes=VMEM_LIMIT),
        name="in_projection",
    )(x2d, g.reshape(1, D_MODEL), sc, sh, w_pad)


def _softplus(x):
    return jnp.maximum(x, 0.0) + jnp.log(1.0 + jnp.exp(-jnp.abs(x)))


def _ssd_kernel(xbc_ref, dt_ref, cw_ref, cb_ref, dtb_ref, alog_ref, dsk_ref, e_ref, tril_ref,
                y_ref, hfin_ref, xe_sc, st_sc):
    c = pl.program_id(1)
    q = SSD_CHUNK
    n_pairs = SSD_HEADS // 2

    @pl.when(c == 0)
    def _():
        xe_sc[0:8, :] = jnp.zeros((8, CONV_DIM), F32)
        st_sc[...] = jnp.zeros(st_sc.shape, F32)

    xe_sc[8:8 + q, :] = xbc_ref[0]
    acc = cb_ref[...] + cw_ref[0:1, :] * xe_sc[5:5 + q, :]
    for k in range(1, SSD_CONV):
        acc = acc + cw_ref[k:k + 1, :] * xe_sc[5 + k:5 + k + q, :]
    u = _silu(acc)
    xe_sc[0:8, :] = xe_sc[q:q + 8, :]

    xs = u[:, :SSD_WIDTH]
    gn = SSD_GROUPS * SSD_STATE
    bm = u[:, SSD_WIDTH:SSD_WIDTH + gn]
    cm = u[:, SSD_WIDTH + gn:]

    dt = _softplus(dt_ref[0] + dtb_ref[...])
    a = dt * (-jnp.exp(alog_ref[...]))
    cs = _sel_dot(tril_ref[...], a, 3)
    cs_t = cs.T
    cs_last = cs[q - 1:q, :]
    e = e_ref[...]
    dt_e = _dot_sel(dt, e, 3)
    w_e = _dot_sel(dt * jnp.exp(cs_last - cs), e, 3)
    ecs_e = _dot_sel(jnp.exp(cs), e, 3)
    tot_e = _dot_sel(jnp.broadcast_to(jnp.exp(cs_last), (8, LANE)), e, 3)[0:1, :]
    xdt = (xs * dt_e).astype(BF16)
    xw = (xs * w_e).astype(BF16)

    li = lax.broadcasted_iota(jnp.int32, (q, q), 0)
    si = lax.broadcasted_iota(jnp.int32, (q, q), 1)
    tri = li >= si
    lane = lax.broadcasted_iota(jnp.int32, (q, LANE), 1)

    for g in range(SSD_GROUPS):
        cg = cm[:, g * SSD_STATE:(g + 1) * SSD_STATE].astype(BF16)
        bg = bm[:, g * SSD_STATE:(g + 1) * SSD_STATE]
        cb = _dot_nt(cg, bg.astype(BF16))
        bg_t = bg.T.astype(BF16)
        for jp in range(n_pairs // SSD_GROUPS):
            j = g * (n_pairs // SSD_GROUPS) + jp
            sl = slice(j * LANE, (j + 1) * LANE)
            xdt_p = xdt[:, sl]
            ys = []
            for hh in (2 * j, 2 * j + 1):
                diff = cs[:, hh:hh + 1] - cs_t[hh:hh + 1, :]
                lmat = jnp.exp(jnp.where(tri, diff, NEG))
                ys.append(jnp.dot((cb * lmat).astype(BF16), xdt_p, preferred_element_type=F32))
            y_diag = jnp.where(lane < HALF, ys[0], ys[1])
            st = st_sc[j]
            y_off = jnp.dot(cg, st.astype(BF16), preferred_element_type=F32) * ecs_e[:, sl]
            y_ref[0, :, sl] = y_diag + y_off + xs[:, sl] * dsk_ref[:, sl]
            new = jnp.dot(bg_t, xw[:, sl], preferred_element_type=F32)
            st_sc[j] = st * tot_e[:, sl] + new

    @pl.when(c == pl.num_programs(1) - 1)
    def _():
        for j in range(n_pairs):
            hfin_ref[0, j * LANE:(j + 1) * LANE, :] = st_sc[j].T


def _head_expand():
    e = np.zeros((LANE, SSD_WIDTH), np.float32)
    for h in range(SSD_HEADS):
        e[h, h * HEAD_DIM:(h + 1) * HEAD_DIM] = 1.0
    return jnp.asarray(e)


def _pad_lanes(v):
    return jnp.pad(v.astype(F32), (0, LANE - v.shape[0])).reshape(1, LANE)


def _ssd_prompt(xbc, dt, conv_w, conv_b, dt_bias, a_log, d_skip):
    b, l, _ = xbc.shape
    nc = l // SSD_CHUNK
    full = lambda shape: pl.BlockSpec(shape, lambda i, c: (0,) * len(shape))
    y, hfin = pl.pallas_call(
        _ssd_kernel,
        out_shape=(jax.ShapeDtypeStruct((b, l, SSD_WIDTH), F32),
                   jax.ShapeDtypeStruct((b, SSD_HEADS * HEAD_DIM, SSD_STATE), F32)),
        grid=(b, nc),
        in_specs=[pl.BlockSpec((1, SSD_CHUNK, CONV_DIM), lambda i, c: (i, c, 0)),
                  pl.BlockSpec((1, SSD_CHUNK, LANE), lambda i, c: (i, c, 0)),
                  full((SSD_CONV, CONV_DIM)), full((1, CONV_DIM)), full((1, LANE)), full((1, LANE)),
                  full((1, SSD_WIDTH)), full((LANE, SSD_WIDTH)), full((SSD_CHUNK, SSD_CHUNK))],
        out_specs=(pl.BlockSpec((1, SSD_CHUNK, SSD_WIDTH), lambda i, c: (i, c, 0)),
                   pl.BlockSpec((1, SSD_HEADS * HEAD_DIM, SSD_STATE), lambda i, c: (i, 0, 0))),
        scratch_shapes=[pltpu.VMEM((SSD_CHUNK + 8, CONV_DIM), F32),
                        pltpu.VMEM((SSD_HEADS // 2, SSD_STATE, LANE), F32)],
        compiler_params=pltpu.CompilerParams(dimension_semantics=("arbitrary", "arbitrary"),
                                             vmem_limit_bytes=VMEM_LIMIT),
        name="ssd_prompt",
    )(xbc, dt, conv_w, conv_b.reshape(1, CONV_DIM), _pad_lanes(dt_bias), _pad_lanes(a_log),
      jnp.repeat(d_skip.astype(F32), HEAD_DIM).reshape(1, SSD_WIDTH), _head_expand().astype(BF16),
      jnp.asarray(np.tril(np.ones((SSD_CHUNK, SSD_CHUNK), np.float32)), BF16))
    return y, hfin.reshape(b, SSD_HEADS, HEAD_DIM, SSD_STATE)


def _ssd_step_kernel(xbc_ref, c0_ref, c1_ref, c2_ref, dt_ref, h0_ref, cw_ref, cb_ref, dtb_ref, alog_ref,
                     dsk_ref, e_ref, y_ref, hout_ref, xt_sc, dect_sc, bc_sc, yt_sc, xs_sc):
    b = pl.program_id(0)
    nb = xbc_ref.shape[0]
    rows = SSD_HEADS * HEAD_DIM
    gn = SSD_GROUPS * SSD_STATE

    @pl.when(b == 0)
    def _():
        acc = (cb_ref[...] + cw_ref[0:1, :] * c0_ref[...] + cw_ref[1:2, :] * c1_ref[...]
               + cw_ref[2:3, :] * c2_ref[...] + cw_ref[3:4, :] * xbc_ref[...])
        u = _silu(acc)
        xs = u[:, :SSD_WIDTH]
        dt = _softplus(dt_ref[...] + dtb_ref[...])
        dec = jnp.exp(dt * (-jnp.exp(alog_ref[...])))
        e = e_ref[...]
        xdt = xs * _dot32(dt, e)
        dec_e = _dot32(dec, e)
        pad = jnp.zeros((LANE - nb, SSD_WIDTH), F32)
        xt_sc[...] = jnp.concatenate([xdt, pad], axis=0).T
        dect_sc[...] = jnp.concatenate([dec_e, pad], axis=0).T
        bc_sc[...] = u[:, SSD_WIDTH:]
        xs_sc[...] = xs
        yt_sc[...] = jnp.zeros(yt_sc.shape, F32)

    ri = lax.broadcasted_iota(jnp.int32, (LANE, LANE), 0)
    onehot = jnp.where(ri == b, 1.0, 0.0)
    xcol = _dot32(xt_sc[...], onehot)
    dcol = _dot32(dect_sc[...], onehot)
    bc = bc_sc[pl.ds(b, 1), :]
    row = lax.broadcasted_iota(jnp.int32, (rows, SSD_STATE), 0)
    first = row < rows // SSD_GROUPS
    b_full = jnp.where(first, bc[:, 0:SSD_STATE], bc[:, SSD_STATE:gn])
    c_full = jnp.where(first, bc[:, gn:gn + SSD_STATE], bc[:, gn + SSD_STATE:])
    new = dcol * h0_ref[0] + xcol * b_full
    hout_ref[0] = new
    ycol = _dot32(new * c_full, jnp.ones((SSD_STATE, LANE), F32))
    lane = lax.broadcasted_iota(jnp.int32, (rows, LANE), 1)
    yt_sc[...] = jnp.where(lane == b, ycol, yt_sc[...])

    @pl.when(b == nb - 1)
    def _():
        y_ref[...] = yt_sc[...].T[0:nb, :] + xs_sc[...] * dsk_ref[...]


def _ssd_step(xbc, conv_state, dt, h0, conv_w, conv_b, dt_bias, a_log, d_skip):
    nb = xbc.shape[0]
    rows = SSD_HEADS * HEAD_DIM
    full = lambda shape: pl.BlockSpec(shape, lambda i: (0,) * len(shape))
    y, hout = pl.pallas_call(
        _ssd_step_kernel,
        out_shape=(jax.ShapeDtypeStruct((nb, SSD_WIDTH), F32),
                   jax.ShapeDtypeStruct((nb, rows, SSD_STATE), F32)),
        grid=(nb,),
        in_specs=[full((nb, CONV_DIM)), full((nb, CONV_DIM)), full((nb, CONV_DIM)), full((nb, CONV_DIM)),
                  full((nb, LANE)),
                  pl.BlockSpec((1, rows, SSD_STATE), lambda i: (i, 0, 0)),
                  full((SSD_CONV, CONV_DIM)), full((1, CONV_DIM)), full((1, LANE)), full((1, LANE)),
                  full((1, SSD_WIDTH)), full((LANE, SSD_WIDTH))],
        out_specs=(full((nb, SSD_WIDTH)),
                   pl.BlockSpec((1, rows, SSD_STATE), lambda i: (i, 0, 0))),
        scratch_shapes=[pltpu.VMEM((rows, LANE), F32), pltpu.VMEM((rows, LANE), F32),
                        pltpu.VMEM((nb, 2 * SSD_GROUPS * SSD_STATE), F32),
                        pltpu.VMEM((rows, LANE), F32), pltpu.VMEM((nb, SSD_WIDTH), F32)],
        compiler_params=pltpu.CompilerParams(dimension_semantics=("arbitrary",)),
        name="ssd_step",
    )(xbc, conv_state[:, 0], conv_state[:, 1], conv_state[:, 2], dt, h0.reshape(nb, rows, SSD_STATE),
      conv_w, conv_b.reshape(1, CONV_DIM), _pad_lanes(dt_bias), _pad_lanes(a_log),
      jnp.repeat(d_skip.astype(F32), HEAD_DIM).reshape(1, SSD_WIDTH), _head_expand())
    return y, hout.reshape(nb, SSD_HEADS, HEAD_DIM, SSD_STATE)


def _compress_kernel(pt_ref, *refs, transposed, CMP_PAGES):
    if transposed:
        w1_ref, pe_ref, w2_ref, out_ref, sh_sc, pe_sc, xs_sc = refs[CMP_PAGES:]
        for i in range(CMP_PAGES):
            for k in range(2):
                xs_sc[i, k] = refs[i][0, k * LANE:(k + 1) * LANE, :].T
        pages = tuple(tuple(xs_sc.at[i, k] for i in range(CMP_PAGES)) for k in range(2))
    else:
        w1_ref, pe_ref, w2_ref, out_ref, sh_sc, pe_sc = refs[2 * CMP_PAGES:]
        pages = tuple(tuple(refs[k * CMP_PAGES + i].at[0] for i in range(CMP_PAGES)) for k in range(2))
    s = pl.program_id(1)
    segs = PAGE_SIZE // CMP_STRIDE
    rows = CMP_PAGES * segs
    hid2 = NSA_KV_HEADS * CMP_HID

    @pl.when(s == 0)
    def _():
        sh_sc[:, 0:8, :] = jnp.zeros((2, 8, hid2), F32)
        for k in range(2):
            t = jnp.zeros((8, 2 * hid2), F32)
            for o in range(CMP_STRIDE):
                t = t + jnp.dot(pe_ref[o, k].astype(BF16), w1_ref[o, k], preferred_element_type=F32)
            pe_sc[k] = jnp.broadcast_to(t[0:1, 0:hid2] + t[1:2, hid2:], (8, hid2))

    for k in range(2):
        acc = jnp.zeros((rows, 2 * hid2), F32)
        for o in range(CMP_STRIDE):
            xo = jnp.concatenate([p[pl.ds(o, segs, stride=CMP_STRIDE), :] for p in pages[k]], axis=0)
            acc = acc + jnp.dot(xo.astype(BF16), w1_ref[o, k], preferred_element_type=F32)
        sh_sc[k, 8:8 + rows, :] = acc[:, 0:hid2]
        pre = acc[:, hid2:] + sh_sc[k, 7:7 + rows, :] + pe_sc[k, 0:1, :]
        sh_sc[k, 0:8, :] = sh_sc[k, rows:rows + 8, :]
        out_ref[0, :, k * LANE:(k + 1) * LANE] = jnp.dot(_silu(pre).astype(BF16), w2_ref[k],
                                                         preferred_element_type=F32)


def _compress_weights(cmp_pe, cmp_w1, cmp_w2):
    span = CMP_BLOCK // CMP_STRIDE
    w1s = cmp_w1.astype(F32).reshape(2, span, CMP_STRIDE, HEAD_DIM, CMP_HID)
    z = jnp.zeros((2, span, CMP_STRIDE, HEAD_DIM, CMP_HID), F32)
    top = jnp.concatenate([w1s, z], axis=-1)
    bot = jnp.concatenate([z, w1s], axis=-1)
    bd = jnp.concatenate([top, bot], axis=-2)
    w1 = jnp.transpose(bd, (2, 0, 3, 1, 4)).reshape(CMP_STRIDE, 2, LANE, span * 2 * CMP_HID).astype(BF16)
    pe = cmp_pe.astype(F32).reshape(2, span, CMP_STRIDE, HEAD_DIM)
    pe = jnp.transpose(pe, (2, 0, 1, 3))
    pe = jnp.concatenate([pe, pe], axis=-1)
    pe = jnp.pad(pe, ((0, 0), (0, 0), (0, 8 - span), (0, 0)))
    w2 = cmp_w2.astype(F32)
    z2 = jnp.zeros_like(w2)
    w2bd = jnp.concatenate([jnp.concatenate([w2, z2], axis=-1), jnp.concatenate([z2, w2], axis=-1)],
                           axis=-2).astype(BF16)
    return w1, pe, w2bd


def _compress(pages_arr, page_ids, cweights, transposed):
    nb, n_pages = page_ids.shape
    CMP_PAGES = math.gcd(n_pages, CMP_PAGES_MAX)
    steps = n_pages // CMP_PAGES
    segs = PAGE_SIZE // CMP_STRIDE
    rows = CMP_PAGES * segs
    w1, pe, w2bd = cweights
    hid2 = NSA_KV_HEADS * CMP_HID
    page_of = lambda b, s, pt, i: pt[(b * steps + s) * CMP_PAGES + i]
    scratch = [pltpu.VMEM((2, rows + 8, hid2), F32), pltpu.VMEM((2, 8, hid2), F32)]
    if transposed:
        page_specs = [pl.BlockSpec((1, KV_COLS, PAGE_SIZE), lambda b, s, pt, i=i: (page_of(b, s, pt, i), 0, 0))
                      for i in range(CMP_PAGES)]
        scratch.append(pltpu.VMEM((CMP_PAGES, 2, PAGE_SIZE, LANE), F32))
    else:
        page_specs = [pl.BlockSpec((1, PAGE_SIZE, LANE), lambda b, s, pt, i=i, k=k: (page_of(b, s, pt, i), 0, k))
                      for k in range(2) for i in range(CMP_PAGES)]
    full = lambda shape: pl.BlockSpec(shape, lambda b, s, pt: (0,) * len(shape))
    return pl.pallas_call(
        functools.partial(_compress_kernel, transposed=transposed, CMP_PAGES=CMP_PAGES),
        out_shape=jax.ShapeDtypeStruct((nb, n_pages * segs, KV_COLS), F32),
        grid_spec=pltpu.PrefetchScalarGridSpec(
            num_scalar_prefetch=1,
            grid=(nb, steps),
            in_specs=page_specs + [full(w1.shape), full(pe.shape), full(w2bd.shape)],
            out_specs=pl.BlockSpec((1, rows, KV_COLS), lambda b, s, pt: (b, s, 0)),
            scratch_shapes=scratch),
        compiler_params=pltpu.CompilerParams(dimension_semantics=("arbitrary", "arbitrary"),
                                             vmem_limit_bytes=VMEM_LIMIT),
        name="nsa_compress",
    )(page_ids.reshape(-1), *([pages_arr] * len(page_specs)), w1, pe, w2bd)


def _overlap_matrix(n_rows, n_blocks, n_cols):
    m = np.arange(n_rows)[:, None]
    j = np.arange(n_cols)[None, :]
    cs = (m - 1) * CMP_STRIDE
    ov = (m >= 1) & (j < n_blocks) & (cs < j * SLC_BLOCK + SLC_BLOCK) & (cs + CMP_BLOCK > j * SLC_BLOCK)
    return jnp.asarray(ov.astype(np.float32))


def _stack_q(q, g):
    return _stack_q_f32(q, g).astype(BF16)


def _stack_q_f32(q, g):
    lane = lax.broadcasted_iota(jnp.int32, (Q_BLOCK, LANE), 1)
    keep = (lane < HALF) if g == 0 else (lane >= HALF)
    parts = []
    for jp in range(NSA_REP // 2):
        j = g * (NSA_REP // 2) + jp
        slab = q[:, j * LANE:(j + 1) * LANE]
        rolled = pltpu.roll(slab, HALF, 1)
        first, second = (slab, rolled) if g == 0 else (rolled, slab)
        parts.append(jnp.where(keep, first, 0.0))
        parts.append(jnp.where(keep, second, 0.0))
    return jnp.concatenate(parts, axis=0)


def _unstack_o(acc, g):
    lane = lax.broadcasted_iota(jnp.int32, (Q_BLOCK, LANE), 1)
    outs = []
    for jp in range(NSA_REP // 2):
        a = acc[(2 * jp) * Q_BLOCK:(2 * jp + 1) * Q_BLOCK]
        b = acc[(2 * jp + 1) * Q_BLOCK:(2 * jp + 2) * Q_BLOCK]
        if g == 0:
            outs.append(jnp.where(lane < HALF, a, pltpu.roll(b, HALF, 1)))
        else:
            outs.append(jnp.where(lane < HALF, pltpu.roll(a, HALF, 1), b))
    return jnp.concatenate(outs, axis=1)


def _tile8(x):
    return jnp.concatenate([x] * NSA_REP, axis=0)


def _rank_rows(imp, n_valid):
    sub = 8
    chunks = [imp[c:c + sub] for c in range(0, imp.shape[0], sub)]
    ranks = [jnp.zeros(ch.shape, F32) for ch in chunks]
    jrow = lax.broadcasted_iota(jnp.int32, chunks[0].shape, 0)
    for k in range(n_valid):
        rk = imp[k:k + 1, :]
        for c, ch in enumerate(chunks):
            if c * sub > k:
                ahead = rk >= ch
            elif c * sub + sub - 1 < k:
                ahead = rk > ch
            else:
                ahead = (rk > ch) | ((rk == ch) & (jrow + c * sub > k))
            ranks[c] = ranks[c] + jnp.where(ahead, 1.0, 0.0)
    return jnp.concatenate(ranks, axis=0)


def _cmp_kernel(q_ref, kv_ref, pb_ref, ov_ref, o_ref, sel_ref, bias_sc, *, n_keys, n_blocks):
    i = pl.program_id(0)

    @pl.when(pl.program_id(1) == 0)
    def _():
        for h in range(NSA_HEADS):
            bias_sc[h] = pltpu.roll(pb_ref[h], (4 * i + 4) % n_keys, 1)

    q = q_ref[0]
    kv = kv_ref[0]
    kc = kv[:, 0:LANE].astype(BF16)
    vc = kv[:, LANE:].astype(BF16)
    qi = lax.broadcasted_iota(jnp.int32, (Q_BLOCK, n_keys), 0)
    mi = lax.broadcasted_iota(jnp.int32, (Q_BLOCK, n_keys), 1)
    qpos = Q_BLOCK * i + qi
    valid8 = _tile8((mi >= 1) & (CMP_STRIDE * mi + CMP_STRIDE - 1 <= qpos))
    rowvalid8 = _tile8(jnp.where(qpos[:, 0:1] >= CMP_BLOCK - 1, 1.0, 0.0))
    jj = lax.broadcasted_iota(jnp.int32, (LANE, LANE), 1)
    psum = []
    for g in range(NSA_KV_HEADS):
        s = _dot_nt(_stack_q(q, g), kc)
        s = jnp.where(valid8, s + bias_sc[g * NSA_REP:(g + 1) * NSA_REP].reshape(ROWS, n_keys), NEG)
        p = jnp.exp(s - jnp.max(s, axis=-1, keepdims=True))
        pc = p / jnp.sum(p, axis=-1, keepdims=True) * rowvalid8
        o_ref[0, :, g * GROUP_W:(g + 1) * GROUP_W] = _unstack_o(
            jnp.dot(pc.astype(BF16), vc, preferred_element_type=F32), g)
        ps = pc[0:Q_BLOCK]
        for r in range(1, NSA_REP):
            ps = ps + pc[r * Q_BLOCK:(r + 1) * Q_BLOCK]
        psum.append(ps)
    ps = jnp.concatenate(psum, axis=0)
    hi = ps.astype(BF16)
    lo = (ps - hi.astype(F32)).astype(BF16)
    ov = ov_ref[...].astype(BF16)
    imp = (jnp.dot(hi, ov, preferred_element_type=F32)
           + jnp.dot(lo, ov, preferred_element_type=F32))
    forced = (jj == 0) | (jj == i) | (jj == i - 1)
    imp = jnp.where(forced, 1e6, jnp.where(jj <= i, imp, -1e6))
    imp = jnp.where(jj < n_blocks, imp, -2e6)
    rank = _rank_rows(imp.T, n_blocks)
    sel_ref[0, 0] = jnp.where(rank < SLC_TOP, 1.0, 0.0).astype(BF16)


def _cmp_prompt(q, kvc_cmp, pattern, ov):
    b, l, _ = q.shape
    n_keys = kvc_cmp.shape[1]
    n_blocks = l // SLC_BLOCK
    return pl.pallas_call(
        functools.partial(_cmp_kernel, n_keys=n_keys, n_blocks=n_blocks),
        out_shape=(jax.ShapeDtypeStruct((b, l, NSA_WIDTH), F32),
                   jax.ShapeDtypeStruct((b, l // Q_BLOCK, LANE, LANE), BF16)),
        grid=(l // Q_BLOCK, b),
        in_specs=[pl.BlockSpec((1, Q_BLOCK, NSA_WIDTH), lambda i, bi: (bi, i, 0)),
                  pl.BlockSpec((1, n_keys, KV_COLS), lambda i, bi: (bi, 0, 0)),
                  pl.BlockSpec(pattern.shape, lambda i, bi: (0, 0, 0)),
                  pl.BlockSpec(ov.shape, lambda i, bi: (0, 0))],
        out_specs=(pl.BlockSpec((1, Q_BLOCK, NSA_WIDTH), lambda i, bi: (bi, i, 0)),
                   pl.BlockSpec((1, 1, LANE, LANE), lambda i, bi: (bi, i, 0, 0))),
        scratch_shapes=[pltpu.VMEM(pattern.shape, F32)],
        compiler_params=pltpu.CompilerParams(dimension_semantics=("arbitrary", "arbitrary"),
                                             vmem_limit_bytes=VMEM_LIMIT),
        name="nsa_cmp_prompt",
    )(q, kvc_cmp, pattern, ov)


def _flash_update_t(st, vt, m, acc):
    m_new = jnp.maximum(m, jnp.max(st, axis=0, keepdims=True))
    p = jnp.exp2(st - m_new)
    acc = jnp.exp2(m - m_new) * acc + jnp.dot(vt, p.astype(BF16), preferred_element_type=F32)
    return m_new, acc


def _slc_kernel(q_ref, sel_ref, kx_ref, vt_ref, tnt_ref, tile_ref, o_ref, far_sc, near_sc, sta_sc, stb_sc):
    i = pl.program_id(1)
    q = q_ref[0] * LOG2E
    n_far = (jnp.maximum(i - 3, 0) + 3) // 4
    sel = sel_ref[0, 0]
    jrow = lax.broadcasted_iota(jnp.int32, (LANE, ROWS), 0)
    a = ((i + 1) // 2) * LANE
    delta = Q_BLOCK * (i + 1) - a
    row = lax.broadcasted_iota(jnp.int32, (SLC_NEARW, ROWS), 0)
    qi = lax.broadcasted_iota(jnp.int32, (SLC_NEARW, ROWS), 1) % Q_BLOCK
    rel = row - delta - (SLC_NEAR - Q_BLOCK)
    near_ok = (rel >= -(SLC_NEAR - Q_BLOCK)) & (rel <= qi) & (Q_BLOCK * i + rel >= 0)
    groups = range(NSA_KV_HEADS)
    for g in groups:
        qs_t = _stack_q_f32(q, g).T.astype(BF16)
        hit = jnp.dot(sel, tile_ref[g], preferred_element_type=F32) > 0.5
        near_sc[g] = jnp.concatenate([qs_t, jnp.where(hit, 0.0, NEG).astype(BF16)], axis=0)
        far_sc[g] = jnp.concatenate([qs_t, jnp.where(hit & (jrow < i - 3), 0.0, NEG).astype(BF16)], axis=0)

    def far_scores(t, dst):
        start = pl.multiple_of(SLC_PAD + SLC_NEAR * t, SLC_NEAR)
        kx = kx_ref[0, pl.ds(start, SLC_NEAR), :]
        for g in groups:
            dst[g] = jnp.dot(kx, far_sc[g], preferred_element_type=F32)

    def far_softmax(t, src, carry):
        start = pl.multiple_of(SLC_PAD + SLC_NEAR * t, SLC_NEAR)
        return tuple(_flash_update_t(src[g], vt_ref[0, g, :, pl.ds(start, SLC_NEAR)], *carry[g]) for g in groups)

    def far_pair(u, carry):
        far_scores(2 * u + 1, stb_sc)
        carry = far_softmax(2 * u, sta_sc, carry)
        far_scores(2 * u + 2, sta_sc)
        return far_softmax(2 * u + 1, stb_sc, carry)

    far_scores(0, sta_sc)
    init = (jnp.full((1, ROWS), NEG, F32), jnp.zeros((LANE, ROWS), F32))
    carry = lax.fori_loop(0, (n_far + 1) // 2, far_pair, (init, init))

    start = pl.multiple_of(a, LANE)
    kx = kx_ref[0, pl.ds(start, SLC_NEARW), :]
    lane_t = lax.broadcasted_iota(jnp.int32, (ROWS, LANE), 1)
    for g in groups:
        st = jnp.dot(kx, near_sc[g], preferred_element_type=F32) + tnt_ref[delta // Q_BLOCK, g]
        _, acc = _flash_update_t(jnp.where(near_ok, st, NEG), vt_ref[0, g, :, pl.ds(start, SLC_NEARW)], *carry[g])
        acc_t = acc.T
        den = jnp.where(lane_t < HALF, pltpu.roll(acc_t, HALF, 1), 1.0)
        o_ref[0, :, g * GROUP_W:(g + 1) * GROUP_W] = _unstack_o(acc_t / den, 0)


def _slc_prompt(q, sel, kx_pad, vt_pad, tnt, tile):
    b, l, _ = q.shape
    lp = kx_pad.shape[1]
    return pl.pallas_call(
        _slc_kernel,
        out_shape=jax.ShapeDtypeStruct((b, l, NSA_WIDTH), F32),
        grid=(b, l // Q_BLOCK),
        in_specs=[pl.BlockSpec((1, Q_BLOCK, NSA_WIDTH), lambda bi, i: (bi, i, 0)),
                  pl.BlockSpec((1, 1, LANE, LANE), lambda bi, i: (bi, i, 0, 0)),
                  pl.BlockSpec((1, lp, 2 * LANE), lambda bi, i: (bi, 0, 0)),
                  pl.BlockSpec((1, NSA_KV_HEADS, LANE, lp), lambda bi, i: (bi, 0, 0, 0)),
                  pl.BlockSpec(tnt.shape, lambda bi, i: (0, 0, 0, 0)),
                  pl.BlockSpec(tile.shape, lambda bi, i: (0, 0, 0))],
        out_specs=pl.BlockSpec((1, Q_BLOCK, NSA_WIDTH), lambda bi, i: (bi, i, 0)),
        scratch_shapes=[pltpu.VMEM((NSA_KV_HEADS, 2 * LANE, ROWS), BF16),
                        pltpu.VMEM((NSA_KV_HEADS, 2 * LANE, ROWS), BF16),
                        pltpu.VMEM((NSA_KV_HEADS, SLC_NEAR, ROWS), F32),
                        pltpu.VMEM((NSA_KV_HEADS, SLC_NEAR, ROWS), F32)],
        compiler_params=pltpu.CompilerParams(vmem_limit_bytes=VMEM_LIMIT),
        name="nsa_slc_prompt",
    )(q, sel, kx_pad, vt_pad, tnt, tile)


def _group_tile():
    t = np.zeros((NSA_KV_HEADS, LANE, ROWS), np.float32)
    for g in range(NSA_KV_HEADS):
        for r in range(NSA_REP):
            for qq in range(Q_BLOCK):
                t[g, g * Q_BLOCK + qq, r * Q_BLOCK + qq] = 1.0
    return jnp.asarray(t, BF16)


def _win_kernel(q_ref, k_ref, vx_ref, tw_ref, o_ref):
    i = pl.program_id(1)
    q = q_ref[0] * LOG2E
    start = pl.multiple_of(Q_BLOCK * i, Q_BLOCK)
    k = k_ref[0, pl.ds(start, WIN_W), :]
    ci = lax.broadcasted_iota(jnp.int32, (ROWS, WIN_W), 1)
    key_ok = Q_BLOCK * i + ci - WIN_PAD >= 0
    lane = lax.broadcasted_iota(jnp.int32, (ROWS, LANE), 1)
    for g in range(NSA_KV_HEADS):
        bias = tw_ref[g * NSA_REP:(g + 1) * NSA_REP].reshape(ROWS, WIN_W)
        s = jnp.where(key_ok, _dot_nt(_stack_q(q, g), k) + bias, NEG)
        p = jnp.exp2(s - jnp.max(s, axis=-1, keepdims=True))
        acc = jnp.dot(p.astype(BF16), vx_ref[0, g, pl.ds(start, WIN_W), :], preferred_element_type=F32)
        den = jnp.where(lane < HALF, pltpu.roll(acc, HALF, 1), 1.0)
        o_ref[0, :, g * GROUP_W:(g + 1) * GROUP_W] = _unstack_o(acc / den, 0)


def _win_prompt(q, k_pad, vx_pad, tw):
    b, l, _ = q.shape
    lp = k_pad.shape[1]
    return pl.pallas_call(
        _win_kernel,
        out_shape=jax.ShapeDtypeStruct((b, l, NSA_WIDTH), F32),
        grid=(b, l // Q_BLOCK),
        in_specs=[pl.BlockSpec((1, Q_BLOCK, NSA_WIDTH), lambda bi, i: (bi, i, 0)),
                  pl.BlockSpec((1, lp, LANE), lambda bi, i: (bi, 0, 0)),
                  pl.BlockSpec((1, NSA_KV_HEADS, lp, LANE), lambda bi, i: (bi, 0, 0, 0)),
                  pl.BlockSpec(tw.shape, lambda bi, i: (0, 0, 0))],
        out_specs=pl.BlockSpec((1, Q_BLOCK, NSA_WIDTH), lambda bi, i: (bi, i, 0)),
        compiler_params=pltpu.CompilerParams(vmem_limit_bytes=VMEM_LIMIT),
        name="nsa_win_prompt",
    )(q, k_pad, vx_pad, tw)


def _dense1_kernel(q_ref, k_ref, v_ref, bm_ref, o_ref, ps_ref):
    k = k_ref[0].astype(BF16)
    v = v_ref[0].astype(BF16)
    for g in range(NSA_KV_HEADS):
        rows = slice(g * NSA_REP, (g + 1) * NSA_REP)
        s = _dot_nt(q_ref[0, rows, :], k) + bm_ref[rows, :]
        p = jnp.exp(s - jnp.max(s, axis=-1, keepdims=True))
        pc = p / jnp.sum(p, axis=-1, keepdims=True)
        o_ref[0, rows, :] = jnp.dot(pc.astype(BF16), v, preferred_element_type=F32)
        ps_ref[0, g:g + 1, :] = jnp.sum(pc, axis=0, keepdims=True)


def _dense1(q_pad, kv, biasmask):
    nb, n, _ = kv.shape
    return pl.pallas_call(
        _dense1_kernel,
        out_shape=(jax.ShapeDtypeStruct((nb, NSA_HEADS, LANE), F32),
                   jax.ShapeDtypeStruct((nb, NSA_KV_HEADS, n), F32)),
        grid=(nb,),
        in_specs=[pl.BlockSpec((1, NSA_HEADS, LANE), lambda b: (b, 0, 0)),
                  pl.BlockSpec((1, n, LANE), lambda b: (b, 0, 0)),
                  pl.BlockSpec((1, n, LANE), lambda b: (b, 0, 1)),
                  pl.BlockSpec((NSA_HEADS, n), lambda b: (0, 0))],
        out_specs=(pl.BlockSpec((1, NSA_HEADS, LANE), lambda b: (b, 0, 0)),
                   pl.BlockSpec((1, NSA_KV_HEADS, n), lambda b: (b, 0, 0))),
        name="nsa_dense_sample",
    )(q_pad, kv, kv, biasmask)


def _take_group_half(o_pad):
    nb = o_pad.shape[0]
    o = o_pad.reshape(nb, NSA_KV_HEADS, NSA_REP, NSA_KV_HEADS, HEAD_DIM)
    o = jnp.stack([o[:, g, :, g, :] for g in range(NSA_KV_HEADS)], axis=1)
    return o.reshape(nb, NSA_WIDTH)


def _topk_kernel(ps_ref, ov_ref, idx_ref, imp_sc, *, n_blocks, cur):
    n_rows = ps_ref.shape[0]
    n_cols = ov_ref.shape[1]
    ps = jnp.concatenate([ps_ref[...], jnp.zeros((LANE - n_rows, ps_ref.shape[1]), F32)], axis=0)
    imp = _dot32(ps, ov_ref[...])
    jj = lax.broadcasted_iota(jnp.int32, (LANE, n_cols), 1)
    forced = (jj == 0) | (jj == cur) | (jj == cur - 1)
    imp = jnp.where(forced, 1e6, jnp.where(jj <= cur, imp, -1e6))
    imp = jnp.where(jj < n_blocks, imp, -2e6)
    imp_t = imp.T
    imp_sc[...] = imp_t
    jrow = lax.broadcasted_iota(jnp.int32, (n_cols, LANE), 0)

    def body(k, rank):
        rk = imp_sc[pl.ds(k, 1), :]
        ahead = (rk > imp_t) | ((rk == imp_t) & (jrow > k))
        return rank + jnp.where(ahead, 1.0, 0.0)

    rank = lax.fori_loop(0, n_blocks, body, jnp.zeros((n_cols, LANE), F32))
    jf = jrow.astype(F32)
    rows = [jnp.sum(jnp.where(rank == float(r), jf, 0.0), axis=0, keepdims=True) for r in range(SLC_TOP)]
    idx_ref[...] = jnp.concatenate(rows, axis=0).astype(jnp.int32)


def _topk_sample(psum, ov, n_blocks, cur):
    n_rows, n_keys = psum.shape
    n_cols = ov.shape[1]
    return pl.pallas_call(
        functools.partial(_topk_kernel, n_blocks=n_blocks, cur=cur),
        out_shape=jax.ShapeDtypeStruct((SLC_TOP, LANE), jnp.int32),
        grid=(1,),
        in_specs=[pl.BlockSpec((n_rows, n_keys), lambda i: (0, 0)),
                  pl.BlockSpec(ov.shape, lambda i: (0, 0))],
        out_specs=pl.BlockSpec((SLC_TOP, LANE), lambda i: (0, 0)),
        scratch_shapes=[pltpu.VMEM((n_cols, LANE), F32)],
        name="nsa_topk_sample",
    )(psum, ov)


def _slc1_kernel(phys_ref, jsel_ref, q_ref, *refs, cur, past):
    pages = refs[:SLC_TOP]
    new_ref, bb_ref, o_ref = refs[SLC_TOP:]
    b = pl.program_id(0)
    g = pl.program_id(1)
    goff = pl.multiple_of(g * HEAD_DIM, HEAD_DIM)
    q = q_ref[0]
    lane = lax.broadcasted_iota(jnp.int32, (NSA_REP, PAGE_SIZE), 1)
    first = lax.broadcasted_iota(jnp.int32, (HEAD_DIM, PAGE_SIZE), 1) == 0
    new_k = jnp.where(first, new_ref[0, pl.ds(goff, HEAD_DIM), :], 0.0)
    new_v = jnp.where(first, new_ref[0, pl.ds(LANE + goff, HEAD_DIM), :], 0.0)
    scores, values = [], []
    for n in range(SLC_TOP):
        j = jsel_ref[(b * NSA_KV_HEADS + g) * SLC_TOP + n]
        kt = jnp.where(j == cur, new_k, pages[n][0, pl.ds(goff, HEAD_DIM), :])
        vt = jnp.where(j == cur, new_v, pages[n][0, pl.ds(LANE + goff, HEAD_DIM), :])
        s = jnp.dot(q, kt.astype(BF16), preferred_element_type=F32)
        ok = (lane // SLC_BLOCK == j % 2) & ((j // 2) * PAGE_SIZE + lane <= past)
        bias = bb_ref[jnp.clip(j - (cur - 3), 0, 3), pl.ds(pl.multiple_of(g * NSA_REP, NSA_REP), NSA_REP), :]
        scores.append(jnp.where(ok, s + bias, NEG))
        values.append(vt.astype(BF16))
    s_all = jnp.concatenate(scores, axis=1)
    p = jnp.exp(s_all - jnp.max(s_all, axis=-1, keepdims=True))
    acc = jnp.zeros((NSA_REP, HEAD_DIM), F32)
    for n in range(SLC_TOP):
        acc = acc + _dot_nt(p[:, n * PAGE_SIZE:(n + 1) * PAGE_SIZE].astype(BF16), values[n])
    o_ref[0] = acc / jnp.sum(p, axis=-1, keepdims=True)


def _slc_sample(q, cache_pages_t, phys, jsel, new_cols, biasblk, cur, past):
    nb = q.shape[0]
    idx = lambda b, g, n: (b * NSA_KV_HEADS + g) * SLC_TOP + n
    page_specs = [pl.BlockSpec((1, KV_COLS, PAGE_SIZE), lambda b, g, ph, js, n=n: (ph[idx(b, g, n)], 0, 0))
                  for n in range(SLC_TOP)]
    return pl.pallas_call(
        functools.partial(_slc1_kernel, cur=cur, past=past),
        out_shape=jax.ShapeDtypeStruct((nb, NSA_HEADS, HEAD_DIM), F32),
        grid_spec=pltpu.PrefetchScalarGridSpec(
            num_scalar_prefetch=2,
            grid=(nb, NSA_KV_HEADS),
            in_specs=[pl.BlockSpec((1, NSA_REP, HEAD_DIM), lambda b, g, ph, js: (b, g, 0))] + page_specs
            + [pl.BlockSpec((1, KV_COLS, 1), lambda b, g, ph, js: (b, 0, 0)),
               pl.BlockSpec(biasblk.shape, lambda b, g, ph, js: (0, 0, 0))],
            out_specs=pl.BlockSpec((1, NSA_REP, HEAD_DIM), lambda b, g, ph, js: (b, g, 0))),
        name="nsa_slc_sample",
    )(phys, jsel, q, *([cache_pages_t] * SLC_TOP), new_cols, biasblk)


def _out_kernel(x_ref, gate_ref, yssd_ref, zs_ref, oc_ref, os_ref, ow_ref, gl_ref, za_ref,
                nw1_ref, nw2_ref, w_ref, eg_ref, fg_ref, o_ref, *, per_row, final):
    gates = _sigmoid(gl_ref[...])
    gparts = _split_bf16(gates, 2)
    expand = lambda br: sum(jnp.dot(p, eg_ref[br], preferred_element_type=F32) for p in gparts)
    y_nsa = expand(0) * oc_ref[...] + expand(1) * os_ref[...] + expand(2) * ow_ref[...]

    def gated_norm(y, z, w):
        u = y * _silu(z)
        half = u.shape[1] // 2
        parts = []
        for g in range(2):
            ug = u[:, g * half:(g + 1) * half]
            parts.append(ug * lax.rsqrt(jnp.mean(ug * ug, axis=-1, keepdims=True) + NORM_EPS))
        return (jnp.concatenate(parts, axis=1) * w).astype(BF16)

    m1 = gated_norm(yssd_ref[...], zs_ref[...], nw1_ref[...])
    m2 = gated_norm(y_nsa, za_ref[...], nw2_ref[...])
    proj = (jnp.dot(m1, w_ref[0:SSD_WIDTH, :], preferred_element_type=F32)
            + jnp.dot(m2, w_ref[SSD_WIDTH:, :], preferred_element_type=F32))
    gate = gate_ref[...] if per_row else gate_ref[0]
    out = x_ref[...] + gate * proj
    if final:
        out = out * lax.rsqrt(jnp.mean(out * out, axis=-1, keepdims=True) + NORM_EPS) * fg_ref[...]
    o_ref[...] = out


def _gate_expand():
    e = np.zeros((3, LANE, NSA_WIDTH), np.float32)
    for br in range(3):
        for h in range(NSA_HEADS):
            e[br, br * NSA_HEADS + h, h * HEAD_DIM:(h + 1) * HEAD_DIM] = 1.0
    return jnp.asarray(e, BF16)


def _layer_out(x2d, gate, y_ssd, z_s, o_cmp, o_slc, o_win, gl, z_a, nw1, nw2, w_out_bf, final_g,
               rows_per_batch, final):
    m = x2d.shape[0]
    per_row = rows_per_batch == 1
    tm = m if per_row else 256
    if per_row:
        gate_spec = pl.BlockSpec((tm, D_MODEL), lambda i: (0, 0))
        gt = gate
    else:
        gate_spec = pl.BlockSpec((1, 1, D_MODEL), lambda i: ((i * tm) // rows_per_batch, 0, 0))
        gt = gate[:, None, :]
    row = lambda w: pl.BlockSpec((tm, w), lambda i: (i, 0))
    full = lambda shape: pl.BlockSpec(shape, lambda i: (0,) * len(shape))
    eg = _gate_expand()
    return pl.pallas_call(
        functools.partial(_out_kernel, per_row=per_row, final=final),
        out_shape=jax.ShapeDtypeStruct((m, D_MODEL), F32),
        grid=(m // tm,),
        in_specs=[row(D_MODEL), gate_spec, row(SSD_WIDTH), row(SSD_WIDTH), row(NSA_WIDTH), row(NSA_WIDTH),
                  row(NSA_WIDTH), row(LANE), row(NSA_WIDTH), full((1, SSD_WIDTH)), full((1, NSA_WIDTH)),
                  full(w_out_bf.shape), full(eg.shape), full((1, D_MODEL))],
        out_specs=row(D_MODEL),
        compiler_params=pltpu.CompilerParams(vmem_limit_bytes=VMEM_LIMIT),
        name="layer_out",
    )(x2d, gt, y_ssd, z_s, o_cmp, o_slc, o_win, gl, z_a, nw1.reshape(1, SSD_WIDTH), nw2.reshape(1, NSA_WIDTH),
      w_out_bf, eg, final_g.reshape(1, D_MODEL))


def _pad_in_weights(w_in):
    cols = []
    off = 0
    for size, width in zip(_SEG_SIZES, _SEG_PAD):
        seg = w_in[:, off:off + size]
        cols.append(jnp.pad(seg, ((0, 0), (0, width - size))))
        off += size
    return jnp.concatenate(cols, axis=1).astype(BF16)


def _front_pad_bf16(kv, rows):
    return jnp.pad(kv, ((0, 0), (rows, 0), (0, 0))).astype(BF16)


def kernel(x_prompt, x_sample, cache_cmp_kv, cache_slc_kv, state_win_kv, state_conv, state_ssm, page_table,
           c_prompt, c_sample, norm_g, ada_w, ada_b, w_in, conv_w, conv_b, dt_bias, a_log, d_skip,
           ssd_norm_w, cmp_pe, cmp_w1, cmp_w2, nsa_norm_w, w_out, rel_bias, final_norm_g):
    nbp, lp, _ = x_prompt.shape
    nbs = x_sample.shape[0]
    depth = w_in.shape[0]
    n_pool = cache_cmp_kv.shape[1]
    n_pages = page_table.shape[1]
    past = n_pages * PAGE_SIZE
    w_buf = state_win_kv.shape[2]
    kv_row = (2, NSA_KV_HEADS, HEAD_DIM)

    n_ck = lp // CMP_STRIDE
    n_sb = lp // SLC_BLOCK
    qi = np.arange(Q_BLOCK)[:, None]
    c0 = n_ck - 4
    pattern = _bias_of_dist(rel_bias, qi - CMP_STRIDE * (np.arange(n_ck)[None, :] - c0) - (CMP_STRIDE - 1))
    far = rel_bias.astype(F32)[REL_BUCKETS - 1][:, None, None]
    par = np.arange(2)[:, None, None]
    dist_n = (np.arange(Q_BLOCK)[None, None, :] - np.arange(SLC_NEARW)[None, :, None] + Q_BLOCK * par
              + (SLC_NEAR - Q_BLOCK))
    tnt = (_bias_of_dist(rel_bias, dist_n) - far[..., None]).reshape(NSA_KV_HEADS, NSA_REP, 2, SLC_NEARW, Q_BLOCK)
    tnt = jnp.transpose(tnt, (2, 0, 3, 1, 4)).reshape(2, NSA_KV_HEADS, SLC_NEARW, ROWS) * LOG2E
    tile = _group_tile()
    key_row = np.arange(SLC_PAD + lp + SLC_BACK) - SLC_PAD
    blk_onehot = jnp.asarray((key_row[:, None] >= 0)
                             & (key_row[:, None] // SLC_BLOCK == np.arange(LANE)[None, :]), BF16)
    dist_w = qi - np.arange(WIN_W)[None, :] + WIN_PAD
    tw = jnp.where(jnp.asarray((dist_w >= 0) & (dist_w < WINDOW))[None], _bias_of_dist(rel_bias, dist_w) * LOG2E, NEG)
    ov_p = _overlap_matrix(n_ck, n_sb, LANE)

    n_cs = past // CMP_STRIDE
    cur = past // SLC_BLOCK
    n_blk_s = cur + 1
    n_cols_s = -(-n_blk_s // LANE) * LANE
    ov_s = _overlap_matrix(n_cs, n_blk_s, n_cols_s)
    m_s = np.arange(n_cs)
    bm_c = jnp.where(jnp.asarray(m_s >= 1)[None, :],
                     _bias_of_dist(rel_bias, past - (CMP_STRIDE * m_s + CMP_STRIDE - 1)), NEG)
    n_w = -(-(w_buf + 1) // LANE) * LANE
    iw = np.arange(n_w)
    dw = w_buf - iw
    ok_w = (iw <= w_buf) & (dw >= 0) & (dw < WINDOW) & (past - w_buf + iw >= 0)
    bm_w = jnp.where(jnp.asarray(ok_w)[None, :], _bias_of_dist(rel_bias, dw), NEG)
    jb = (cur - 3 + np.arange(4))[:, None]
    biasblk = _bias_of_dist(rel_bias, past - SLC_BLOCK * jb - np.arange(SLC_BLOCK)[None, :])
    biasblk = jnp.transpose(biasblk, (1, 0, 2))
    biasblk = jnp.concatenate([biasblk, biasblk], axis=-1)

    pages_t = lambda c: jnp.transpose(c, (0, 1, 3, 4, 5, 2)).reshape(depth * n_pool, KV_COLS, PAGE_SIZE)
    cmp_pages_t = pages_t(cache_cmp_kv)
    slc_pages_t = pages_t(cache_slc_kv)
    prompt_pages = jnp.arange(nbp * (lp // PAGE_SIZE), dtype=jnp.int32).reshape(nbp, lp // PAGE_SIZE)

    c_all = jnp.concatenate([c_prompt, c_sample], axis=0)
    xp = x_prompt.reshape(nbp * lp, D_MODEL)
    xs = x_sample.reshape(nbs, D_MODEL)
    outs = {k: [] for k in ("pc", "ps", "pw", "pconv", "pssm", "sc", "ss", "sw", "sconv", "sssm")}

    for l in range(depth):
        final = l == depth - 1
        w_pad = _pad_in_weights(w_in[l])
        w_out_bf = w_out[l].astype(BF16)
        cweights = _compress_weights(cmp_pe[l], cmp_w1[l], cmp_w2[l])
        mod = _modulation(c_all, ada_w[l], ada_b[l])
        shift, scale, gate = mod[:, :D_MODEL], mod[:, D_MODEL:2 * D_MODEL], mod[:, 2 * D_MODEL:]

        z_s, xbc, dt, q, kvc, kvs, kvw, gl, z_a = _in_projection(xp, norm_g[l], scale[:nbp], shift[:nbp], w_pad, lp)
        xbc3 = xbc.reshape(nbp, lp, CONV_DIM)
        y_ssd, h_fin = _ssd_prompt(xbc3, dt.reshape(nbp, lp, LANE), conv_w[l], conv_b[l], dt_bias[l], a_log[l],
                                   d_skip[l])
        q3 = q.reshape(nbp, lp, NSA_WIDTH)
        kvc3, kvs3, kvw3 = (t.reshape(nbp, lp, KV_COLS) for t in (kvc, kvs, kvw))
        kc = _compress(kvc.reshape(nbp * (lp // PAGE_SIZE), PAGE_SIZE, KV_COLS), prompt_pages, cweights, False)
        o_cmp, sel = _cmp_prompt(q3, kc, pattern, ov_p)
        ks_pad = jnp.pad(kvs3, ((0, 0), (SLC_PAD, SLC_BACK), (0, 0))).astype(BF16)
        kx_pad = jnp.concatenate([ks_pad[:, :, :LANE], jnp.broadcast_to(blk_onehot, (nbp,) + blk_onehot.shape)],
                                 axis=-1)
        vt = jnp.transpose(ks_pad[:, :, LANE:].reshape(nbp, -1, NSA_KV_HEADS, HEAD_DIM), (0, 2, 3, 1))
        vt_pad = jnp.concatenate([vt, jnp.ones_like(vt)], axis=2)
        o_slc = _slc_prompt(q3, sel, kx_pad, vt_pad, tnt, tile)
        kw_pad = _front_pad_bf16(kvw3, WIN_PAD)
        vw = jnp.transpose(kw_pad[:, :, LANE:].reshape(nbp, -1, NSA_KV_HEADS, HEAD_DIM), (0, 2, 1, 3))
        o_win = _win_prompt(q3, kw_pad[:, :, :LANE], jnp.concatenate([vw, jnp.ones_like(vw)], axis=-1), tw)
        xp = _layer_out(xp, gate[:nbp], y_ssd.reshape(nbp * lp, SSD_WIDTH), z_s,
                        o_cmp.reshape(nbp * lp, NSA_WIDTH), o_slc.reshape(nbp * lp, NSA_WIDTH),
                        o_win.reshape(nbp * lp, NSA_WIDTH), gl, z_a, ssd_norm_w[l], nsa_norm_w[l], w_out_bf,
                        final_norm_g, lp, final)
        outs["pc"].append(kvc3.reshape((nbp, lp) + kv_row))
        outs["ps"].append(kvs3.reshape((nbp, lp) + kv_row))
        outs["pw"].append(kvw3[:, -min(WINDOW, lp):].reshape((nbp, min(WINDOW, lp)) + kv_row))
        outs["pconv"].append(xbc3[:, -(SSD_CONV - 1):])
        outs["pssm"].append(h_fin)

        z_s2, xbc2, dt2, q2, kvc2, kvs2, kvw2, gl2, z_a2 = _in_projection(
            xs, norm_g[l], scale[nbp:], shift[nbp:], w_pad, 1)
        y_ssd2, h2 = _ssd_step(xbc2, state_conv[l], dt2, state_ssm[l], conv_w[l], conv_b[l], dt_bias[l],
                               a_log[l], d_skip[l])
        qh = q2.reshape(nbs, NSA_KV_HEADS, NSA_REP, HEAD_DIM)
        zq = jnp.zeros((nbs, NSA_REP, HEAD_DIM), F32)
        q_pad = jnp.stack([jnp.concatenate([qh[:, 0], zq], axis=-1), jnp.concatenate([zq, qh[:, 1]], axis=-1)],
                          axis=1).reshape(nbs, NSA_HEADS, LANE).astype(BF16)
        kc2 = _compress(cmp_pages_t, page_table + l * n_pool, cweights, True)
        oc2, psum = _dense1(q_pad, kc2, bm_c)
        sel_idx = _topk_sample(psum.reshape(nbs * NSA_KV_HEADS, n_cs), ov_s, n_blk_s, cur)
        jsel = sel_idx[:, :nbs * NSA_KV_HEADS].T.reshape(nbs, NSA_KV_HEADS, SLC_TOP)
        jc = jnp.minimum(jsel, cur - 1)
        page = jnp.take_along_axis(page_table, (jc // 2).reshape(nbs, -1), axis=1).reshape(jsel.shape)
        phys = jnp.where(jsel < cur, page + l * n_pool, 0).astype(jnp.int32)
        os2 = _slc_sample(q2.reshape(nbs, NSA_HEADS, HEAD_DIM).astype(BF16), slc_pages_t, phys.reshape(-1),
                          jsel.reshape(-1), kvs2.reshape(nbs, KV_COLS, 1), biasblk, cur, past)
        kw_full = jnp.concatenate([state_win_kv[l].reshape(nbs, w_buf, KV_COLS), kvw2[:, None, :]], axis=1)
        kw_in = jnp.pad(kw_full, ((0, 0), (0, n_w - (w_buf + 1)), (0, 0)))
        ow2, _ = _dense1(q_pad, kw_in, bm_w)
        xs = _layer_out(xs, gate[nbp:], y_ssd2, z_s2, _take_group_half(oc2), os2.reshape(nbs, NSA_WIDTH),
                        _take_group_half(ow2), gl2, z_a2, ssd_norm_w[l], nsa_norm_w[l], w_out_bf,
                        final_norm_g, 1, final)
        outs["sc"].append(kvc2.reshape((nbs, 1) + kv_row))
        outs["ss"].append(kvs2.reshape((nbs, 1) + kv_row))
        outs["sw"].append(kw_full[:, -w_buf:].reshape((nbs, w_buf) + kv_row))
        outs["sconv"].append(jnp.concatenate([state_conv[l][:, 1:], xbc2[:, None, :]], axis=1))
        outs["sssm"].append(h2)

    st = lambda k: jnp.stack(outs[k])
    return (xp.reshape(nbp, lp, D_MODEL), xs.reshape(nbs, 1, D_MODEL),
            st("pc"), st("ps"), st("pw"), st("pconv"), st("pssm"),
            st("sc"), st("ss"), st("sw"), st("sconv"), st("sssm"))
```

```python
import functools
import math

import numpy as np
import jax
import jax.numpy as jnp
from jax import lax
from jax.experimental import pallas as pl
from jax.experimental.pallas import tpu as pltpu

F32 = jnp.float32
BF16 = jnp.bfloat16
HIGHEST = lax.Precision.HIGHEST

D_MODEL = 1024
HEAD_DIM = 64
SSD_WIDTH = 1024
SSD_HEADS = 16
SSD_GROUPS = 2
SSD_STATE = 128
SSD_CONV = 4
SSD_CHUNK = 256
CONV_DIM = SSD_WIDTH + 2 * SSD_GROUPS * SSD_STATE
NSA_WIDTH = 1024
NSA_HEADS = 16
NSA_KV_HEADS = 2
NSA_REP = NSA_HEADS // NSA_KV_HEADS
CMP_BLOCK = 32
CMP_STRIDE = 16
CMP_HID = 2 * HEAD_DIM
SLC_BLOCK = 64
SLC_TOP = 16
WINDOW = 512
Q_BLOCK = 64
REL_BUCKETS = 32
REL_MAX_DIST = 128
NORM_EPS = 1e-6
KV_COLS = 2 * NSA_KV_HEADS * HEAD_DIM
PAGE_SIZE = 128
NEG = -1e30
LOG2E = 1.4426950408889634

LANE = 128
HALF = LANE // 2
GROUP_W = NSA_REP * HEAD_DIM
ROWS = NSA_REP * Q_BLOCK
SLC_NEAR = 4 * SLC_BLOCK
SLC_PAD = SLC_NEAR
SLC_NEARW = SLC_NEAR + LANE
ONES_ROWS = 8
SLC_BACK = 2 * SLC_NEAR
WIN_W = WINDOW + 2 * Q_BLOCK
WIN_PAD = WINDOW
CMP_PAGES_MAX = 32
VMEM_LIMIT = 48 * 1024 * 1024

_SEG_NAMES = ("z_s", "xbc", "dt", "q", "kvc", "kvs", "kvw", "gl", "z_a")
_SEG_SIZES = (SSD_WIDTH, CONV_DIM, SSD_HEADS, NSA_WIDTH, KV_COLS, KV_COLS, KV_COLS, 3 * NSA_HEADS, NSA_WIDTH)
_SEG_PAD = tuple(-(-s // LANE) * LANE for s in _SEG_SIZES)
_SEG_OFF = tuple(int(o) for o in np.cumsum((0,) + _SEG_PAD[:-1]))
IN_PAD = int(sum(_SEG_PAD))


def _sigmoid(x):
    return 1.0 / (1.0 + jnp.exp(-x))


def _silu(x):
    return x * _sigmoid(x)


def _dot32(a, b):
    return jnp.dot(a, b, precision=HIGHEST, preferred_element_type=F32)


def _split_bf16(x, terms):
    parts = []
    for _ in range(terms):
        p = x.astype(BF16)
        parts.append(p)
        x = x - p.astype(F32)
    return parts


def _dot_sel(x, sel_bf16, terms):
    return sum(jnp.dot(p, sel_bf16, preferred_element_type=F32) for p in _split_bf16(x, terms))


def _sel_dot(sel_bf16, x, terms):
    return sum(jnp.dot(sel_bf16, p, preferred_element_type=F32) for p in _split_bf16(x, terms))


def _dot_nt(a, b):
    return lax.dot_general(a, b, (((1,), (1,)), ((), ())), preferred_element_type=F32)


def _bucket_table():
    n = np.arange(REL_MAX_DIST + 1)
    max_exact = REL_BUCKETS // 2
    nf = np.maximum(n, 1).astype(np.float32)
    large = max_exact + (np.log(nf / np.float32(max_exact)) / np.float32(math.log(REL_MAX_DIST / max_exact))
                         * np.float32(REL_BUCKETS - max_exact)).astype(np.int32)
    large = np.minimum(large, REL_BUCKETS - 1)
    return np.where(n < max_exact, n, large).astype(np.int32)


_BUCKETS = _bucket_table()


def _bias_of_dist(rel_bias, dist):
    idx = _BUCKETS[np.clip(dist, 0, REL_MAX_DIST)]
    out = jnp.take(rel_bias.astype(F32), jnp.asarray(idx.reshape(-1)), axis=0)
    return out.T.reshape((NSA_HEADS,) + dist.shape)


def _mod_kernel(c_ref, w_ref, b_ref, o_ref):
    o_ref[...] = _dot32(_silu(c_ref[...]), w_ref[...]) + b_ref[...]


def _modulation(c, w, b):
    m, d = c.shape
    n = w.shape[1]
    tn = 512
    return pl.pallas_call(
        _mod_kernel,
        out_shape=jax.ShapeDtypeStruct((m, n), F32),
        grid=(n // tn,),
        in_specs=[pl.BlockSpec((m, d), lambda j: (0, 0)),
                  pl.BlockSpec((d, tn), lambda j: (0, j)),
                  pl.BlockSpec((1, tn), lambda j: (0, j))],
        out_specs=pl.BlockSpec((m, tn), lambda j: (0, j)),
        name="adaln_mod",
    )(c, w, b.reshape(1, n))


def _inproj_kernel(x_ref, g_ref, sc_ref, sh_ref, w_ref, *out_refs, per_row):
    x = x_ref[...]
    xn = x * lax.rsqrt(jnp.mean(x * x, axis=-1, keepdims=True) + NORM_EPS)
    sc = sc_ref[...] if per_row else sc_ref[0]
    sh = sh_ref[...] if per_row else sh_ref[0]
    h = ((xn * g_ref[...]) * (1.0 + sc) + sh).astype(BF16)
    for name, off, width, ref in zip(_SEG_NAMES, _SEG_OFF, _SEG_PAD, out_refs):
        r = jnp.dot(h, w_ref[:, off:off + width], preferred_element_type=F32)
        if name == "q":
            r = r * (HEAD_DIM ** -0.5)
        ref[...] = r


def _in_projection(x2d, g, scale, shift, w_pad, rows_per_batch):
    m = x2d.shape[0]
    per_row = rows_per_batch == 1
    tm = m if per_row else 256
    if per_row:
        mod_spec = pl.BlockSpec((tm, D_MODEL), lambda i: (0, 0))
        sc, sh = scale, shift
    else:
        mod_spec = pl.BlockSpec((1, 1, D_MODEL), lambda i: ((i * tm) // rows_per_batch, 0, 0))
        sc, sh = scale[:, None, :], shift[:, None, :]
    outs = tuple(jax.ShapeDtypeStruct((m, w), F32) for w in _SEG_PAD)
    return pl.pallas_call(
        functools.partial(_inproj_kernel, per_row=per_row),
        out_shape=outs,
        grid=(m // tm,),
        in_specs=[pl.BlockSpec((tm, D_MODEL), lambda i: (i, 0)),
                  pl.BlockSpec((1, D_MODEL), lambda i: (0, 0)),
                  mod_spec, mod_spec,
                  pl.BlockSpec((D_MODEL, IN_PAD), lambda i: (0, 0))],
        out_specs=tuple(pl.BlockSpec((tm, w), lambda i: (i, 0)) for w in _SEG_PAD),
        compiler_params=pltpu.CompilerParams(vmem_limit_bytes=VMEM_LIMIT),
        name="in_projection",
    )(x2d, g.reshape(1, D_MODEL), sc, sh, w_pad)


def _softplus(x):
    return jnp.maximum(x, 0.0) + jnp.log(1.0 + jnp.exp(-jnp.abs(x)))


def _ssd_kernel(xbc_ref, dt_ref, cw_ref, cb_ref, dtb_ref, alog_ref, dsk_ref, e_ref, tril_ref,
                y_ref, hfin_ref, xe_sc, st_sc):
    c = pl.program_id(1)
    q = SSD_CHUNK
    n_pairs = SSD_HEADS // 2

    @pl.when(c == 0)
    def _():
        xe_sc[0:8, :] = jnp.zeros((8, CONV_DIM), F32)
        st_sc[...] = jnp.zeros(st_sc.shape, F32)

    xe_sc[8:8 + q, :] = xbc_ref[0]
    acc = cb_ref[...] + cw_ref[0:1, :] * xe_sc[5:5 + q, :]
    for k in range(1, SSD_CONV):
        acc = acc + cw_ref[k:k + 1, :] * xe_sc[5 + k:5 + k + q, :]
    u = _silu(acc)
    xe_sc[0:8, :] = xe_sc[q:q + 8, :]

    xs = u[:, :SSD_WIDTH]
    gn = SSD_GROUPS * SSD_STATE
    bm = u[:, SSD_WIDTH:SSD_WIDTH + gn]
    cm = u[:, SSD_WIDTH + gn:]

    dt = _softplus(dt_ref[0] + dtb_ref[...])
    a = dt * (-jnp.exp(alog_ref[...]))
    cs = _sel_dot(tril_ref[...], a, 3)
    cs_t = cs.T
    cs_last = cs[q - 1:q, :]
    e = e_ref[...]
    dt_e = _dot_sel(dt, e, 3)
    w_e = _dot_sel(dt * jnp.exp(cs_last - cs), e, 3)
    ecs_e = _dot_sel(jnp.exp(cs), e, 3)
    tot_e = _dot_sel(jnp.broadcast_to(jnp.exp(cs_last), (8, LANE)), e, 3)[0:1, :]
    xdt = (xs * dt_e).astype(BF16)
    xw = (xs * w_e).astype(BF16)

    li = lax.broadcasted_iota(jnp.int32, (q, q), 0)
    si = lax.broadcasted_iota(jnp.int32, (q, q), 1)
    tri = li >= si
    lane = lax.broadcasted_iota(jnp.int32, (q, LANE), 1)

    for g in range(SSD_GROUPS):
        cg = cm[:, g * SSD_STATE:(g + 1) * SSD_STATE].astype(BF16)
        bg = bm[:, g * SSD_STATE:(g + 1) * SSD_STATE]
        cb = _dot_nt(cg, bg.astype(BF16))
        bg_t = bg.T.astype(BF16)
        for jp in range(n_pairs // SSD_GROUPS):
            j = g * (n_pairs // SSD_GROUPS) + jp
            sl = slice(j * LANE, (j + 1) * LANE)
            xdt_p = xdt[:, sl]
            ys = []
            for hh in (2 * j, 2 * j + 1):
                diff = cs[:, hh:hh + 1] - cs_t[hh:hh + 1, :]
                lmat = jnp.exp(jnp.where(tri, diff, NEG))
                ys.append(jnp.dot((cb * lmat).astype(BF16), xdt_p, preferred_element_type=F32))
            y_diag = jnp.where(lane < HALF, ys[0], ys[1])
            st = st_sc[j]
            y_off = jnp.dot(cg, st.astype(BF16), preferred_element_type=F32) * ecs_e[:, sl]
            y_ref[0, :, sl] = y_diag + y_off + xs[:, sl] * dsk_ref[:, sl]
            new = jnp.dot(bg_t, xw[:, sl], preferred_element_type=F32)
            st_sc[j] = st * tot_e[:, sl] + new

    @pl.when(c == pl.num_programs(1) - 1)
    def _():
        for j in range(n_pairs):
            hfin_ref[0, j * LANE:(j + 1) * LANE, :] = st_sc[j].T


def _head_expand():
    e = np.zeros((LANE, SSD_WIDTH), np.float32)
    for h in range(SSD_HEADS):
        e[h, h * HEAD_DIM:(h + 1) * HEAD_DIM] = 1.0
    return jnp.asarray(e)


def _pad_lanes(v):
    return jnp.pad(v.astype(F32), (0, LANE - v.shape[0])).reshape(1, LANE)


def _ssd_prompt(xbc, dt, conv_w, conv_b, dt_bias, a_log, d_skip):
    b, l, _ = xbc.shape
    nc = l // SSD_CHUNK
    full = lambda shape: pl.BlockSpec(shape, lambda i, c: (0,) * len(shape))
    y, hfin = pl.pallas_call(
        _ssd_kernel,
        out_shape=(jax.ShapeDtypeStruct((b, l, SSD_WIDTH), F32),
                   jax.ShapeDtypeStruct((b, SSD_HEADS * HEAD_DIM, SSD_STATE), F32)),
        grid=(b, nc),
        in_specs=[pl.BlockSpec((1, SSD_CHUNK, CONV_DIM), lambda i, c: (i, c, 0)),
                  pl.BlockSpec((1, SSD_CHUNK, LANE), lambda i, c: (i, c, 0)),
                  full((SSD_CONV, CONV_DIM)), full((1, CONV_DIM)), full((1, LANE)), full((1, LANE)),
                  full((1, SSD_WIDTH)), full((LANE, SSD_WIDTH)), full((SSD_CHUNK, SSD_CHUNK))],
        out_specs=(pl.BlockSpec((1, SSD_CHUNK, SSD_WIDTH), lambda i, c: (i, c, 0)),
                   pl.BlockSpec((1, SSD_HEADS * HEAD_DIM, SSD_STATE), lambda i, c: (i, 0, 0))),
        scratch_shapes=[pltpu.VMEM((SSD_CHUNK + 8, CONV_DIM), F32),
                        pltpu.VMEM((SSD_HEADS // 2, SSD_STATE, LANE), F32)],
        compiler_params=pltpu.CompilerParams(dimension_semantics=("arbitrary", "arbitrary"),
                                             vmem_limit_bytes=VMEM_LIMIT),
        name="ssd_prompt",
    )(xbc, dt, conv_w, conv_b.reshape(1, CONV_DIM), _pad_lanes(dt_bias), _pad_lanes(a_log),
      jnp.repeat(d_skip.astype(F32), HEAD_DIM).reshape(1, SSD_WIDTH), _head_expand().astype(BF16),
      jnp.asarray(np.tril(np.ones((SSD_CHUNK, SSD_CHUNK), np.float32)), BF16))
    return y, hfin.reshape(b, SSD_HEADS, HEAD_DIM, SSD_STATE)


def _ssd_step_kernel(xbc_ref, c0_ref, c1_ref, c2_ref, dt_ref, h0_ref, cw_ref, cb_ref, dtb_ref, alog_ref,
                     dsk_ref, e_ref, y_ref, hout_ref, xt_sc, dect_sc, bc_sc, yt_sc, xs_sc):
    b = pl.program_id(0)
    nb = xbc_ref.shape[0]
    rows = SSD_HEADS * HEAD_DIM
    gn = SSD_GROUPS * SSD_STATE

    @pl.when(b == 0)
    def _():
        acc = (cb_ref[...] + cw_ref[0:1, :] * c0_ref[...] + cw_ref[1:2, :] * c1_ref[...]
               + cw_ref[2:3, :] * c2_ref[...] + cw_ref[3:4, :] * xbc_ref[...])
        u = _silu(acc)
        xs = u[:, :SSD_WIDTH]
        dt = _softplus(dt_ref[...] + dtb_ref[...])
        dec = jnp.exp(dt * (-jnp.exp(alog_ref[...])))
        e = e_ref[...]
        xdt = xs * _dot32(dt, e)
        dec_e = _dot32(dec, e)
        pad = jnp.zeros((LANE - nb, SSD_WIDTH), F32)
        xt_sc[...] = jnp.concatenate([xdt, pad], axis=0).T
        dect_sc[...] = jnp.concatenate([dec_e, pad], axis=0).T
        bc_sc[...] = u[:, SSD_WIDTH:]
        xs_sc[...] = xs
        yt_sc[...] = jnp.zeros(yt_sc.shape, F32)

    ri = lax.broadcasted_iota(jnp.int32, (LANE, LANE), 0)
    onehot = jnp.where(ri == b, 1.0, 0.0)
    xcol = _dot32(xt_sc[...], onehot)
    dcol = _dot32(dect_sc[...], onehot)
    bc = bc_sc[pl.ds(b, 1), :]
    row = lax.broadcasted_iota(jnp.int32, (rows, SSD_STATE), 0)
    first = row < rows // SSD_GROUPS
    b_full = jnp.where(first, bc[:, 0:SSD_STATE], bc[:, SSD_STATE:gn])
    c_full = jnp.where(first, bc[:, gn:gn + SSD_STATE], bc[:, gn + SSD_STATE:])
    new = dcol * h0_ref[0] + xcol * b_full
    hout_ref[0] = new
    ycol = _dot32(new * c_full, jnp.ones((SSD_STATE, LANE), F32))
    lane = lax.broadcasted_iota(jnp.int32, (rows, LANE), 1)
    yt_sc[...] = jnp.where(lane == b, ycol, yt_sc[...])

    @pl.when(b == nb - 1)
    def _():
        y_ref[...] = yt_sc[...].T[0:nb, :] + xs_sc[...] * dsk_ref[...]


def _ssd_step(xbc, conv_state, dt, h0, conv_w, conv_b, dt_bias, a_log, d_skip):
    nb = xbc.shape[0]
    rows = SSD_HEADS * HEAD_DIM
    full = lambda shape: pl.BlockSpec(shape, lambda i: (0,) * len(shape))
    y, hout = pl.pallas_call(
        _ssd_step_kernel,
        out_shape=(jax.ShapeDtypeStruct((nb, SSD_WIDTH), F32),
                   jax.ShapeDtypeStruct((nb, rows, SSD_STATE), F32)),
        grid=(nb,),
        in_specs=[full((nb, CONV_DIM)), full((nb, CONV_DIM)), full((nb, CONV_DIM)), full((nb, CONV_DIM)),
                  full((nb, LANE)),
                  pl.BlockSpec((1, rows, SSD_STATE), lambda i: (i, 0, 0)),
                  full((SSD_CONV, CONV_DIM)), full((1, CONV_DIM)), full((1, LANE)), full((1, LANE)),
                  full((1, SSD_WIDTH)), full((LANE, SSD_WIDTH))],
        out_specs=(full((nb, SSD_WIDTH)),
                   pl.BlockSpec((1, rows, SSD_STATE), lambda i: (i, 0, 0))),
        scratch_shapes=[pltpu.VMEM((rows, LANE), F32), pltpu.VMEM((rows, LANE), F32),
                        pltpu.VMEM((nb, 2 * SSD_GROUPS * SSD_STATE), F32),
                        pltpu.VMEM((rows, LANE), F32), pltpu.VMEM((nb, SSD_WIDTH), F32)],
        compiler_params=pltpu.CompilerParams(dimension_semantics=("arbitrary",)),
        name="ssd_step",
    )(xbc, conv_state[:, 0], conv_state[:, 1], conv_state[:, 2], dt, h0.reshape(nb, rows, SSD_STATE),
      conv_w, conv_b.reshape(1, CONV_DIM), _pad_lanes(dt_bias), _pad_lanes(a_log),
      jnp.repeat(d_skip.astype(F32), HEAD_DIM).reshape(1, SSD_WIDTH), _head_expand())
    return y, hout.reshape(nb, SSD_HEADS, HEAD_DIM, SSD_STATE)


def _compress_kernel(pt_ref, *refs, transposed, CMP_PAGES):
    if transposed:
        w1_ref, pe_ref, w2_ref, out_ref, sh_sc, pe_sc, xs_sc = refs[CMP_PAGES:]
        for i in range(CMP_PAGES):
            for k in range(2):
                xs_sc[i, k] = refs[i][0, k * LANE:(k + 1) * LANE, :].T
        pages = tuple(tuple(xs_sc.at[i, k] for i in range(CMP_PAGES)) for k in range(2))
    else:
        w1_ref, pe_ref, w2_ref, out_ref, sh_sc, pe_sc = refs[2 * CMP_PAGES:]
        pages = tuple(tuple(refs[k * CMP_PAGES + i].at[0] for i in range(CMP_PAGES)) for k in range(2))
    s = pl.program_id(1)
    segs = PAGE_SIZE // CMP_STRIDE
    rows = CMP_PAGES * segs
    hid2 = NSA_KV_HEADS * CMP_HID

    @pl.when(s == 0)
    def _():
        sh_sc[:, 0:8, :] = jnp.zeros((2, 8, hid2), F32)
        for k in range(2):
            t = jnp.zeros((8, 2 * hid2), F32)
            for o in range(CMP_STRIDE):
                t = t + jnp.dot(pe_ref[o, k].astype(BF16), w1_ref[o, k], preferred_element_type=F32)
            pe_sc[k] = jnp.broadcast_to(t[0:1, 0:hid2] + t[1:2, hid2:], (8, hid2))

    for k in range(2):
        acc = jnp.zeros((rows, 2 * hid2), F32)
        for o in range(CMP_STRIDE):
            xo = jnp.concatenate([p[pl.ds(o, segs, stride=CMP_STRIDE), :] for p in pages[k]], axis=0)
            acc = acc + jnp.dot(xo.astype(BF16), w1_ref[o, k], preferred_element_type=F32)
        sh_sc[k, 8:8 + rows, :] = acc[:, 0:hid2]
        pre = acc[:, hid2:] + sh_sc[k, 7:7 + rows, :] + pe_sc[k, 0:1, :]
        sh_sc[k, 0:8, :] = sh_sc[k, rows:rows + 8, :]
        out_ref[0, :, k * LANE:(k + 1) * LANE] = jnp.dot(_silu(pre).astype(BF16), w2_ref[k],
                                                         preferred_element_type=F32)


def _compress_weights(cmp_pe, cmp_w1, cmp_w2):
    span = CMP_BLOCK // CMP_STRIDE
    w1s = cmp_w1.astype(F32).reshape(2, span, CMP_STRIDE, HEAD_DIM, CMP_HID)
    z = jnp.zeros((2, span, CMP_STRIDE, HEAD_DIM, CMP_HID), F32)
    top = jnp.concatenate([w1s, z], axis=-1)
    bot = jnp.concatenate([z, w1s], axis=-1)
    bd = jnp.concatenate([top, bot], axis=-2)
    w1 = jnp.transpose(bd, (2, 0, 3, 1, 4)).reshape(CMP_STRIDE, 2, LANE, span * 2 * CMP_HID).astype(BF16)
    pe = cmp_pe.astype(F32).reshape(2, span, CMP_STRIDE, HEAD_DIM)
    pe = jnp.transpose(pe, (2, 0, 1, 3))
    pe = jnp.concatenate([pe, pe], axis=-1)
    pe = jnp.pad(pe, ((0, 0), (0, 0), (0, 8 - span), (0, 0)))
    w2 = cmp_w2.astype(F32)
    z2 = jnp.zeros_like(w2)
    w2bd = jnp.concatenate([jnp.concatenate([w2, z2], axis=-1), jnp.concatenate([z2, w2], axis=-1)],
                           axis=-2).astype(BF16)
    return w1, pe, w2bd


def _compress(pages_arr, page_ids, cweights, transposed):
    nb, n_pages = page_ids.shape
    CMP_PAGES = math.gcd(n_pages, CMP_PAGES_MAX)
    steps = n_pages // CMP_PAGES
    segs = PAGE_SIZE // CMP_STRIDE
    rows = CMP_PAGES * segs
    w1, pe, w2bd = cweights
    hid2 = NSA_KV_HEADS * CMP_HID
    page_of = lambda b, s, pt, i: pt[(b * steps + s) * CMP_PAGES + i]
    scratch = [pltpu.VMEM((2, rows + 8, hid2), F32), pltpu.VMEM((2, 8, hid2), F32)]
    if transposed:
        page_specs = [pl.BlockSpec((1, KV_COLS, PAGE_SIZE), lambda b, s, pt, i=i: (page_of(b, s, pt, i), 0, 0))
                      for i in range(CMP_PAGES)]
        scratch.append(pltpu.VMEM((CMP_PAGES, 2, PAGE_SIZE, LANE), F32))
    else:
        page_specs = [pl.BlockSpec((1, PAGE_SIZE, LANE), lambda b, s, pt, i=i, k=k: (page_of(b, s, pt, i), 0, k))
                      for k in range(2) for i in range(CMP_PAGES)]
    full = lambda shape: pl.BlockSpec(shape, lambda b, s, pt: (0,) * len(shape))
    return pl.pallas_call(
        functools.partial(_compress_kernel, transposed=transposed, CMP_PAGES=CMP_PAGES),
        out_shape=jax.ShapeDtypeStruct((nb, n_pages * segs, KV_COLS), F32),
        grid_spec=pltpu.PrefetchScalarGridSpec(
            num_scalar_prefetch=1,
            grid=(nb, steps),
            in_specs=page_specs + [full(w1.shape), full(pe.shape), full(w2bd.shape)],
            out_specs=pl.BlockSpec((1, rows, KV_COLS), lambda b, s, pt: (b, s, 0)),
            scratch_shapes=scratch),
        compiler_params=pltpu.CompilerParams(dimension_semantics=("arbitrary", "arbitrary"),
                                             vmem_limit_bytes=VMEM_LIMIT),
        name="nsa_compress",
    )(page_ids.reshape(-1), *([pages_arr] * len(page_specs)), w1, pe, w2bd)


def _overlap_matrix(n_rows, n_blocks, n_cols):
    m = np.arange(n_rows)[:, None]
    j = np.arange(n_cols)[None, :]
    cs = (m - 1) * CMP_STRIDE
    ov = (m >= 1) & (j < n_blocks) & (cs < j * SLC_BLOCK + SLC_BLOCK) & (cs + CMP_BLOCK > j * SLC_BLOCK)
    return jnp.asarray(ov.astype(np.float32))


def _stack_q(q, g):
    return _stack_q_f32(q, g).astype(BF16)


def _stack_q_f32(q, g):
    lane = lax.broadcasted_iota(jnp.int32, (Q_BLOCK, LANE), 1)
    keep = (lane < HALF) if g == 0 else (lane >= HALF)
    parts = []
    for jp in range(NSA_REP // 2):
        j = g * (NSA_REP // 2) + jp
        slab = q[:, j * LANE:(j + 1) * LANE]
        rolled = pltpu.roll(slab, HALF, 1)
        first, second = (slab, rolled) if g == 0 else (rolled, slab)
        parts.append(jnp.where(keep, first, 0.0))
        parts.append(jnp.where(keep, second, 0.0))
    return jnp.concatenate(parts, axis=0)


def _unstack_o(acc, g):
    lane = lax.broadcasted_iota(jnp.int32, (Q_BLOCK, LANE), 1)
    outs = []
    for jp in range(NSA_REP // 2):
        a = acc[(2 * jp) * Q_BLOCK:(2 * jp + 1) * Q_BLOCK]
        b = acc[(2 * jp + 1) * Q_BLOCK:(2 * jp + 2) * Q_BLOCK]
        if g == 0:
            outs.append(jnp.where(lane < HALF, a, pltpu.roll(b, HALF, 1)))
        else:
            outs.append(jnp.where(lane < HALF, pltpu.roll(a, HALF, 1), b))
    return jnp.concatenate(outs, axis=1)


def _tile8(x):
    return jnp.concatenate([x] * NSA_REP, axis=0)


def _rank_rows(imp, n_valid):
    sub = 8
    chunks = [imp[c:c + sub] for c in range(0, imp.shape[0], sub)]
    ranks = [jnp.zeros(ch.shape, F32) for ch in chunks]
    jrow = lax.broadcasted_iota(jnp.int32, chunks[0].shape, 0)
    for k in range(n_valid):
        rk = imp[k:k + 1, :]
        for c, ch in enumerate(chunks):
            if c * sub > k:
                ahead = rk >= ch
            elif c * sub + sub - 1 < k:
                ahead = rk > ch
            else:
                ahead = (rk > ch) | ((rk == ch) & (jrow + c * sub > k))
            ranks[c] = ranks[c] + jnp.where(ahead, 1.0, 0.0)
    return jnp.concatenate(ranks, axis=0)


def _cmp_kernel(q_ref, kv_ref, pb_ref, ov_ref, o_ref, sel_ref, bias_sc, *, n_keys, n_blocks):
    i = pl.program_id(0)

    @pl.when(pl.program_id(1) == 0)
    def _():
        for h in range(NSA_HEADS):
            bias_sc[h] = pltpu.roll(pb_ref[h], (4 * i + 4) % n_keys, 1)

    q = q_ref[0]
    kv = kv_ref[0]
    kc = kv[:, 0:LANE].astype(BF16)
    vc = kv[:, LANE:].astype(BF16)
    qi = lax.broadcasted_iota(jnp.int32, (Q_BLOCK, n_keys), 0)
    mi = lax.broadcasted_iota(jnp.int32, (Q_BLOCK, n_keys), 1)
    qpos = Q_BLOCK * i + qi
    valid8 = _tile8((mi >= 1) & (CMP_STRIDE * mi + CMP_STRIDE - 1 <= qpos))
    rowvalid8 = _tile8(jnp.where(qpos[:, 0:1] >= CMP_BLOCK - 1, 1.0, 0.0))
    jj = lax.broadcasted_iota(jnp.int32, (LANE, LANE), 1)
    psum = []
    for g in range(NSA_KV_HEADS):
        s = _dot_nt(_stack_q(q, g), kc)
        s = jnp.where(valid8, s + bias_sc[g * NSA_REP:(g + 1) * NSA_REP].reshape(ROWS, n_keys), NEG)
        p = jnp.exp(s - jnp.max(s, axis=-1, keepdims=True))
        pc = p / jnp.sum(p, axis=-1, keepdims=True) * rowvalid8
        o_ref[0, :, g * GROUP_W:(g + 1) * GROUP_W] = _unstack_o(
            jnp.dot(pc.astype(BF16), vc, preferred_element_type=F32), g)
        ps = pc[0:Q_BLOCK]
        for r in range(1, NSA_REP):
            ps = ps + pc[r * Q_BLOCK:(r + 1) * Q_BLOCK]
        psum.append(ps)
    ps = jnp.concatenate(psum, axis=0)
    hi = ps.astype(BF16)
    lo = (ps - hi.astype(F32)).astype(BF16)
    ov = ov_ref[...].astype(BF16)
    imp = (jnp.dot(hi, ov, preferred_element_type=F32)
           + jnp.dot(lo, ov, preferred_element_type=F32))
    forced = (jj == 0) | (jj == i) | (jj == i - 1)
    imp = jnp.where(forced, 1e6, jnp.where(jj <= i, imp, -1e6))
    imp = jnp.where(jj < n_blocks, imp, -2e6)
    rank = _rank_rows(imp.T, n_blocks)
    sel_ref[0, 0] = jnp.where(rank < SLC_TOP, 1.0, 0.0).astype(BF16)


def _cmp_prompt(q, kvc_cmp, pattern, ov):
    b, l, _ = q.shape
    n_keys = kvc_cmp.shape[1]
    n_blocks = l // SLC_BLOCK
    return pl.pallas_call(
        functools.partial(_cmp_kernel, n_keys=n_keys, n_blocks=n_blocks),
        out_shape=(jax.ShapeDtypeStruct((b, l, NSA_WIDTH), F32),
                   jax.ShapeDtypeStruct((b, l // Q_BLOCK, LANE, LANE), BF16)),
        grid=(l // Q_BLOCK, b),
        in_specs=[pl.BlockSpec((1, Q_BLOCK, NSA_WIDTH), lambda i, bi: (bi, i, 0)),
                  pl.BlockSpec((1, n_keys, KV_COLS), lambda i, bi: (bi, 0, 0)),
                  pl.BlockSpec(pattern.shape, lambda i, bi: (0, 0, 0)),
                  pl.BlockSpec(ov.shape, lambda i, bi: (0, 0))],
        out_specs=(pl.BlockSpec((1, Q_BLOCK, NSA_WIDTH), lambda i, bi: (bi, i, 0)),
                   pl.BlockSpec((1, 1, LANE, LANE), lambda i, bi: (bi, i, 0, 0))),
        scratch_shapes=[pltpu.VMEM(pattern.shape, F32)],
        compiler_params=pltpu.CompilerParams(dimension_semantics=("arbitrary", "arbitrary"),
                                             vmem_limit_bytes=VMEM_LIMIT),
        name="nsa_cmp_prompt",
    )(q, kvc_cmp, pattern, ov)


def _flash_update_t(st, vt, m, acc):
    m_new = jnp.maximum(m, jnp.max(st, axis=0, keepdims=True))
    p = jnp.exp2(st - m_new)
    acc = jnp.exp2(m - m_new) * acc + jnp.dot(vt, p.astype(BF16), preferred_element_type=F32)
    return m_new, acc


def _finish_t(acc):
    o_t = acc[0:HEAD_DIM] / acc[HEAD_DIM:HEAD_DIM + 1]
    return _unstack_o(jnp.concatenate([o_t, jnp.zeros_like(o_t)], axis=0).T, 0)


def _slc_kernel(q_ref, sel_ref, kx_ref, vt_ref, tnt_ref, tile_ref, o_ref, far_sc, near_sc, sta_sc, stb_sc):
    i = pl.program_id(1)
    q = q_ref[0] * LOG2E
    n_far = (jnp.maximum(i - 3, 0) + 3) // 4
    sel = sel_ref[0, 0]
    jrow = lax.broadcasted_iota(jnp.int32, (LANE, ROWS), 0)
    a = ((i + 1) // 2) * LANE
    delta = Q_BLOCK * (i + 1) - a
    row = lax.broadcasted_iota(jnp.int32, (SLC_NEARW, ROWS), 0)
    qi = lax.broadcasted_iota(jnp.int32, (SLC_NEARW, ROWS), 1) % Q_BLOCK
    rel = row - delta - (SLC_NEAR - Q_BLOCK)
    near_ok = (rel >= -(SLC_NEAR - Q_BLOCK)) & (rel <= qi) & (Q_BLOCK * i + rel >= 0)
    groups = range(NSA_KV_HEADS)
    for g in groups:
        qs_t = _stack_q_f32(q, g).T.astype(BF16)
        hit = jnp.dot(sel, tile_ref[g], preferred_element_type=F32) > 0.5
        near_sc[g] = jnp.concatenate([qs_t, jnp.where(hit, 0.0, NEG).astype(BF16)], axis=0)
        far_sc[g] = jnp.concatenate([qs_t, jnp.where(hit & (jrow < i - 3), 0.0, NEG).astype(BF16)], axis=0)

    def far_scores(t, dst):
        start = pl.multiple_of(SLC_PAD + SLC_NEAR * t, SLC_NEAR)
        kx = kx_ref[0, pl.ds(start, SLC_NEAR), :]
        for g in groups:
            dst[g] = jnp.dot(kx, far_sc[g], preferred_element_type=F32)

    def far_softmax(t, src, carry):
        start = pl.multiple_of(SLC_PAD + SLC_NEAR * t, SLC_NEAR)
        return tuple(_flash_update_t(src[g], vt_ref[0, g, :, pl.ds(start, SLC_NEAR)], *carry[g]) for g in groups)

    def far_pair(u, carry):
        far_scores(2 * u + 1, stb_sc)
        carry = far_softmax(2 * u, sta_sc, carry)
        far_scores(2 * u + 2, sta_sc)
        return far_softmax(2 * u + 1, stb_sc, carry)

    far_scores(0, sta_sc)
    init = (jnp.full((1, ROWS), NEG, F32), jnp.zeros((HEAD_DIM + ONES_ROWS, ROWS), F32))
    carry = lax.fori_loop(0, (n_far + 1) // 2, far_pair, (init, init))

    start = pl.multiple_of(a, LANE)
    kx = kx_ref[0, pl.ds(start, SLC_NEARW), :]
    for g in groups:
        st = jnp.dot(kx, near_sc[g], preferred_element_type=F32) + tnt_ref[delta // Q_BLOCK, g]
        _, acc = _flash_update_t(jnp.where(near_ok, st, NEG), vt_ref[0, g, :, pl.ds(start, SLC_NEARW)], *carry[g])
        o_ref[0, :, g * GROUP_W:(g + 1) * GROUP_W] = _finish_t(acc)


def _slc_prompt(q, sel, kx_pad, vt_pad, tnt, tile):
    b, l, _ = q.shape
    lp = kx_pad.shape[1]
    return pl.pallas_call(
        _slc_kernel,
        out_shape=jax.ShapeDtypeStruct((b, l, NSA_WIDTH), F32),
        grid=(b, l // Q_BLOCK),
        in_specs=[pl.BlockSpec((1, Q_BLOCK, NSA_WIDTH), lambda bi, i: (bi, i, 0)),
                  pl.BlockSpec((1, 1, LANE, LANE), lambda bi, i: (bi, i, 0, 0)),
                  pl.BlockSpec((1, lp, 2 * LANE), lambda bi, i: (bi, 0, 0)),
                  pl.BlockSpec((1, NSA_KV_HEADS, HEAD_DIM + ONES_ROWS, lp), lambda bi, i: (bi, 0, 0, 0)),
                  pl.BlockSpec(tnt.shape, lambda bi, i: (0, 0, 0, 0)),
                  pl.BlockSpec(tile.shape, lambda bi, i: (0, 0, 0))],
        out_specs=pl.BlockSpec((1, Q_BLOCK, NSA_WIDTH), lambda bi, i: (bi, i, 0)),
        scratch_shapes=[pltpu.VMEM((NSA_KV_HEADS, 2 * LANE, ROWS), BF16),
                        pltpu.VMEM((NSA_KV_HEADS, 2 * LANE, ROWS), BF16),
                        pltpu.VMEM((NSA_KV_HEADS, SLC_NEAR, ROWS), F32),
                        pltpu.VMEM((NSA_KV_HEADS, SLC_NEAR, ROWS), F32)],
        compiler_params=pltpu.CompilerParams(vmem_limit_bytes=VMEM_LIMIT),
        name="nsa_slc_prompt",
    )(q, sel, kx_pad, vt_pad, tnt, tile)


def _group_tile():
    t = np.zeros((NSA_KV_HEADS, LANE, ROWS), np.float32)
    for g in range(NSA_KV_HEADS):
        for r in range(NSA_REP):
            for qq in range(Q_BLOCK):
                t[g, g * Q_BLOCK + qq, r * Q_BLOCK + qq] = 1.0
    return jnp.asarray(t, BF16)


def _win_kernel(q_ref, kx_ref, vt_ref, tw_ref, o_ref, st_sc):
    i2 = pl.program_id(1)
    q = q_ref[0] * LOG2E
    start = pl.multiple_of(i2 * LANE, LANE)
    kx = kx_ref[0, pl.ds(start, WIN_W), :]
    groups = range(NSA_KV_HEADS)
    row = lax.broadcasted_iota(jnp.int32, (LANE, 2 * ROWS), 0)
    pad_rows = jnp.where(row == 0, NEG, 0.0).astype(BF16)
    for g in groups:
        qs_t = jnp.concatenate([_stack_q_f32(q[h * Q_BLOCK:(h + 1) * Q_BLOCK], g).T for h in range(2)], axis=1)
        qx = jnp.concatenate([qs_t.astype(BF16), pad_rows], axis=0)
        st_sc[g] = jnp.dot(kx, qx, preferred_element_type=F32)
    for g in groups:
        st = st_sc[g] + tw_ref[g]
        p = jnp.exp2(st - jnp.max(st, axis=0, keepdims=True))
        acc = jnp.dot(vt_ref[0, g, :, pl.ds(start, WIN_W)], p.astype(BF16), preferred_element_type=F32)
        for h in range(2):
            o_ref[0, h * Q_BLOCK:(h + 1) * Q_BLOCK, g * GROUP_W:(g + 1) * GROUP_W] = _finish_t(
                acc[:, h * ROWS:(h + 1) * ROWS])


def _win_prompt(q, kx_pad, vt_pad, tw):
    b, l, _ = q.shape
    lp = kx_pad.shape[1]
    tq = 2 * Q_BLOCK
    return pl.pallas_call(
        _win_kernel,
        out_shape=jax.ShapeDtypeStruct((b, l, NSA_WIDTH), F32),
        grid=(b, l // tq),
        in_specs=[pl.BlockSpec((1, tq, NSA_WIDTH), lambda bi, i: (bi, i, 0)),
                  pl.BlockSpec((1, lp, 2 * LANE), lambda bi, i: (bi, 0, 0)),
                  pl.BlockSpec((1, NSA_KV_HEADS, HEAD_DIM + ONES_ROWS, lp), lambda bi, i: (bi, 0, 0, 0)),
                  pl.BlockSpec(tw.shape, lambda bi, i: (0, 0, 0))],
        out_specs=pl.BlockSpec((1, tq, NSA_WIDTH), lambda bi, i: (bi, i, 0)),
        scratch_shapes=[pltpu.VMEM((NSA_KV_HEADS, WIN_W, 2 * ROWS), F32)],
        compiler_params=pltpu.CompilerParams(vmem_limit_bytes=VMEM_LIMIT),
        name="nsa_win_prompt",
    )(q, kx_pad, vt_pad, tw)


def _dense1_kernel(q_ref, k_ref, v_ref, bm_ref, o_ref, ps_ref):
    k = k_ref[0].astype(BF16)
    v = v_ref[0].astype(BF16)
    for g in range(NSA_KV_HEADS):
        rows = slice(g * NSA_REP, (g + 1) * NSA_REP)
        s = _dot_nt(q_ref[0, rows, :], k) + bm_ref[rows, :]
        p = jnp.exp(s - jnp.max(s, axis=-1, keepdims=True))
        pc = p / jnp.sum(p, axis=-1, keepdims=True)
        o_ref[0, rows, :] = jnp.dot(pc.astype(BF16), v, preferred_element_type=F32)
        ps_ref[0, g:g + 1, :] = jnp.sum(pc, axis=0, keepdims=True)


def _dense1(q_pad, kv, biasmask):
    nb, n, _ = kv.shape
    return pl.pallas_call(
        _dense1_kernel,
        out_shape=(jax.ShapeDtypeStruct((nb, NSA_HEADS, LANE), F32),
                   jax.ShapeDtypeStruct((nb, NSA_KV_HEADS, n), F32)),
        grid=(nb,),
        in_specs=[pl.BlockSpec((1, NSA_HEADS, LANE), lambda b: (b, 0, 0)),
                  pl.BlockSpec((1, n, LANE), lambda b: (b, 0, 0)),
                  pl.BlockSpec((1, n, LANE), lambda b: (b, 0, 1)),
                  pl.BlockSpec((NSA_HEADS, n), lambda b: (0, 0))],
        out_specs=(pl.BlockSpec((1, NSA_HEADS, LANE), lambda b: (b, 0, 0)),
                   pl.BlockSpec((1, NSA_KV_HEADS, n), lambda b: (b, 0, 0))),
        name="nsa_dense_sample",
    )(q_pad, kv, kv, biasmask)


def _take_group_half(o_pad):
    nb = o_pad.shape[0]
    o = o_pad.reshape(nb, NSA_KV_HEADS, NSA_REP, NSA_KV_HEADS, HEAD_DIM)
    o = jnp.stack([o[:, g, :, g, :] for g in range(NSA_KV_HEADS)], axis=1)
    return o.reshape(nb, NSA_WIDTH)


def _topk_kernel(ps_ref, ov_ref, idx_ref, imp_sc, *, n_blocks, cur):
    n_rows = ps_ref.shape[0]
    n_cols = ov_ref.shape[1]
    ps = jnp.concatenate([ps_ref[...], jnp.zeros((LANE - n_rows, ps_ref.shape[1]), F32)], axis=0)
    imp = _dot32(ps, ov_ref[...])
    jj = lax.broadcasted_iota(jnp.int32, (LANE, n_cols), 1)
    forced = (jj == 0) | (jj == cur) | (jj == cur - 1)
    imp = jnp.where(forced, 1e6, jnp.where(jj <= cur, imp, -1e6))
    imp = jnp.where(jj < n_blocks, imp, -2e6)
    imp_t = imp.T
    imp_sc[...] = imp_t
    jrow = lax.broadcasted_iota(jnp.int32, (n_cols, LANE), 0)

    def body(k, rank):
        rk = imp_sc[pl.ds(k, 1), :]
        ahead = (rk > imp_t) | ((rk == imp_t) & (jrow > k))
        return rank + jnp.where(ahead, 1.0, 0.0)

    rank = lax.fori_loop(0, n_blocks, body, jnp.zeros((n_cols, LANE), F32))
    jf = jrow.astype(F32)
    rows = [jnp.sum(jnp.where(rank == float(r), jf, 0.0), axis=0, keepdims=True) for r in range(SLC_TOP)]
    idx_ref[...] = jnp.concatenate(rows, axis=0).astype(jnp.int32)


def _topk_sample(psum, ov, n_blocks, cur):
    n_rows, n_keys = psum.shape
    n_cols = ov.shape[1]
    return pl.pallas_call(
        functools.partial(_topk_kernel, n_blocks=n_blocks, cur=cur),
        out_shape=jax.ShapeDtypeStruct((SLC_TOP, LANE), jnp.int32),
        grid=(1,),
        in_specs=[pl.BlockSpec((n_rows, n_keys), lambda i: (0, 0)),
                  pl.BlockSpec(ov.shape, lambda i: (0, 0))],
        out_specs=pl.BlockSpec((SLC_TOP, LANE), lambda i: (0, 0)),
        scratch_shapes=[pltpu.VMEM((n_cols, LANE), F32)],
        name="nsa_topk_sample",
    )(psum, ov)


def _slc1_kernel(phys_ref, jsel_ref, q_ref, *refs, cur, past):
    pages = refs[:SLC_TOP]
    new_ref, bb_ref, o_ref = refs[SLC_TOP:]
    b = pl.program_id(0)
    g = pl.program_id(1)
    goff = pl.multiple_of(g * HEAD_DIM, HEAD_DIM)
    q = q_ref[0]
    lane = lax.broadcasted_iota(jnp.int32, (NSA_REP, PAGE_SIZE), 1)
    first = lax.broadcasted_iota(jnp.int32, (HEAD_DIM, PAGE_SIZE), 1) == 0
    new_k = jnp.where(first, new_ref[0, pl.ds(goff, HEAD_DIM), :], 0.0)
    new_v = jnp.where(first, new_ref[0, pl.ds(LANE + goff, HEAD_DIM), :], 0.0)
    scores, values = [], []
    for n in range(SLC_TOP):
        j = jsel_ref[(b * NSA_KV_HEADS + g) * SLC_TOP + n]
        kt = jnp.where(j == cur, new_k, pages[n][0, pl.ds(goff, HEAD_DIM), :])
        vt = jnp.where(j == cur, new_v, pages[n][0, pl.ds(LANE + goff, HEAD_DIM), :])
        s = jnp.dot(q, kt.astype(BF16), preferred_element_type=F32)
        ok = (lane // SLC_BLOCK == j % 2) & ((j // 2) * PAGE_SIZE + lane <= past)
        bias = bb_ref[jnp.clip(j - (cur - 3), 0, 3), pl.ds(pl.multiple_of(g * NSA_REP, NSA_REP), NSA_REP), :]
        scores.append(jnp.where(ok, s + bias, NEG))
        values.append(vt.astype(BF16))
    s_all = jnp.concatenate(scores, axis=1)
    p = jnp.exp(s_all - jnp.max(s_all, axis=-1, keepdims=True))
    acc = jnp.zeros((NSA_REP, HEAD_DIM), F32)
    for n in range(SLC_TOP):
        acc = acc + _dot_nt(p[:, n * PAGE_SIZE:(n + 1) * PAGE_SIZE].astype(BF16), values[n])
    o_ref[0] = acc / jnp.sum(p, axis=-1, keepdims=True)


def _slc_sample(q, cache_pages_t, phys, jsel, new_cols, biasblk, cur, past):
    nb = q.shape[0]
    idx = lambda b, g, n: (b * NSA_KV_HEADS + g) * SLC_TOP + n
    page_specs = [pl.BlockSpec((1, KV_COLS, PAGE_SIZE), lambda b, g, ph, js, n=n: (ph[idx(b, g, n)], 0, 0))
                  for n in range(SLC_TOP)]
    return pl.pallas_call(
        functools.partial(_slc1_kernel, cur=cur, past=past),
        out_shape=jax.ShapeDtypeStruct((nb, NSA_HEADS, HEAD_DIM), F32),
        grid_spec=pltpu.PrefetchScalarGridSpec(
            num_scalar_prefetch=2,
            grid=(nb, NSA_KV_HEADS),
            in_specs=[pl.BlockSpec((1, NSA_REP, HEAD_DIM), lambda b, g, ph, js: (b, g, 0))] + page_specs
            + [pl.BlockSpec((1, KV_COLS, 1), lambda b, g, ph, js: (b, 0, 0)),
               pl.BlockSpec(biasblk.shape, lambda b, g, ph, js: (0, 0, 0))],
            out_specs=pl.BlockSpec((1, NSA_REP, HEAD_DIM), lambda b, g, ph, js: (b, g, 0))),
        name="nsa_slc_sample",
    )(phys, jsel, q, *([cache_pages_t] * SLC_TOP), new_cols, biasblk)


def _out_kernel(x_ref, gate_ref, yssd_ref, zs_ref, oc_ref, os_ref, ow_ref, gl_ref, za_ref,
                nw1_ref, nw2_ref, w_ref, eg_ref, fg_ref, o_ref, *, per_row, final):
    gates = _sigmoid(gl_ref[...])
    gparts = _split_bf16(gates, 2)
    expand = lambda br: sum(jnp.dot(p, eg_ref[br], preferred_element_type=F32) for p in gparts)
    y_nsa = expand(0) * oc_ref[...] + expand(1) * os_ref[...] + expand(2) * ow_ref[...]

    def gated_norm(y, z, w):
        u = y * _silu(z)
        half = u.shape[1] // 2
        parts = []
        for g in range(2):
            ug = u[:, g * half:(g + 1) * half]
            parts.append(ug * lax.rsqrt(jnp.mean(ug * ug, axis=-1, keepdims=True) + NORM_EPS))
        return (jnp.concatenate(parts, axis=1) * w).astype(BF16)

    m1 = gated_norm(yssd_ref[...], zs_ref[...], nw1_ref[...])
    m2 = gated_norm(y_nsa, za_ref[...], nw2_ref[...])
    proj = (jnp.dot(m1, w_ref[0:SSD_WIDTH, :], preferred_element_type=F32)
            + jnp.dot(m2, w_ref[SSD_WIDTH:, :], preferred_element_type=F32))
    gate = gate_ref[...] if per_row else gate_ref[0]
    out = x_ref[...] + gate * proj
    if final:
        out = out * lax.rsqrt(jnp.mean(out * out, axis=-1, keepdims=True) + NORM_EPS) * fg_ref[...]
    o_ref[...] = out


def _gate_expand():
    e = np.zeros((3, LANE, NSA_WIDTH), np.float32)
    for br in range(3):
        for h in range(NSA_HEADS):
            e[br, br * NSA_HEADS + h, h * HEAD_DIM:(h + 1) * HEAD_DIM] = 1.0
    return jnp.asarray(e, BF16)


def _layer_out(x2d, gate, y_ssd, z_s, o_cmp, o_slc, o_win, gl, z_a, nw1, nw2, w_out_bf, final_g,
               rows_per_batch, final):
    m = x2d.shape[0]
    per_row = rows_per_batch == 1
    tm = m if per_row else 256
    if per_row:
        gate_spec = pl.BlockSpec((tm, D_MODEL), lambda i: (0, 0))
        gt = gate
    else:
        gate_spec = pl.BlockSpec((1, 1, D_MODEL), lambda i: ((i * tm) // rows_per_batch, 0, 0))
        gt = gate[:, None, :]
    row = lambda w: pl.BlockSpec((tm, w), lambda i: (i, 0))
    full = lambda shape: pl.BlockSpec(shape, lambda i: (0,) * len(shape))
    eg = _gate_expand()
    return pl.pallas_call(
        functools.partial(_out_kernel, per_row=per_row, final=final),
        out_shape=jax.ShapeDtypeStruct((m, D_MODEL), F32),
        grid=(m // tm,),
        in_specs=[row(D_MODEL), gate_spec, row(SSD_WIDTH), row(SSD_WIDTH), row(NSA_WIDTH), row(NSA_WIDTH),
                  row(NSA_WIDTH), row(LANE), row(NSA_WIDTH), full((1, SSD_WIDTH)), full((1, NSA_WIDTH)),
                  full(w_out_bf.shape), full(eg.shape), full((1, D_MODEL))],
        out_specs=row(D_MODEL),
        compiler_params=pltpu.CompilerParams(vmem_limit_bytes=VMEM_LIMIT),
        name="layer_out",
    )(x2d, gt, y_ssd, z_s, o_cmp, o_slc, o_win, gl, z_a, nw1.reshape(1, SSD_WIDTH), nw2.reshape(1, NSA_WIDTH),
      w_out_bf, eg, final_g.reshape(1, D_MODEL))


def _pad_in_weights(w_in):
    cols = []
    off = 0
    for size, width in zip(_SEG_SIZES, _SEG_PAD):
        seg = w_in[:, off:off + size]
        cols.append(jnp.pad(seg, ((0, 0), (0, width - size))))
        off += size
    return jnp.concatenate(cols, axis=1).astype(BF16)


def _values_t(v_pad):
    b, rows, _ = v_pad.shape
    vt = jnp.transpose(v_pad.reshape(b, rows, NSA_KV_HEADS, HEAD_DIM), (0, 2, 3, 1))
    return jnp.concatenate([vt, jnp.ones((b, NSA_KV_HEADS, ONES_ROWS, rows), BF16)], axis=2)


def _front_pad_bf16(kv, rows):
    return jnp.pad(kv, ((0, 0), (rows, 0), (0, 0))).astype(BF16)


def kernel(x_prompt, x_sample, cache_cmp_kv, cache_slc_kv, state_win_kv, state_conv, state_ssm, page_table,
           c_prompt, c_sample, norm_g, ada_w, ada_b, w_in, conv_w, conv_b, dt_bias, a_log, d_skip,
           ssd_norm_w, cmp_pe, cmp_w1, cmp_w2, nsa_norm_w, w_out, rel_bias, final_norm_g):
    nbp, lp, _ = x_prompt.shape
    nbs = x_sample.shape[0]
    depth = w_in.shape[0]
    n_pool = cache_cmp_kv.shape[1]
    n_pages = page_table.shape[1]
    past = n_pages * PAGE_SIZE
    w_buf = state_win_kv.shape[2]
    kv_row = (2, NSA_KV_HEADS, HEAD_DIM)

    n_ck = lp // CMP_STRIDE
    n_sb = lp // SLC_BLOCK
    qi = np.arange(Q_BLOCK)[:, None]
    c0 = n_ck - 4
    pattern = _bias_of_dist(rel_bias, qi - CMP_STRIDE * (np.arange(n_ck)[None, :] - c0) - (CMP_STRIDE - 1))
    far = rel_bias.astype(F32)[REL_BUCKETS - 1][:, None, None]
    par = np.arange(2)[:, None, None]
    dist_n = (np.arange(Q_BLOCK)[None, None, :] - np.arange(SLC_NEARW)[None, :, None] + Q_BLOCK * par
              + (SLC_NEAR - Q_BLOCK))
    tnt = (_bias_of_dist(rel_bias, dist_n) - far[..., None]).reshape(NSA_KV_HEADS, NSA_REP, 2, SLC_NEARW, Q_BLOCK)
    tnt = jnp.transpose(tnt, (2, 0, 3, 1, 4)).reshape(2, NSA_KV_HEADS, SLC_NEARW, ROWS) * LOG2E
    tile = _group_tile()
    key_row = np.arange(SLC_PAD + lp + SLC_BACK) - SLC_PAD
    blk_onehot = jnp.asarray((key_row[:, None] >= 0)
                             & (key_row[:, None] // SLC_BLOCK == np.arange(LANE)[None, :]), BF16)
    dist_w = (np.arange(Q_BLOCK)[None, None, :] - np.arange(WIN_W)[None, :, None] + Q_BLOCK * par + WIN_PAD)
    tw = jnp.where(jnp.asarray((dist_w >= 0) & (dist_w < WINDOW))[None], _bias_of_dist(rel_bias, dist_w) * LOG2E, NEG)
    tw = jnp.transpose(tw.reshape(NSA_KV_HEADS, NSA_REP, 2, WIN_W, Q_BLOCK), (2, 0, 3, 1, 4))
    tw = tw.reshape(2, NSA_KV_HEADS, WIN_W, ROWS)
    tw = jnp.concatenate([tw[0], tw[1]], axis=-1)
    win_flag = jnp.asarray((np.arange(WIN_PAD + lp)[:, None] < WIN_PAD) & (np.arange(LANE)[None, :] == 0), BF16)
    ov_p = _overlap_matrix(n_ck, n_sb, LANE)

    n_cs = past // CMP_STRIDE
    cur = past // SLC_BLOCK
    n_blk_s = cur + 1
    n_cols_s = -(-n_blk_s // LANE) * LANE
    ov_s = _overlap_matrix(n_cs, n_blk_s, n_cols_s)
    m_s = np.arange(n_cs)
    bm_c = jnp.where(jnp.asarray(m_s >= 1)[None, :],
                     _bias_of_dist(rel_bias, past - (CMP_STRIDE * m_s + CMP_STRIDE - 1)), NEG)
    n_w = -(-(w_buf + 1) // LANE) * LANE
    iw = np.arange(n_w)
    dw = w_buf - iw
    ok_w = (iw <= w_buf) & (dw >= 0) & (dw < WINDOW) & (past - w_buf + iw >= 0)
    bm_w = jnp.where(jnp.asarray(ok_w)[None, :], _bias_of_dist(rel_bias, dw), NEG)
    jb = (cur - 3 + np.arange(4))[:, None]
    biasblk = _bias_of_dist(rel_bias, past - SLC_BLOCK * jb - np.arange(SLC_BLOCK)[None, :])
    biasblk = jnp.transpose(biasblk, (1, 0, 2))
    biasblk = jnp.concatenate([biasblk, biasblk], axis=-1)

    pages_t = lambda c: jnp.transpose(c, (0, 1, 3, 4, 5, 2)).reshape(depth * n_pool, KV_COLS, PAGE_SIZE)
    cmp_pages_t = pages_t(cache_cmp_kv)
    slc_pages_t = pages_t(cache_slc_kv)
    prompt_pages = jnp.arange(nbp * (lp // PAGE_SIZE), dtype=jnp.int32).reshape(nbp, lp // PAGE_SIZE)

    c_all = jnp.concatenate([c_prompt, c_sample], axis=0)
    xp = x_prompt.reshape(nbp * lp, D_MODEL)
    xs = x_sample.reshape(nbs, D_MODEL)
    outs = {k: [] for k in ("pc", "ps", "pw", "pconv", "pssm", "sc", "ss", "sw", "sconv", "sssm")}

    for l in range(depth):
        final = l == depth - 1
        w_pad = _pad_in_weights(w_in[l])
        w_out_bf = w_out[l].astype(BF16)
        cweights = _compress_weights(cmp_pe[l], cmp_w1[l], cmp_w2[l])
        mod = _modulation(c_all, ada_w[l], ada_b[l])
        shift, scale, gate = mod[:, :D_MODEL], mod[:, D_MODEL:2 * D_MODEL], mod[:, 2 * D_MODEL:]

        z_s, xbc, dt, q, kvc, kvs, kvw, gl, z_a = _in_projection(xp, norm_g[l], scale[:nbp], shift[:nbp], w_pad, lp)
        xbc3 = xbc.reshape(nbp, lp, CONV_DIM)
        y_ssd, h_fin = _ssd_prompt(xbc3, dt.reshape(nbp, lp, LANE), conv_w[l], conv_b[l], dt_bias[l], a_log[l],
                                   d_skip[l])
        q3 = q.reshape(nbp, lp, NSA_WIDTH)
        kvc3, kvs3, kvw3 = (t.reshape(nbp, lp, KV_COLS) for t in (kvc, kvs, kvw))
        kc = _compress(kvc.reshape(nbp * (lp // PAGE_SIZE), PAGE_SIZE, KV_COLS), prompt_pages, cweights, False)
        o_cmp, sel = _cmp_prompt(q3, kc, pattern, ov_p)
        ks_pad = jnp.pad(kvs3, ((0, 0), (SLC_PAD, SLC_BACK), (0, 0))).astype(BF16)
        kx_pad = jnp.concatenate([ks_pad[:, :, :LANE], jnp.broadcast_to(blk_onehot, (nbp,) + blk_onehot.shape)],
                                 axis=-1)
        o_slc = _slc_prompt(q3, sel, kx_pad, _values_t(ks_pad[:, :, LANE:]), tnt, tile)
        kw_pad = _front_pad_bf16(kvw3, WIN_PAD)
        kwx_pad = jnp.concatenate([kw_pad[:, :, :LANE], jnp.broadcast_to(win_flag, (nbp,) + win_flag.shape)], axis=-1)
        o_win = _win_prompt(q3, kwx_pad, _values_t(kw_pad[:, :, LANE:]), tw)
        xp = _layer_out(xp, gate[:nbp], y_ssd.reshape(nbp * lp, SSD_WIDTH), z_s,
                        o_cmp.reshape(nbp * lp, NSA_WIDTH), o_slc.reshape(nbp * lp, NSA_WIDTH),
                        o_win.reshape(nbp * lp, NSA_WIDTH), gl, z_a, ssd_norm_w[l], nsa_norm_w[l], w_out_bf,
                        final_norm_g, lp, final)
        outs["pc"].append(kvc3.reshape((nbp, lp) + kv_row))
        outs["ps"].append(kvs3.reshape((nbp, lp) + kv_row))
        outs["pw"].append(kvw3[:, -min(WINDOW, lp):].reshape((nbp, min(WINDOW, lp)) + kv_row))
        outs["pconv"].append(xbc3[:, -(SSD_CONV - 1):])
        outs["pssm"].append(h_fin)

        z_s2, xbc2, dt2, q2, kvc2, kvs2, kvw2, gl2, z_a2 = _in_projection(
            xs, norm_g[l], scale[nbp:], shift[nbp:], w_pad, 1)
        y_ssd2, h2 = _ssd_step(xbc2, state_conv[l], dt2, state_ssm[l], conv_w[l], conv_b[l], dt_bias[l],
                               a_log[l], d_skip[l])
        qh = q2.reshape(nbs, NSA_KV_HEADS, NSA_REP, HEAD_DIM)
        zq = jnp.zeros((nbs, NSA_REP, HEAD_DIM), F32)
        q_pad = jnp.stack([jnp.concatenate([qh[:, 0], zq], axis=-1), jnp.concatenate([zq, qh[:, 1]], axis=-1)],
                          axis=1).reshape(nbs, NSA_HEADS, LANE).astype(BF16)
        kc2 = _compress(cmp_pages_t, page_table + l * n_pool, cweights, True)
        oc2, psum = _dense1(q_pad, kc2, bm_c)
        sel_idx = _topk_sample(psum.reshape(nbs * NSA_KV_HEADS, n_cs), ov_s, n_blk_s, cur)
        jsel = sel_idx[:, :nbs * NSA_KV_HEADS].T.reshape(nbs, NSA_KV_HEADS, SLC_TOP)
        jc = jnp.minimum(jsel, cur - 1)
        page = jnp.take_along_axis(page_table, (jc // 2).reshape(nbs, -1), axis=1).reshape(jsel.shape)
        phys = jnp.where(jsel < cur, page + l * n_pool, 0).astype(jnp.int32)
        os2 = _slc_sample(q2.reshape(nbs, NSA_HEADS, HEAD_DIM).astype(BF16), slc_pages_t, phys.reshape(-1),
                          jsel.reshape(-1), kvs2.reshape(nbs, KV_COLS, 1), biasblk, cur, past)
        kw_full = jnp.concatenate([state_win_kv[l].reshape(nbs, w_buf, KV_COLS), kvw2[:, None, :]], axis=1)
        kw_in = jnp.pad(kw_full, ((0, 0), (0, n_w - (w_buf + 1)), (0, 0)))
        ow2, _ = _dense1(q_pad, kw_in, bm_w)
        xs = _layer_out(xs, gate[nbp:], y_ssd2, z_s2, _take_group_half(oc2), os2.reshape(nbs, NSA_WIDTH),
                        _take_group_half(ow2), gl2, z_a2, ssd_norm_w[l], nsa_norm_w[l], w_out_bf,
                        final_norm_g, 1, final)
        outs["sc"].append(kvc2.reshape((nbs, 1) + kv_row))
        outs["ss"].append(kvs2.reshape((nbs, 1) + kv_row))
        outs["sw"].append(kw_full[:, -w_buf:].reshape((nbs, w_buf) + kv_row))
        outs["sconv"].append(jnp.concatenate([state_conv[l][:, 1:], xbc2[:, None, :]], axis=1))
        outs["sssm"].append(h2)

    st = lambda k: jnp.stack(outs[k])
    return (xp.reshape(nbp, lp, D_MODEL), xs.reshape(nbs, 1, D_MODEL),
            st("pc"), st("ps"), st("pw"), st("pconv"), st("pssm"),
            st("sc"), st("ss"), st("sw"), st("sconv"), st("sssm"))
```

```python
import functools
import math

import numpy as np
import jax
import jax.numpy as jnp
from jax import lax
from jax.experimental import pallas as pl
from jax.experimental.pallas import tpu as pltpu

F32 = jnp.float32
BF16 = jnp.bfloat16
HIGHEST = lax.Precision.HIGHEST

D_MODEL = 1024
HEAD_DIM = 64
SSD_WIDTH = 1024
SSD_HEADS = 16
SSD_GROUPS = 2
SSD_STATE = 128
SSD_CONV = 4
SSD_CHUNK = 256
CONV_DIM = SSD_WIDTH + 2 * SSD_GROUPS * SSD_STATE
NSA_WIDTH = 1024
NSA_HEADS = 16
NSA_KV_HEADS = 2
NSA_REP = NSA_HEADS // NSA_KV_HEADS
CMP_BLOCK = 32
CMP_STRIDE = 16
CMP_HID = 2 * HEAD_DIM
SLC_BLOCK = 64
SLC_TOP = 16
WINDOW = 512
Q_BLOCK = 64
REL_BUCKETS = 32
REL_MAX_DIST = 128
NORM_EPS = 1e-6
KV_COLS = 2 * NSA_KV_HEADS * HEAD_DIM
PAGE_SIZE = 128
NEG = -1e30
LOG2E = 1.4426950408889634

LANE = 128
HALF = LANE // 2
GROUP_W = NSA_REP * HEAD_DIM
ROWS = NSA_REP * Q_BLOCK
SLC_NEAR = 4 * SLC_BLOCK
SLC_PAD = SLC_NEAR
SLC_NEARW = SLC_NEAR + LANE
ONES_ROWS = 8
PAD_FLAG_COL = LANE - 1
SLC_BACK = 2 * SLC_NEAR
WIN_W = WINDOW + 2 * Q_BLOCK
WIN_PAD = WINDOW
CMP_PAGES_MAX = 32
VMEM_LIMIT = 48 * 1024 * 1024

_SEG_NAMES = ("z_s", "xbc", "dt", "q", "kvc", "kvs", "kvw", "gl", "z_a")
_SEG_SIZES = (SSD_WIDTH, CONV_DIM, SSD_HEADS, NSA_WIDTH, KV_COLS, KV_COLS, KV_COLS, 3 * NSA_HEADS, NSA_WIDTH)
_SEG_PAD = tuple(-(-s // LANE) * LANE for s in _SEG_SIZES)
_SEG_OFF = tuple(int(o) for o in np.cumsum((0,) + _SEG_PAD[:-1]))
IN_PAD = int(sum(_SEG_PAD))


def _sigmoid(x):
    return 1.0 / (1.0 + jnp.exp(-x))


def _silu(x):
    return x * _sigmoid(x)


def _dot32(a, b):
    return jnp.dot(a, b, precision=HIGHEST, preferred_element_type=F32)


def _split_bf16(x, terms):
    parts = []
    for _ in range(terms):
        p = x.astype(BF16)
        parts.append(p)
        x = x - p.astype(F32)
    return parts


def _dot_sel(x, sel_bf16, terms):
    return sum(jnp.dot(p, sel_bf16, preferred_element_type=F32) for p in _split_bf16(x, terms))


def _sel_dot(sel_bf16, x, terms):
    return sum(jnp.dot(sel_bf16, p, preferred_element_type=F32) for p in _split_bf16(x, terms))


def _dot_nt(a, b):
    return lax.dot_general(a, b, (((1,), (1,)), ((), ())), preferred_element_type=F32)


def _bucket_table():
    n = np.arange(REL_MAX_DIST + 1)
    max_exact = REL_BUCKETS // 2
    nf = np.maximum(n, 1).astype(np.float32)
    large = max_exact + (np.log(nf / np.float32(max_exact)) / np.float32(math.log(REL_MAX_DIST / max_exact))
                         * np.float32(REL_BUCKETS - max_exact)).astype(np.int32)
    large = np.minimum(large, REL_BUCKETS - 1)
    return np.where(n < max_exact, n, large).astype(np.int32)


_BUCKETS = _bucket_table()


def _bias_of_dist(rel_bias, dist):
    idx = _BUCKETS[np.clip(dist, 0, REL_MAX_DIST)]
    out = jnp.take(rel_bias.astype(F32), jnp.asarray(idx.reshape(-1)), axis=0)
    return out.T.reshape((NSA_HEADS,) + dist.shape)


def _toeplitz_bias(rel_bias, rows, cols, c0):
    n = rows + cols
    d = np.arange(n) + c0 - (rows - 1)
    v = jnp.take(rel_bias.astype(F32), jnp.asarray(_BUCKETS[np.clip(d, 0, REL_MAX_DIST)]), axis=0).T
    flat = jnp.tile(v, (1, rows + 1))[:, :rows * (n + 1)]
    hank = flat.reshape(NSA_HEADS, rows, n + 1)[:, :, :cols]
    return hank[:, ::-1, :]


def _mod_kernel(c_ref, w_ref, b_ref, o_ref):
    o_ref[...] = _dot32(_silu(c_ref[...]), w_ref[...]) + b_ref[...]


def _modulation(c, w, b):
    m, d = c.shape
    n = w.shape[1]
    tn = 512
    return pl.pallas_call(
        _mod_kernel,
        out_shape=jax.ShapeDtypeStruct((m, n), F32),
        grid=(n // tn,),
        in_specs=[pl.BlockSpec((m, d), lambda j: (0, 0)),
                  pl.BlockSpec((d, tn), lambda j: (0, j)),
                  pl.BlockSpec((1, tn), lambda j: (0, j))],
        out_specs=pl.BlockSpec((m, tn), lambda j: (0, j)),
        name="adaln_mod",
    )(c, w, b.reshape(1, n))


def _inproj_kernel(x_ref, g_ref, sc_ref, sh_ref, w_ref, *out_refs, per_row):
    x = x_ref[...]
    xn = x * lax.rsqrt(jnp.mean(x * x, axis=-1, keepdims=True) + NORM_EPS)
    sc = sc_ref[...] if per_row else sc_ref[0]
    sh = sh_ref[...] if per_row else sh_ref[0]
    h = ((xn * g_ref[...]) * (1.0 + sc) + sh).astype(BF16)
    for name, off, width, ref in zip(_SEG_NAMES, _SEG_OFF, _SEG_PAD, out_refs):
        r = jnp.dot(h, w_ref[:, off:off + width], preferred_element_type=F32)
        if name == "q":
            r = r * (HEAD_DIM ** -0.5)
        ref[...] = r


def _in_projection(x2d, g, scale, shift, w_pad, rows_per_batch):
    m = x2d.shape[0]
    per_row = rows_per_batch == 1
    tm = m if per_row else 256
    if per_row:
        mod_spec = pl.BlockSpec((tm, D_MODEL), lambda i: (0, 0))
        sc, sh = scale, shift
    else:
        mod_spec = pl.BlockSpec((1, 1, D_MODEL), lambda i: ((i * tm) // rows_per_batch, 0, 0))
        sc, sh = scale[:, None, :], shift[:, None, :]
    outs = tuple(jax.ShapeDtypeStruct((m, w), F32) for w in _SEG_PAD)
    return pl.pallas_call(
        functools.partial(_inproj_kernel, per_row=per_row),
        out_shape=outs,
        grid=(m // tm,),
        in_specs=[pl.BlockSpec((tm, D_MODEL), lambda i: (i, 0)),
                  pl.BlockSpec((1, D_MODEL), lambda i: (0, 0)),
                  mod_spec, mod_spec,
                  pl.BlockSpec((D_MODEL, IN_PAD), lambda i: (0, 0))],
        out_specs=tuple(pl.BlockSpec((tm, w), lambda i: (i, 0)) for w in _SEG_PAD),
        compiler_params=pltpu.CompilerParams(vmem_limit_bytes=VMEM_LIMIT),
        name="in_projection",
    )(x2d, g.reshape(1, D_MODEL), sc, sh, w_pad)


def _softplus(x):
    return jnp.maximum(x, 0.0) + jnp.log(1.0 + jnp.exp(-jnp.abs(x)))


def _ssd_kernel(xbc_ref, dt_ref, cw_ref, cb_ref, dtb_ref, alog_ref, dsk_ref, e_ref, tril_ref,
                y_ref, hfin_ref, xe_sc, st_sc):
    c = pl.program_id(1)
    q = SSD_CHUNK
    n_pairs = SSD_HEADS // 2

    @pl.when(c == 0)
    def _():
        xe_sc[0:8, :] = jnp.zeros((8, CONV_DIM), F32)
        st_sc[...] = jnp.zeros(st_sc.shape, F32)

    xe_sc[8:8 + q, :] = xbc_ref[0]
    acc = cb_ref[...] + cw_ref[0:1, :] * xe_sc[5:5 + q, :]
    for k in range(1, SSD_CONV):
        acc = acc + cw_ref[k:k + 1, :] * xe_sc[5 + k:5 + k + q, :]
    u = _silu(acc)
    xe_sc[0:8, :] = xe_sc[q:q + 8, :]

    xs = u[:, :SSD_WIDTH]
    gn = SSD_GROUPS * SSD_STATE
    bm = u[:, SSD_WIDTH:SSD_WIDTH + gn]
    cm = u[:, SSD_WIDTH + gn:]

    dt = _softplus(dt_ref[0] + dtb_ref[...])
    a = dt * (-jnp.exp(alog_ref[...]))
    cs = _sel_dot(tril_ref[...], a, 3)
    cs_t = cs.T
    cs_last = cs[q - 1:q, :]
    e = e_ref[...]
    dt_e = _dot_sel(dt, e, 3)
    w_e = _dot_sel(dt * jnp.exp(cs_last - cs), e, 3)
    ecs_e = _dot_sel(jnp.exp(cs), e, 3)
    tot_e = _dot_sel(jnp.broadcast_to(jnp.exp(cs_last), (8, LANE)), e, 3)[0:1, :]
    xdt = (xs * dt_e).astype(BF16)
    xw = (xs * w_e).astype(BF16)

    li = lax.broadcasted_iota(jnp.int32, (q, q), 0)
    si = lax.broadcasted_iota(jnp.int32, (q, q), 1)
    tri = li >= si
    lane = lax.broadcasted_iota(jnp.int32, (q, LANE), 1)

    for g in range(SSD_GROUPS):
        cg = cm[:, g * SSD_STATE:(g + 1) * SSD_STATE].astype(BF16)
        bg = bm[:, g * SSD_STATE:(g + 1) * SSD_STATE]
        cb = _dot_nt(cg, bg.astype(BF16))
        bg_t = bg.T.astype(BF16)
        for jp in range(n_pairs // SSD_GROUPS):
            j = g * (n_pairs // SSD_GROUPS) + jp
            sl = slice(j * LANE, (j + 1) * LANE)
            xdt_p = xdt[:, sl]
            ys = []
            for hh in (2 * j, 2 * j + 1):
                diff = cs[:, hh:hh + 1] - cs_t[hh:hh + 1, :]
                lmat = jnp.exp(jnp.where(tri, diff, NEG))
                ys.append(jnp.dot((cb * lmat).astype(BF16), xdt_p, preferred_element_type=F32))
            y_diag = jnp.where(lane < HALF, ys[0], ys[1])
            st = st_sc[j]
            y_off = jnp.dot(cg, st.astype(BF16), preferred_element_type=F32) * ecs_e[:, sl]
            y_ref[0, :, sl] = y_diag + y_off + xs[:, sl] * dsk_ref[:, sl]
            new = jnp.dot(bg_t, xw[:, sl], preferred_element_type=F32)
            st_sc[j] = st * tot_e[:, sl] + new

    @pl.when(c == pl.num_programs(1) - 1)
    def _():
        for j in range(n_pairs):
            hfin_ref[0, j * LANE:(j + 1) * LANE, :] = st_sc[j].T


def _head_expand():
    e = np.zeros((LANE, SSD_WIDTH), np.float32)
    for h in range(SSD_HEADS):
        e[h, h * HEAD_DIM:(h + 1) * HEAD_DIM] = 1.0
    return jnp.asarray(e)


def _pad_lanes(v):
    return jnp.pad(v.astype(F32), (0, LANE - v.shape[0])).reshape(1, LANE)


def _ssd_prompt(xbc, dt, conv_w, conv_b, dt_bias, a_log, d_skip):
    b, l, _ = xbc.shape
    nc = l // SSD_CHUNK
    full = lambda shape: pl.BlockSpec(shape, lambda i, c: (0,) * len(shape))
    y, hfin = pl.pallas_call(
        _ssd_kernel,
        out_shape=(jax.ShapeDtypeStruct((b, l, SSD_WIDTH), F32),
                   jax.ShapeDtypeStruct((b, SSD_HEADS * HEAD_DIM, SSD_STATE), F32)),
        grid=(b, nc),
        in_specs=[pl.BlockSpec((1, SSD_CHUNK, CONV_DIM), lambda i, c: (i, c, 0)),
                  pl.BlockSpec((1, SSD_CHUNK, LANE), lambda i, c: (i, c, 0)),
                  full((SSD_CONV, CONV_DIM)), full((1, CONV_DIM)), full((1, LANE)), full((1, LANE)),
                  full((1, SSD_WIDTH)), full((LANE, SSD_WIDTH)), full((SSD_CHUNK, SSD_CHUNK))],
        out_specs=(pl.BlockSpec((1, SSD_CHUNK, SSD_WIDTH), lambda i, c: (i, c, 0)),
                   pl.BlockSpec((1, SSD_HEADS * HEAD_DIM, SSD_STATE), lambda i, c: (i, 0, 0))),
        scratch_shapes=[pltpu.VMEM((SSD_CHUNK + 8, CONV_DIM), F32),
                        pltpu.VMEM((SSD_HEADS // 2, SSD_STATE, LANE), F32)],
        compiler_params=pltpu.CompilerParams(dimension_semantics=("arbitrary", "arbitrary"),
                                             vmem_limit_bytes=VMEM_LIMIT),
        name="ssd_prompt",
    )(xbc, dt, conv_w, conv_b.reshape(1, CONV_DIM), _pad_lanes(dt_bias), _pad_lanes(a_log),
      jnp.repeat(d_skip.astype(F32), HEAD_DIM).reshape(1, SSD_WIDTH), _head_expand().astype(BF16),
      jnp.asarray(np.tril(np.ones((SSD_CHUNK, SSD_CHUNK), np.float32)), BF16))
    return y, hfin.reshape(b, SSD_HEADS, HEAD_DIM, SSD_STATE)


def _ssd_step_kernel(xbc_ref, c0_ref, c1_ref, c2_ref, dt_ref, h0_ref, cw_ref, cb_ref, dtb_ref, alog_ref,
                     dsk_ref, e_ref, y_ref, hout_ref, xt_sc, dect_sc, bc_sc, yt_sc, xs_sc):
    b = pl.program_id(0)
    nb = xbc_ref.shape[0]
    rows = SSD_HEADS * HEAD_DIM
    gn = SSD_GROUPS * SSD_STATE

    @pl.when(b == 0)
    def _():
        acc = (cb_ref[...] + cw_ref[0:1, :] * c0_ref[...] + cw_ref[1:2, :] * c1_ref[...]
               + cw_ref[2:3, :] * c2_ref[...] + cw_ref[3:4, :] * xbc_ref[...])
        u = _silu(acc)
        xs = u[:, :SSD_WIDTH]
        dt = _softplus(dt_ref[...] + dtb_ref[...])
        dec = jnp.exp(dt * (-jnp.exp(alog_ref[...])))
        e = e_ref[...]
        xdt = xs * _dot32(dt, e)
        dec_e = _dot32(dec, e)
        pad = jnp.zeros((LANE - nb, SSD_WIDTH), F32)
        xt_sc[...] = jnp.concatenate([xdt, pad], axis=0).T
        dect_sc[...] = jnp.concatenate([dec_e, pad], axis=0).T
        bc_sc[...] = u[:, SSD_WIDTH:]
        xs_sc[...] = xs
        yt_sc[...] = jnp.zeros(yt_sc.shape, F32)

    ri = lax.broadcasted_iota(jnp.int32, (LANE, LANE), 0)
    onehot = jnp.where(ri == b, 1.0, 0.0)
    xcol = _dot32(xt_sc[...], onehot)
    dcol = _dot32(dect_sc[...], onehot)
    bc = bc_sc[pl.ds(b, 1), :]
    row = lax.broadcasted_iota(jnp.int32, (rows, SSD_STATE), 0)
    first = row < rows // SSD_GROUPS
    b_full = jnp.where(first, bc[:, 0:SSD_STATE], bc[:, SSD_STATE:gn])
    c_full = jnp.where(first, bc[:, gn:gn + SSD_STATE], bc[:, gn + SSD_STATE:])
    new = dcol * h0_ref[0] + xcol * b_full
    hout_ref[0] = new
    ycol = _dot32(new * c_full, jnp.ones((SSD_STATE, LANE), F32))
    lane = lax.broadcasted_iota(jnp.int32, (rows, LANE), 1)
    yt_sc[...] = jnp.where(lane == b, ycol, yt_sc[...])

    @pl.when(b == nb - 1)
    def _():
        y_ref[...] = yt_sc[...].T[0:nb, :] + xs_sc[...] * dsk_ref[...]


def _ssd_step(xbc, conv_state, dt, h0, conv_w, conv_b, dt_bias, a_log, d_skip):
    nb = xbc.shape[0]
    rows = SSD_HEADS * HEAD_DIM
    full = lambda shape: pl.BlockSpec(shape, lambda i: (0,) * len(shape))
    y, hout = pl.pallas_call(
        _ssd_step_kernel,
        out_shape=(jax.ShapeDtypeStruct((nb, SSD_WIDTH), F32),
                   jax.ShapeDtypeStruct((nb, rows, SSD_STATE), F32)),
        grid=(nb,),
        in_specs=[full((nb, CONV_DIM)), full((nb, CONV_DIM)), full((nb, CONV_DIM)), full((nb, CONV_DIM)),
                  full((nb, LANE)),
                  pl.BlockSpec((1, rows, SSD_STATE), lambda i: (i, 0, 0)),
                  full((SSD_CONV, CONV_DIM)), full((1, CONV_DIM)), full((1, LANE)), full((1, LANE)),
                  full((1, SSD_WIDTH)), full((LANE, SSD_WIDTH))],
        out_specs=(full((nb, SSD_WIDTH)),
                   pl.BlockSpec((1, rows, SSD_STATE), lambda i: (i, 0, 0))),
        scratch_shapes=[pltpu.VMEM((rows, LANE), F32), pltpu.VMEM((rows, LANE), F32),
                        pltpu.VMEM((nb, 2 * SSD_GROUPS * SSD_STATE), F32),
                        pltpu.VMEM((rows, LANE), F32), pltpu.VMEM((nb, SSD_WIDTH), F32)],
        compiler_params=pltpu.CompilerParams(dimension_semantics=("arbitrary",)),
        name="ssd_step",
    )(xbc, conv_state[:, 0], conv_state[:, 1], conv_state[:, 2], dt, h0.reshape(nb, rows, SSD_STATE),
      conv_w, conv_b.reshape(1, CONV_DIM), _pad_lanes(dt_bias), _pad_lanes(a_log),
      jnp.repeat(d_skip.astype(F32), HEAD_DIM).reshape(1, SSD_WIDTH), _head_expand())
    return y, hout.reshape(nb, SSD_HEADS, HEAD_DIM, SSD_STATE)


def _compress_kernel(pt_ref, *refs, transposed, CMP_PAGES):
    if transposed:
        w1_ref, pe_ref, w2_ref, out_ref, sh_sc, pe_sc, xs_sc = refs[CMP_PAGES:]
        for i in range(CMP_PAGES):
            for k in range(2):
                xs_sc[i, k] = refs[i][0, k * LANE:(k + 1) * LANE, :].T
        pages = tuple(tuple(xs_sc.at[i, k] for i in range(CMP_PAGES)) for k in range(2))
    else:
        w1_ref, pe_ref, w2_ref, out_ref, sh_sc, pe_sc = refs[2 * CMP_PAGES:]
        pages = tuple(tuple(refs[k * CMP_PAGES + i].at[0] for i in range(CMP_PAGES)) for k in range(2))
    s = pl.program_id(1)
    segs = PAGE_SIZE // CMP_STRIDE
    rows = CMP_PAGES * segs
    hid2 = NSA_KV_HEADS * CMP_HID

    @pl.when(s == 0)
    def _():
        sh_sc[:, 0:8, :] = jnp.zeros((2, 8, hid2), F32)
        for k in range(2):
            t = jnp.zeros((8, 2 * hid2), F32)
            for o in range(CMP_STRIDE):
                t = t + jnp.dot(pe_ref[o, k].astype(BF16), w1_ref[o, k], preferred_element_type=F32)
            pe_sc[k] = jnp.broadcast_to(t[0:1, 0:hid2] + t[1:2, hid2:], (8, hid2))

    for k in range(2):
        acc = jnp.zeros((rows, 2 * hid2), F32)
        for o in range(CMP_STRIDE):
            xo = jnp.concatenate([p[pl.ds(o, segs, stride=CMP_STRIDE), :] for p in pages[k]], axis=0)
            acc = acc + jnp.dot(xo.astype(BF16), w1_ref[o, k], preferred_element_type=F32)
        sh_sc[k, 8:8 + rows, :] = acc[:, 0:hid2]
        pre = acc[:, hid2:] + sh_sc[k, 7:7 + rows, :] + pe_sc[k, 0:1, :]
        sh_sc[k, 0:8, :] = sh_sc[k, rows:rows + 8, :]
        out_ref[0, :, k * LANE:(k + 1) * LANE] = jnp.dot(_silu(pre).astype(BF16), w2_ref[k],
                                                         preferred_element_type=F32)


def _compress_weights(cmp_pe, cmp_w1, cmp_w2):
    span = CMP_BLOCK // CMP_STRIDE
    w1s = cmp_w1.astype(F32).reshape(2, span, CMP_STRIDE, HEAD_DIM, CMP_HID)
    z = jnp.zeros((2, span, CMP_STRIDE, HEAD_DIM, CMP_HID), F32)
    top = jnp.concatenate([w1s, z], axis=-1)
    bot = jnp.concatenate([z, w1s], axis=-1)
    bd = jnp.concatenate([top, bot], axis=-2)
    w1 = jnp.transpose(bd, (2, 0, 3, 1, 4)).reshape(CMP_STRIDE, 2, LANE, span * 2 * CMP_HID).astype(BF16)
    pe = cmp_pe.astype(F32).reshape(2, span, CMP_STRIDE, HEAD_DIM)
    pe = jnp.transpose(pe, (2, 0, 1, 3))
    pe = jnp.concatenate([pe, pe], axis=-1)
    pe = jnp.pad(pe, ((0, 0), (0, 0), (0, 8 - span), (0, 0)))
    w2 = cmp_w2.astype(F32)
    z2 = jnp.zeros_like(w2)
    w2bd = jnp.concatenate([jnp.concatenate([w2, z2], axis=-1), jnp.concatenate([z2, w2], axis=-1)],
                           axis=-2).astype(BF16)
    return w1, pe, w2bd


def _compress(pages_arr, page_ids, cweights, transposed):
    nb, n_pages = page_ids.shape
    CMP_PAGES = math.gcd(n_pages, CMP_PAGES_MAX)
    steps = n_pages // CMP_PAGES
    segs = PAGE_SIZE // CMP_STRIDE
    rows = CMP_PAGES * segs
    w1, pe, w2bd = cweights
    hid2 = NSA_KV_HEADS * CMP_HID
    page_of = lambda b, s, pt, i: pt[(b * steps + s) * CMP_PAGES + i]
    scratch = [pltpu.VMEM((2, rows + 8, hid2), F32), pltpu.VMEM((2, 8, hid2), F32)]
    if transposed:
        page_specs = [pl.BlockSpec((1, KV_COLS, PAGE_SIZE), lambda b, s, pt, i=i: (page_of(b, s, pt, i), 0, 0))
                      for i in range(CMP_PAGES)]
        scratch.append(pltpu.VMEM((CMP_PAGES, 2, PAGE_SIZE, LANE), F32))
    else:
        page_specs = [pl.BlockSpec((1, PAGE_SIZE, LANE), lambda b, s, pt, i=i, k=k: (page_of(b, s, pt, i), 0, k))
                      for k in range(2) for i in range(CMP_PAGES)]
    full = lambda shape: pl.BlockSpec(shape, lambda b, s, pt: (0,) * len(shape))
    return pl.pallas_call(
        functools.partial(_compress_kernel, transposed=transposed, CMP_PAGES=CMP_PAGES),
        out_shape=jax.ShapeDtypeStruct((nb, n_pages * segs, KV_COLS), F32),
        grid_spec=pltpu.PrefetchScalarGridSpec(
            num_scalar_prefetch=1,
            grid=(nb, steps),
            in_specs=page_specs + [full(w1.shape), full(pe.shape), full(w2bd.shape)],
            out_specs=pl.BlockSpec((1, rows, KV_COLS), lambda b, s, pt: (b, s, 0)),
            scratch_shapes=scratch),
        compiler_params=pltpu.CompilerParams(dimension_semantics=("arbitrary", "arbitrary"),
                                             vmem_limit_bytes=VMEM_LIMIT),
        name="nsa_compress",
    )(page_ids.reshape(-1), *([pages_arr] * len(page_specs)), w1, pe, w2bd)


def _overlap_matrix(n_rows, n_blocks, n_cols):
    m = np.arange(n_rows)[:, None]
    j = np.arange(n_cols)[None, :]
    cs = (m - 1) * CMP_STRIDE
    ov = (m >= 1) & (j < n_blocks) & (cs < j * SLC_BLOCK + SLC_BLOCK) & (cs + CMP_BLOCK > j * SLC_BLOCK)
    return jnp.asarray(ov.astype(np.float32))


def _stack_q(q, g):
    return _stack_q_f32(q, g).astype(BF16)


def _stack_q_f32(q, g):
    lane = lax.broadcasted_iota(jnp.int32, (Q_BLOCK, LANE), 1)
    keep = (lane < HALF) if g == 0 else (lane >= HALF)
    parts = []
    for jp in range(NSA_REP // 2):
        j = g * (NSA_REP // 2) + jp
        slab = q[:, j * LANE:(j + 1) * LANE]
        rolled = pltpu.roll(slab, HALF, 1)
        first, second = (slab, rolled) if g == 0 else (rolled, slab)
        parts.append(jnp.where(keep, first, 0.0))
        parts.append(jnp.where(keep, second, 0.0))
    return jnp.concatenate(parts, axis=0)


def _unstack_o(acc, g):
    lane = lax.broadcasted_iota(jnp.int32, (Q_BLOCK, LANE), 1)
    outs = []
    for jp in range(NSA_REP // 2):
        a = acc[(2 * jp) * Q_BLOCK:(2 * jp + 1) * Q_BLOCK]
        b = acc[(2 * jp + 1) * Q_BLOCK:(2 * jp + 2) * Q_BLOCK]
        if g == 0:
            outs.append(jnp.where(lane < HALF, a, pltpu.roll(b, HALF, 1)))
        else:
            outs.append(jnp.where(lane < HALF, pltpu.roll(a, HALF, 1), b))
    return jnp.concatenate(outs, axis=1)


def _tile8(x):
    return jnp.concatenate([x] * NSA_REP, axis=0)


def _rank_rows(imp, n_valid):
    sub = 8
    chunks = [imp[c:c + sub] for c in range(0, imp.shape[0], sub)]
    ranks = [jnp.zeros(ch.shape, F32) for ch in chunks]
    jrow = lax.broadcasted_iota(jnp.int32, chunks[0].shape, 0)
    for k in range(n_valid):
        rk = imp[k:k + 1, :]
        for c, ch in enumerate(chunks):
            if c * sub > k:
                ahead = rk >= ch
            elif c * sub + sub - 1 < k:
                ahead = rk > ch
            else:
                ahead = (rk > ch) | ((rk == ch) & (jrow + c * sub > k))
            ranks[c] = ranks[c] + jnp.where(ahead, 1.0, 0.0)
    return jnp.concatenate(ranks, axis=0)


def _cmp_kernel(q_ref, kv_ref, pb_ref, ov_ref, o_ref, sel_ref, bias_sc, *, n_keys, n_blocks):
    i = pl.program_id(0)

    @pl.when(pl.program_id(1) == 0)
    def _():
        for h in range(NSA_HEADS):
            bias_sc[h] = pltpu.roll(pb_ref[h], (4 * i + 4) % n_keys, 1)

    q = q_ref[0]
    kv = kv_ref[0]
    kc = kv[:, 0:LANE].astype(BF16)
    vc = kv[:, LANE:].astype(BF16)
    qi = lax.broadcasted_iota(jnp.int32, (Q_BLOCK, n_keys), 0)
    mi = lax.broadcasted_iota(jnp.int32, (Q_BLOCK, n_keys), 1)
    qpos = Q_BLOCK * i + qi
    valid8 = _tile8((mi >= 1) & (CMP_STRIDE * mi + CMP_STRIDE - 1 <= qpos))
    rowvalid8 = _tile8(jnp.where(qpos[:, 0:1] >= CMP_BLOCK - 1, 1.0, 0.0))
    jj = lax.broadcasted_iota(jnp.int32, (LANE, LANE), 1)
    psum = []
    for g in range(NSA_KV_HEADS):
        s = _dot_nt(_stack_q(q, g), kc)
        s = jnp.where(valid8, s + bias_sc[g * NSA_REP:(g + 1) * NSA_REP].reshape(ROWS, n_keys), NEG)
        p = jnp.exp(s - jnp.max(s, axis=-1, keepdims=True))
        pc = p / jnp.sum(p, axis=-1, keepdims=True) * rowvalid8
        o_ref[0, :, g * GROUP_W:(g + 1) * GROUP_W] = _unstack_o(
            jnp.dot(pc.astype(BF16), vc, preferred_element_type=F32), g)
        ps = pc[0:Q_BLOCK]
        for r in range(1, NSA_REP):
            ps = ps + pc[r * Q_BLOCK:(r + 1) * Q_BLOCK]
        psum.append(ps)
    ps = jnp.concatenate(psum, axis=0)
    hi = ps.astype(BF16)
    lo = (ps - hi.astype(F32)).astype(BF16)
    ov = ov_ref[...].astype(BF16)
    imp = (jnp.dot(hi, ov, preferred_element_type=F32)
           + jnp.dot(lo, ov, preferred_element_type=F32))
    forced = (jj == 0) | (jj == i) | (jj == i - 1)
    imp = jnp.where(forced, 1e6, jnp.where(jj <= i, imp, -1e6))
    imp = jnp.where(jj < n_blocks, imp, -2e6)
    rank = _rank_rows(imp.T, n_blocks)
    sel_ref[0, 0] = jnp.where(rank < SLC_TOP, 1.0, 0.0).astype(BF16)


def _cmp_prompt(q, kvc_cmp, pattern, ov):
    b, l, _ = q.shape
    n_keys = kvc_cmp.shape[1]
    n_blocks = l // SLC_BLOCK
    return pl.pallas_call(
        functools.partial(_cmp_kernel, n_keys=n_keys, n_blocks=n_blocks),
        out_shape=(jax.ShapeDtypeStruct((b, l, NSA_WIDTH), F32),
                   jax.ShapeDtypeStruct((b, l // Q_BLOCK, LANE, LANE), BF16)),
        grid=(l // Q_BLOCK, b),
        in_specs=[pl.BlockSpec((1, Q_BLOCK, NSA_WIDTH), lambda i, bi: (bi, i, 0)),
                  pl.BlockSpec((1, n_keys, KV_COLS), lambda i, bi: (bi, 0, 0)),
                  pl.BlockSpec(pattern.shape, lambda i, bi: (0, 0, 0)),
                  pl.BlockSpec(ov.shape, lambda i, bi: (0, 0))],
        out_specs=(pl.BlockSpec((1, Q_BLOCK, NSA_WIDTH), lambda i, bi: (bi, i, 0)),
                   pl.BlockSpec((1, 1, LANE, LANE), lambda i, bi: (bi, i, 0, 0))),
        scratch_shapes=[pltpu.VMEM(pattern.shape, F32)],
        compiler_params=pltpu.CompilerParams(dimension_semantics=("arbitrary", "arbitrary"),
                                             vmem_limit_bytes=VMEM_LIMIT),
        name="nsa_cmp_prompt",
    )(q, kvc_cmp, pattern, ov)


def _flash_update_t(st, vt, m, acc):
    m_new = jnp.maximum(m, jnp.max(st, axis=0, keepdims=True))
    p = jnp.exp2(st - m_new)
    acc = jnp.exp2(m - m_new) * acc + jnp.dot(vt, p.astype(BF16), preferred_element_type=F32)
    return m_new, acc


def _finish_t(acc):
    o_t = acc[0:HEAD_DIM] / acc[HEAD_DIM:HEAD_DIM + 1]
    return _unstack_o(jnp.concatenate([o_t, jnp.zeros_like(o_t)], axis=0).T, 0)


def _slc_kernel(q_ref, sel_ref, kx_ref, vt_ref, tnt_ref, tile_ref, o_ref, far_sc, near_sc, sta_sc, stb_sc, stn_sc):
    i = pl.program_id(1)
    q = q_ref[0] * LOG2E
    n_far = (jnp.maximum(i - 3, 0) + 3) // 4
    sel = sel_ref[0, 0]
    jrow = lax.broadcasted_iota(jnp.int32, (LANE, ROWS), 0)
    a = ((i + 1) // 2) * LANE
    delta = Q_BLOCK * (i + 1) - a
    groups = range(NSA_KV_HEADS)
    for g in groups:
        qs_t = _stack_q_f32(q, g).T.astype(BF16)
        hit = jnp.dot(sel, tile_ref[g], preferred_element_type=F32) > 0.5
        near_sc[g] = jnp.concatenate([qs_t, jnp.where(hit & (jrow != PAD_FLAG_COL), 0.0, NEG).astype(BF16)], axis=0)
        far_sc[g] = jnp.concatenate([qs_t, jnp.where(hit & (jrow < i - 3), 0.0, NEG).astype(BF16)], axis=0)

    def far_scores(t, dst):
        start = pl.multiple_of(SLC_PAD + SLC_NEAR * t, SLC_NEAR)
        kx = kx_ref[0, pl.ds(start, SLC_NEAR), :]
        for g in groups:
            dst[g] = jnp.dot(kx, far_sc[g], preferred_element_type=F32)

    def far_softmax(t, src, carry):
        start = pl.multiple_of(SLC_PAD + SLC_NEAR * t, SLC_NEAR)
        return tuple(_flash_update_t(src[g], vt_ref[0, g, :, pl.ds(start, SLC_NEAR)], *carry[g]) for g in groups)

    def far_pair(u, carry):
        far_scores(2 * u + 1, stb_sc)
        carry = far_softmax(2 * u, sta_sc, carry)
        far_scores(2 * u + 2, sta_sc)
        return far_softmax(2 * u + 1, stb_sc, carry)

    far_scores(0, sta_sc)
    start = pl.multiple_of(a, LANE)
    kx = kx_ref[0, pl.ds(start, SLC_NEARW), :]
    for g in groups:
        stn_sc[g] = jnp.dot(kx, near_sc[g], preferred_element_type=F32)
    init = (jnp.full((1, ROWS), NEG, F32), jnp.zeros((HEAD_DIM + ONES_ROWS, ROWS), F32))
    carry = lax.fori_loop(0, (n_far + 1) // 2, far_pair, (init, init))

    for g in groups:
        st = stn_sc[g] + tnt_ref[delta // Q_BLOCK, g]
        _, acc = _flash_update_t(st, vt_ref[0, g, :, pl.ds(start, SLC_NEARW)], *carry[g])
        o_ref[0, :, g * GROUP_W:(g + 1) * GROUP_W] = _finish_t(acc)


def _slc_prompt(q, sel, kx_pad, vt_pad, tnt, tile):
    b, l, _ = q.shape
    lp = kx_pad.shape[1]
    return pl.pallas_call(
        _slc_kernel,
        out_shape=jax.ShapeDtypeStruct((b, l, NSA_WIDTH), F32),
        grid=(b, l // Q_BLOCK),
        in_specs=[pl.BlockSpec((1, Q_BLOCK, NSA_WIDTH), lambda bi, i: (bi, i, 0)),
                  pl.BlockSpec((1, 1, LANE, LANE), lambda bi, i: (bi, i, 0, 0)),
                  pl.BlockSpec((1, lp, 2 * LANE), lambda bi, i: (bi, 0, 0)),
                  pl.BlockSpec((1, NSA_KV_HEADS, HEAD_DIM + ONES_ROWS, lp), lambda bi, i: (bi, 0, 0, 0)),
                  pl.BlockSpec(tnt.shape, lambda bi, i: (0, 0, 0, 0)),
                  pl.BlockSpec(tile.shape, lambda bi, i: (0, 0, 0))],
        out_specs=pl.BlockSpec((1, Q_BLOCK, NSA_WIDTH), lambda bi, i: (bi, i, 0)),
        scratch_shapes=[pltpu.VMEM((NSA_KV_HEADS, 2 * LANE, ROWS), BF16),
                        pltpu.VMEM((NSA_KV_HEADS, 2 * LANE, ROWS), BF16),
                        pltpu.VMEM((NSA_KV_HEADS, SLC_NEAR, ROWS), F32),
                        pltpu.VMEM((NSA_KV_HEADS, SLC_NEAR, ROWS), F32),
                        pltpu.VMEM((NSA_KV_HEADS, SLC_NEARW, ROWS), F32)],
        compiler_params=pltpu.CompilerParams(vmem_limit_bytes=VMEM_LIMIT),
        name="nsa_slc_prompt",
    )(q, sel, kx_pad, vt_pad, tnt, tile)


def _group_tile():
    t = np.zeros((NSA_KV_HEADS, LANE, ROWS), np.float32)
    for g in range(NSA_KV_HEADS):
        for r in range(NSA_REP):
            for qq in range(Q_BLOCK):
                t[g, g * Q_BLOCK + qq, r * Q_BLOCK + qq] = 1.0
    return jnp.asarray(t, BF16)


def _win_kernel(q_ref, kx_ref, vt_ref, tw_ref, o_ref, st_sc):
    i2 = pl.program_id(1)
    q = q_ref[0] * LOG2E
    start = pl.multiple_of(i2 * LANE, LANE)
    kx = kx_ref[0, pl.ds(start, WIN_W), :]
    groups = range(NSA_KV_HEADS)
    row = lax.broadcasted_iota(jnp.int32, (LANE, 2 * ROWS), 0)
    pad_rows = jnp.where(row == 0, NEG, 0.0).astype(BF16)
    for g in groups:
        qs_t = jnp.concatenate([_stack_q_f32(q[h * Q_BLOCK:(h + 1) * Q_BLOCK], g).T for h in range(2)], axis=1)
        qx = jnp.concatenate([qs_t.astype(BF16), pad_rows], axis=0)
        st_sc[g] = jnp.dot(kx, qx, preferred_element_type=F32)
    for g in groups:
        st = st_sc[g] + tw_ref[g]
        p = jnp.exp2(st - jnp.max(st, axis=0, keepdims=True))
        acc = jnp.dot(vt_ref[0, g, :, pl.ds(start, WIN_W)], p.astype(BF16), preferred_element_type=F32)
        for h in range(2):
            o_ref[0, h * Q_BLOCK:(h + 1) * Q_BLOCK, g * GROUP_W:(g + 1) * GROUP_W] = _finish_t(
                acc[:, h * ROWS:(h + 1) * ROWS])


def _win_prompt(q, kx_pad, vt_pad, tw):
    b, l, _ = q.shape
    lp = kx_pad.shape[1]
    tq = 2 * Q_BLOCK
    return pl.pallas_call(
        _win_kernel,
        out_shape=jax.ShapeDtypeStruct((b, l, NSA_WIDTH), F32),
        grid=(b, l // tq),
        in_specs=[pl.BlockSpec((1, tq, NSA_WIDTH), lambda bi, i: (bi, i, 0)),
                  pl.BlockSpec((1, lp, 2 * LANE), lambda bi, i: (bi, 0, 0)),
                  pl.BlockSpec((1, NSA_KV_HEADS, HEAD_DIM + ONES_ROWS, lp), lambda bi, i: (bi, 0, 0, 0)),
                  pl.BlockSpec(tw.shape, lambda bi, i: (0, 0, 0))],
        out_specs=pl.BlockSpec((1, tq, NSA_WIDTH), lambda bi, i: (bi, i, 0)),
        scratch_shapes=[pltpu.VMEM((NSA_KV_HEADS, WIN_W, 2 * ROWS), F32)],
        compiler_params=pltpu.CompilerParams(vmem_limit_bytes=VMEM_LIMIT),
        name="nsa_win_prompt",
    )(q, kx_pad, vt_pad, tw)


def _dense1_kernel(q_ref, k_ref, v_ref, bm_ref, o_ref, ps_ref):
    k = k_ref[0].astype(BF16)
    v = v_ref[0].astype(BF16)
    for g in range(NSA_KV_HEADS):
        rows = slice(g * NSA_REP, (g + 1) * NSA_REP)
        s = _dot_nt(q_ref[0, rows, :], k) + bm_ref[rows, :]
        p = jnp.exp(s - jnp.max(s, axis=-1, keepdims=True))
        pc = p / jnp.sum(p, axis=-1, keepdims=True)
        o_ref[0, rows, :] = jnp.dot(pc.astype(BF16), v, preferred_element_type=F32)
        ps_ref[0, g:g + 1, :] = jnp.sum(pc, axis=0, keepdims=True)


def _dense1(q_pad, kv, biasmask):
    nb, n, _ = kv.shape
    return pl.pallas_call(
        _dense1_kernel,
        out_shape=(jax.ShapeDtypeStruct((nb, NSA_HEADS, LANE), F32),
                   jax.ShapeDtypeStruct((nb, NSA_KV_HEADS, n), F32)),
        grid=(nb,),
        in_specs=[pl.BlockSpec((1, NSA_HEADS, LANE), lambda b: (b, 0, 0)),
                  pl.BlockSpec((1, n, LANE), lambda b: (b, 0, 0)),
                  pl.BlockSpec((1, n, LANE), lambda b: (b, 0, 1)),
                  pl.BlockSpec((NSA_HEADS, n), lambda b: (0, 0))],
        out_specs=(pl.BlockSpec((1, NSA_HEADS, LANE), lambda b: (b, 0, 0)),
                   pl.BlockSpec((1, NSA_KV_HEADS, n), lambda b: (b, 0, 0))),
        name="nsa_dense_sample",
    )(q_pad, kv, kv, biasmask)


def _take_group_half(o_pad):
    nb = o_pad.shape[0]
    o = o_pad.reshape(nb, NSA_KV_HEADS, NSA_REP, NSA_KV_HEADS, HEAD_DIM)
    o = jnp.stack([o[:, g, :, g, :] for g in range(NSA_KV_HEADS)], axis=1)
    return o.reshape(nb, NSA_WIDTH)


def _topk_kernel(ps_ref, ov_ref, idx_ref, imp_sc, *, n_blocks, cur):
    n_rows = ps_ref.shape[0]
    n_cols = ov_ref.shape[1]
    ps = jnp.concatenate([ps_ref[...], jnp.zeros((LANE - n_rows, ps_ref.shape[1]), F32)], axis=0)
    imp = _dot32(ps, ov_ref[...])
    jj = lax.broadcasted_iota(jnp.int32, (LANE, n_cols), 1)
    forced = (jj == 0) | (jj == cur) | (jj == cur - 1)
    imp = jnp.where(forced, 1e6, jnp.where(jj <= cur, imp, -1e6))
    imp = jnp.where(jj < n_blocks, imp, -2e6)
    imp_t = imp.T
    imp_sc[...] = imp_t
    jrow = lax.broadcasted_iota(jnp.int32, (n_cols, LANE), 0)

    def body(k, rank):
        rk = imp_sc[pl.ds(k, 1), :]
        ahead = (rk > imp_t) | ((rk == imp_t) & (jrow > k))
        return rank + jnp.where(ahead, 1.0, 0.0)

    rank = lax.fori_loop(0, n_blocks, body, jnp.zeros((n_cols, LANE), F32))
    jf = jrow.astype(F32)
    rows = [jnp.sum(jnp.where(rank == float(r), jf, 0.0), axis=0, keepdims=True) for r in range(SLC_TOP)]
    idx_ref[...] = jnp.concatenate(rows, axis=0).astype(jnp.int32)


def _topk_sample(psum, ov, n_blocks, cur):
    n_rows, n_keys = psum.shape
    n_cols = ov.shape[1]
    return pl.pallas_call(
        functools.partial(_topk_kernel, n_blocks=n_blocks, cur=cur),
        out_shape=jax.ShapeDtypeStruct((SLC_TOP, LANE), jnp.int32),
        grid=(1,),
        in_specs=[pl.BlockSpec((n_rows, n_keys), lambda i: (0, 0)),
                  pl.BlockSpec(ov.shape, lambda i: (0, 0))],
        out_specs=pl.BlockSpec((SLC_TOP, LANE), lambda i: (0, 0)),
        scratch_shapes=[pltpu.VMEM((n_cols, LANE), F32)],
        name="nsa_topk_sample",
    )(psum, ov)


def _slc1_kernel(phys_ref, jsel_ref, q_ref, *refs, cur, past):
    pages = refs[:SLC_TOP]
    new_ref, bb_ref, o_ref = refs[SLC_TOP:]
    b = pl.program_id(0)
    g = pl.program_id(1)
    goff = pl.multiple_of(g * HEAD_DIM, HEAD_DIM)
    q = q_ref[0]
    lane = lax.broadcasted_iota(jnp.int32, (NSA_REP, PAGE_SIZE), 1)
    first = lax.broadcasted_iota(jnp.int32, (HEAD_DIM, PAGE_SIZE), 1) == 0
    new_k = jnp.where(first, new_ref[0, pl.ds(goff, HEAD_DIM), :], 0.0)
    new_v = jnp.where(first, new_ref[0, pl.ds(LANE + goff, HEAD_DIM), :], 0.0)
    scores, values = [], []
    for n in range(SLC_TOP):
        j = jsel_ref[(b * NSA_KV_HEADS + g) * SLC_TOP + n]
        kt = jnp.where(j == cur, new_k, pages[n][0, pl.ds(goff, HEAD_DIM), :])
        vt = jnp.where(j == cur, new_v, pages[n][0, pl.ds(LANE + goff, HEAD_DIM), :])
        s = jnp.dot(q, kt.astype(BF16), preferred_element_type=F32)
        ok = (lane // SLC_BLOCK == j % 2) & ((j // 2) * PAGE_SIZE + lane <= past)
        bias = bb_ref[jnp.clip(j - (cur - 3), 0, 3), pl.ds(pl.multiple_of(g * NSA_REP, NSA_REP), NSA_REP), :]
        scores.append(jnp.where(ok, s + bias, NEG))
        values.append(vt.astype(BF16))
    s_all = jnp.concatenate(scores, axis=1)
    p = jnp.exp(s_all - jnp.max(s_all, axis=-1, keepdims=True))
    acc = jnp.zeros((NSA_REP, HEAD_DIM), F32)
    for n in range(SLC_TOP):
        acc = acc + _dot_nt(p[:, n * PAGE_SIZE:(n + 1) * PAGE_SIZE].astype(BF16), values[n])
    o_ref[0] = acc / jnp.sum(p, axis=-1, keepdims=True)


def _slc_sample(q, cache_pages_t, phys, jsel, new_cols, biasblk, cur, past):
    nb = q.shape[0]
    idx = lambda b, g, n: (b * NSA_KV_HEADS + g) * SLC_TOP + n
    page_specs = [pl.BlockSpec((1, KV_COLS, PAGE_SIZE), lambda b, g, ph, js, n=n: (ph[idx(b, g, n)], 0, 0))
                  for n in range(SLC_TOP)]
    return pl.pallas_call(
        functools.partial(_slc1_kernel, cur=cur, past=past),
        out_shape=jax.ShapeDtypeStruct((nb, NSA_HEADS, HEAD_DIM), F32),
        grid_spec=pltpu.PrefetchScalarGridSpec(
            num_scalar_prefetch=2,
            grid=(nb, NSA_KV_HEADS),
            in_specs=[pl.BlockSpec((1, NSA_REP, HEAD_DIM), lambda b, g, ph, js: (b, g, 0))] + page_specs
            + [pl.BlockSpec((1, KV_COLS, 1), lambda b, g, ph, js: (b, 0, 0)),
               pl.BlockSpec(biasblk.shape, lambda b, g, ph, js: (0, 0, 0))],
            out_specs=pl.BlockSpec((1, NSA_REP, HEAD_DIM), lambda b, g, ph, js: (b, g, 0))),
        name="nsa_slc_sample",
    )(phys, jsel, q, *([cache_pages_t] * SLC_TOP), new_cols, biasblk)


def _out_kernel(x_ref, gate_ref, yssd_ref, zs_ref, oc_ref, os_ref, ow_ref, gl_ref, za_ref,
                nw1_ref, nw2_ref, w_ref, eg_ref, fg_ref, o_ref, *, per_row, final):
    gates = _sigmoid(gl_ref[...])
    gparts = _split_bf16(gates, 2)
    expand = lambda br: sum(jnp.dot(p, eg_ref[br], preferred_element_type=F32) for p in gparts)
    y_nsa = expand(0) * oc_ref[...] + expand(1) * os_ref[...] + expand(2) * ow_ref[...]

    def gated_norm(y, z, w):
        u = y * _silu(z)
        half = u.shape[1] // 2
        parts = []
        for g in range(2):
            ug = u[:, g * half:(g + 1) * half]
            parts.append(ug * lax.rsqrt(jnp.mean(ug * ug, axis=-1, keepdims=True) + NORM_EPS))
        return (jnp.concatenate(parts, axis=1) * w).astype(BF16)

    m1 = gated_norm(yssd_ref[...], zs_ref[...], nw1_ref[...])
    m2 = gated_norm(y_nsa, za_ref[...], nw2_ref[...])
    proj = (jnp.dot(m1, w_ref[0:SSD_WIDTH, :], preferred_element_type=F32)
            + jnp.dot(m2, w_ref[SSD_WIDTH:, :], preferred_element_type=F32))
    gate = gate_ref[...] if per_row else gate_ref[0]
    out = x_ref[...] + gate * proj
    if final:
        out = out * lax.rsqrt(jnp.mean(out * out, axis=-1, keepdims=True) + NORM_EPS) * fg_ref[...]
    o_ref[...] = out


def _gate_expand():
    e = np.zeros((3, LANE, NSA_WIDTH), np.float32)
    for br in range(3):
        for h in range(NSA_HEADS):
            e[br, br * NSA_HEADS + h, h * HEAD_DIM:(h + 1) * HEAD_DIM] = 1.0
    return jnp.asarray(e, BF16)


def _layer_out(x2d, gate, y_ssd, z_s, o_cmp, o_slc, o_win, gl, z_a, nw1, nw2, w_out_bf, final_g,
               rows_per_batch, final):
    m = x2d.shape[0]
    per_row = rows_per_batch == 1
    tm = m if per_row else 256
    if per_row:
        gate_spec = pl.BlockSpec((tm, D_MODEL), lambda i: (0, 0))
        gt = gate
    else:
        gate_spec = pl.BlockSpec((1, 1, D_MODEL), lambda i: ((i * tm) // rows_per_batch, 0, 0))
        gt = gate[:, None, :]
    row = lambda w: pl.BlockSpec((tm, w), lambda i: (i, 0))
    full = lambda shape: pl.BlockSpec(shape, lambda i: (0,) * len(shape))
    eg = _gate_expand()
    return pl.pallas_call(
        functools.partial(_out_kernel, per_row=per_row, final=final),
        out_shape=jax.ShapeDtypeStruct((m, D_MODEL), F32),
        grid=(m // tm,),
        in_specs=[row(D_MODEL), gate_spec, row(SSD_WIDTH), row(SSD_WIDTH), row(NSA_WIDTH), row(NSA_WIDTH),
                  row(NSA_WIDTH), row(LANE), row(NSA_WIDTH), full((1, SSD_WIDTH)), full((1, NSA_WIDTH)),
                  full(w_out_bf.shape), full(eg.shape), full((1, D_MODEL))],
        out_specs=row(D_MODEL),
        compiler_params=pltpu.CompilerParams(vmem_limit_bytes=VMEM_LIMIT),
        name="layer_out",
    )(x2d, gt, y_ssd, z_s, o_cmp, o_slc, o_win, gl, z_a, nw1.reshape(1, SSD_WIDTH), nw2.reshape(1, NSA_WIDTH),
      w_out_bf, eg, final_g.reshape(1, D_MODEL))


def _pad_in_weights(w_in):
    cols = []
    off = 0
    for size, width in zip(_SEG_SIZES, _SEG_PAD):
        seg = w_in[:, off:off + size]
        cols.append(jnp.pad(seg, ((0, 0), (0, width - size))))
        off += size
    return jnp.concatenate(cols, axis=1).astype(BF16)


def _values_t(v_pad):
    b, rows, _ = v_pad.shape
    vt = jnp.transpose(v_pad.reshape(b, rows, NSA_KV_HEADS, HEAD_DIM), (0, 2, 3, 1))
    return jnp.concatenate([vt, jnp.ones((b, NSA_KV_HEADS, ONES_ROWS, rows), BF16)], axis=2)


def _front_pad_bf16(kv, rows):
    return jnp.pad(kv, ((0, 0), (rows, 0), (0, 0))).astype(BF16)


def kernel(x_prompt, x_sample, cache_cmp_kv, cache_slc_kv, state_win_kv, state_conv, state_ssm, page_table,
           c_prompt, c_sample, norm_g, ada_w, ada_b, w_in, conv_w, conv_b, dt_bias, a_log, d_skip,
           ssd_norm_w, cmp_pe, cmp_w1, cmp_w2, nsa_norm_w, w_out, rel_bias, final_norm_g):
    nbp, lp, _ = x_prompt.shape
    nbs = x_sample.shape[0]
    depth = w_in.shape[0]
    n_pool = cache_cmp_kv.shape[1]
    n_pages = page_table.shape[1]
    past = n_pages * PAGE_SIZE
    w_buf = state_win_kv.shape[2]
    kv_row = (2, NSA_KV_HEADS, HEAD_DIM)

    n_ck = lp // CMP_STRIDE
    n_sb = lp // SLC_BLOCK
    qi = np.arange(Q_BLOCK)[:, None]
    c0 = n_ck - 4
    pattern = _bias_of_dist(rel_bias, qi - CMP_STRIDE * (np.arange(n_ck)[None, :] - c0) - (CMP_STRIDE - 1))
    far = rel_bias.astype(F32)[REL_BUCKETS - 1][:, None, None]
    par = np.arange(2)[:, None, None]
    qq = np.arange(Q_BLOCK)[None, None, :]
    dist_n = qq - np.arange(SLC_NEARW)[None, :, None] + Q_BLOCK * par + (SLC_NEAR - Q_BLOCK)
    tnt = jnp.stack([_toeplitz_bias(rel_bias, SLC_NEARW, Q_BLOCK, Q_BLOCK * p + SLC_NEAR - Q_BLOCK)
                     for p in range(2)], axis=1)
    tnt = jnp.where(jnp.asarray((dist_n >= 0) & (dist_n <= qq + SLC_NEAR - Q_BLOCK))[None],
                    (tnt - far[..., None]) * LOG2E, NEG)
    tnt = jnp.transpose(tnt.reshape(NSA_KV_HEADS, NSA_REP, 2, SLC_NEARW, Q_BLOCK), (2, 0, 3, 1, 4))
    tnt = tnt.reshape(2, NSA_KV_HEADS, SLC_NEARW, ROWS)
    tile = _group_tile()
    key_row = np.arange(SLC_PAD + lp + SLC_BACK) - SLC_PAD
    assert (key_row[-1] // SLC_BLOCK) < PAD_FLAG_COL
    cols = np.arange(LANE)[None, :]
    blk_onehot = jnp.asarray(((key_row[:, None] >= 0) & (key_row[:, None] // SLC_BLOCK == cols))
                             | ((key_row[:, None] < 0) & (cols == PAD_FLAG_COL)), BF16)
    dist_w = qq - np.arange(WIN_W)[None, :, None] + Q_BLOCK * par + WIN_PAD
    tw = jnp.stack([_toeplitz_bias(rel_bias, WIN_W, Q_BLOCK, Q_BLOCK * p + WIN_PAD) for p in range(2)], axis=1)
    tw = jnp.where(jnp.asarray((dist_w >= 0) & (dist_w < WINDOW))[None], tw * LOG2E, NEG)
    tw = jnp.transpose(tw.reshape(NSA_KV_HEADS, NSA_REP, 2, WIN_W, Q_BLOCK), (2, 0, 3, 1, 4))
    tw = tw.reshape(2, NSA_KV_HEADS, WIN_W, ROWS)
    tw = jnp.concatenate([tw[0], tw[1]], axis=-1)
    win_flag = jnp.asarray((np.arange(WIN_PAD + lp)[:, None] < WIN_PAD) & (np.arange(LANE)[None, :] == 0), BF16)
    ov_p = _overlap_matrix(n_ck, n_sb, LANE)

    n_cs = past // CMP_STRIDE
    cur = past // SLC_BLOCK
    n_blk_s = cur + 1
    n_cols_s = -(-n_blk_s // LANE) * LANE
    ov_s = _overlap_matrix(n_cs, n_blk_s, n_cols_s)
    m_s = np.arange(n_cs)
    bm_c = jnp.where(jnp.asarray(m_s >= 1)[None, :],
                     _bias_of_dist(rel_bias, past - (CMP_STRIDE * m_s + CMP_STRIDE - 1)), NEG)
    n_w = -(-(w_buf + 1) // LANE) * LANE
    iw = np.arange(n_w)
    dw = w_buf - iw
    ok_w = (iw <= w_buf) & (dw >= 0) & (dw < WINDOW) & (past - w_buf + iw >= 0)
    bm_w = jnp.where(jnp.asarray(ok_w)[None, :], _bias_of_dist(rel_bias, dw), NEG)
    jb = (cur - 3 + np.arange(4))[:, None]
    biasblk = _bias_of_dist(rel_bias, past - SLC_BLOCK * jb - np.arange(SLC_BLOCK)[None, :])
    biasblk = jnp.transpose(biasblk, (1, 0, 2))
    biasblk = jnp.concatenate([biasblk, biasblk], axis=-1)

    pages_t = lambda c: jnp.transpose(c, (0, 1, 3, 4, 5, 2)).reshape(depth * n_pool, KV_COLS, PAGE_SIZE)
    cmp_pages_t = pages_t(cache_cmp_kv)
    slc_pages_t = pages_t(cache_slc_kv)
    prompt_pages = jnp.arange(nbp * (lp // PAGE_SIZE), dtype=jnp.int32).reshape(nbp, lp // PAGE_SIZE)

    c_all = jnp.concatenate([c_prompt, c_sample], axis=0)
    xp = x_prompt.reshape(nbp * lp, D_MODEL)
    xs = x_sample.reshape(nbs, D_MODEL)
    outs = {k: [] for k in ("pc", "ps", "pw", "pconv", "pssm", "sc", "ss", "sw", "sconv", "sssm")}

    for l in range(depth):
        final = l == depth - 1
        w_pad = _pad_in_weights(w_in[l])
        w_out_bf = w_out[l].astype(BF16)
        cweights = _compress_weights(cmp_pe[l], cmp_w1[l], cmp_w2[l])
        mod = _modulation(c_all, ada_w[l], ada_b[l])
        shift, scale, gate = mod[:, :D_MODEL], mod[:, D_MODEL:2 * D_MODEL], mod[:, 2 * D_MODEL:]

        z_s, xbc, dt, q, kvc, kvs, kvw, gl, z_a = _in_projection(xp, norm_g[l], scale[:nbp], shift[:nbp], w_pad, lp)
        xbc3 = xbc.reshape(nbp, lp, CONV_DIM)
        y_ssd, h_fin = _ssd_prompt(xbc3, dt.reshape(nbp, lp, LANE), conv_w[l], conv_b[l], dt_bias[l], a_log[l],
                                   d_skip[l])
        q3 = q.reshape(nbp, lp, NSA_WIDTH)
        kvc3, kvs3, kvw3 = (t.reshape(nbp, lp, KV_COLS) for t in (kvc, kvs, kvw))
        kc = _compress(kvc.reshape(nbp * (lp // PAGE_SIZE), PAGE_SIZE, KV_COLS), prompt_pages, cweights, False)
        o_cmp, sel = _cmp_prompt(q3, kc, pattern, ov_p)
        ks_pad = jnp.pad(kvs3, ((0, 0), (SLC_PAD, SLC_BACK), (0, 0))).astype(BF16)
        kx_pad = jnp.concatenate([ks_pad[:, :, :LANE], jnp.broadcast_to(blk_onehot, (nbp,) + blk_onehot.shape)],
                                 axis=-1)
        o_slc = _slc_prompt(q3, sel, kx_pad, _values_t(ks_pad[:, :, LANE:]), tnt, tile)
        kw_pad = _front_pad_bf16(kvw3, WIN_PAD)
        kwx_pad = jnp.concatenate([kw_pad[:, :, :LANE], jnp.broadcast_to(win_flag, (nbp,) + win_flag.shape)], axis=-1)
        o_win = _win_prompt(q3, kwx_pad, _values_t(kw_pad[:, :, LANE:]), tw)
        xp = _layer_out(xp, gate[:nbp], y_ssd.reshape(nbp * lp, SSD_WIDTH), z_s,
                        o_cmp.reshape(nbp * lp, NSA_WIDTH), o_slc.reshape(nbp * lp, NSA_WIDTH),
                        o_win.reshape(nbp * lp, NSA_WIDTH), gl, z_a, ssd_norm_w[l], nsa_norm_w[l], w_out_bf,
                        final_norm_g, lp, final)
        outs["pc"].append(kvc3.reshape((nbp, lp) + kv_row))
        outs["ps"].append(kvs3.reshape((nbp, lp) + kv_row))
        outs["pw"].append(kvw3[:, -min(WINDOW, lp):].reshape((nbp, min(WINDOW, lp)) + kv_row))
        outs["pconv"].append(xbc3[:, -(SSD_CONV - 1):])
        outs["pssm"].append(h_fin)

        z_s2, xbc2, dt2, q2, kvc2, kvs2, kvw2, gl2, z_a2 = _in_projection(
            xs, norm_g[l], scale[nbp:], shift[nbp:], w_pad, 1)
        y_ssd2, h2 = _ssd_step(xbc2, state_conv[l], dt2, state_ssm[l], conv_w[l], conv_b[l], dt_bias[l],
                               a_log[l], d_skip[l])
        qh = q2.reshape(nbs, NSA_KV_HEADS, NSA_REP, HEAD_DIM)
        zq = jnp.zeros((nbs, NSA_REP, HEAD_DIM), F32)
        q_pad = jnp.stack([jnp.concatenate([qh[:, 0], zq], axis=-1), jnp.concatenate([zq, qh[:, 1]], axis=-1)],
                          axis=1).reshape(nbs, NSA_HEADS, LANE).astype(BF16)
        kc2 = _compress(cmp_pages_t, page_table + l * n_pool, cweights, True)
        oc2, psum = _dense1(q_pad, kc2, bm_c)
        sel_idx = _topk_sample(psum.reshape(nbs * NSA_KV_HEADS, n_cs), ov_s, n_blk_s, cur)
        jsel = sel_idx[:, :nbs * NSA_KV_HEADS].T.reshape(nbs, NSA_KV_HEADS, SLC_TOP)
        jc = jnp.minimum(jsel, cur - 1)
        page = jnp.take_along_axis(page_table, (jc // 2).reshape(nbs, -1), axis=1).reshape(jsel.shape)
        phys = jnp.where(jsel < cur, page + l * n_pool, 0).astype(jnp.int32)
        os2 = _slc_sample(q2.reshape(nbs, NSA_HEADS, HEAD_DIM).astype(BF16), slc_pages_t, phys.reshape(-1),
                          jsel.reshape(-1), kvs2.reshape(nbs, KV_COLS, 1), biasblk, cur, past)
        kw_full = jnp.concatenate([state_win_kv[l].reshape(nbs, w_buf, KV_COLS), kvw2[:, None, :]], axis=1)
        kw_in = jnp.pad(kw_full, ((0, 0), (0, n_w - (w_buf + 1)), (0, 0)))
        ow2, _ = _dense1(q_pad, kw_in, bm_w)
        xs = _layer_out(xs, gate[nbp:], y_ssd2, z_s2, _take_group_half(oc2), os2.reshape(nbs, NSA_WIDTH),
                        _take_group_half(ow2), gl2, z_a2, ssd_norm_w[l], nsa_norm_w[l], w_out_bf,
                        final_norm_g, 1, final)
        outs["sc"].append(kvc2.reshape((nbs, 1) + kv_row))
        outs["ss"].append(kvs2.reshape((nbs, 1) + kv_row))
        outs["sw"].append(kw_full[:, -w_buf:].reshape((nbs, w_buf) + kv_row))
        outs["sconv"].append(jnp.concatenate([state_conv[l][:, 1:], xbc2[:, None, :]], axis=1))
        outs["sssm"].append(h2)

    st = lambda k: jnp.stack(outs[k])
    return (xp.reshape(nbp, lp, D_MODEL), xs.reshape(nbs, 1, D_MODEL),
            st("pc"), st("ps"), st("pw"), st("pconv"), st("pssm"),
            st("sc"), st("ss"), st("sw"), st("sconv"), st("sssm"))
```

```python
import functools
import math

import numpy as np
import jax
import jax.numpy as jnp
from jax import lax
from jax.experimental import pallas as pl
from jax.experimental.pallas import tpu as pltpu

F32 = jnp.float32
BF16 = jnp.bfloat16
HIGHEST = lax.Precision.HIGHEST

D_MODEL = 1024
HEAD_DIM = 64
SSD_WIDTH = 1024
SSD_HEADS = 16
SSD_GROUPS = 2
SSD_STATE = 128
SSD_CONV = 4
SSD_CHUNK = 256
CONV_DIM = SSD_WIDTH + 2 * SSD_GROUPS * SSD_STATE
NSA_WIDTH = 1024
NSA_HEADS = 16
NSA_KV_HEADS = 2
NSA_REP = NSA_HEADS // NSA_KV_HEADS
CMP_BLOCK = 32
CMP_STRIDE = 16
CMP_HID = 2 * HEAD_DIM
SLC_BLOCK = 64
SLC_TOP = 16
WINDOW = 512
Q_BLOCK = 64
REL_BUCKETS = 32
REL_MAX_DIST = 128
NORM_EPS = 1e-6
KV_COLS = 2 * NSA_KV_HEADS * HEAD_DIM
PAGE_SIZE = 128
NEG = -1e30
LOG2E = 1.4426950408889634

LANE = 128
HALF = LANE // 2
GROUP_W = NSA_REP * HEAD_DIM
ROWS = NSA_REP * Q_BLOCK
SLC_NEAR = 4 * SLC_BLOCK
SLC_PAD = SLC_NEAR
SLC_NEARW = SLC_NEAR + LANE
ONES_ROWS = 8
PAD_FLAG_COL = LANE - 1
SLC_BACK = 2 * SLC_NEAR
WIN_W = WINDOW + 2 * Q_BLOCK
WIN_PAD = WINDOW
CMP_PAGES_MAX = 32
VMEM_LIMIT = 48 * 1024 * 1024

_SEG_NAMES = ("z_s", "xbc", "dt", "q", "kvc", "kvs", "kvw", "gl", "z_a")
_SEG_SIZES = (SSD_WIDTH, CONV_DIM, SSD_HEADS, NSA_WIDTH, KV_COLS, KV_COLS, KV_COLS, 3 * NSA_HEADS, NSA_WIDTH)
_SEG_PAD = tuple(-(-s // LANE) * LANE for s in _SEG_SIZES)
_SEG_OFF = tuple(int(o) for o in np.cumsum((0,) + _SEG_PAD[:-1]))
IN_PAD = int(sum(_SEG_PAD))


def _sigmoid(x):
    return 1.0 / (1.0 + jnp.exp(-x))


def _silu(x):
    return x * _sigmoid(x)


def _dot32(a, b):
    return jnp.dot(a, b, precision=HIGHEST, preferred_element_type=F32)


def _split_bf16(x, terms):
    parts = []
    for _ in range(terms):
        p = x.astype(BF16)
        parts.append(p)
        x = x - p.astype(F32)
    return parts


def _expand2(x, sel2_bf16):
    hi, lo = _split_bf16(x, 2)
    return jnp.dot(jnp.concatenate([hi, lo], axis=1), sel2_bf16, preferred_element_type=F32)


def _sel_dot(sel_bf16, x, terms):
    return sum(jnp.dot(sel_bf16, p, preferred_element_type=F32) for p in _split_bf16(x, terms))


def _dot_nt(a, b):
    return lax.dot_general(a, b, (((1,), (1,)), ((), ())), preferred_element_type=F32)


def _bucket_table():
    n = np.arange(REL_MAX_DIST + 1)
    max_exact = REL_BUCKETS // 2
    nf = np.maximum(n, 1).astype(np.float32)
    large = max_exact + (np.log(nf / np.float32(max_exact)) / np.float32(math.log(REL_MAX_DIST / max_exact))
                         * np.float32(REL_BUCKETS - max_exact)).astype(np.int32)
    large = np.minimum(large, REL_BUCKETS - 1)
    return np.where(n < max_exact, n, large).astype(np.int32)


_BUCKETS = _bucket_table()


def _bias_of_dist(rel_bias, dist):
    idx = _BUCKETS[np.clip(dist, 0, REL_MAX_DIST)]
    out = jnp.take(rel_bias.astype(F32), jnp.asarray(idx.reshape(-1)), axis=0)
    return out.T.reshape((NSA_HEADS,) + dist.shape)


def _toeplitz_bias(rel_bias, rows, cols, c0):
    n = rows + cols
    d = np.arange(n) + c0 - (rows - 1)
    v = jnp.take(rel_bias.astype(F32), jnp.asarray(_BUCKETS[np.clip(d, 0, REL_MAX_DIST)]), axis=0).T
    flat = jnp.tile(v, (1, rows + 1))[:, :rows * (n + 1)]
    hank = flat.reshape(NSA_HEADS, rows, n + 1)[:, :, :cols]
    return hank[:, ::-1, :]


def _mod_kernel(c_ref, w_ref, b_ref, o_ref):
    o_ref[...] = _dot32(_silu(c_ref[...]), w_ref[...]) + b_ref[...]


def _modulation(c, w, b):
    m, d = c.shape
    n = w.shape[1]
    tn = 512
    return pl.pallas_call(
        _mod_kernel,
        out_shape=jax.ShapeDtypeStruct((m, n), F32),
        grid=(n // tn,),
        in_specs=[pl.BlockSpec((m, d), lambda j: (0, 0)),
                  pl.BlockSpec((d, tn), lambda j: (0, j)),
                  pl.BlockSpec((1, tn), lambda j: (0, j))],
        out_specs=pl.BlockSpec((m, tn), lambda j: (0, j)),
        name="adaln_mod",
    )(c, w, b.reshape(1, n))


def _inproj_kernel(x_ref, g_ref, sc_ref, sh_ref, w_ref, *out_refs, per_row):
    x = x_ref[...]
    xn = x * lax.rsqrt(jnp.mean(x * x, axis=-1, keepdims=True) + NORM_EPS)
    sc = sc_ref[...] if per_row else sc_ref[0]
    sh = sh_ref[...] if per_row else sh_ref[0]
    h = ((xn * g_ref[...]) * (1.0 + sc) + sh).astype(BF16)
    for name, off, width, ref in zip(_SEG_NAMES, _SEG_OFF, _SEG_PAD, out_refs):
        r = jnp.dot(h, w_ref[:, off:off + width], preferred_element_type=F32)
        if name == "q":
            r = r * (HEAD_DIM ** -0.5)
        ref[...] = r


def _in_projection(x2d, g, scale, shift, w_pad, rows_per_batch):
    m = x2d.shape[0]
    per_row = rows_per_batch == 1
    tm = m if per_row else 256
    if per_row:
        mod_spec = pl.BlockSpec((tm, D_MODEL), lambda i: (0, 0))
        sc, sh = scale, shift
    else:
        mod_spec = pl.BlockSpec((1, 1, D_MODEL), lambda i: ((i * tm) // rows_per_batch, 0, 0))
        sc, sh = scale[:, None, :], shift[:, None, :]
    outs = tuple(jax.ShapeDtypeStruct((m, w), F32) for w in _SEG_PAD)
    return pl.pallas_call(
        functools.partial(_inproj_kernel, per_row=per_row),
        out_shape=outs,
        grid=(m // tm,),
        in_specs=[pl.BlockSpec((tm, D_MODEL), lambda i: (i, 0)),
                  pl.BlockSpec((1, D_MODEL), lambda i: (0, 0)),
                  mod_spec, mod_spec,
                  pl.BlockSpec((D_MODEL, IN_PAD), lambda i: (0, 0))],
        out_specs=tuple(pl.BlockSpec((tm, w), lambda i: (i, 0)) for w in _SEG_PAD),
        compiler_params=pltpu.CompilerParams(vmem_limit_bytes=VMEM_LIMIT),
        name="in_projection",
    )(x2d, g.reshape(1, D_MODEL), sc, sh, w_pad)


def _softplus(x):
    return jnp.maximum(x, 0.0) + jnp.log(1.0 + jnp.exp(-jnp.abs(x)))


def _ssd_kernel(xbc_ref, dt_ref, cw_ref, cb_ref, dtb_ref, alog_ref, dsk_ref, e_ref, tril_ref,
                y_ref, hfin_ref, xe_sc, st_sc):
    c = pl.program_id(1)
    q = SSD_CHUNK
    n_pairs = SSD_HEADS // 2

    @pl.when(c == 0)
    def _():
        xe_sc[0:8, :] = jnp.zeros((8, CONV_DIM), F32)
        st_sc[...] = jnp.zeros(st_sc.shape, F32)

    xe_sc[8:8 + q, :] = xbc_ref[0]
    acc = cb_ref[...] + cw_ref[0:1, :] * xe_sc[5:5 + q, :]
    for k in range(1, SSD_CONV):
        acc = acc + cw_ref[k:k + 1, :] * xe_sc[5 + k:5 + k + q, :]
    u = _silu(acc)
    xe_sc[0:8, :] = xe_sc[q:q + 8, :]

    xs = u[:, :SSD_WIDTH]
    gn = SSD_GROUPS * SSD_STATE
    bm = u[:, SSD_WIDTH:SSD_WIDTH + gn]
    cm = u[:, SSD_WIDTH + gn:]

    dt = _softplus(dt_ref[0] + dtb_ref[...])
    a = dt * (-jnp.exp(alog_ref[...]))
    cs = _sel_dot(tril_ref[...], a, 3)
    cs_t = cs.T
    cs_last = cs[q - 1:q, :]
    e = e_ref[...]
    dt_e = _expand2(dt, e)
    w_e = _expand2(dt * jnp.exp(cs_last - cs), e)
    ecs_e = _expand2(jnp.exp(cs), e)
    tot_e = _expand2(jnp.broadcast_to(jnp.exp(cs_last), (8, LANE)), e)[0:1, :]
    xdt = (xs * dt_e).astype(BF16)
    xw = (xs * w_e).astype(BF16)

    li = lax.broadcasted_iota(jnp.int32, (q, q), 0)
    si = lax.broadcasted_iota(jnp.int32, (q, q), 1)
    tri = li >= si
    lane = lax.broadcasted_iota(jnp.int32, (q, LANE), 1)

    for g in range(SSD_GROUPS):
        cg = cm[:, g * SSD_STATE:(g + 1) * SSD_STATE].astype(BF16)
        bg = bm[:, g * SSD_STATE:(g + 1) * SSD_STATE]
        cb = _dot_nt(cg, bg.astype(BF16))
        bg_t = bg.T.astype(BF16)
        for jp in range(n_pairs // SSD_GROUPS):
            j = g * (n_pairs // SSD_GROUPS) + jp
            sl = slice(j * LANE, (j + 1) * LANE)
            xdt_p = xdt[:, sl]
            ys = []
            for hh in (2 * j, 2 * j + 1):
                diff = cs[:, hh:hh + 1] - cs_t[hh:hh + 1, :]
                lmat = jnp.exp(jnp.where(tri, diff, NEG))
                ys.append(jnp.dot((cb * lmat).astype(BF16), xdt_p, preferred_element_type=F32))
            y_diag = jnp.where(lane < HALF, ys[0], ys[1])
            st = st_sc[j]
            y_off = jnp.dot(cg, st.astype(BF16), preferred_element_type=F32) * ecs_e[:, sl]
            y_ref[0, :, sl] = y_diag + y_off + xs[:, sl] * dsk_ref[:, sl]
            new = jnp.dot(bg_t, xw[:, sl], preferred_element_type=F32)
            st_sc[j] = st * tot_e[:, sl] + new

    @pl.when(c == pl.num_programs(1) - 1)
    def _():
        for j in range(n_pairs):
            hfin_ref[0, j * LANE:(j + 1) * LANE, :] = st_sc[j].T


def _head_expand():
    e = np.zeros((LANE, SSD_WIDTH), np.float32)
    for h in range(SSD_HEADS):
        e[h, h * HEAD_DIM:(h + 1) * HEAD_DIM] = 1.0
    return jnp.asarray(e)


def _pad_lanes(v):
    return jnp.pad(v.astype(F32), (0, LANE - v.shape[0])).reshape(1, LANE)


def _ssd_prompt(xbc, dt, conv_w, conv_b, dt_bias, a_log, d_skip):
    b, l, _ = xbc.shape
    nc = l // SSD_CHUNK
    full = lambda shape: pl.BlockSpec(shape, lambda i, c: (0,) * len(shape))
    y, hfin = pl.pallas_call(
        _ssd_kernel,
        out_shape=(jax.ShapeDtypeStruct((b, l, SSD_WIDTH), F32),
                   jax.ShapeDtypeStruct((b, SSD_HEADS * HEAD_DIM, SSD_STATE), F32)),
        grid=(b, nc),
        in_specs=[pl.BlockSpec((1, SSD_CHUNK, CONV_DIM), lambda i, c: (i, c, 0)),
                  pl.BlockSpec((1, SSD_CHUNK, LANE), lambda i, c: (i, c, 0)),
                  full((SSD_CONV, CONV_DIM)), full((1, CONV_DIM)), full((1, LANE)), full((1, LANE)),
                  full((1, SSD_WIDTH)), full((2 * LANE, SSD_WIDTH)), full((SSD_CHUNK, SSD_CHUNK))],
        out_specs=(pl.BlockSpec((1, SSD_CHUNK, SSD_WIDTH), lambda i, c: (i, c, 0)),
                   pl.BlockSpec((1, SSD_HEADS * HEAD_DIM, SSD_STATE), lambda i, c: (i, 0, 0))),
        scratch_shapes=[pltpu.VMEM((SSD_CHUNK + 8, CONV_DIM), F32),
                        pltpu.VMEM((SSD_HEADS // 2, SSD_STATE, LANE), F32)],
        compiler_params=pltpu.CompilerParams(dimension_semantics=("arbitrary", "arbitrary"),
                                             vmem_limit_bytes=VMEM_LIMIT),
        name="ssd_prompt",
    )(xbc, dt, conv_w, conv_b.reshape(1, CONV_DIM), _pad_lanes(dt_bias), _pad_lanes(a_log),
      jnp.repeat(d_skip.astype(F32), HEAD_DIM).reshape(1, SSD_WIDTH),
      jnp.concatenate([_head_expand(), _head_expand()], axis=0).astype(BF16),
      jnp.asarray(np.tril(np.ones((SSD_CHUNK, SSD_CHUNK), np.float32)), BF16))
    return y, hfin.reshape(b, SSD_HEADS, HEAD_DIM, SSD_STATE)


def _ssd_step_kernel(xbc_ref, c0_ref, c1_ref, c2_ref, dt_ref, h0_ref, cw_ref, cb_ref, dtb_ref, alog_ref,
                     dsk_ref, e_ref, y_ref, hout_ref, xt_sc, dect_sc, bc_sc, yt_sc, xs_sc):
    b = pl.program_id(0)
    nb = xbc_ref.shape[0]
    rows = SSD_HEADS * HEAD_DIM
    gn = SSD_GROUPS * SSD_STATE

    @pl.when(b == 0)
    def _():
        acc = (cb_ref[...] + cw_ref[0:1, :] * c0_ref[...] + cw_ref[1:2, :] * c1_ref[...]
               + cw_ref[2:3, :] * c2_ref[...] + cw_ref[3:4, :] * xbc_ref[...])
        u = _silu(acc)
        xs = u[:, :SSD_WIDTH]
        dt = _softplus(dt_ref[...] + dtb_ref[...])
        dec = jnp.exp(dt * (-jnp.exp(alog_ref[...])))
        e = e_ref[...]
        xdt = xs * _dot32(dt, e)
        dec_e = _dot32(dec, e)
        pad = jnp.zeros((LANE - nb, SSD_WIDTH), F32)
        xt_sc[...] = jnp.concatenate([xdt, pad], axis=0).T
        dect_sc[...] = jnp.concatenate([dec_e, pad], axis=0).T
        bc_sc[...] = u[:, SSD_WIDTH:]
        xs_sc[...] = xs
        yt_sc[...] = jnp.zeros(yt_sc.shape, F32)

    ri = lax.broadcasted_iota(jnp.int32, (LANE, LANE), 0)
    onehot = jnp.where(ri == b, 1.0, 0.0)
    xcol = _dot32(xt_sc[...], onehot)
    dcol = _dot32(dect_sc[...], onehot)
    bc = bc_sc[pl.ds(b, 1), :]
    row = lax.broadcasted_iota(jnp.int32, (rows, SSD_STATE), 0)
    first = row < rows // SSD_GROUPS
    b_full = jnp.where(first, bc[:, 0:SSD_STATE], bc[:, SSD_STATE:gn])
    c_full = jnp.where(first, bc[:, gn:gn + SSD_STATE], bc[:, gn + SSD_STATE:])
    new = dcol * h0_ref[0] + xcol * b_full
    hout_ref[0] = new
    ycol = _dot32(new * c_full, jnp.ones((SSD_STATE, LANE), F32))
    lane = lax.broadcasted_iota(jnp.int32, (rows, LANE), 1)
    yt_sc[...] = jnp.where(lane == b, ycol, yt_sc[...])

    @pl.when(b == nb - 1)
    def _():
        y_ref[...] = yt_sc[...].T[0:nb, :] + xs_sc[...] * dsk_ref[...]


def _ssd_step(xbc, conv_state, dt, h0, conv_w, conv_b, dt_bias, a_log, d_skip):
    nb = xbc.shape[0]
    rows = SSD_HEADS * HEAD_DIM
    full = lambda shape: pl.BlockSpec(shape, lambda i: (0,) * len(shape))
    y, hout = pl.pallas_call(
        _ssd_step_kernel,
        out_shape=(jax.ShapeDtypeStruct((nb, SSD_WIDTH), F32),
                   jax.ShapeDtypeStruct((nb, rows, SSD_STATE), F32)),
        grid=(nb,),
        in_specs=[full((nb, CONV_DIM)), full((nb, CONV_DIM)), full((nb, CONV_DIM)), full((nb, CONV_DIM)),
                  full((nb, LANE)),
                  pl.BlockSpec((1, rows, SSD_STATE), lambda i: (i, 0, 0)),
                  full((SSD_CONV, CONV_DIM)), full((1, CONV_DIM)), full((1, LANE)), full((1, LANE)),
                  full((1, SSD_WIDTH)), full((LANE, SSD_WIDTH))],
        out_specs=(full((nb, SSD_WIDTH)),
                   pl.BlockSpec((1, rows, SSD_STATE), lambda i: (i, 0, 0))),
        scratch_shapes=[pltpu.VMEM((rows, LANE), F32), pltpu.VMEM((rows, LANE), F32),
                        pltpu.VMEM((nb, 2 * SSD_GROUPS * SSD_STATE), F32),
                        pltpu.VMEM((rows, LANE), F32), pltpu.VMEM((nb, SSD_WIDTH), F32)],
        compiler_params=pltpu.CompilerParams(dimension_semantics=("arbitrary",)),
        name="ssd_step",
    )(xbc, conv_state[:, 0], conv_state[:, 1], conv_state[:, 2], dt, h0.reshape(nb, rows, SSD_STATE),
      conv_w, conv_b.reshape(1, CONV_DIM), _pad_lanes(dt_bias), _pad_lanes(a_log),
      jnp.repeat(d_skip.astype(F32), HEAD_DIM).reshape(1, SSD_WIDTH), _head_expand())
    return y, hout.reshape(nb, SSD_HEADS, HEAD_DIM, SSD_STATE)


def _compress_kernel(pt_ref, *refs, transposed, CMP_PAGES):
    if transposed:
        w1_ref, pe_ref, w2_ref, out_ref, sh_sc, pe_sc, xs_sc = refs[CMP_PAGES:]
        for i in range(CMP_PAGES):
            for k in range(2):
                xs_sc[i, k] = refs[i][0, k * LANE:(k + 1) * LANE, :].T
        pages = tuple(tuple(xs_sc.at[i, k] for i in range(CMP_PAGES)) for k in range(2))
    else:
        w1_ref, pe_ref, w2_ref, out_ref, sh_sc, pe_sc = refs[2 * CMP_PAGES:]
        pages = tuple(tuple(refs[k * CMP_PAGES + i].at[0] for i in range(CMP_PAGES)) for k in range(2))
    s = pl.program_id(1)
    segs = PAGE_SIZE // CMP_STRIDE
    rows = CMP_PAGES * segs
    hid2 = NSA_KV_HEADS * CMP_HID

    @pl.when(s == 0)
    def _():
        sh_sc[:, 0:8, :] = jnp.zeros((2, 8, hid2), F32)
        for k in range(2):
            t = jnp.zeros((8, 2 * hid2), F32)
            for o in range(0, CMP_STRIDE, 2):
                pe2 = jnp.concatenate([pe_ref[o, k], pe_ref[o + 1, k]], axis=1)
                t = t + jnp.dot(pe2.astype(BF16), w1_ref[o // 2, k], preferred_element_type=F32)
            pe_sc[k] = jnp.broadcast_to(t[0:1, 0:hid2] + t[1:2, hid2:], (8, hid2))

    def token_rows(k, o):
        return jnp.concatenate([p[pl.ds(o, segs, stride=CMP_STRIDE), :] for p in pages[k]], axis=0)

    for k in range(2):
        acc = jnp.zeros((rows, 2 * hid2), F32)
        for o in range(0, CMP_STRIDE, 2):
            xo = jnp.concatenate([token_rows(k, o), token_rows(k, o + 1)], axis=1)
            acc = acc + jnp.dot(xo.astype(BF16), w1_ref[o // 2, k], preferred_element_type=F32)
        sh_sc[k, 8:8 + rows, :] = acc[:, 0:hid2]
        pre = acc[:, hid2:] + sh_sc[k, 7:7 + rows, :] + pe_sc[k, 0:1, :]
        sh_sc[k, 0:8, :] = sh_sc[k, rows:rows + 8, :]
        out_ref[0, :, k * LANE:(k + 1) * LANE] = jnp.dot(_silu(pre).astype(BF16), w2_ref[k],
                                                         preferred_element_type=F32)


def _compress_weights(cmp_pe, cmp_w1, cmp_w2):
    span = CMP_BLOCK // CMP_STRIDE
    w1s = cmp_w1.astype(F32).reshape(2, span, CMP_STRIDE, HEAD_DIM, CMP_HID)
    z = jnp.zeros((2, span, CMP_STRIDE, HEAD_DIM, CMP_HID), F32)
    top = jnp.concatenate([w1s, z], axis=-1)
    bot = jnp.concatenate([z, w1s], axis=-1)
    bd = jnp.concatenate([top, bot], axis=-2)
    w1 = jnp.transpose(bd, (2, 0, 3, 1, 4)).reshape(CMP_STRIDE, 2, LANE, span * 2 * CMP_HID).astype(BF16)
    w1 = jnp.transpose(w1.reshape(CMP_STRIDE // 2, 2, 2, LANE, span * 2 * CMP_HID), (0, 2, 1, 3, 4))
    w1 = w1.reshape(CMP_STRIDE // 2, 2, 2 * LANE, span * 2 * CMP_HID)
    pe = cmp_pe.astype(F32).reshape(2, span, CMP_STRIDE, HEAD_DIM)
    pe = jnp.transpose(pe, (2, 0, 1, 3))
    pe = jnp.concatenate([pe, pe], axis=-1)
    pe = jnp.pad(pe, ((0, 0), (0, 0), (0, 8 - span), (0, 0)))
    w2 = cmp_w2.astype(F32)
    z2 = jnp.zeros_like(w2)
    w2bd = jnp.concatenate([jnp.concatenate([w2, z2], axis=-1), jnp.concatenate([z2, w2], axis=-1)],
                           axis=-2).astype(BF16)
    return w1, pe, w2bd


def _compress(pages_arr, page_ids, cweights, transposed):
    nb, n_pages = page_ids.shape
    CMP_PAGES = math.gcd(n_pages, CMP_PAGES_MAX)
    steps = n_pages // CMP_PAGES
    segs = PAGE_SIZE // CMP_STRIDE
    rows = CMP_PAGES * segs
    w1, pe, w2bd = cweights
    hid2 = NSA_KV_HEADS * CMP_HID
    page_of = lambda b, s, pt, i: pt[(b * steps + s) * CMP_PAGES + i]
    scratch = [pltpu.VMEM((2, rows + 8, hid2), F32), pltpu.VMEM((2, 8, hid2), F32)]
    if transposed:
        page_specs = [pl.BlockSpec((1, KV_COLS, PAGE_SIZE), lambda b, s, pt, i=i: (page_of(b, s, pt, i), 0, 0))
                      for i in range(CMP_PAGES)]
        scratch.append(pltpu.VMEM((CMP_PAGES, 2, PAGE_SIZE, LANE), F32))
    else:
        page_specs = [pl.BlockSpec((1, PAGE_SIZE, LANE), lambda b, s, pt, i=i, k=k: (page_of(b, s, pt, i), 0, k))
                      for k in range(2) for i in range(CMP_PAGES)]
    full = lambda shape: pl.BlockSpec(shape, lambda b, s, pt: (0,) * len(shape))
    return pl.pallas_call(
        functools.partial(_compress_kernel, transposed=transposed, CMP_PAGES=CMP_PAGES),
        out_shape=jax.ShapeDtypeStruct((nb, n_pages * segs, KV_COLS), F32),
        grid_spec=pltpu.PrefetchScalarGridSpec(
            num_scalar_prefetch=1,
            grid=(nb, steps),
            in_specs=page_specs + [full(w1.shape), full(pe.shape), full(w2bd.shape)],
            out_specs=pl.BlockSpec((1, rows, KV_COLS), lambda b, s, pt: (b, s, 0)),
            scratch_shapes=scratch),
        compiler_params=pltpu.CompilerParams(dimension_semantics=("arbitrary", "arbitrary"),
                                             vmem_limit_bytes=VMEM_LIMIT),
        name="nsa_compress",
    )(page_ids.reshape(-1), *([pages_arr] * len(page_specs)), w1, pe, w2bd)


def _overlap_matrix(n_rows, n_blocks, n_cols):
    m = np.arange(n_rows)[:, None]
    j = np.arange(n_cols)[None, :]
    cs = (m - 1) * CMP_STRIDE
    ov = (m >= 1) & (j < n_blocks) & (cs < j * SLC_BLOCK + SLC_BLOCK) & (cs + CMP_BLOCK > j * SLC_BLOCK)
    return jnp.asarray(ov.astype(np.float32))


def _stack_q(q, g):
    return _stack_q_f32(q, g).astype(BF16)


def _stack_q_f32(q, g):
    lane = lax.broadcasted_iota(jnp.int32, (Q_BLOCK, LANE), 1)
    keep = (lane < HALF) if g == 0 else (lane >= HALF)
    parts = []
    for jp in range(NSA_REP // 2):
        j = g * (NSA_REP // 2) + jp
        slab = q[:, j * LANE:(j + 1) * LANE]
        rolled = pltpu.roll(slab, HALF, 1)
        first, second = (slab, rolled) if g == 0 else (rolled, slab)
        parts.append(jnp.where(keep, first, 0.0))
        parts.append(jnp.where(keep, second, 0.0))
    return jnp.concatenate(parts, axis=0)


def _unstack_o(acc, g):
    lane = lax.broadcasted_iota(jnp.int32, (Q_BLOCK, LANE), 1)
    outs = []
    for jp in range(NSA_REP // 2):
        a = acc[(2 * jp) * Q_BLOCK:(2 * jp + 1) * Q_BLOCK]
        b = acc[(2 * jp + 1) * Q_BLOCK:(2 * jp + 2) * Q_BLOCK]
        if g == 0:
            outs.append(jnp.where(lane < HALF, a, pltpu.roll(b, HALF, 1)))
        else:
            outs.append(jnp.where(lane < HALF, pltpu.roll(a, HALF, 1), b))
    return jnp.concatenate(outs, axis=1)


def _tile8(x):
    return jnp.concatenate([x] * NSA_REP, axis=0)


def _rank_rows(imp, n_valid):
    sub = 8
    n_rank = -(-n_valid // sub) * sub
    chunks = [imp[c:c + sub] for c in range(0, n_rank, sub)]
    ranks = [jnp.zeros(ch.shape, F32) for ch in chunks]
    jrow = lax.broadcasted_iota(jnp.int32, chunks[0].shape, 0)
    for k in range(n_valid):
        rk = imp[k:k + 1, :]
        for c, ch in enumerate(chunks):
            if c * sub > k:
                ahead = rk >= ch
            elif c * sub + sub - 1 < k:
                ahead = rk > ch
            else:
                ahead = (rk > ch) | ((rk == ch) & (jrow + c * sub > k))
            ranks[c] = ranks[c] + jnp.where(ahead, 1.0, 0.0)
    rest = jnp.full((imp.shape[0] - n_rank, imp.shape[1]), float(n_valid), F32)
    return jnp.concatenate(ranks + [rest], axis=0)


def _cmp_kernel(q_ref, kv_ref, pb_ref, ov_ref, o_ref, sel_ref, bias_sc, *, n_keys, n_blocks):
    i = pl.program_id(0)

    @pl.when(pl.program_id(1) == 0)
    def _():
        for h in range(NSA_HEADS):
            bias_sc[h] = pltpu.roll(pb_ref[h], (4 * i + 4) % n_keys, 1)

    q = q_ref[0]
    kv = kv_ref[0]
    kc = kv[:, 0:LANE].astype(BF16)
    vc = kv[:, LANE:].astype(BF16)
    qi = lax.broadcasted_iota(jnp.int32, (Q_BLOCK, n_keys), 0)
    mi = lax.broadcasted_iota(jnp.int32, (Q_BLOCK, n_keys), 1)
    qpos = Q_BLOCK * i + qi
    valid8 = _tile8((mi >= 1) & (CMP_STRIDE * mi + CMP_STRIDE - 1 <= qpos))
    rowvalid8 = _tile8(jnp.where(qpos[:, 0:1] >= CMP_BLOCK - 1, 1.0, 0.0))
    jj = lax.broadcasted_iota(jnp.int32, (LANE, LANE), 1)
    psum = []
    for g in range(NSA_KV_HEADS):
        s = _dot_nt(_stack_q(q, g), kc)
        s = jnp.where(valid8, s + bias_sc[g * NSA_REP:(g + 1) * NSA_REP].reshape(ROWS, n_keys), NEG)
        p = jnp.exp(s - jnp.max(s, axis=-1, keepdims=True))
        pc = p / jnp.sum(p, axis=-1, keepdims=True) * rowvalid8
        o_ref[0, :, g * GROUP_W:(g + 1) * GROUP_W] = _unstack_o(
            jnp.dot(pc.astype(BF16), vc, preferred_element_type=F32), g)
        ps = pc[0:Q_BLOCK]
        for r in range(1, NSA_REP):
            ps = ps + pc[r * Q_BLOCK:(r + 1) * Q_BLOCK]
        psum.append(ps)
    ps = jnp.concatenate(psum, axis=0)
    hi = ps.astype(BF16)
    lo = (ps - hi.astype(F32)).astype(BF16)
    ov = ov_ref[...].astype(BF16)
    imp = (jnp.dot(hi, ov, preferred_element_type=F32)
           + jnp.dot(lo, ov, preferred_element_type=F32))
    forced = (jj == 0) | (jj == i) | (jj == i - 1)
    imp = jnp.where(forced, 1e6, jnp.where(jj <= i, imp, -1e6))
    imp = jnp.where(jj < n_blocks, imp, -2e6)
    rank = _rank_rows(imp.T, n_blocks)
    sel_ref[0, 0] = jnp.where(rank < SLC_TOP, 1.0, 0.0).astype(BF16)


def _cmp_prompt(q, kvc_cmp, pattern, ov):
    b, l, _ = q.shape
    n_keys = kvc_cmp.shape[1]
    n_blocks = l // SLC_BLOCK
    return pl.pallas_call(
        functools.partial(_cmp_kernel, n_keys=n_keys, n_blocks=n_blocks),
        out_shape=(jax.ShapeDtypeStruct((b, l, NSA_WIDTH), F32),
                   jax.ShapeDtypeStruct((b, l // Q_BLOCK, LANE, LANE), BF16)),
        grid=(l // Q_BLOCK, b),
        in_specs=[pl.BlockSpec((1, Q_BLOCK, NSA_WIDTH), lambda i, bi: (bi, i, 0)),
                  pl.BlockSpec((1, n_keys, KV_COLS), lambda i, bi: (bi, 0, 0)),
                  pl.BlockSpec(pattern.shape, lambda i, bi: (0, 0, 0)),
                  pl.BlockSpec(ov.shape, lambda i, bi: (0, 0))],
        out_specs=(pl.BlockSpec((1, Q_BLOCK, NSA_WIDTH), lambda i, bi: (bi, i, 0)),
                   pl.BlockSpec((1, 1, LANE, LANE), lambda i, bi: (bi, i, 0, 0))),
        scratch_shapes=[pltpu.VMEM(pattern.shape, F32)],
        compiler_params=pltpu.CompilerParams(dimension_semantics=("arbitrary", "arbitrary"),
                                             vmem_limit_bytes=VMEM_LIMIT),
        name="nsa_cmp_prompt",
    )(q, kvc_cmp, pattern, ov)


def _flash_update_t(st, vt, m, acc):
    m_new = jnp.maximum(m, jnp.max(st, axis=0, keepdims=True))
    p = jnp.exp2(st - m_new)
    acc = jnp.exp2(m - m_new) * acc + jnp.dot(vt, p.astype(BF16), preferred_element_type=F32)
    return m_new, acc


def _finish_t(acc):
    o_t = acc[0:HEAD_DIM] / acc[HEAD_DIM:HEAD_DIM + 1]
    return _unstack_o(jnp.concatenate([o_t, jnp.zeros_like(o_t)], axis=0).T, 0)


def _slc_kernel(q_ref, sel_ref, kx_ref, vt_ref, tnt_ref, tile_ref, o_ref, far_sc, near_sc, sta_sc, stb_sc, stn_sc):
    i = pl.program_id(1)
    q = q_ref[0] * LOG2E
    n_far = (jnp.maximum(i - 3, 0) + 3) // 4
    sel = sel_ref[0, 0]
    jrow = lax.broadcasted_iota(jnp.int32, (LANE, ROWS), 0)
    a = ((i + 1) // 2) * LANE
    delta = Q_BLOCK * (i + 1) - a
    groups = range(NSA_KV_HEADS)
    for g in groups:
        qs_t = _stack_q_f32(q, g).T.astype(BF16)
        hit = jnp.dot(sel, tile_ref[g], preferred_element_type=F32) > 0.5
        near_sc[g] = jnp.concatenate([qs_t, jnp.where(hit & (jrow != PAD_FLAG_COL), 0.0, NEG).astype(BF16)], axis=0)
        far_sc[g] = jnp.concatenate([qs_t, jnp.where(hit & (jrow < i - 3), 0.0, NEG).astype(BF16)], axis=0)

    def far_scores(t, dst):
        start = pl.multiple_of(SLC_PAD + SLC_NEAR * t, SLC_NEAR)
        kx = kx_ref[0, pl.ds(start, SLC_NEAR), :]
        for g in groups:
            dst[g] = jnp.dot(kx, far_sc[g], preferred_element_type=F32)

    def far_softmax(t, src, carry):
        start = pl.multiple_of(SLC_PAD + SLC_NEAR * t, SLC_NEAR)
        return tuple(_flash_update_t(src[g], vt_ref[0, g, :, pl.ds(start, SLC_NEAR)], *carry[g]) for g in groups)

    def far_pair(u, carry):
        far_scores(2 * u + 1, stb_sc)
        carry = far_softmax(2 * u, sta_sc, carry)
        far_scores(2 * u + 2, sta_sc)
        return far_softmax(2 * u + 1, stb_sc, carry)

    far_scores(0, sta_sc)
    start = pl.multiple_of(a, LANE)
    kx = kx_ref[0, pl.ds(start, SLC_NEARW), :]
    for g in groups:
        stn_sc[g] = jnp.dot(kx, near_sc[g], preferred_element_type=F32)
    init = (jnp.full((1, ROWS), NEG, F32), jnp.zeros((HEAD_DIM + ONES_ROWS, ROWS), F32))
    carry = lax.fori_loop(0, (n_far + 1) // 2, far_pair, (init, init))

    for g in groups:
        st = stn_sc[g] + tnt_ref[delta // Q_BLOCK, g]
        _, acc = _flash_update_t(st, vt_ref[0, g, :, pl.ds(start, SLC_NEARW)], *carry[g])
        o_ref[0, :, g * GROUP_W:(g + 1) * GROUP_W] = _finish_t(acc)


def _slc_prompt(q, sel, kx_pad, vt_pad, tnt, tile):
    b, l, _ = q.shape
    lp = kx_pad.shape[1]
    return pl.pallas_call(
        _slc_kernel,
        out_shape=jax.ShapeDtypeStruct((b, l, NSA_WIDTH), F32),
        grid=(b, l // Q_BLOCK),
        in_specs=[pl.BlockSpec((1, Q_BLOCK, NSA_WIDTH), lambda bi, i: (bi, i, 0)),
                  pl.BlockSpec((1, 1, LANE, LANE), lambda bi, i: (bi, i, 0, 0)),
                  pl.BlockSpec((1, lp, 2 * LANE), lambda bi, i: (bi, 0, 0)),
                  pl.BlockSpec((1, NSA_KV_HEADS, HEAD_DIM + ONES_ROWS, lp), lambda bi, i: (bi, 0, 0, 0)),
                  pl.BlockSpec(tnt.shape, lambda bi, i: (0, 0, 0, 0)),
                  pl.BlockSpec(tile.shape, lambda bi, i: (0, 0, 0))],
        out_specs=pl.BlockSpec((1, Q_BLOCK, NSA_WIDTH), lambda bi, i: (bi, i, 0)),
        scratch_shapes=[pltpu.VMEM((NSA_KV_HEADS, 2 * LANE, ROWS), BF16),
                        pltpu.VMEM((NSA_KV_HEADS, 2 * LANE, ROWS), BF16),
                        pltpu.VMEM((NSA_KV_HEADS, SLC_NEAR, ROWS), F32),
                        pltpu.VMEM((NSA_KV_HEADS, SLC_NEAR, ROWS), F32),
                        pltpu.VMEM((NSA_KV_HEADS, SLC_NEARW, ROWS), F32)],
        compiler_params=pltpu.CompilerParams(vmem_limit_bytes=VMEM_LIMIT),
        name="nsa_slc_prompt",
    )(q, sel, kx_pad, vt_pad, tnt, tile)


def _group_tile():
    t = np.zeros((NSA_KV_HEADS, LANE, ROWS), np.float32)
    for g in range(NSA_KV_HEADS):
        for r in range(NSA_REP):
            for qq in range(Q_BLOCK):
                t[g, g * Q_BLOCK + qq, r * Q_BLOCK + qq] = 1.0
    return jnp.asarray(t, BF16)


def _win_kernel(q_ref, kx_ref, vt_ref, tw_ref, o_ref, st_sc):
    i2 = pl.program_id(1)
    q = q_ref[0] * LOG2E
    start = pl.multiple_of(i2 * LANE, LANE)
    kx = kx_ref[0, pl.ds(start, WIN_W), :]
    groups = range(NSA_KV_HEADS)
    row = lax.broadcasted_iota(jnp.int32, (LANE, 2 * ROWS), 0)
    pad_rows = jnp.where(row == 0, NEG, 0.0).astype(BF16)
    for g in groups:
        qs_t = jnp.concatenate([_stack_q_f32(q[h * Q_BLOCK:(h + 1) * Q_BLOCK], g).T for h in range(2)], axis=1)
        qx = jnp.concatenate([qs_t.astype(BF16), pad_rows], axis=0)
        st_sc[g] = jnp.dot(kx, qx, preferred_element_type=F32)
    for g in groups:
        st = st_sc[g] + tw_ref[g]
        p = jnp.exp2(st - jnp.max(st, axis=0, keepdims=True))
        acc = jnp.dot(vt_ref[0, g, :, pl.ds(start, WIN_W)], p.astype(BF16), preferred_element_type=F32)
        for h in range(2):
            o_ref[0, h * Q_BLOCK:(h + 1) * Q_BLOCK, g * GROUP_W:(g + 1) * GROUP_W] = _finish_t(
                acc[:, h * ROWS:(h + 1) * ROWS])


def _win_prompt(q, kx_pad, vt_pad, tw):
    b, l, _ = q.shape
    lp = kx_pad.shape[1]
    tq = 2 * Q_BLOCK
    return pl.pallas_call(
        _win_kernel,
        out_shape=jax.ShapeDtypeStruct((b, l, NSA_WIDTH), F32),
        grid=(b, l // tq),
        in_specs=[pl.BlockSpec((1, tq, NSA_WIDTH), lambda bi, i: (bi, i, 0)),
                  pl.BlockSpec((1, lp, 2 * LANE), lambda bi, i: (bi, 0, 0)),
                  pl.BlockSpec((1, NSA_KV_HEADS, HEAD_DIM + ONES_ROWS, lp), lambda bi, i: (bi, 0, 0, 0)),
                  pl.BlockSpec(tw.shape, lambda bi, i: (0, 0, 0))],
        out_specs=pl.BlockSpec((1, tq, NSA_WIDTH), lambda bi, i: (bi, i, 0)),
        scratch_shapes=[pltpu.VMEM((NSA_KV_HEADS, WIN_W, 2 * ROWS), F32)],
        compiler_params=pltpu.CompilerParams(vmem_limit_bytes=VMEM_LIMIT),
        name="nsa_win_prompt",
    )(q, kx_pad, vt_pad, tw)


def _dense1_kernel(q_ref, k_ref, v_ref, bm_ref, o_ref, ps_ref):
    k = k_ref[0].astype(BF16)
    v = v_ref[0].astype(BF16)
    for g in range(NSA_KV_HEADS):
        rows = slice(g * NSA_REP, (g + 1) * NSA_REP)
        s = _dot_nt(q_ref[0, rows, :], k) + bm_ref[rows, :]
        p = jnp.exp(s - jnp.max(s, axis=-1, keepdims=True))
        pc = p / jnp.sum(p, axis=-1, keepdims=True)
        o_ref[0, rows, :] = jnp.dot(pc.astype(BF16), v, preferred_element_type=F32)
        ps_ref[0, g:g + 1, :] = jnp.sum(pc, axis=0, keepdims=True)


def _dense1(q_pad, kv, biasmask):
    nb, n, _ = kv.shape
    return pl.pallas_call(
        _dense1_kernel,
        out_shape=(jax.ShapeDtypeStruct((nb, NSA_HEADS, LANE), F32),
                   jax.ShapeDtypeStruct((nb, NSA_KV_HEADS, n), F32)),
        grid=(nb,),
        in_specs=[pl.BlockSpec((1, NSA_HEADS, LANE), lambda b: (b, 0, 0)),
                  pl.BlockSpec((1, n, LANE), lambda b: (b, 0, 0)),
                  pl.BlockSpec((1, n, LANE), lambda b: (b, 0, 1)),
                  pl.BlockSpec((NSA_HEADS, n), lambda b: (0, 0))],
        out_specs=(pl.BlockSpec((1, NSA_HEADS, LANE), lambda b: (b, 0, 0)),
                   pl.BlockSpec((1, NSA_KV_HEADS, n), lambda b: (b, 0, 0))),
        name="nsa_dense_sample",
    )(q_pad, kv, kv, biasmask)


def _take_group_half(o_pad):
    nb = o_pad.shape[0]
    o = o_pad.reshape(nb, NSA_KV_HEADS, NSA_REP, NSA_KV_HEADS, HEAD_DIM)
    o = jnp.stack([o[:, g, :, g, :] for g in range(NSA_KV_HEADS)], axis=1)
    return o.reshape(nb, NSA_WIDTH)


def _topk_kernel(ps_ref, ov_ref, idx_ref, imp_sc, *, n_blocks, cur):
    n_rows = ps_ref.shape[0]
    n_cols = ov_ref.shape[1]
    ps = jnp.concatenate([ps_ref[...], jnp.zeros((LANE - n_rows, ps_ref.shape[1]), F32)], axis=0)
    imp = _dot32(ps, ov_ref[...])
    jj = lax.broadcasted_iota(jnp.int32, (LANE, n_cols), 1)
    forced = (jj == 0) | (jj == cur) | (jj == cur - 1)
    imp = jnp.where(forced, 1e6, jnp.where(jj <= cur, imp, -1e6))
    imp = jnp.where(jj < n_blocks, imp, -2e6)
    imp_t = imp.T
    imp_sc[...] = imp_t
    jrow = lax.broadcasted_iota(jnp.int32, (n_cols, LANE), 0)

    def body(k, rank):
        rk = imp_sc[pl.ds(k, 1), :]
        ahead = (rk > imp_t) | ((rk == imp_t) & (jrow > k))
        return rank + jnp.where(ahead, 1.0, 0.0)

    rank = lax.fori_loop(0, n_blocks, body, jnp.zeros((n_cols, LANE), F32))
    jf = jrow.astype(F32)
    rows = [jnp.sum(jnp.where(rank == float(r), jf, 0.0), axis=0, keepdims=True) for r in range(SLC_TOP)]
    idx_ref[...] = jnp.concatenate(rows, axis=0).astype(jnp.int32)


def _topk_sample(psum, ov, n_blocks, cur):
    n_rows, n_keys = psum.shape
    n_cols = ov.shape[1]
    return pl.pallas_call(
        functools.partial(_topk_kernel, n_blocks=n_blocks, cur=cur),
        out_shape=jax.ShapeDtypeStruct((SLC_TOP, LANE), jnp.int32),
        grid=(1,),
        in_specs=[pl.BlockSpec((n_rows, n_keys), lambda i: (0, 0)),
                  pl.BlockSpec(ov.shape, lambda i: (0, 0))],
        out_specs=pl.BlockSpec((SLC_TOP, LANE), lambda i: (0, 0)),
        scratch_shapes=[pltpu.VMEM((n_cols, LANE), F32)],
        name="nsa_topk_sample",
    )(psum, ov)


def _slc1_kernel(phys_ref, jsel_ref, q_ref, *refs, cur, past):
    pages = refs[:SLC_TOP]
    new_ref, bb_ref, o_ref = refs[SLC_TOP:]
    b = pl.program_id(0)
    g = pl.program_id(1)
    goff = pl.multiple_of(g * HEAD_DIM, HEAD_DIM)
    q = q_ref[0]
    lane = lax.broadcasted_iota(jnp.int32, (NSA_REP, PAGE_SIZE), 1)
    first = lax.broadcasted_iota(jnp.int32, (HEAD_DIM, PAGE_SIZE), 1) == 0
    new_k = jnp.where(first, new_ref[0, pl.ds(goff, HEAD_DIM), :], 0.0)
    new_v = jnp.where(first, new_ref[0, pl.ds(LANE + goff, HEAD_DIM), :], 0.0)
    scores, values = [], []
    for n in range(SLC_TOP):
        j = jsel_ref[(b * NSA_KV_HEADS + g) * SLC_TOP + n]
        kt = jnp.where(j == cur, new_k, pages[n][0, pl.ds(goff, HEAD_DIM), :])
        vt = jnp.where(j == cur, new_v, pages[n][0, pl.ds(LANE + goff, HEAD_DIM), :])
        s = jnp.dot(q, kt.astype(BF16), preferred_element_type=F32)
        ok = (lane // SLC_BLOCK == j % 2) & ((j // 2) * PAGE_SIZE + lane <= past)
        bias = bb_ref[jnp.clip(j - (cur - 3), 0, 3), pl.ds(pl.multiple_of(g * NSA_REP, NSA_REP), NSA_REP), :]
        scores.append(jnp.where(ok, s + bias, NEG))
        values.append(vt.astype(BF16))
    s_all = jnp.concatenate(scores, axis=1)
    p = jnp.exp(s_all - jnp.max(s_all, axis=-1, keepdims=True))
    acc = jnp.zeros((NSA_REP, HEAD_DIM), F32)
    for n in range(SLC_TOP):
        acc = acc + _dot_nt(p[:, n * PAGE_SIZE:(n + 1) * PAGE_SIZE].astype(BF16), values[n])
    o_ref[0] = acc / jnp.sum(p, axis=-1, keepdims=True)


def _slc_sample(q, cache_pages_t, phys, jsel, new_cols, biasblk, cur, past):
    nb = q.shape[0]
    idx = lambda b, g, n: (b * NSA_KV_HEADS + g) * SLC_TOP + n
    page_specs = [pl.BlockSpec((1, KV_COLS, PAGE_SIZE), lambda b, g, ph, js, n=n: (ph[idx(b, g, n)], 0, 0))
                  for n in range(SLC_TOP)]
    return pl.pallas_call(
        functools.partial(_slc1_kernel, cur=cur, past=past),
        out_shape=jax.ShapeDtypeStruct((nb, NSA_HEADS, HEAD_DIM), F32),
        grid_spec=pltpu.PrefetchScalarGridSpec(
            num_scalar_prefetch=2,
            grid=(nb, NSA_KV_HEADS),
            in_specs=[pl.BlockSpec((1, NSA_REP, HEAD_DIM), lambda b, g, ph, js: (b, g, 0))] + page_specs
            + [pl.BlockSpec((1, KV_COLS, 1), lambda b, g, ph, js: (b, 0, 0)),
               pl.BlockSpec(biasblk.shape, lambda b, g, ph, js: (0, 0, 0))],
            out_specs=pl.BlockSpec((1, NSA_REP, HEAD_DIM), lambda b, g, ph, js: (b, g, 0))),
        name="nsa_slc_sample",
    )(phys, jsel, q, *([cache_pages_t] * SLC_TOP), new_cols, biasblk)


def _out_kernel(x_ref, gate_ref, yssd_ref, zs_ref, oc_ref, os_ref, ow_ref, gl_ref, za_ref,
                nw1_ref, nw2_ref, w_ref, eg_ref, fg_ref, o_ref, *, per_row, final):
    gates = _sigmoid(gl_ref[...])
    g2 = jnp.concatenate(_split_bf16(gates, 2), axis=1)
    expand = lambda br: jnp.dot(g2, eg_ref[br], preferred_element_type=F32)
    y_nsa = expand(0) * oc_ref[...] + expand(1) * os_ref[...] + expand(2) * ow_ref[...]

    def gated_norm(y, z, w):
        u = y * _silu(z)
        half = u.shape[1] // 2
        parts = []
        for g in range(2):
            ug = u[:, g * half:(g + 1) * half]
            parts.append(ug * lax.rsqrt(jnp.mean(ug * ug, axis=-1, keepdims=True) + NORM_EPS))
        return (jnp.concatenate(parts, axis=1) * w).astype(BF16)

    m1 = gated_norm(yssd_ref[...], zs_ref[...], nw1_ref[...])
    m2 = gated_norm(y_nsa, za_ref[...], nw2_ref[...])
    proj = (jnp.dot(m1, w_ref[0:SSD_WIDTH, :], preferred_element_type=F32)
            + jnp.dot(m2, w_ref[SSD_WIDTH:, :], preferred_element_type=F32))
    gate = gate_ref[...] if per_row else gate_ref[0]
    out = x_ref[...] + gate * proj
    if final:
        out = out * lax.rsqrt(jnp.mean(out * out, axis=-1, keepdims=True) + NORM_EPS) * fg_ref[...]
    o_ref[...] = out


def _gate_expand():
    e = np.zeros((3, LANE, NSA_WIDTH), np.float32)
    for br in range(3):
        for h in range(NSA_HEADS):
            e[br, br * NSA_HEADS + h, h * HEAD_DIM:(h + 1) * HEAD_DIM] = 1.0
    return jnp.asarray(np.concatenate([e, e], axis=1), BF16)


def _layer_out(x2d, gate, y_ssd, z_s, o_cmp, o_slc, o_win, gl, z_a, nw1, nw2, w_out_bf, final_g,
               rows_per_batch, final):
    m = x2d.shape[0]
    per_row = rows_per_batch == 1
    tm = m if per_row else 256
    if per_row:
        gate_spec = pl.BlockSpec((tm, D_MODEL), lambda i: (0, 0))
        gt = gate
    else:
        gate_spec = pl.BlockSpec((1, 1, D_MODEL), lambda i: ((i * tm) // rows_per_batch, 0, 0))
        gt = gate[:, None, :]
    row = lambda w: pl.BlockSpec((tm, w), lambda i: (i, 0))
    full = lambda shape: pl.BlockSpec(shape, lambda i: (0,) * len(shape))
    eg = _gate_expand()
    return pl.pallas_call(
        functools.partial(_out_kernel, per_row=per_row, final=final),
        out_shape=jax.ShapeDtypeStruct((m, D_MODEL), F32),
        grid=(m // tm,),
        in_specs=[row(D_MODEL), gate_spec, row(SSD_WIDTH), row(SSD_WIDTH), row(NSA_WIDTH), row(NSA_WIDTH),
                  row(NSA_WIDTH), row(LANE), row(NSA_WIDTH), full((1, SSD_WIDTH)), full((1, NSA_WIDTH)),
                  full(w_out_bf.shape), full(eg.shape), full((1, D_MODEL))],
        out_specs=row(D_MODEL),
        compiler_params=pltpu.CompilerParams(vmem_limit_bytes=VMEM_LIMIT),
        name="layer_out",
    )(x2d, gt, y_ssd, z_s, o_cmp, o_slc, o_win, gl, z_a, nw1.reshape(1, SSD_WIDTH), nw2.reshape(1, NSA_WIDTH),
      w_out_bf, eg, final_g.reshape(1, D_MODEL))


def _pad_in_weights(w_in):
    cols = []
    off = 0
    for size, width in zip(_SEG_SIZES, _SEG_PAD):
        seg = w_in[:, off:off + size]
        cols.append(jnp.pad(seg, ((0, 0), (0, width - size))))
        off += size
    return jnp.concatenate(cols, axis=1).astype(BF16)


def _values_t(v_pad):
    b, rows, _ = v_pad.shape
    vt = jnp.transpose(v_pad.reshape(b, rows, NSA_KV_HEADS, HEAD_DIM), (0, 2, 3, 1))
    return jnp.concatenate([vt, jnp.ones((b, NSA_KV_HEADS, ONES_ROWS, rows), BF16)], axis=2)


def _front_pad_bf16(kv, rows):
    return jnp.pad(kv, ((0, 0), (rows, 0), (0, 0))).astype(BF16)


def kernel(x_prompt, x_sample, cache_cmp_kv, cache_slc_kv, state_win_kv, state_conv, state_ssm, page_table,
           c_prompt, c_sample, norm_g, ada_w, ada_b, w_in, conv_w, conv_b, dt_bias, a_log, d_skip,
           ssd_norm_w, cmp_pe, cmp_w1, cmp_w2, nsa_norm_w, w_out, rel_bias, final_norm_g):
    nbp, lp, _ = x_prompt.shape
    nbs = x_sample.shape[0]
    depth = w_in.shape[0]
    n_pool = cache_cmp_kv.shape[1]
    n_pages = page_table.shape[1]
    past = n_pages * PAGE_SIZE
    w_buf = state_win_kv.shape[2]
    kv_row = (2, NSA_KV_HEADS, HEAD_DIM)

    n_ck = lp // CMP_STRIDE
    n_sb = lp // SLC_BLOCK
    qi = np.arange(Q_BLOCK)[:, None]
    c0 = n_ck - 4
    pattern = _bias_of_dist(rel_bias, qi - CMP_STRIDE * (np.arange(n_ck)[None, :] - c0) - (CMP_STRIDE - 1))
    far = rel_bias.astype(F32)[REL_BUCKETS - 1][:, None, None]
    par = np.arange(2)[:, None, None]
    qq = np.arange(Q_BLOCK)[None, None, :]
    dist_n = qq - np.arange(SLC_NEARW)[None, :, None] + Q_BLOCK * par + (SLC_NEAR - Q_BLOCK)
    tnt = jnp.stack([_toeplitz_bias(rel_bias, SLC_NEARW, Q_BLOCK, Q_BLOCK * p + SLC_NEAR - Q_BLOCK)
                     for p in range(2)], axis=1)
    tnt = jnp.where(jnp.asarray((dist_n >= 0) & (dist_n <= qq + SLC_NEAR - Q_BLOCK))[None],
                    (tnt - far[..., None]) * LOG2E, NEG)
    tnt = jnp.transpose(tnt.reshape(NSA_KV_HEADS, NSA_REP, 2, SLC_NEARW, Q_BLOCK), (2, 0, 3, 1, 4))
    tnt = tnt.reshape(2, NSA_KV_HEADS, SLC_NEARW, ROWS)
    tile = _group_tile()
    key_row = np.arange(SLC_PAD + lp + SLC_BACK) - SLC_PAD
    assert (key_row[-1] // SLC_BLOCK) < PAD_FLAG_COL
    cols = np.arange(LANE)[None, :]
    blk_onehot = jnp.asarray(((key_row[:, None] >= 0) & (key_row[:, None] // SLC_BLOCK == cols))
                             | ((key_row[:, None] < 0) & (cols == PAD_FLAG_COL)), BF16)
    dist_w = qq - np.arange(WIN_W)[None, :, None] + Q_BLOCK * par + WIN_PAD
    tw = jnp.stack([_toeplitz_bias(rel_bias, WIN_W, Q_BLOCK, Q_BLOCK * p + WIN_PAD) for p in range(2)], axis=1)
    tw = jnp.where(jnp.asarray((dist_w >= 0) & (dist_w < WINDOW))[None], tw * LOG2E, NEG)
    tw = jnp.transpose(tw.reshape(NSA_KV_HEADS, NSA_REP, 2, WIN_W, Q_BLOCK), (2, 0, 3, 1, 4))
    tw = tw.reshape(2, NSA_KV_HEADS, WIN_W, ROWS)
    tw = jnp.concatenate([tw[0], tw[1]], axis=-1)
    win_flag = jnp.asarray((np.arange(WIN_PAD + lp)[:, None] < WIN_PAD) & (np.arange(LANE)[None, :] == 0), BF16)
    ov_p = _overlap_matrix(n_ck, n_sb, LANE)

    n_cs = past // CMP_STRIDE
    cur = past // SLC_BLOCK
    n_blk_s = cur + 1
    n_cols_s = -(-n_blk_s // LANE) * LANE
    ov_s = _overlap_matrix(n_cs, n_blk_s, n_cols_s)
    m_s = np.arange(n_cs)
    bm_c = jnp.where(jnp.asarray(m_s >= 1)[None, :],
                     _bias_of_dist(rel_bias, past - (CMP_STRIDE * m_s + CMP_STRIDE - 1)), NEG)
    n_w = -(-(w_buf + 1) // LANE) * LANE
    iw = np.arange(n_w)
    dw = w_buf - iw
    ok_w = (iw <= w_buf) & (dw >= 0) & (dw < WINDOW) & (past - w_buf + iw >= 0)
    bm_w = jnp.where(jnp.asarray(ok_w)[None, :], _bias_of_dist(rel_bias, dw), NEG)
    jb = (cur - 3 + np.arange(4))[:, None]
    biasblk = _bias_of_dist(rel_bias, past - SLC_BLOCK * jb - np.arange(SLC_BLOCK)[None, :])
    biasblk = jnp.transpose(biasblk, (1, 0, 2))
    biasblk = jnp.concatenate([biasblk, biasblk], axis=-1)

    pages_t = lambda c: jnp.transpose(c, (0, 1, 3, 4, 5, 2)).reshape(depth * n_pool, KV_COLS, PAGE_SIZE)
    cmp_pages_t = pages_t(cache_cmp_kv)
    slc_pages_t = pages_t(cache_slc_kv)
    prompt_pages = jnp.arange(nbp * (lp // PAGE_SIZE), dtype=jnp.int32).reshape(nbp, lp // PAGE_SIZE)

    c_all = jnp.concatenate([c_prompt, c_sample], axis=0)
    xp = x_prompt.reshape(nbp * lp, D_MODEL)
    xs = x_sample.reshape(nbs, D_MODEL)
    outs = {k: [] for k in ("pc", "ps", "pw", "pconv", "pssm", "sc", "ss", "sw", "sconv", "sssm")}

    for l in range(depth):
        final = l == depth - 1
        w_pad = _pad_in_weights(w_in[l])
        w_out_bf = w_out[l].astype(BF16)
        cweights = _compress_weights(cmp_pe[l], cmp_w1[l], cmp_w2[l])
        mod = _modulation(c_all, ada_w[l], ada_b[l])
        shift, scale, gate = mod[:, :D_MODEL], mod[:, D_MODEL:2 * D_MODEL], mod[:, 2 * D_MODEL:]

        z_s, xbc, dt, q, kvc, kvs, kvw, gl, z_a = _in_projection(xp, norm_g[l], scale[:nbp], shift[:nbp], w_pad, lp)
        xbc3 = xbc.reshape(nbp, lp, CONV_DIM)
        y_ssd, h_fin = _ssd_prompt(xbc3, dt.reshape(nbp, lp, LANE), conv_w[l], conv_b[l], dt_bias[l], a_log[l],
                                   d_skip[l])
        q3 = q.reshape(nbp, lp, NSA_WIDTH)
        kvc3, kvs3, kvw3 = (t.reshape(nbp, lp, KV_COLS) for t in (kvc, kvs, kvw))
        kc = _compress(kvc.reshape(nbp * (lp // PAGE_SIZE), PAGE_SIZE, KV_COLS), prompt_pages, cweights, False)
        o_cmp, sel = _cmp_prompt(q3, kc, pattern, ov_p)
        ks_pad = jnp.pad(kvs3, ((0, 0), (SLC_PAD, SLC_BACK), (0, 0))).astype(BF16)
        kx_pad = jnp.concatenate([ks_pad[:, :, :LANE], jnp.broadcast_to(blk_onehot, (nbp,) + blk_onehot.shape)],
                                 axis=-1)
        o_slc = _slc_prompt(q3, sel, kx_pad, _values_t(ks_pad[:, :, LANE:]), tnt, tile)
        kw_pad = _front_pad_bf16(kvw3, WIN_PAD)
        kwx_pad = jnp.concatenate([kw_pad[:, :, :LANE], jnp.broadcast_to(win_flag, (nbp,) + win_flag.shape)], axis=-1)
        o_win = _win_prompt(q3, kwx_pad, _values_t(kw_pad[:, :, LANE:]), tw)
        xp = _layer_out(xp, gate[:nbp], y_ssd.reshape(nbp * lp, SSD_WIDTH), z_s,
                        o_cmp.reshape(nbp * lp, NSA_WIDTH), o_slc.reshape(nbp * lp, NSA_WIDTH),
                        o_win.reshape(nbp * lp, NSA_WIDTH), gl, z_a, ssd_norm_w[l], nsa_norm_w[l], w_out_bf,
                        final_norm_g, lp, final)
        outs["pc"].append(kvc3.reshape((nbp, lp) + kv_row))
        outs["ps"].append(kvs3.reshape((nbp, lp) + kv_row))
        outs["pw"].append(kvw3[:, -min(WINDOW, lp):].reshape((nbp, min(WINDOW, lp)) + kv_row))
        outs["pconv"].append(xbc3[:, -(SSD_CONV - 1):])
        outs["pssm"].append(h_fin)

        z_s2, xbc2, dt2, q2, kvc2, kvs2, kvw2, gl2, z_a2 = _in_projection(
            xs, norm_g[l], scale[nbp:], shift[nbp:], w_pad, 1)
        y_ssd2, h2 = _ssd_step(xbc2, state_conv[l], dt2, state_ssm[l], conv_w[l], conv_b[l], dt_bias[l],
                               a_log[l], d_skip[l])
        qh = q2.reshape(nbs, NSA_KV_HEADS, NSA_REP, HEAD_DIM)
        zq = jnp.zeros((nbs, NSA_REP, HEAD_DIM), F32)
        q_pad = jnp.stack([jnp.concatenate([qh[:, 0], zq], axis=-1), jnp.concatenate([zq, qh[:, 1]], axis=-1)],
                          axis=1).reshape(nbs, NSA_HEADS, LANE).astype(BF16)
        kc2 = _compress(cmp_pages_t, page_table + l * n_pool, cweights, True)
        oc2, psum = _dense1(q_pad, kc2, bm_c)
        sel_idx = _topk_sample(psum.reshape(nbs * NSA_KV_HEADS, n_cs), ov_s, n_blk_s, cur)
        jsel = sel_idx[:, :nbs * NSA_KV_HEADS].T.reshape(nbs, NSA_KV_HEADS, SLC_TOP)
        jc = jnp.minimum(jsel, cur - 1)
        page = jnp.take_along_axis(page_table, (jc // 2).reshape(nbs, -1), axis=1).reshape(jsel.shape)
        phys = jnp.where(jsel < cur, page + l * n_pool, 0).astype(jnp.int32)
        os2 = _slc_sample(q2.reshape(nbs, NSA_HEADS, HEAD_DIM).astype(BF16), slc_pages_t, phys.reshape(-1),
                          jsel.reshape(-1), kvs2.reshape(nbs, KV_COLS, 1), biasblk, cur, past)
        kw_full = jnp.concatenate([state_win_kv[l].reshape(nbs, w_buf, KV_COLS), kvw2[:, None, :]], axis=1)
        kw_in = jnp.pad(kw_full, ((0, 0), (0, n_w - (w_buf + 1)), (0, 0)))
        ow2, _ = _dense1(q_pad, kw_in, bm_w)
        xs = _layer_out(xs, gate[nbp:], y_ssd2, z_s2, _take_group_half(oc2), os2.reshape(nbs, NSA_WIDTH),
                        _take_group_half(ow2), gl2, z_a2, ssd_norm_w[l], nsa_norm_w[l], w_out_bf,
                        final_norm_g, 1, final)
        outs["sc"].append(kvc2.reshape((nbs, 1) + kv_row))
        outs["ss"].append(kvs2.reshape((nbs, 1) + kv_row))
        outs["sw"].append(kw_full[:, -w_buf:].reshape((nbs, w_buf) + kv_row))
        outs["sconv"].append(jnp.concatenate([state_conv[l][:, 1:], xbc2[:, None, :]], axis=1))
        outs["sssm"].append(h2)

    st = lambda k: jnp.stack(outs[k])
    return (xp.reshape(nbp, lp, D_MODEL), xs.reshape(nbs, 1, D_MODEL),
            st("pc"), st("ps"), st("pw"), st("pconv"), st("pssm"),
            st("sc"), st("ss"), st("sw"), st("sconv"), st("sssm"))
```

```python
import functools
import math

import numpy as np
import jax
import jax.numpy as jnp
from jax import lax
from jax.experimental import pallas as pl
from jax.experimental.pallas import tpu as pltpu

F32 = jnp.float32
BF16 = jnp.bfloat16
HIGHEST = lax.Precision.HIGHEST

D_MODEL = 1024
HEAD_DIM = 64
SSD_WIDTH = 1024
SSD_HEADS = 16
SSD_GROUPS = 2
SSD_STATE = 128
SSD_CONV = 4
SSD_CHUNK = 256
CONV_DIM = SSD_WIDTH + 2 * SSD_GROUPS * SSD_STATE
NSA_WIDTH = 1024
NSA_HEADS = 16
NSA_KV_HEADS = 2
NSA_REP = NSA_HEADS // NSA_KV_HEADS
CMP_BLOCK = 32
CMP_STRIDE = 16
CMP_HID = 2 * HEAD_DIM
SLC_BLOCK = 64
SLC_TOP = 16
WINDOW = 512
Q_BLOCK = 64
REL_BUCKETS = 32
REL_MAX_DIST = 128
NORM_EPS = 1e-6
KV_COLS = 2 * NSA_KV_HEADS * HEAD_DIM
PAGE_SIZE = 128
NEG = -1e30
LOG2E = 1.4426950408889634

LANE = 128
HALF = LANE // 2
GROUP_W = NSA_REP * HEAD_DIM
ROWS = NSA_REP * Q_BLOCK
SLC_NEAR = 4 * SLC_BLOCK
SLC_PAD = SLC_NEAR
SLC_NEARW = SLC_NEAR + LANE
ONES_ROWS = 8
PAD_FLAG_COL = LANE - 1
SLC_BACK = 2 * SLC_NEAR
WIN_W = WINDOW + 2 * Q_BLOCK
WIN_PAD = WINDOW
CMP_PAGES_MAX = 32
VMEM_LIMIT = 48 * 1024 * 1024

_SEG_NAMES = ("z_s", "xbc", "dt", "q", "kvc", "kvs", "kvw", "gl", "z_a")
_SEG_SIZES = (SSD_WIDTH, CONV_DIM, SSD_HEADS, NSA_WIDTH, KV_COLS, KV_COLS, KV_COLS, 3 * NSA_HEADS, NSA_WIDTH)
_SEG_PAD = tuple(-(-s // LANE) * LANE for s in _SEG_SIZES)
_SEG_OFF = tuple(int(o) for o in np.cumsum((0,) + _SEG_PAD[:-1]))
IN_PAD = int(sum(_SEG_PAD))


def _sigmoid(x):
    return 1.0 / (1.0 + jnp.exp(-x))


def _silu(x):
    return x * _sigmoid(x)


def _dot32(a, b):
    return jnp.dot(a, b, precision=HIGHEST, preferred_element_type=F32)


def _split_bf16(x, terms):
    parts = []
    for _ in range(terms):
        p = x.astype(BF16)
        parts.append(p)
        x = x - p.astype(F32)
    return parts


def _expand2(x, sel2_bf16):
    hi, lo = _split_bf16(x, 2)
    return jnp.dot(jnp.concatenate([hi, lo], axis=1), sel2_bf16, preferred_element_type=F32)


def _sel_dot(sel_bf16, x, terms):
    return sum(jnp.dot(sel_bf16, p, preferred_element_type=F32) for p in _split_bf16(x, terms))


def _dot_nt(a, b):
    return lax.dot_general(a, b, (((1,), (1,)), ((), ())), preferred_element_type=F32)


def _bucket_table():
    n = np.arange(REL_MAX_DIST + 1)
    max_exact = REL_BUCKETS // 2
    nf = np.maximum(n, 1).astype(np.float32)
    large = max_exact + (np.log(nf / np.float32(max_exact)) / np.float32(math.log(REL_MAX_DIST / max_exact))
                         * np.float32(REL_BUCKETS - max_exact)).astype(np.int32)
    large = np.minimum(large, REL_BUCKETS - 1)
    return np.where(n < max_exact, n, large).astype(np.int32)


_BUCKETS = _bucket_table()


def _bias_of_dist(rel_bias, dist):
    idx = _BUCKETS[np.clip(dist, 0, REL_MAX_DIST)]
    out = jnp.take(rel_bias.astype(F32), jnp.asarray(idx.reshape(-1)), axis=0)
    return out.T.reshape((NSA_HEADS,) + dist.shape)


def _toeplitz_bias(rel_bias, rows, cols, c0):
    n = rows + cols
    d = np.arange(n) + c0 - (rows - 1)
    v = jnp.take(rel_bias.astype(F32), jnp.asarray(_BUCKETS[np.clip(d, 0, REL_MAX_DIST)]), axis=0).T
    flat = jnp.tile(v, (1, rows + 1))[:, :rows * (n + 1)]
    hank = flat.reshape(NSA_HEADS, rows, n + 1)[:, :, :cols]
    return hank[:, ::-1, :]


def _mod_kernel(c_ref, w_ref, b_ref, o_ref):
    o_ref[...] = _dot32(_silu(c_ref[...]), w_ref[...]) + b_ref[...]


def _modulation(c, w, b):
    m, d = c.shape
    n = w.shape[1]
    tn = 512
    return pl.pallas_call(
        _mod_kernel,
        out_shape=jax.ShapeDtypeStruct((m, n), F32),
        grid=(n // tn,),
        in_specs=[pl.BlockSpec((m, d), lambda j: (0, 0)),
                  pl.BlockSpec((d, tn), lambda j: (0, j)),
                  pl.BlockSpec((1, tn), lambda j: (0, j))],
        out_specs=pl.BlockSpec((m, tn), lambda j: (0, j)),
        name="adaln_mod",
    )(c, w, b.reshape(1, n))


def _inproj_kernel(x_ref, g_ref, sc_ref, sh_ref, w_ref, *out_refs, per_row):
    x = x_ref[...]
    xn = x * lax.rsqrt(jnp.mean(x * x, axis=-1, keepdims=True) + NORM_EPS)
    sc = sc_ref[...] if per_row else sc_ref[0]
    sh = sh_ref[...] if per_row else sh_ref[0]
    h = ((xn * g_ref[...]) * (1.0 + sc) + sh).astype(BF16)
    for name, off, width, ref in zip(_SEG_NAMES, _SEG_OFF, _SEG_PAD, out_refs):
        r = jnp.dot(h, w_ref[:, off:off + width], preferred_element_type=F32)
        if name == "q":
            r = r * (HEAD_DIM ** -0.5)
        ref[...] = r


def _in_projection(x2d, g, scale, shift, w_pad, rows_per_batch):
    m = x2d.shape[0]
    per_row = rows_per_batch == 1
    tm = m if per_row else 256
    if per_row:
        mod_spec = pl.BlockSpec((tm, D_MODEL), lambda i: (0, 0))
        sc, sh = scale, shift
    else:
        mod_spec = pl.BlockSpec((1, 1, D_MODEL), lambda i: ((i * tm) // rows_per_batch, 0, 0))
        sc, sh = scale[:, None, :], shift[:, None, :]
    outs = tuple(jax.ShapeDtypeStruct((m, w), F32) for w in _SEG_PAD)
    return pl.pallas_call(
        functools.partial(_inproj_kernel, per_row=per_row),
        out_shape=outs,
        grid=(m // tm,),
        in_specs=[pl.BlockSpec((tm, D_MODEL), lambda i: (i, 0)),
                  pl.BlockSpec((1, D_MODEL), lambda i: (0, 0)),
                  mod_spec, mod_spec,
                  pl.BlockSpec((D_MODEL, IN_PAD), lambda i: (0, 0))],
        out_specs=tuple(pl.BlockSpec((tm, w), lambda i: (i, 0)) for w in _SEG_PAD),
        compiler_params=pltpu.CompilerParams(vmem_limit_bytes=VMEM_LIMIT),
        name="in_projection",
    )(x2d, g.reshape(1, D_MODEL), sc, sh, w_pad)


def _softplus(x):
    return jnp.maximum(x, 0.0) + jnp.log(1.0 + jnp.exp(-jnp.abs(x)))


def _ssd_kernel(xbc_ref, dt_ref, cw_ref, cb_ref, dtb_ref, alog_ref, dsk_ref, e_ref, tril_ref,
                y_ref, hfin_ref, xe_sc, st_sc):
    c = pl.program_id(1)
    q = SSD_CHUNK
    n_pairs = SSD_HEADS // 2

    @pl.when(c == 0)
    def _():
        xe_sc[0:8, :] = jnp.zeros((8, CONV_DIM), F32)
        st_sc[...] = jnp.zeros(st_sc.shape, F32)

    xe_sc[8:8 + q, :] = xbc_ref[0]
    acc = cb_ref[...] + cw_ref[0:1, :] * xe_sc[5:5 + q, :]
    for k in range(1, SSD_CONV):
        acc = acc + cw_ref[k:k + 1, :] * xe_sc[5 + k:5 + k + q, :]
    u = _silu(acc)
    xe_sc[0:8, :] = xe_sc[q:q + 8, :]

    xs = u[:, :SSD_WIDTH]
    gn = SSD_GROUPS * SSD_STATE
    bm = u[:, SSD_WIDTH:SSD_WIDTH + gn]
    cm = u[:, SSD_WIDTH + gn:]

    dt = _softplus(dt_ref[0] + dtb_ref[...])
    a = dt * (-jnp.exp(alog_ref[...]))
    cs = _sel_dot(tril_ref[...], a, 3)
    cs_t = cs.T
    cs_last = cs[q - 1:q, :]
    e = e_ref[...]
    dt_e = _expand2(dt, e)
    w_e = _expand2(dt * jnp.exp(cs_last - cs), e)
    ecs_e = _expand2(jnp.exp(cs), e)
    tot_e = _expand2(jnp.broadcast_to(jnp.exp(cs_last), (8, LANE)), e)[0:1, :]
    xdt = (xs * dt_e).astype(BF16)
    xw = (xs * w_e).astype(BF16)

    li = lax.broadcasted_iota(jnp.int32, (q, q), 0)
    si = lax.broadcasted_iota(jnp.int32, (q, q), 1)
    tri = li >= si
    lane = lax.broadcasted_iota(jnp.int32, (q, LANE), 1)

    for g in range(SSD_GROUPS):
        cg = cm[:, g * SSD_STATE:(g + 1) * SSD_STATE].astype(BF16)
        bg = bm[:, g * SSD_STATE:(g + 1) * SSD_STATE]
        cb = _dot_nt(cg, bg.astype(BF16))
        bg_t = bg.T.astype(BF16)
        for jp in range(n_pairs // SSD_GROUPS):
            j = g * (n_pairs // SSD_GROUPS) + jp
            sl = slice(j * LANE, (j + 1) * LANE)
            xdt_p = xdt[:, sl]
            ys = []
            for hh in (2 * j, 2 * j + 1):
                diff = cs[:, hh:hh + 1] - cs_t[hh:hh + 1, :]
                lmat = jnp.exp(jnp.where(tri, diff, NEG))
                ys.append(jnp.dot((cb * lmat).astype(BF16), xdt_p, preferred_element_type=F32))
            y_diag = jnp.where(lane < HALF, ys[0], ys[1])
            st = st_sc[j]
            y_off = jnp.dot(cg, st.astype(BF16), preferred_element_type=F32) * ecs_e[:, sl]
            y_ref[0, :, sl] = y_diag + y_off + xs[:, sl] * dsk_ref[:, sl]
            new = jnp.dot(bg_t, xw[:, sl], preferred_element_type=F32)
            st_sc[j] = st * tot_e[:, sl] + new

    @pl.when(c == pl.num_programs(1) - 1)
    def _():
        for j in range(n_pairs):
            hfin_ref[0, j * LANE:(j + 1) * LANE, :] = st_sc[j].T


def _head_expand():
    e = np.zeros((LANE, SSD_WIDTH), np.float32)
    for h in range(SSD_HEADS):
        e[h, h * HEAD_DIM:(h + 1) * HEAD_DIM] = 1.0
    return jnp.asarray(e)


def _pad_lanes(v):
    return jnp.pad(v.astype(F32), (0, LANE - v.shape[0])).reshape(1, LANE)


def _ssd_prompt(xbc, dt, conv_w, conv_b, dt_bias, a_log, d_skip):
    b, l, _ = xbc.shape
    nc = l // SSD_CHUNK
    full = lambda shape: pl.BlockSpec(shape, lambda i, c: (0,) * len(shape))
    y, hfin = pl.pallas_call(
        _ssd_kernel,
        out_shape=(jax.ShapeDtypeStruct((b, l, SSD_WIDTH), F32),
                   jax.ShapeDtypeStruct((b, SSD_HEADS * HEAD_DIM, SSD_STATE), F32)),
        grid=(b, nc),
        in_specs=[pl.BlockSpec((1, SSD_CHUNK, CONV_DIM), lambda i, c: (i, c, 0)),
                  pl.BlockSpec((1, SSD_CHUNK, LANE), lambda i, c: (i, c, 0)),
                  full((SSD_CONV, CONV_DIM)), full((1, CONV_DIM)), full((1, LANE)), full((1, LANE)),
                  full((1, SSD_WIDTH)), full((2 * LANE, SSD_WIDTH)), full((SSD_CHUNK, SSD_CHUNK))],
        out_specs=(pl.BlockSpec((1, SSD_CHUNK, SSD_WIDTH), lambda i, c: (i, c, 0)),
                   pl.BlockSpec((1, SSD_HEADS * HEAD_DIM, SSD_STATE), lambda i, c: (i, 0, 0))),
        scratch_shapes=[pltpu.VMEM((SSD_CHUNK + 8, CONV_DIM), F32),
                        pltpu.VMEM((SSD_HEADS // 2, SSD_STATE, LANE), F32)],
        compiler_params=pltpu.CompilerParams(dimension_semantics=("arbitrary", "arbitrary"),
                                             vmem_limit_bytes=VMEM_LIMIT),
        name="ssd_prompt",
    )(xbc, dt, conv_w, conv_b.reshape(1, CONV_DIM), _pad_lanes(dt_bias), _pad_lanes(a_log),
      jnp.repeat(d_skip.astype(F32), HEAD_DIM).reshape(1, SSD_WIDTH),
      jnp.concatenate([_head_expand(), _head_expand()], axis=0).astype(BF16),
      jnp.asarray(np.tril(np.ones((SSD_CHUNK, SSD_CHUNK), np.float32)), BF16))
    return y, hfin.reshape(b, SSD_HEADS, HEAD_DIM, SSD_STATE)


def _ssd_step_kernel(xbc_ref, c0_ref, c1_ref, c2_ref, dt_ref, h0_ref, cw_ref, cb_ref, dtb_ref, alog_ref,
                     dsk_ref, e_ref, y_ref, hout_ref, xt_sc, dect_sc, bc_sc, yt_sc, xs_sc):
    b = pl.program_id(0)
    nb = xbc_ref.shape[0]
    rows = SSD_HEADS * HEAD_DIM
    gn = SSD_GROUPS * SSD_STATE

    @pl.when(b == 0)
    def _():
        acc = (cb_ref[...] + cw_ref[0:1, :] * c0_ref[...] + cw_ref[1:2, :] * c1_ref[...]
               + cw_ref[2:3, :] * c2_ref[...] + cw_ref[3:4, :] * xbc_ref[...])
        u = _silu(acc)
        xs = u[:, :SSD_WIDTH]
        dt = _softplus(dt_ref[...] + dtb_ref[...])
        dec = jnp.exp(dt * (-jnp.exp(alog_ref[...])))
        e = e_ref[...]
        xdt = xs * _dot32(dt, e)
        dec_e = _dot32(dec, e)
        pad = jnp.zeros((LANE - nb, SSD_WIDTH), F32)
        xt_sc[...] = jnp.concatenate([xdt, pad], axis=0).T
        dect_sc[...] = jnp.concatenate([dec_e, pad], axis=0).T
        bc_sc[...] = u[:, SSD_WIDTH:]
        xs_sc[...] = xs
        yt_sc[...] = jnp.zeros(yt_sc.shape, F32)

    ri = lax.broadcasted_iota(jnp.int32, (LANE, LANE), 0)
    onehot = jnp.where(ri == b, 1.0, 0.0)
    xcol = _dot32(xt_sc[...], onehot)
    dcol = _dot32(dect_sc[...], onehot)
    bc = bc_sc[pl.ds(b, 1), :]
    row = lax.broadcasted_iota(jnp.int32, (rows, SSD_STATE), 0)
    first = row < rows // SSD_GROUPS
    b_full = jnp.where(first, bc[:, 0:SSD_STATE], bc[:, SSD_STATE:gn])
    c_full = jnp.where(first, bc[:, gn:gn + SSD_STATE], bc[:, gn + SSD_STATE:])
    new = dcol * h0_ref[0] + xcol * b_full
    hout_ref[0] = new
    ycol = _dot32(new * c_full, jnp.ones((SSD_STATE, LANE), F32))
    lane = lax.broadcasted_iota(jnp.int32, (rows, LANE), 1)
    yt_sc[...] = jnp.where(lane == b, ycol, yt_sc[...])

    @pl.when(b == nb - 1)
    def _():
        y_ref[...] = yt_sc[...].T[0:nb, :] + xs_sc[...] * dsk_ref[...]


def _ssd_step(xbc, conv_state, dt, h0, conv_w, conv_b, dt_bias, a_log, d_skip):
    nb = xbc.shape[0]
    rows = SSD_HEADS * HEAD_DIM
    full = lambda shape: pl.BlockSpec(shape, lambda i: (0,) * len(shape))
    y, hout = pl.pallas_call(
        _ssd_step_kernel,
        out_shape=(jax.ShapeDtypeStruct((nb, SSD_WIDTH), F32),
                   jax.ShapeDtypeStruct((nb, rows, SSD_STATE), F32)),
        grid=(nb,),
        in_specs=[full((nb, CONV_DIM)), full((nb, CONV_DIM)), full((nb, CONV_DIM)), full((nb, CONV_DIM)),
                  full((nb, LANE)),
                  pl.BlockSpec((1, rows, SSD_STATE), lambda i: (i, 0, 0)),
                  full((SSD_CONV, CONV_DIM)), full((1, CONV_DIM)), full((1, LANE)), full((1, LANE)),
                  full((1, SSD_WIDTH)), full((LANE, SSD_WIDTH))],
        out_specs=(full((nb, SSD_WIDTH)),
                   pl.BlockSpec((1, rows, SSD_STATE), lambda i: (i, 0, 0))),
        scratch_shapes=[pltpu.VMEM((rows, LANE), F32), pltpu.VMEM((rows, LANE), F32),
                        pltpu.VMEM((nb, 2 * SSD_GROUPS * SSD_STATE), F32),
                        pltpu.VMEM((rows, LANE), F32), pltpu.VMEM((nb, SSD_WIDTH), F32)],
        compiler_params=pltpu.CompilerParams(dimension_semantics=("arbitrary",)),
        name="ssd_step",
    )(xbc, conv_state[:, 0], conv_state[:, 1], conv_state[:, 2], dt, h0.reshape(nb, rows, SSD_STATE),
      conv_w, conv_b.reshape(1, CONV_DIM), _pad_lanes(dt_bias), _pad_lanes(a_log),
      jnp.repeat(d_skip.astype(F32), HEAD_DIM).reshape(1, SSD_WIDTH), _head_expand())
    return y, hout.reshape(nb, SSD_HEADS, HEAD_DIM, SSD_STATE)


def _compress_kernel(pt_ref, *refs, transposed, CMP_PAGES):
    segs = PAGE_SIZE // CMP_STRIDE
    if transposed:
        perm_ref, w1_ref, pe_ref, w2_ref, out_ref, sh_sc, pe_sc, xs_sc = refs[CMP_PAGES:]
        for i in range(CMP_PAGES):
            xs_sc[i] = _dot_nt(perm_ref[...], refs[i][0].astype(BF16))

        def token_rows(k, o):
            return jnp.concatenate([xs_sc[i, o * segs:(o + 1) * segs, k * LANE:(k + 1) * LANE]
                                    for i in range(CMP_PAGES)], axis=0)
    else:
        w1_ref, pe_ref, w2_ref, out_ref, sh_sc, pe_sc = refs[2 * CMP_PAGES:]

        def token_rows(k, o):
            return jnp.concatenate([refs[k * CMP_PAGES + i][0, pl.ds(o, segs, stride=CMP_STRIDE), :]
                                    for i in range(CMP_PAGES)], axis=0)
    s = pl.program_id(1)
    rows = CMP_PAGES * segs
    hid2 = NSA_KV_HEADS * CMP_HID

    @pl.when(s == 0)
    def _():
        sh_sc[:, 0:8, :] = jnp.zeros((2, 8, hid2), F32)
        for k in range(2):
            t = jnp.zeros((8, 2 * hid2), F32)
            for o in range(0, CMP_STRIDE, 2):
                pe2 = jnp.concatenate([pe_ref[o, k], pe_ref[o + 1, k]], axis=1)
                t = t + jnp.dot(pe2.astype(BF16), w1_ref[o // 2, k], preferred_element_type=F32)
            pe_sc[k] = jnp.broadcast_to(t[0:1, 0:hid2] + t[1:2, hid2:], (8, hid2))

    for k in range(2):
        acc = jnp.zeros((rows, 2 * hid2), F32)
        for o in range(0, CMP_STRIDE, 2):
            xo = jnp.concatenate([token_rows(k, o), token_rows(k, o + 1)], axis=1)
            acc = acc + jnp.dot(xo.astype(BF16), w1_ref[o // 2, k], preferred_element_type=F32)
        sh_sc[k, 8:8 + rows, :] = acc[:, 0:hid2]
        pre = acc[:, hid2:] + sh_sc[k, 7:7 + rows, :] + pe_sc[k, 0:1, :]
        sh_sc[k, 0:8, :] = sh_sc[k, rows:rows + 8, :]
        out_ref[0, :, k * LANE:(k + 1) * LANE] = jnp.dot(_silu(pre).astype(BF16), w2_ref[k],
                                                         preferred_element_type=F32)


def _compress_weights(cmp_pe, cmp_w1, cmp_w2):
    span = CMP_BLOCK // CMP_STRIDE
    w1s = cmp_w1.astype(F32).reshape(2, span, CMP_STRIDE, HEAD_DIM, CMP_HID)
    z = jnp.zeros((2, span, CMP_STRIDE, HEAD_DIM, CMP_HID), F32)
    top = jnp.concatenate([w1s, z], axis=-1)
    bot = jnp.concatenate([z, w1s], axis=-1)
    bd = jnp.concatenate([top, bot], axis=-2)
    w1 = jnp.transpose(bd, (2, 0, 3, 1, 4)).reshape(CMP_STRIDE, 2, LANE, span * 2 * CMP_HID).astype(BF16)
    w1 = jnp.transpose(w1.reshape(CMP_STRIDE // 2, 2, 2, LANE, span * 2 * CMP_HID), (0, 2, 1, 3, 4))
    w1 = w1.reshape(CMP_STRIDE // 2, 2, 2 * LANE, span * 2 * CMP_HID)
    pe = cmp_pe.astype(F32).reshape(2, span, CMP_STRIDE, HEAD_DIM)
    pe = jnp.transpose(pe, (2, 0, 1, 3))
    pe = jnp.concatenate([pe, pe], axis=-1)
    pe = jnp.pad(pe, ((0, 0), (0, 0), (0, 8 - span), (0, 0)))
    w2 = cmp_w2.astype(F32)
    z2 = jnp.zeros_like(w2)
    w2bd = jnp.concatenate([jnp.concatenate([w2, z2], axis=-1), jnp.concatenate([z2, w2], axis=-1)],
                           axis=-2).astype(BF16)
    return w1, pe, w2bd


def _compress(pages_arr, page_ids, cweights, transposed):
    nb, n_pages = page_ids.shape
    CMP_PAGES = math.gcd(n_pages, CMP_PAGES_MAX)
    steps = n_pages // CMP_PAGES
    segs = PAGE_SIZE // CMP_STRIDE
    rows = CMP_PAGES * segs
    w1, pe, w2bd = cweights
    hid2 = NSA_KV_HEADS * CMP_HID
    page_of = lambda b, s, pt, i: pt[(b * steps + s) * CMP_PAGES + i]
    scratch = [pltpu.VMEM((2, rows + 8, hid2), F32), pltpu.VMEM((2, 8, hid2), F32)]
    if transposed:
        page_specs = [pl.BlockSpec((1, KV_COLS, PAGE_SIZE), lambda b, s, pt, i=i: (page_of(b, s, pt, i), 0, 0))
                      for i in range(CMP_PAGES)]
        scratch.append(pltpu.VMEM((CMP_PAGES, PAGE_SIZE, KV_COLS), F32))
        perm = np.zeros((PAGE_SIZE, PAGE_SIZE), np.float32)
        for o in range(CMP_STRIDE):
            for sg in range(segs):
                perm[o * segs + sg, sg * CMP_STRIDE + o] = 1.0
        extra, extra_specs = [jnp.asarray(perm, BF16)], [pl.BlockSpec((PAGE_SIZE, PAGE_SIZE), lambda b, s, pt: (0, 0))]
    else:
        extra, extra_specs = [], []
        page_specs = [pl.BlockSpec((1, PAGE_SIZE, LANE), lambda b, s, pt, i=i, k=k: (page_of(b, s, pt, i), 0, k))
                      for k in range(2) for i in range(CMP_PAGES)]
    full = lambda shape: pl.BlockSpec(shape, lambda b, s, pt: (0,) * len(shape))
    return pl.pallas_call(
        functools.partial(_compress_kernel, transposed=transposed, CMP_PAGES=CMP_PAGES),
        out_shape=jax.ShapeDtypeStruct((nb, n_pages * segs, KV_COLS), F32),
        grid_spec=pltpu.PrefetchScalarGridSpec(
            num_scalar_prefetch=1,
            grid=(nb, steps),
            in_specs=page_specs + extra_specs + [full(w1.shape), full(pe.shape), full(w2bd.shape)],
            out_specs=pl.BlockSpec((1, rows, KV_COLS), lambda b, s, pt: (b, s, 0)),
            scratch_shapes=scratch),
        compiler_params=pltpu.CompilerParams(dimension_semantics=("arbitrary", "arbitrary"),
                                             vmem_limit_bytes=VMEM_LIMIT),
        name="nsa_compress",
    )(page_ids.reshape(-1), *([pages_arr] * len(page_specs)), *extra, w1, pe, w2bd)


def _overlap_matrix(n_rows, n_blocks, n_cols):
    m = np.arange(n_rows)[:, None]
    j = np.arange(n_cols)[None, :]
    cs = (m - 1) * CMP_STRIDE
    ov = (m >= 1) & (j < n_blocks) & (cs < j * SLC_BLOCK + SLC_BLOCK) & (cs + CMP_BLOCK > j * SLC_BLOCK)
    return jnp.asarray(ov.astype(np.float32))


def _stack_q(q, g):
    return _stack_q_f32(q, g).astype(BF16)


def _stack_q_f32(q, g):
    lane = lax.broadcasted_iota(jnp.int32, (Q_BLOCK, LANE), 1)
    keep = (lane < HALF) if g == 0 else (lane >= HALF)
    parts = []
    for jp in range(NSA_REP // 2):
        j = g * (NSA_REP // 2) + jp
        slab = q[:, j * LANE:(j + 1) * LANE]
        rolled = pltpu.roll(slab, HALF, 1)
        first, second = (slab, rolled) if g == 0 else (rolled, slab)
        parts.append(jnp.where(keep, first, 0.0))
        parts.append(jnp.where(keep, second, 0.0))
    return jnp.concatenate(parts, axis=0)


def _unstack_o(acc, g):
    lane = lax.broadcasted_iota(jnp.int32, (Q_BLOCK, LANE), 1)
    outs = []
    for jp in range(NSA_REP // 2):
        a = acc[(2 * jp) * Q_BLOCK:(2 * jp + 1) * Q_BLOCK]
        b = acc[(2 * jp + 1) * Q_BLOCK:(2 * jp + 2) * Q_BLOCK]
        if g == 0:
            outs.append(jnp.where(lane < HALF, a, pltpu.roll(b, HALF, 1)))
        else:
            outs.append(jnp.where(lane < HALF, pltpu.roll(a, HALF, 1), b))
    return jnp.concatenate(outs, axis=1)


def _tile8(x):
    return jnp.concatenate([x] * NSA_REP, axis=0)


def _rank_rows(imp, n_valid):
    sub = 8
    n_rank = -(-n_valid // sub) * sub
    chunks = [imp[c:c + sub] for c in range(0, n_rank, sub)]
    ranks = [jnp.zeros(ch.shape, F32) for ch in chunks]
    jrow = lax.broadcasted_iota(jnp.int32, chunks[0].shape, 0)
    for k in range(n_valid):
        rk = imp[k:k + 1, :]
        for c, ch in enumerate(chunks):
            if c * sub > k:
                ahead = rk >= ch
            elif c * sub + sub - 1 < k:
                ahead = rk > ch
            else:
                ahead = (rk > ch) | ((rk == ch) & (jrow + c * sub > k))
            ranks[c] = ranks[c] + jnp.where(ahead, 1.0, 0.0)
    rest = jnp.full((imp.shape[0] - n_rank, imp.shape[1]), float(n_valid), F32)
    return jnp.concatenate(ranks + [rest], axis=0)


def _cmp_kernel(q_ref, kv_ref, pb_ref, ov_ref, o_ref, sel_ref, bias_sc, *, n_keys, n_blocks):
    i = pl.program_id(0)

    @pl.when(pl.program_id(1) == 0)
    def _():
        for h in range(NSA_HEADS):
            bias_sc[h] = pltpu.roll(pb_ref[h], (4 * i + 4) % n_keys, 1)

    q = q_ref[0]
    kv = kv_ref[0]
    kc = kv[:, 0:LANE].astype(BF16)
    vc = kv[:, LANE:].astype(BF16)
    qi = lax.broadcasted_iota(jnp.int32, (Q_BLOCK, n_keys), 0)
    mi = lax.broadcasted_iota(jnp.int32, (Q_BLOCK, n_keys), 1)
    qpos = Q_BLOCK * i + qi
    valid8 = _tile8((mi >= 1) & (CMP_STRIDE * mi + CMP_STRIDE - 1 <= qpos))
    rowvalid8 = _tile8(jnp.where(qpos[:, 0:1] >= CMP_BLOCK - 1, 1.0, 0.0))
    jj = lax.broadcasted_iota(jnp.int32, (LANE, LANE), 1)
    psum = []
    for g in range(NSA_KV_HEADS):
        s = _dot_nt(_stack_q(q, g), kc)
        s = jnp.where(valid8, s + bias_sc[g * NSA_REP:(g + 1) * NSA_REP].reshape(ROWS, n_keys), NEG)
        p = jnp.exp(s - jnp.max(s, axis=-1, keepdims=True))
        pc = p / jnp.sum(p, axis=-1, keepdims=True) * rowvalid8
        o_ref[0, :, g * GROUP_W:(g + 1) * GROUP_W] = _unstack_o(
            jnp.dot(pc.astype(BF16), vc, preferred_element_type=F32), g)
        ps = pc[0:Q_BLOCK]
        for r in range(1, NSA_REP):
            ps = ps + pc[r * Q_BLOCK:(r + 1) * Q_BLOCK]
        psum.append(ps)
    ps = jnp.concatenate(psum, axis=0)
    hi = ps.astype(BF16)
    lo = (ps - hi.astype(F32)).astype(BF16)
    ov = ov_ref[...].astype(BF16)
    imp = (jnp.dot(hi, ov, preferred_element_type=F32)
           + jnp.dot(lo, ov, preferred_element_type=F32))
    forced = (jj == 0) | (jj == i) | (jj == i - 1)
    imp = jnp.where(forced, 1e6, jnp.where(jj <= i, imp, -1e6))
    imp = jnp.where(jj < n_blocks, imp, -2e6)
    rank = _rank_rows(imp.T, n_blocks)
    sel_ref[0, 0] = jnp.where(rank < SLC_TOP, 1.0, 0.0).astype(BF16)


def _cmp_prompt(q, kvc_cmp, pattern, ov):
    b, l, _ = q.shape
    n_keys = kvc_cmp.shape[1]
    n_blocks = l // SLC_BLOCK
    return pl.pallas_call(
        functools.partial(_cmp_kernel, n_keys=n_keys, n_blocks=n_blocks),
        out_shape=(jax.ShapeDtypeStruct((b, l, NSA_WIDTH), F32),
                   jax.ShapeDtypeStruct((b, l // Q_BLOCK, LANE, LANE), BF16)),
        grid=(l // Q_BLOCK, b),
        in_specs=[pl.BlockSpec((1, Q_BLOCK, NSA_WIDTH), lambda i, bi: (bi, i, 0)),
                  pl.BlockSpec((1, n_keys, KV_COLS), lambda i, bi: (bi, 0, 0)),
                  pl.BlockSpec(pattern.shape, lambda i, bi: (0, 0, 0)),
                  pl.BlockSpec(ov.shape, lambda i, bi: (0, 0))],
        out_specs=(pl.BlockSpec((1, Q_BLOCK, NSA_WIDTH), lambda i, bi: (bi, i, 0)),
                   pl.BlockSpec((1, 1, LANE, LANE), lambda i, bi: (bi, i, 0, 0))),
        scratch_shapes=[pltpu.VMEM(pattern.shape, F32)],
        compiler_params=pltpu.CompilerParams(dimension_semantics=("arbitrary", "arbitrary"),
                                             vmem_limit_bytes=VMEM_LIMIT),
        name="nsa_cmp_prompt",
    )(q, kvc_cmp, pattern, ov)


def _flash_update_t(st, vt, m, acc):
    m_new = jnp.maximum(m, jnp.max(st, axis=0, keepdims=True))
    p = jnp.exp2(st - m_new)
    acc = jnp.exp2(m - m_new) * acc + jnp.dot(vt, p.astype(BF16), preferred_element_type=F32)
    return m_new, acc


def _finish_t(acc):
    o = (acc[0:HEAD_DIM] / acc[HEAD_DIM:HEAD_DIM + 1]).T
    return jnp.concatenate([o[r * Q_BLOCK:(r + 1) * Q_BLOCK] for r in range(NSA_REP)], axis=1)


def _slc_kernel(q_ref, sel_ref, kx_ref, vt_ref, tnt_ref, tile_ref, o_ref, far_sc, near_sc, sta_sc, stb_sc, stn_sc):
    i = pl.program_id(1)
    q = q_ref[0] * LOG2E
    n_far = (jnp.maximum(i - 3, 0) + 3) // 4
    sel = sel_ref[0, 0]
    jrow = lax.broadcasted_iota(jnp.int32, (LANE, ROWS), 0)
    a = ((i + 1) // 2) * LANE
    delta = Q_BLOCK * (i + 1) - a
    groups = range(NSA_KV_HEADS)
    for g in groups:
        qs_t = _stack_q_f32(q, g).T.astype(BF16)
        hit = jnp.dot(sel, tile_ref[g], preferred_element_type=F32) > 0.5
        near_sc[g] = jnp.concatenate([qs_t, jnp.where(hit & (jrow != PAD_FLAG_COL), 0.0, NEG).astype(BF16)], axis=0)
        far_sc[g] = jnp.concatenate([qs_t, jnp.where(hit & (jrow < i - 3), 0.0, NEG).astype(BF16)], axis=0)

    def far_scores(t, dst):
        start = pl.multiple_of(SLC_PAD + SLC_NEAR * t, SLC_NEAR)
        kx = kx_ref[0, pl.ds(start, SLC_NEAR), :]
        for g in groups:
            dst[g] = jnp.dot(kx, far_sc[g], preferred_element_type=F32)

    def far_softmax(t, src, carry):
        start = pl.multiple_of(SLC_PAD + SLC_NEAR * t, SLC_NEAR)
        return tuple(_flash_update_t(src[g], vt_ref[0, g, :, pl.ds(start, SLC_NEAR)], *carry[g]) for g in groups)

    def far_pair(u, carry):
        far_scores(2 * u + 1, stb_sc)
        carry = far_softmax(2 * u, sta_sc, carry)
        far_scores(2 * u + 2, sta_sc)
        return far_softmax(2 * u + 1, stb_sc, carry)

    far_scores(0, sta_sc)
    start = pl.multiple_of(a, LANE)
    kx = kx_ref[0, pl.ds(start, SLC_NEARW), :]
    for g in groups:
        stn_sc[g] = jnp.dot(kx, near_sc[g], preferred_element_type=F32)
    init = (jnp.full((1, ROWS), NEG, F32), jnp.zeros((HEAD_DIM + ONES_ROWS, ROWS), F32))
    carry = lax.fori_loop(0, (n_far + 1) // 2, far_pair, (init, init))

    for g in groups:
        st = stn_sc[g] + tnt_ref[delta // Q_BLOCK, g]
        _, acc = _flash_update_t(st, vt_ref[0, g, :, pl.ds(start, SLC_NEARW)], *carry[g])
        o_ref[0, :, g * GROUP_W:(g + 1) * GROUP_W] = _finish_t(acc)


def _slc_prompt(q, sel, kx_pad, vt_pad, tnt, tile):
    b, l, _ = q.shape
    lp = kx_pad.shape[1]
    tq = Q_BLOCK
    n_chains = NSA_KV_HEADS
    return pl.pallas_call(
        _slc_kernel,
        out_shape=jax.ShapeDtypeStruct((b, l, NSA_WIDTH), F32),
        grid=(b, l // tq),
        in_specs=[pl.BlockSpec((1, tq, NSA_WIDTH), lambda bi, i: (bi, i, 0)),
                  pl.BlockSpec((1, 1, LANE, LANE), lambda bi, i: (bi, i, 0, 0)),
                  pl.BlockSpec((1, lp, 2 * LANE), lambda bi, i: (bi, 0, 0)),
                  pl.BlockSpec((1, NSA_KV_HEADS, HEAD_DIM + ONES_ROWS, lp), lambda bi, i: (bi, 0, 0, 0)),
                  pl.BlockSpec(tnt.shape, lambda bi, i: (0, 0, 0, 0)),
                  pl.BlockSpec(tile.shape, lambda bi, i: (0, 0, 0))],
        out_specs=pl.BlockSpec((1, tq, NSA_WIDTH), lambda bi, i: (bi, i, 0)),
        scratch_shapes=[pltpu.VMEM((n_chains, 2 * LANE, ROWS), BF16),
                        pltpu.VMEM((n_chains, 2 * LANE, ROWS), BF16),
                        pltpu.VMEM((n_chains, SLC_NEAR, ROWS), F32),
                        pltpu.VMEM((n_chains, SLC_NEAR, ROWS), F32),
                        pltpu.VMEM((n_chains, SLC_NEARW, ROWS), F32)],
        compiler_params=pltpu.CompilerParams(vmem_limit_bytes=VMEM_LIMIT),
        name="nsa_slc_prompt",
    )(q, sel, kx_pad, vt_pad, tnt, tile)


def _group_tile():
    t = np.zeros((NSA_KV_HEADS, LANE, ROWS), np.float32)
    for g in range(NSA_KV_HEADS):
        for r in range(NSA_REP):
            for qq in range(Q_BLOCK):
                t[g, g * Q_BLOCK + qq, r * Q_BLOCK + qq] = 1.0
    return jnp.asarray(t, BF16)


def _win_kernel(q_ref, kx_ref, vt_ref, tw_ref, o_ref, st_sc):
    i2 = pl.program_id(1)
    q = q_ref[0] * LOG2E
    start = pl.multiple_of(i2 * LANE, LANE)
    kx = kx_ref[0, pl.ds(start, WIN_W), :]
    groups = range(NSA_KV_HEADS)
    row = lax.broadcasted_iota(jnp.int32, (LANE, 2 * ROWS), 0)
    pad_rows = jnp.where(row == 0, NEG, 0.0).astype(BF16)
    for g in groups:
        qs_t = jnp.concatenate([_stack_q_f32(q[h * Q_BLOCK:(h + 1) * Q_BLOCK], g).T for h in range(2)], axis=1)
        qx = jnp.concatenate([qs_t.astype(BF16), pad_rows], axis=0)
        st_sc[g] = jnp.dot(kx, qx, preferred_element_type=F32)
    for g in groups:
        st = st_sc[g] + tw_ref[g]
        p = jnp.exp2(st - jnp.max(st, axis=0, keepdims=True))
        acc = jnp.dot(vt_ref[0, g, :, pl.ds(start, WIN_W)], p.astype(BF16), preferred_element_type=F32)
        for h in range(2):
            o_ref[0, h * Q_BLOCK:(h + 1) * Q_BLOCK, g * GROUP_W:(g + 1) * GROUP_W] = _finish_t(
                acc[:, h * ROWS:(h + 1) * ROWS])


def _win_prompt(q, kx_pad, vt_pad, tw):
    b, l, _ = q.shape
    lp = kx_pad.shape[1]
    tq = 2 * Q_BLOCK
    return pl.pallas_call(
        _win_kernel,
        out_shape=jax.ShapeDtypeStruct((b, l, NSA_WIDTH), F32),
        grid=(b, l // tq),
        in_specs=[pl.BlockSpec((1, tq, NSA_WIDTH), lambda bi, i: (bi, i, 0)),
                  pl.BlockSpec((1, lp, 2 * LANE), lambda bi, i: (bi, 0, 0)),
                  pl.BlockSpec((1, NSA_KV_HEADS, HEAD_DIM + ONES_ROWS, lp), lambda bi, i: (bi, 0, 0, 0)),
                  pl.BlockSpec(tw.shape, lambda bi, i: (0, 0, 0))],
        out_specs=pl.BlockSpec((1, tq, NSA_WIDTH), lambda bi, i: (bi, i, 0)),
        scratch_shapes=[pltpu.VMEM((NSA_KV_HEADS, WIN_W, 2 * ROWS), F32)],
        compiler_params=pltpu.CompilerParams(vmem_limit_bytes=VMEM_LIMIT),
        name="nsa_win_prompt",
    )(q, kx_pad, vt_pad, tw)


def _dense1_kernel(q_ref, k_ref, v_ref, bm_ref, o_ref, ps_ref):
    k = k_ref[0].astype(BF16)
    v = v_ref[0].astype(BF16)
    for g in range(NSA_KV_HEADS):
        rows = slice(g * NSA_REP, (g + 1) * NSA_REP)
        s = _dot_nt(q_ref[0, rows, :], k) + bm_ref[rows, :]
        p = jnp.exp(s - jnp.max(s, axis=-1, keepdims=True))
        pc = p / jnp.sum(p, axis=-1, keepdims=True)
        o_ref[0, rows, :] = jnp.dot(pc.astype(BF16), v, preferred_element_type=F32)
        ps_ref[0, g:g + 1, :] = jnp.sum(pc, axis=0, keepdims=True)


def _dense1(q_pad, kv, biasmask):
    nb, n, _ = kv.shape
    return pl.pallas_call(
        _dense1_kernel,
        out_shape=(jax.ShapeDtypeStruct((nb, NSA_HEADS, LANE), F32),
                   jax.ShapeDtypeStruct((nb, NSA_KV_HEADS, n), F32)),
        grid=(nb,),
        in_specs=[pl.BlockSpec((1, NSA_HEADS, LANE), lambda b: (b, 0, 0)),
                  pl.BlockSpec((1, n, LANE), lambda b: (b, 0, 0)),
                  pl.BlockSpec((1, n, LANE), lambda b: (b, 0, 1)),
                  pl.BlockSpec((NSA_HEADS, n), lambda b: (0, 0))],
        out_specs=(pl.BlockSpec((1, NSA_HEADS, LANE), lambda b: (b, 0, 0)),
                   pl.BlockSpec((1, NSA_KV_HEADS, n), lambda b: (b, 0, 0))),
        name="nsa_dense_sample",
    )(q_pad, kv, kv, biasmask)


def _take_group_half(o_pad):
    nb = o_pad.shape[0]
    o = o_pad.reshape(nb, NSA_KV_HEADS, NSA_REP, NSA_KV_HEADS, HEAD_DIM)
    o = jnp.stack([o[:, g, :, g, :] for g in range(NSA_KV_HEADS)], axis=1)
    return o.reshape(nb, NSA_WIDTH)


def _topk_kernel(ps_ref, ov_ref, idx_ref, imp_sc, *, n_blocks, cur):
    n_rows = ps_ref.shape[0]
    n_cols = ov_ref.shape[1]
    ps = jnp.concatenate([ps_ref[...], jnp.zeros((LANE - n_rows, ps_ref.shape[1]), F32)], axis=0)
    imp = _dot32(ps, ov_ref[...])
    jj = lax.broadcasted_iota(jnp.int32, (LANE, n_cols), 1)
    forced = (jj == 0) | (jj == cur) | (jj == cur - 1)
    imp = jnp.where(forced, 1e6, jnp.where(jj <= cur, imp, -1e6))
    imp = jnp.where(jj < n_blocks, imp, -2e6)
    imp_t = imp.T
    imp_sc[...] = imp_t
    jrow = lax.broadcasted_iota(jnp.int32, (n_cols, LANE), 0)

    def body(k, rank):
        rk = imp_sc[pl.ds(k, 1), :]
        ahead = (rk > imp_t) | ((rk == imp_t) & (jrow > k))
        return rank + jnp.where(ahead, 1.0, 0.0)

    rank = lax.fori_loop(0, n_blocks, body, jnp.zeros((n_cols, LANE), F32))
    jf = jrow.astype(F32)
    rows = [jnp.sum(jnp.where(rank == float(r), jf, 0.0), axis=0, keepdims=True) for r in range(SLC_TOP)]
    idx_ref[...] = jnp.concatenate(rows, axis=0).astype(jnp.int32)


def _topk_sample(psum, ov, n_blocks, cur):
    n_rows, n_keys = psum.shape
    n_cols = ov.shape[1]
    return pl.pallas_call(
        functools.partial(_topk_kernel, n_blocks=n_blocks, cur=cur),
        out_shape=jax.ShapeDtypeStruct((SLC_TOP, LANE), jnp.int32),
        grid=(1,),
        in_specs=[pl.BlockSpec((n_rows, n_keys), lambda i: (0, 0)),
                  pl.BlockSpec(ov.shape, lambda i: (0, 0))],
        out_specs=pl.BlockSpec((SLC_TOP, LANE), lambda i: (0, 0)),
        scratch_shapes=[pltpu.VMEM((n_cols, LANE), F32)],
        name="nsa_topk_sample",
    )(psum, ov)


def _slc1_kernel(phys_ref, jsel_ref, q_ref, *refs, cur, past):
    pages = refs[:SLC_TOP]
    new_ref, bb_ref, o_ref = refs[SLC_TOP:]
    b = pl.program_id(0)
    g = pl.program_id(1)
    goff = pl.multiple_of(g * HEAD_DIM, HEAD_DIM)
    q = q_ref[0]
    lane = lax.broadcasted_iota(jnp.int32, (NSA_REP, PAGE_SIZE), 1)
    first = lax.broadcasted_iota(jnp.int32, (HEAD_DIM, PAGE_SIZE), 1) == 0
    new_k = jnp.where(first, new_ref[0, pl.ds(goff, HEAD_DIM), :], 0.0)
    new_v = jnp.where(first, new_ref[0, pl.ds(LANE + goff, HEAD_DIM), :], 0.0)
    scores, values = [], []
    for n in range(SLC_TOP):
        j = jsel_ref[(b * NSA_KV_HEADS + g) * SLC_TOP + n]
        kt = jnp.where(j == cur, new_k, pages[n][0, pl.ds(goff, HEAD_DIM), :])
        vt = jnp.where(j == cur, new_v, pages[n][0, pl.ds(LANE + goff, HEAD_DIM), :])
        s = jnp.dot(q, kt.astype(BF16), preferred_element_type=F32)
        ok = (lane // SLC_BLOCK == j % 2) & ((j // 2) * PAGE_SIZE + lane <= past)
        bias = bb_ref[jnp.clip(j - (cur - 3), 0, 3), pl.ds(pl.multiple_of(g * NSA_REP, NSA_REP), NSA_REP), :]
        scores.append(jnp.where(ok, s + bias, NEG))
        values.append(vt.astype(BF16))
    s_all = jnp.concatenate(scores, axis=1)
    p = jnp.exp(s_all - jnp.max(s_all, axis=-1, keepdims=True))
    acc = jnp.zeros((NSA_REP, HEAD_DIM), F32)
    for n in range(SLC_TOP):
        acc = acc + _dot_nt(p[:, n * PAGE_SIZE:(n + 1) * PAGE_SIZE].astype(BF16), values[n])
    o_ref[0] = acc / jnp.sum(p, axis=-1, keepdims=True)


def _slc_sample(q, cache_pages_t, phys, jsel, new_cols, biasblk, cur, past):
    nb = q.shape[0]
    idx = lambda b, g, n: (b * NSA_KV_HEADS + g) * SLC_TOP + n
    page_specs = [pl.BlockSpec((1, KV_COLS, PAGE_SIZE), lambda b, g, ph, js, n=n: (ph[idx(b, g, n)], 0, 0))
                  for n in range(SLC_TOP)]
    return pl.pallas_call(
        functools.partial(_slc1_kernel, cur=cur, past=past),
        out_shape=jax.ShapeDtypeStruct((nb, NSA_HEADS, HEAD_DIM), F32),
        grid_spec=pltpu.PrefetchScalarGridSpec(
            num_scalar_prefetch=2,
            grid=(nb, NSA_KV_HEADS),
            in_specs=[pl.BlockSpec((1, NSA_REP, HEAD_DIM), lambda b, g, ph, js: (b, g, 0))] + page_specs
            + [pl.BlockSpec((1, KV_COLS, 1), lambda b, g, ph, js: (b, 0, 0)),
               pl.BlockSpec(biasblk.shape, lambda b, g, ph, js: (0, 0, 0))],
            out_specs=pl.BlockSpec((1, NSA_REP, HEAD_DIM), lambda b, g, ph, js: (b, g, 0))),
        name="nsa_slc_sample",
    )(phys, jsel, q, *([cache_pages_t] * SLC_TOP), new_cols, biasblk)


def _out_kernel(x_ref, gate_ref, yssd_ref, zs_ref, oc_ref, os_ref, ow_ref, gl_ref, za_ref,
                nw1_ref, nw2_ref, w_ref, eg_ref, fg_ref, o_ref, *, per_row, final):
    gates = _sigmoid(gl_ref[...])
    g2 = jnp.concatenate(_split_bf16(gates, 2), axis=1)
    expand = lambda br: jnp.dot(g2, eg_ref[br], preferred_element_type=F32)
    y_nsa = expand(0) * oc_ref[...] + expand(1) * os_ref[...] + expand(2) * ow_ref[...]

    def gated_norm(y, z, w):
        u = y * _silu(z)
        half = u.shape[1] // 2
        parts = []
        for g in range(2):
            ug = u[:, g * half:(g + 1) * half]
            parts.append(ug * lax.rsqrt(jnp.mean(ug * ug, axis=-1, keepdims=True) + NORM_EPS))
        return (jnp.concatenate(parts, axis=1) * w).astype(BF16)

    m1 = gated_norm(yssd_ref[...], zs_ref[...], nw1_ref[...])
    m2 = gated_norm(y_nsa, za_ref[...], nw2_ref[...])
    proj = (jnp.dot(m1, w_ref[0:SSD_WIDTH, :], preferred_element_type=F32)
            + jnp.dot(m2, w_ref[SSD_WIDTH:, :], preferred_element_type=F32))
    gate = gate_ref[...] if per_row else gate_ref[0]
    out = x_ref[...] + gate * proj
    if final:
        out = out * lax.rsqrt(jnp.mean(out * out, axis=-1, keepdims=True) + NORM_EPS) * fg_ref[...]
    o_ref[...] = out


def _gate_expand():
    e = np.zeros((3, LANE, NSA_WIDTH), np.float32)
    for br in range(3):
        for h in range(NSA_HEADS):
            e[br, br * NSA_HEADS + h, h * HEAD_DIM:(h + 1) * HEAD_DIM] = 1.0
    return jnp.asarray(np.concatenate([e, e], axis=1), BF16)


def _layer_out(x2d, gate, y_ssd, z_s, o_cmp, o_slc, o_win, gl, z_a, nw1, nw2, w_out_bf, final_g,
               rows_per_batch, final):
    m = x2d.shape[0]
    per_row = rows_per_batch == 1
    tm = m if per_row else 256
    if per_row:
        gate_spec = pl.BlockSpec((tm, D_MODEL), lambda i: (0, 0))
        gt = gate
    else:
        gate_spec = pl.BlockSpec((1, 1, D_MODEL), lambda i: ((i * tm) // rows_per_batch, 0, 0))
        gt = gate[:, None, :]
    row = lambda w: pl.BlockSpec((tm, w), lambda i: (i, 0))
    full = lambda shape: pl.BlockSpec(shape, lambda i: (0,) * len(shape))
    eg = _gate_expand()
    return pl.pallas_call(
        functools.partial(_out_kernel, per_row=per_row, final=final),
        out_shape=jax.ShapeDtypeStruct((m, D_MODEL), F32),
        grid=(m // tm,),
        in_specs=[row(D_MODEL), gate_spec, row(SSD_WIDTH), row(SSD_WIDTH), row(NSA_WIDTH), row(NSA_WIDTH),
                  row(NSA_WIDTH), row(LANE), row(NSA_WIDTH), full((1, SSD_WIDTH)), full((1, NSA_WIDTH)),
                  full(w_out_bf.shape), full(eg.shape), full((1, D_MODEL))],
        out_specs=row(D_MODEL),
        compiler_params=pltpu.CompilerParams(vmem_limit_bytes=VMEM_LIMIT),
        name="layer_out",
    )(x2d, gt, y_ssd, z_s, o_cmp, o_slc, o_win, gl, z_a, nw1.reshape(1, SSD_WIDTH), nw2.reshape(1, NSA_WIDTH),
      w_out_bf, eg, final_g.reshape(1, D_MODEL))


def _pad_in_weights(w_in):
    cols = []
    off = 0
    for size, width in zip(_SEG_SIZES, _SEG_PAD):
        seg = w_in[:, off:off + size]
        cols.append(jnp.pad(seg, ((0, 0), (0, width - size))))
        off += size
    return jnp.concatenate(cols, axis=1).astype(BF16)


def _values_t(v_pad):
    b, rows, _ = v_pad.shape
    vt = jnp.transpose(v_pad.reshape(b, rows, NSA_KV_HEADS, HEAD_DIM), (0, 2, 3, 1))
    return jnp.concatenate([vt, jnp.ones((b, NSA_KV_HEADS, ONES_ROWS, rows), BF16)], axis=2)


def _front_pad_bf16(kv, rows):
    return jnp.pad(kv, ((0, 0), (rows, 0), (0, 0))).astype(BF16)


def kernel(x_prompt, x_sample, cache_cmp_kv, cache_slc_kv, state_win_kv, state_conv, state_ssm, page_table,
           c_prompt, c_sample, norm_g, ada_w, ada_b, w_in, conv_w, conv_b, dt_bias, a_log, d_skip,
           ssd_norm_w, cmp_pe, cmp_w1, cmp_w2, nsa_norm_w, w_out, rel_bias, final_norm_g):
    nbp, lp, _ = x_prompt.shape
    nbs = x_sample.shape[0]
    depth = w_in.shape[0]
    n_pool = cache_cmp_kv.shape[1]
    n_pages = page_table.shape[1]
    past = n_pages * PAGE_SIZE
    w_buf = state_win_kv.shape[2]
    kv_row = (2, NSA_KV_HEADS, HEAD_DIM)

    n_ck = lp // CMP_STRIDE
    n_sb = lp // SLC_BLOCK
    qi = np.arange(Q_BLOCK)[:, None]
    c0 = n_ck - 4
    pattern = _bias_of_dist(rel_bias, qi - CMP_STRIDE * (np.arange(n_ck)[None, :] - c0) - (CMP_STRIDE - 1))
    far = rel_bias.astype(F32)[REL_BUCKETS - 1][:, None, None]
    par = np.arange(2)[:, None, None]
    qq = np.arange(Q_BLOCK)[None, None, :]
    dist_n = qq - np.arange(SLC_NEARW)[None, :, None] + Q_BLOCK * par + (SLC_NEAR - Q_BLOCK)
    tnt = jnp.stack([_toeplitz_bias(rel_bias, SLC_NEARW, Q_BLOCK, Q_BLOCK * p + SLC_NEAR - Q_BLOCK)
                     for p in range(2)], axis=1)
    tnt = jnp.where(jnp.asarray((dist_n >= 0) & (dist_n <= qq + SLC_NEAR - Q_BLOCK))[None],
                    (tnt - far[..., None]) * LOG2E, NEG)
    tnt = jnp.transpose(tnt.reshape(NSA_KV_HEADS, NSA_REP, 2, SLC_NEARW, Q_BLOCK), (2, 0, 3, 1, 4))
    tnt = tnt.reshape(2, NSA_KV_HEADS, SLC_NEARW, ROWS)
    tile = _group_tile()
    key_row = np.arange(SLC_PAD + lp + SLC_BACK) - SLC_PAD
    assert (key_row[-1] // SLC_BLOCK) < PAD_FLAG_COL
    cols = np.arange(LANE)[None, :]
    blk_onehot = jnp.asarray(((key_row[:, None] >= 0) & (key_row[:, None] // SLC_BLOCK == cols))
                             | ((key_row[:, None] < 0) & (cols == PAD_FLAG_COL)), BF16)
    dist_w = qq - np.arange(WIN_W)[None, :, None] + Q_BLOCK * par + WIN_PAD
    tw = jnp.stack([_toeplitz_bias(rel_bias, WIN_W, Q_BLOCK, Q_BLOCK * p + WIN_PAD) for p in range(2)], axis=1)
    tw = jnp.where(jnp.asarray((dist_w >= 0) & (dist_w < WINDOW))[None], tw * LOG2E, NEG)
    tw = jnp.transpose(tw.reshape(NSA_KV_HEADS, NSA_REP, 2, WIN_W, Q_BLOCK), (2, 0, 3, 1, 4))
    tw = tw.reshape(2, NSA_KV_HEADS, WIN_W, ROWS)
    tw = jnp.concatenate([tw[0], tw[1]], axis=-1)
    win_flag = jnp.asarray((np.arange(WIN_PAD + lp)[:, None] < WIN_PAD) & (np.arange(LANE)[None, :] == 0), BF16)
    ov_p = _overlap_matrix(n_ck, n_sb, LANE)

    n_cs = past // CMP_STRIDE
    cur = past // SLC_BLOCK
    n_blk_s = cur + 1
    n_cols_s = -(-n_blk_s // LANE) * LANE
    ov_s = _overlap_matrix(n_cs, n_blk_s, n_cols_s)
    m_s = np.arange(n_cs)
    bm_c = jnp.where(jnp.asarray(m_s >= 1)[None, :],
                     _bias_of_dist(rel_bias, past - (CMP_STRIDE * m_s + CMP_STRIDE - 1)), NEG)
    n_w = -(-(w_buf + 1) // LANE) * LANE
    iw = np.arange(n_w)
    dw = w_buf - iw
    ok_w = (iw <= w_buf) & (dw >= 0) & (dw < WINDOW) & (past - w_buf + iw >= 0)
    bm_w = jnp.where(jnp.asarray(ok_w)[None, :], _bias_of_dist(rel_bias, dw), NEG)
    jb = (cur - 3 + np.arange(4))[:, None]
    biasblk = _bias_of_dist(rel_bias, past - SLC_BLOCK * jb - np.arange(SLC_BLOCK)[None, :])
    biasblk = jnp.transpose(biasblk, (1, 0, 2))
    biasblk = jnp.concatenate([biasblk, biasblk], axis=-1)

    pages_t = lambda c: jnp.transpose(c, (0, 1, 3, 4, 5, 2)).reshape(depth * n_pool, KV_COLS, PAGE_SIZE)
    cmp_pages_t = pages_t(cache_cmp_kv)
    slc_pages_t = pages_t(cache_slc_kv)
    prompt_pages = jnp.arange(nbp * (lp // PAGE_SIZE), dtype=jnp.int32).reshape(nbp, lp // PAGE_SIZE)

    c_all = jnp.concatenate([c_prompt, c_sample], axis=0)
    xp = x_prompt.reshape(nbp * lp, D_MODEL)
    xs = x_sample.reshape(nbs, D_MODEL)
    outs = {k: [] for k in ("pc", "ps", "pw", "pconv", "pssm", "sc", "ss", "sw", "sconv", "sssm")}

    for l in range(depth):
        final = l == depth - 1
        w_pad = _pad_in_weights(w_in[l])
        w_out_bf = w_out[l].astype(BF16)
        cweights = _compress_weights(cmp_pe[l], cmp_w1[l], cmp_w2[l])
        mod = _modulation(c_all, ada_w[l], ada_b[l])
        shift, scale, gate = mod[:, :D_MODEL], mod[:, D_MODEL:2 * D_MODEL], mod[:, 2 * D_MODEL:]

        z_s, xbc, dt, q, kvc, kvs, kvw, gl, z_a = _in_projection(xp, norm_g[l], scale[:nbp], shift[:nbp], w_pad, lp)
        xbc3 = xbc.reshape(nbp, lp, CONV_DIM)
        y_ssd, h_fin = _ssd_prompt(xbc3, dt.reshape(nbp, lp, LANE), conv_w[l], conv_b[l], dt_bias[l], a_log[l],
                                   d_skip[l])
        q3 = q.reshape(nbp, lp, NSA_WIDTH)
        kvc3, kvs3, kvw3 = (t.reshape(nbp, lp, KV_COLS) for t in (kvc, kvs, kvw))
        kc = _compress(kvc.reshape(nbp * (lp // PAGE_SIZE), PAGE_SIZE, KV_COLS), prompt_pages, cweights, False)
        o_cmp, sel = _cmp_prompt(q3, kc, pattern, ov_p)
        ks_pad = jnp.pad(kvs3, ((0, 0), (SLC_PAD, SLC_BACK), (0, 0))).astype(BF16)
        kx_pad = jnp.concatenate([ks_pad[:, :, :LANE], jnp.broadcast_to(blk_onehot, (nbp,) + blk_onehot.shape)],
                                 axis=-1)
        o_slc = _slc_prompt(q3, sel, kx_pad, _values_t(ks_pad[:, :, LANE:]), tnt, tile)
        kw_pad = _front_pad_bf16(kvw3, WIN_PAD)
        kwx_pad = jnp.concatenate([kw_pad[:, :, :LANE], jnp.broadcast_to(win_flag, (nbp,) + win_flag.shape)], axis=-1)
        o_win = _win_prompt(q3, kwx_pad, _values_t(kw_pad[:, :, LANE:]), tw)
        xp = _layer_out(xp, gate[:nbp], y_ssd.reshape(nbp * lp, SSD_WIDTH), z_s,
                        o_cmp.reshape(nbp * lp, NSA_WIDTH), o_slc.reshape(nbp * lp, NSA_WIDTH),
                        o_win.reshape(nbp * lp, NSA_WIDTH), gl, z_a, ssd_norm_w[l], nsa_norm_w[l], w_out_bf,
                        final_norm_g, lp, final)
        outs["pc"].append(kvc3.reshape((nbp, lp) + kv_row))
        outs["ps"].append(kvs3.reshape((nbp, lp) + kv_row))
        outs["pw"].append(kvw3[:, -min(WINDOW, lp):].reshape((nbp, min(WINDOW, lp)) + kv_row))
        outs["pconv"].append(xbc3[:, -(SSD_CONV - 1):])
        outs["pssm"].append(h_fin)

        z_s2, xbc2, dt2, q2, kvc2, kvs2, kvw2, gl2, z_a2 = _in_projection(
            xs, norm_g[l], scale[nbp:], shift[nbp:], w_pad, 1)
        y_ssd2, h2 = _ssd_step(xbc2, state_conv[l], dt2, state_ssm[l], conv_w[l], conv_b[l], dt_bias[l],
                               a_log[l], d_skip[l])
        qh = q2.reshape(nbs, NSA_KV_HEADS, NSA_REP, HEAD_DIM)
        zq = jnp.zeros((nbs, NSA_REP, HEAD_DIM), F32)
        q_pad = jnp.stack([jnp.concatenate([qh[:, 0], zq], axis=-1), jnp.concatenate([zq, qh[:, 1]], axis=-1)],
                          axis=1).reshape(nbs, NSA_HEADS, LANE).astype(BF16)
        kc2 = _compress(cmp_pages_t, page_table + l * n_pool, cweights, True)
        oc2, psum = _dense1(q_pad, kc2, bm_c)
        sel_idx = _topk_sample(psum.reshape(nbs * NSA_KV_HEADS, n_cs), ov_s, n_blk_s, cur)
        jsel = sel_idx[:, :nbs * NSA_KV_HEADS].T.reshape(nbs, NSA_KV_HEADS, SLC_TOP)
        jc = jnp.minimum(jsel, cur - 1)
        page = jnp.take_along_axis(page_table, (jc // 2).reshape(nbs, -1), axis=1).reshape(jsel.shape)
        phys = jnp.where(jsel < cur, page + l * n_pool, 0).astype(jnp.int32)
        os2 = _slc_sample(q2.reshape(nbs, NSA_HEADS, HEAD_DIM).astype(BF16), slc_pages_t, phys.reshape(-1),
                          jsel.reshape(-1), kvs2.reshape(nbs, KV_COLS, 1), biasblk, cur, past)
        kw_full = jnp.concatenate([state_win_kv[l].reshape(nbs, w_buf, KV_COLS), kvw2[:, None, :]], axis=1)
        kw_in = jnp.pad(kw_full, ((0, 0), (0, n_w - (w_buf + 1)), (0, 0)))
        ow2, _ = _dense1(q_pad, kw_in, bm_w)
        xs = _layer_out(xs, gate[nbp:], y_ssd2, z_s2, _take_group_half(oc2), os2.reshape(nbs, NSA_WIDTH),
                        _take_group_half(ow2), gl2, z_a2, ssd_norm_w[l], nsa_norm_w[l], w_out_bf,
                        final_norm_g, 1, final)
        outs["sc"].append(kvc2.reshape((nbs, 1) + kv_row))
        outs["ss"].append(kvs2.reshape((nbs, 1) + kv_row))
        outs["sw"].append(kw_full[:, -w_buf:].reshape((nbs, w_buf) + kv_row))
        outs["sconv"].append(jnp.concatenate([state_conv[l][:, 1:], xbc2[:, None, :]], axis=1))
        outs["sssm"].append(h2)

    st = lambda k: jnp.stack(outs[k])
    return (xp.reshape(nbp, lp, D_MODEL), xs.reshape(nbs, 1, D_MODEL),
            st("pc"), st("ps"), st("pw"), st("pconv"), st("pssm"),
            st("sc"), st("ss"), st("sw"), st("sconv"), st("sssm"))
```

```python
import functools
import math

import numpy as np
import jax
import jax.numpy as jnp
from jax import lax
from jax.experimental import pallas as pl
from jax.experimental.pallas import tpu as pltpu

F32 = jnp.float32
BF16 = jnp.bfloat16
HIGHEST = lax.Precision.HIGHEST

D_MODEL = 1024
HEAD_DIM = 64
SSD_WIDTH = 1024
SSD_HEADS = 16
SSD_GROUPS = 2
SSD_STATE = 128
SSD_CONV = 4
SSD_CHUNK = 256
CONV_DIM = SSD_WIDTH + 2 * SSD_GROUPS * SSD_STATE
NSA_WIDTH = 1024
NSA_HEADS = 16
NSA_KV_HEADS = 2
NSA_REP = NSA_HEADS // NSA_KV_HEADS
CMP_BLOCK = 32
CMP_STRIDE = 16
CMP_HID = 2 * HEAD_DIM
SLC_BLOCK = 64
SLC_TOP = 16
WINDOW = 512
Q_BLOCK = 64
REL_BUCKETS = 32
REL_MAX_DIST = 128
NORM_EPS = 1e-6
KV_COLS = 2 * NSA_KV_HEADS * HEAD_DIM
PAGE_SIZE = 128
NEG = -1e30
LOG2E = 1.4426950408889634

LANE = 128
HALF = LANE // 2
GROUP_W = NSA_REP * HEAD_DIM
ROWS = NSA_REP * Q_BLOCK
SLC_NEAR = 4 * SLC_BLOCK
SLC_PAD = SLC_NEAR
SLC_NEARW = SLC_NEAR + LANE
ONES_ROWS = 8
PAD_FLAG_COL = LANE - 1
SLC_BACK = 2 * SLC_NEAR
WIN_W = WINDOW + 2 * Q_BLOCK
WIN_PAD = WINDOW
CMP_PAGES_MAX = 32
VMEM_LIMIT = 48 * 1024 * 1024

_SEG_NAMES = ("z_s", "xbc", "dt", "q", "kvc", "kvs", "kvw", "gl", "z_a")
_SEG_SIZES = (SSD_WIDTH, CONV_DIM, SSD_HEADS, NSA_WIDTH, KV_COLS, KV_COLS, KV_COLS, 3 * NSA_HEADS, NSA_WIDTH)
_SEG_PAD = tuple(-(-s // LANE) * LANE for s in _SEG_SIZES)
_SEG_OFF = tuple(int(o) for o in np.cumsum((0,) + _SEG_PAD[:-1]))
IN_PAD = int(sum(_SEG_PAD))


def _sigmoid(x):
    return 1.0 / (1.0 + jnp.exp(-x))


def _silu(x):
    return x * _sigmoid(x)


def _dot32(a, b):
    return jnp.dot(a, b, precision=HIGHEST, preferred_element_type=F32)


def _split_bf16(x, terms):
    parts = []
    for _ in range(terms):
        p = x.astype(BF16)
        parts.append(p)
        x = x - p.astype(F32)
    return parts


def _expand2(x, sel2_bf16):
    hi, lo = _split_bf16(x, 2)
    return jnp.dot(jnp.concatenate([hi, lo], axis=1), sel2_bf16, preferred_element_type=F32)


def _sel_dot(sel_bf16, x, terms):
    return sum(jnp.dot(sel_bf16, p, preferred_element_type=F32) for p in _split_bf16(x, terms))


def _dot_nt(a, b):
    return lax.dot_general(a, b, (((1,), (1,)), ((), ())), preferred_element_type=F32)


def _bucket_table():
    n = np.arange(REL_MAX_DIST + 1)
    max_exact = REL_BUCKETS // 2
    nf = np.maximum(n, 1).astype(np.float32)
    large = max_exact + (np.log(nf / np.float32(max_exact)) / np.float32(math.log(REL_MAX_DIST / max_exact))
                         * np.float32(REL_BUCKETS - max_exact)).astype(np.int32)
    large = np.minimum(large, REL_BUCKETS - 1)
    return np.where(n < max_exact, n, large).astype(np.int32)


_BUCKETS = _bucket_table()


def _bias_of_dist(rel_bias, dist):
    idx = _BUCKETS[np.clip(dist, 0, REL_MAX_DIST)]
    out = jnp.take(rel_bias.astype(F32), jnp.asarray(idx.reshape(-1)), axis=0)
    return out.T.reshape((NSA_HEADS,) + dist.shape)


def _toeplitz_bias(rel_bias, rows, cols, c0):
    n = rows + cols
    d = np.arange(n) + c0 - (rows - 1)
    v = jnp.take(rel_bias.astype(F32), jnp.asarray(_BUCKETS[np.clip(d, 0, REL_MAX_DIST)]), axis=0).T
    flat = jnp.tile(v, (1, rows + 1))[:, :rows * (n + 1)]
    hank = flat.reshape(NSA_HEADS, rows, n + 1)[:, :, :cols]
    return hank[:, ::-1, :]


def _mod_kernel(c_ref, w_ref, b_ref, o_ref):
    o_ref[...] = _dot32(_silu(c_ref[...]), w_ref[...]) + b_ref[...]


def _modulation(c, w, b):
    m, d = c.shape
    n = w.shape[1]
    tn = 512
    return pl.pallas_call(
        _mod_kernel,
        out_shape=jax.ShapeDtypeStruct((m, n), F32),
        grid=(n // tn,),
        in_specs=[pl.BlockSpec((m, d), lambda j: (0, 0)),
                  pl.BlockSpec((d, tn), lambda j: (0, j)),
                  pl.BlockSpec((1, tn), lambda j: (0, j))],
        out_specs=pl.BlockSpec((m, tn), lambda j: (0, j)),
        name="adaln_mod",
    )(c, w, b.reshape(1, n))


def _inproj_kernel(x_ref, g_ref, sc_ref, sh_ref, w_ref, *out_refs, per_row):
    x = x_ref[...]
    xn = x * lax.rsqrt(jnp.mean(x * x, axis=-1, keepdims=True) + NORM_EPS)
    sc = sc_ref[...] if per_row else sc_ref[0]
    sh = sh_ref[...] if per_row else sh_ref[0]
    h = ((xn * g_ref[...]) * (1.0 + sc) + sh).astype(BF16)
    for name, off, width, ref in zip(_SEG_NAMES, _SEG_OFF, _SEG_PAD, out_refs):
        r = jnp.dot(h, w_ref[:, off:off + width], preferred_element_type=F32)
        if name == "q":
            r = r * (HEAD_DIM ** -0.5)
        ref[...] = r


def _in_projection(x2d, g, scale, shift, w_pad, rows_per_batch):
    m = x2d.shape[0]
    per_row = rows_per_batch == 1
    tm = m if per_row else 256
    if per_row:
        mod_spec = pl.BlockSpec((tm, D_MODEL), lambda i: (0, 0))
        sc, sh = scale, shift
    else:
        mod_spec = pl.BlockSpec((1, 1, D_MODEL), lambda i: ((i * tm) // rows_per_batch, 0, 0))
        sc, sh = scale[:, None, :], shift[:, None, :]
    outs = tuple(jax.ShapeDtypeStruct((m, w), F32) for w in _SEG_PAD)
    return pl.pallas_call(
        functools.partial(_inproj_kernel, per_row=per_row),
        out_shape=outs,
        grid=(m // tm,),
        in_specs=[pl.BlockSpec((tm, D_MODEL), lambda i: (i, 0)),
                  pl.BlockSpec((1, D_MODEL), lambda i: (0, 0)),
                  mod_spec, mod_spec,
                  pl.BlockSpec((D_MODEL, IN_PAD), lambda i: (0, 0))],
        out_specs=tuple(pl.BlockSpec((tm, w), lambda i: (i, 0)) for w in _SEG_PAD),
        compiler_params=pltpu.CompilerParams(vmem_limit_bytes=VMEM_LIMIT),
        name="in_projection",
    )(x2d, g.reshape(1, D_MODEL), sc, sh, w_pad)


def _softplus(x):
    return jnp.maximum(x, 0.0) + jnp.log(1.0 + jnp.exp(-jnp.abs(x)))


def _ssd_kernel(xbc_ref, dt_ref, cw_ref, cb_ref, dtb_ref, alog_ref, dsk_ref, e_ref, tril_ref,
                y_ref, hfin_ref, xe_sc, st_sc):
    c = pl.program_id(1)
    q = SSD_CHUNK
    n_pairs = SSD_HEADS // 2

    @pl.when(c == 0)
    def _():
        xe_sc[0:8, :] = jnp.zeros((8, CONV_DIM), F32)
        st_sc[...] = jnp.zeros(st_sc.shape, F32)

    xe_sc[8:8 + q, :] = xbc_ref[0]
    acc = cb_ref[...] + cw_ref[0:1, :] * xe_sc[5:5 + q, :]
    for k in range(1, SSD_CONV):
        acc = acc + cw_ref[k:k + 1, :] * xe_sc[5 + k:5 + k + q, :]
    u = _silu(acc)
    xe_sc[0:8, :] = xe_sc[q:q + 8, :]

    xs = u[:, :SSD_WIDTH]
    gn = SSD_GROUPS * SSD_STATE
    bm = u[:, SSD_WIDTH:SSD_WIDTH + gn]
    cm = u[:, SSD_WIDTH + gn:]

    dt = _softplus(dt_ref[0] + dtb_ref[...])
    a = dt * (-jnp.exp(alog_ref[...]))
    cs = _sel_dot(tril_ref[...], a, 3)
    cs_t = cs.T
    cs_last = cs[q - 1:q, :]
    e = e_ref[...]
    dt_e = _expand2(dt, e)
    w_e = _expand2(dt * jnp.exp(cs_last - cs), e)
    ecs_e = _expand2(jnp.exp(cs), e)
    tot_e = _expand2(jnp.broadcast_to(jnp.exp(cs_last), (8, LANE)), e)[0:1, :]
    xdt = (xs * dt_e).astype(BF16)
    xw = (xs * w_e).astype(BF16)

    li = lax.broadcasted_iota(jnp.int32, (q, q), 0)
    si = lax.broadcasted_iota(jnp.int32, (q, q), 1)
    tri = li >= si
    lane = lax.broadcasted_iota(jnp.int32, (q, LANE), 1)

    for g in range(SSD_GROUPS):
        cg = cm[:, g * SSD_STATE:(g + 1) * SSD_STATE].astype(BF16)
        bg = bm[:, g * SSD_STATE:(g + 1) * SSD_STATE]
        cb = _dot_nt(cg, bg.astype(BF16))
        bg_t = bg.T.astype(BF16)
        for jp in range(n_pairs // SSD_GROUPS):
            j = g * (n_pairs // SSD_GROUPS) + jp
            sl = slice(j * LANE, (j + 1) * LANE)
            xdt_p = xdt[:, sl]
            ys = []
            for hh in (2 * j, 2 * j + 1):
                diff = cs[:, hh:hh + 1] - cs_t[hh:hh + 1, :]
                lmat = jnp.exp(jnp.where(tri, diff, NEG))
                ys.append(jnp.dot((cb * lmat).astype(BF16), xdt_p, preferred_element_type=F32))
            y_diag = jnp.where(lane < HALF, ys[0], ys[1])
            st = st_sc[j]
            y_off = jnp.dot(cg, st.astype(BF16), preferred_element_type=F32) * ecs_e[:, sl]
            y_ref[0, :, sl] = y_diag + y_off + xs[:, sl] * dsk_ref[:, sl]
            new = jnp.dot(bg_t, xw[:, sl], preferred_element_type=F32)
            st_sc[j] = st * tot_e[:, sl] + new

    @pl.when(c == pl.num_programs(1) - 1)
    def _():
        for j in range(n_pairs):
            hfin_ref[0, j * LANE:(j + 1) * LANE, :] = st_sc[j].T


def _head_expand():
    e = np.zeros((LANE, SSD_WIDTH), np.float32)
    for h in range(SSD_HEADS):
        e[h, h * HEAD_DIM:(h + 1) * HEAD_DIM] = 1.0
    return jnp.asarray(e)


def _pad_lanes(v):
    return jnp.pad(v.astype(F32), (0, LANE - v.shape[0])).reshape(1, LANE)


def _ssd_prompt(xbc, dt, conv_w, conv_b, dt_bias, a_log, d_skip):
    b, l, _ = xbc.shape
    nc = l // SSD_CHUNK
    full = lambda shape: pl.BlockSpec(shape, lambda i, c: (0,) * len(shape))
    y, hfin = pl.pallas_call(
        _ssd_kernel,
        out_shape=(jax.ShapeDtypeStruct((b, l, SSD_WIDTH), F32),
                   jax.ShapeDtypeStruct((b, SSD_HEADS * HEAD_DIM, SSD_STATE), F32)),
        grid=(b, nc),
        in_specs=[pl.BlockSpec((1, SSD_CHUNK, CONV_DIM), lambda i, c: (i, c, 0)),
                  pl.BlockSpec((1, SSD_CHUNK, LANE), lambda i, c: (i, c, 0)),
                  full((SSD_CONV, CONV_DIM)), full((1, CONV_DIM)), full((1, LANE)), full((1, LANE)),
                  full((1, SSD_WIDTH)), full((2 * LANE, SSD_WIDTH)), full((SSD_CHUNK, SSD_CHUNK))],
        out_specs=(pl.BlockSpec((1, SSD_CHUNK, SSD_WIDTH), lambda i, c: (i, c, 0)),
                   pl.BlockSpec((1, SSD_HEADS * HEAD_DIM, SSD_STATE), lambda i, c: (i, 0, 0))),
        scratch_shapes=[pltpu.VMEM((SSD_CHUNK + 8, CONV_DIM), F32),
                        pltpu.VMEM((SSD_HEADS // 2, SSD_STATE, LANE), F32)],
        compiler_params=pltpu.CompilerParams(dimension_semantics=("arbitrary", "arbitrary"),
                                             vmem_limit_bytes=VMEM_LIMIT),
        name="ssd_prompt",
    )(xbc, dt, conv_w, conv_b.reshape(1, CONV_DIM), _pad_lanes(dt_bias), _pad_lanes(a_log),
      jnp.repeat(d_skip.astype(F32), HEAD_DIM).reshape(1, SSD_WIDTH),
      jnp.concatenate([_head_expand(), _head_expand()], axis=0).astype(BF16),
      jnp.asarray(np.tril(np.ones((SSD_CHUNK, SSD_CHUNK), np.float32)), BF16))
    return y, hfin.reshape(b, SSD_HEADS, HEAD_DIM, SSD_STATE)


def _ssd_step_kernel(xbc_ref, c0_ref, c1_ref, c2_ref, dt_ref, h0_ref, cw_ref, cb_ref, dtb_ref, alog_ref,
                     dsk_ref, e_ref, y_ref, hout_ref, xt_sc, dect_sc, bc_sc, yt_sc, xs_sc):
    b = pl.program_id(0)
    nb = xbc_ref.shape[0]
    rows = SSD_HEADS * HEAD_DIM
    gn = SSD_GROUPS * SSD_STATE

    @pl.when(b == 0)
    def _():
        acc = (cb_ref[...] + cw_ref[0:1, :] * c0_ref[...] + cw_ref[1:2, :] * c1_ref[...]
               + cw_ref[2:3, :] * c2_ref[...] + cw_ref[3:4, :] * xbc_ref[...])
        u = _silu(acc)
        xs = u[:, :SSD_WIDTH]
        dt = _softplus(dt_ref[...] + dtb_ref[...])
        dec = jnp.exp(dt * (-jnp.exp(alog_ref[...])))
        e = e_ref[...]
        xdt = xs * _dot32(dt, e)
        dec_e = _dot32(dec, e)
        pad = jnp.zeros((LANE - nb, SSD_WIDTH), F32)
        xt_sc[...] = jnp.concatenate([xdt, pad], axis=0).T
        dect_sc[...] = jnp.concatenate([dec_e, pad], axis=0).T
        bc_sc[...] = u[:, SSD_WIDTH:]
        xs_sc[...] = xs
        yt_sc[...] = jnp.zeros(yt_sc.shape, F32)

    ri = lax.broadcasted_iota(jnp.int32, (LANE, LANE), 0)
    onehot = jnp.where(ri == b, 1.0, 0.0)
    xcol = _dot32(xt_sc[...], onehot)
    dcol = _dot32(dect_sc[...], onehot)
    bc = bc_sc[pl.ds(b, 1), :]
    row = lax.broadcasted_iota(jnp.int32, (rows, SSD_STATE), 0)
    first = row < rows // SSD_GROUPS
    b_full = jnp.where(first, bc[:, 0:SSD_STATE], bc[:, SSD_STATE:gn])
    c_full = jnp.where(first, bc[:, gn:gn + SSD_STATE], bc[:, gn + SSD_STATE:])
    new = dcol * h0_ref[0] + xcol * b_full
    hout_ref[0] = new
    ycol = _dot32(new * c_full, jnp.ones((SSD_STATE, LANE), F32))
    lane = lax.broadcasted_iota(jnp.int32, (rows, LANE), 1)
    yt_sc[...] = jnp.where(lane == b, ycol, yt_sc[...])

    @pl.when(b == nb - 1)
    def _():
        y_ref[...] = yt_sc[...].T[0:nb, :] + xs_sc[...] * dsk_ref[...]


def _ssd_step(xbc, conv_state, dt, h0, conv_w, conv_b, dt_bias, a_log, d_skip):
    nb = xbc.shape[0]
    rows = SSD_HEADS * HEAD_DIM
    full = lambda shape: pl.BlockSpec(shape, lambda i: (0,) * len(shape))
    y, hout = pl.pallas_call(
        _ssd_step_kernel,
        out_shape=(jax.ShapeDtypeStruct((nb, SSD_WIDTH), F32),
                   jax.ShapeDtypeStruct((nb, rows, SSD_STATE), F32)),
        grid=(nb,),
        in_specs=[full((nb, CONV_DIM)), full((nb, CONV_DIM)), full((nb, CONV_DIM)), full((nb, CONV_DIM)),
                  full((nb, LANE)),
                  pl.BlockSpec((1, rows, SSD_STATE), lambda i: (i, 0, 0)),
                  full((SSD_CONV, CONV_DIM)), full((1, CONV_DIM)), full((1, LANE)), full((1, LANE)),
                  full((1, SSD_WIDTH)), full((LANE, SSD_WIDTH))],
        out_specs=(full((nb, SSD_WIDTH)),
                   pl.BlockSpec((1, rows, SSD_STATE), lambda i: (i, 0, 0))),
        scratch_shapes=[pltpu.VMEM((rows, LANE), F32), pltpu.VMEM((rows, LANE), F32),
                        pltpu.VMEM((nb, 2 * SSD_GROUPS * SSD_STATE), F32),
                        pltpu.VMEM((rows, LANE), F32), pltpu.VMEM((nb, SSD_WIDTH), F32)],
        compiler_params=pltpu.CompilerParams(dimension_semantics=("arbitrary",)),
        name="ssd_step",
    )(xbc, conv_state[:, 0], conv_state[:, 1], conv_state[:, 2], dt, h0.reshape(nb, rows, SSD_STATE),
      conv_w, conv_b.reshape(1, CONV_DIM), _pad_lanes(dt_bias), _pad_lanes(a_log),
      jnp.repeat(d_skip.astype(F32), HEAD_DIM).reshape(1, SSD_WIDTH), _head_expand())
    return y, hout.reshape(nb, SSD_HEADS, HEAD_DIM, SSD_STATE)


def _compress_kernel(pt_ref, *refs, transposed, CMP_PAGES):
    segs = PAGE_SIZE // CMP_STRIDE
    if transposed:
        perm_ref, w1_ref, pe_ref, w2_ref, out_ref, sh_sc, pe_sc, xs_sc = refs[CMP_PAGES:]
        for i in range(CMP_PAGES):
            xs_sc[i] = _dot_nt(perm_ref[...], refs[i][0].astype(BF16))

        def token_rows(k, o):
            return jnp.concatenate([xs_sc[i, o * segs:(o + 1) * segs, k * LANE:(k + 1) * LANE]
                                    for i in range(CMP_PAGES)], axis=0)
    else:
        w1_ref, pe_ref, w2_ref, out_ref, sh_sc, pe_sc = refs[2 * CMP_PAGES:]

        def token_rows(k, o):
            return jnp.concatenate([refs[k * CMP_PAGES + i][0, pl.ds(o, segs, stride=CMP_STRIDE), :]
                                    for i in range(CMP_PAGES)], axis=0)
    s = pl.program_id(1)
    rows = CMP_PAGES * segs
    hid2 = NSA_KV_HEADS * CMP_HID

    @pl.when(s == 0)
    def _():
        sh_sc[:, 0:8, :] = jnp.zeros((2, 8, hid2), F32)
        for k in range(2):
            t = jnp.zeros((8, 2 * hid2), F32)
            for o in range(0, CMP_STRIDE, 2):
                pe2 = jnp.concatenate([pe_ref[o, k], pe_ref[o + 1, k]], axis=1)
                t = t + jnp.dot(pe2.astype(BF16), w1_ref[o // 2, k], preferred_element_type=F32)
            pe_sc[k] = jnp.broadcast_to(t[0:1, 0:hid2] + t[1:2, hid2:], (8, hid2))

    for k in range(2):
        acc = jnp.zeros((rows, 2 * hid2), F32)
        for o in range(0, CMP_STRIDE, 2):
            xo = jnp.concatenate([token_rows(k, o), token_rows(k, o + 1)], axis=1)
            acc = acc + jnp.dot(xo.astype(BF16), w1_ref[o // 2, k], preferred_element_type=F32)
        sh_sc[k, 8:8 + rows, :] = acc[:, 0:hid2]
        pre = acc[:, hid2:] + sh_sc[k, 7:7 + rows, :] + pe_sc[k, 0:1, :]
        sh_sc[k, 0:8, :] = sh_sc[k, rows:rows + 8, :]
        out_ref[0, :, k * LANE:(k + 1) * LANE] = jnp.dot(_silu(pre).astype(BF16), w2_ref[k],
                                                         preferred_element_type=F32)


def _compress_weights(cmp_pe, cmp_w1, cmp_w2):
    span = CMP_BLOCK // CMP_STRIDE
    w1s = cmp_w1.astype(F32).reshape(2, span, CMP_STRIDE, HEAD_DIM, CMP_HID)
    z = jnp.zeros((2, span, CMP_STRIDE, HEAD_DIM, CMP_HID), F32)
    top = jnp.concatenate([w1s, z], axis=-1)
    bot = jnp.concatenate([z, w1s], axis=-1)
    bd = jnp.concatenate([top, bot], axis=-2)
    w1 = jnp.transpose(bd, (2, 0, 3, 1, 4)).reshape(CMP_STRIDE, 2, LANE, span * 2 * CMP_HID).astype(BF16)
    w1 = jnp.transpose(w1.reshape(CMP_STRIDE // 2, 2, 2, LANE, span * 2 * CMP_HID), (0, 2, 1, 3, 4))
    w1 = w1.reshape(CMP_STRIDE // 2, 2, 2 * LANE, span * 2 * CMP_HID)
    pe = cmp_pe.astype(F32).reshape(2, span, CMP_STRIDE, HEAD_DIM)
    pe = jnp.transpose(pe, (2, 0, 1, 3))
    pe = jnp.concatenate([pe, pe], axis=-1)
    pe = jnp.pad(pe, ((0, 0), (0, 0), (0, 8 - span), (0, 0)))
    w2 = cmp_w2.astype(F32)
    z2 = jnp.zeros_like(w2)
    w2bd = jnp.concatenate([jnp.concatenate([w2, z2], axis=-1), jnp.concatenate([z2, w2], axis=-1)],
                           axis=-2).astype(BF16)
    return w1, pe, w2bd


def _compress(pages_arr, page_ids, cweights, transposed):
    nb, n_pages = page_ids.shape
    CMP_PAGES = math.gcd(n_pages, CMP_PAGES_MAX)
    steps = n_pages // CMP_PAGES
    segs = PAGE_SIZE // CMP_STRIDE
    rows = CMP_PAGES * segs
    w1, pe, w2bd = cweights
    hid2 = NSA_KV_HEADS * CMP_HID
    page_of = lambda b, s, pt, i: pt[(b * steps + s) * CMP_PAGES + i]
    scratch = [pltpu.VMEM((2, rows + 8, hid2), F32), pltpu.VMEM((2, 8, hid2), F32)]
    if transposed:
        page_specs = [pl.BlockSpec((1, KV_COLS, PAGE_SIZE), lambda b, s, pt, i=i: (page_of(b, s, pt, i), 0, 0))
                      for i in range(CMP_PAGES)]
        scratch.append(pltpu.VMEM((CMP_PAGES, PAGE_SIZE, KV_COLS), F32))
        perm = np.zeros((PAGE_SIZE, PAGE_SIZE), np.float32)
        for o in range(CMP_STRIDE):
            for sg in range(segs):
                perm[o * segs + sg, sg * CMP_STRIDE + o] = 1.0
        extra, extra_specs = [jnp.asarray(perm, BF16)], [pl.BlockSpec((PAGE_SIZE, PAGE_SIZE), lambda b, s, pt: (0, 0))]
    else:
        extra, extra_specs = [], []
        page_specs = [pl.BlockSpec((1, PAGE_SIZE, LANE), lambda b, s, pt, i=i, k=k: (page_of(b, s, pt, i), 0, k))
                      for k in range(2) for i in range(CMP_PAGES)]
    full = lambda shape: pl.BlockSpec(shape, lambda b, s, pt: (0,) * len(shape))
    return pl.pallas_call(
        functools.partial(_compress_kernel, transposed=transposed, CMP_PAGES=CMP_PAGES),
        out_shape=jax.ShapeDtypeStruct((nb, n_pages * segs, KV_COLS), F32),
        grid_spec=pltpu.PrefetchScalarGridSpec(
            num_scalar_prefetch=1,
            grid=(nb, steps),
            in_specs=page_specs + extra_specs + [full(w1.shape), full(pe.shape), full(w2bd.shape)],
            out_specs=pl.BlockSpec((1, rows, KV_COLS), lambda b, s, pt: (b, s, 0)),
            scratch_shapes=scratch),
        compiler_params=pltpu.CompilerParams(dimension_semantics=("arbitrary", "arbitrary"),
                                             vmem_limit_bytes=VMEM_LIMIT),
        name="nsa_compress",
    )(page_ids.reshape(-1), *([pages_arr] * len(page_specs)), *extra, w1, pe, w2bd)


def _overlap_matrix(n_rows, n_blocks, n_cols):
    m = np.arange(n_rows)[:, None]
    j = np.arange(n_cols)[None, :]
    cs = (m - 1) * CMP_STRIDE
    ov = (m >= 1) & (j < n_blocks) & (cs < j * SLC_BLOCK + SLC_BLOCK) & (cs + CMP_BLOCK > j * SLC_BLOCK)
    return jnp.asarray(ov.astype(np.float32))


def _stack_q(q, g):
    return _stack_q_f32(q, g).astype(BF16)


def _stack_q_f32(q, g):
    lane = lax.broadcasted_iota(jnp.int32, (Q_BLOCK, LANE), 1)
    keep = (lane < HALF) if g == 0 else (lane >= HALF)
    parts = []
    for jp in range(NSA_REP // 2):
        j = g * (NSA_REP // 2) + jp
        slab = q[:, j * LANE:(j + 1) * LANE]
        rolled = pltpu.roll(slab, HALF, 1)
        first, second = (slab, rolled) if g == 0 else (rolled, slab)
        parts.append(jnp.where(keep, first, 0.0))
        parts.append(jnp.where(keep, second, 0.0))
    return jnp.concatenate(parts, axis=0)


def _unstack_o(acc, g):
    lane = lax.broadcasted_iota(jnp.int32, (Q_BLOCK, LANE), 1)
    outs = []
    for jp in range(NSA_REP // 2):
        a = acc[(2 * jp) * Q_BLOCK:(2 * jp + 1) * Q_BLOCK]
        b = acc[(2 * jp + 1) * Q_BLOCK:(2 * jp + 2) * Q_BLOCK]
        if g == 0:
            outs.append(jnp.where(lane < HALF, a, pltpu.roll(b, HALF, 1)))
        else:
            outs.append(jnp.where(lane < HALF, pltpu.roll(a, HALF, 1), b))
    return jnp.concatenate(outs, axis=1)


def _tile8(x):
    return jnp.concatenate([x] * NSA_REP, axis=0)


def _rank_rows(imp, n_valid):
    sub = 8
    n_rank = -(-n_valid // sub) * sub
    chunks = [imp[c:c + sub] for c in range(0, n_rank, sub)]
    ranks = [jnp.zeros(ch.shape, F32) for ch in chunks]
    jrow = lax.broadcasted_iota(jnp.int32, chunks[0].shape, 0)
    for k in range(n_valid):
        rk = imp[k:k + 1, :]
        for c, ch in enumerate(chunks):
            if c * sub > k:
                ahead = rk >= ch
            elif c * sub + sub - 1 < k:
                ahead = rk > ch
            else:
                ahead = (rk > ch) | ((rk == ch) & (jrow + c * sub > k))
            ranks[c] = ranks[c] + jnp.where(ahead, 1.0, 0.0)
    rest = jnp.full((imp.shape[0] - n_rank, imp.shape[1]), float(n_valid), F32)
    return jnp.concatenate(ranks + [rest], axis=0)


def _cmp_kernel(q_ref, kv_ref, pb_ref, ov_ref, o_ref, sel_ref, bias_sc, *, n_keys, n_blocks):
    i = pl.program_id(0)
    qi = lax.broadcasted_iota(jnp.int32, (Q_BLOCK, n_keys), 0)
    mi = lax.broadcasted_iota(jnp.int32, (Q_BLOCK, n_keys), 1)
    qpos = Q_BLOCK * i + qi

    @pl.when(pl.program_id(1) == 0)
    def _():
        valid = (mi >= 1) & (CMP_STRIDE * mi + CMP_STRIDE - 1 <= qpos)
        for h in range(NSA_HEADS):
            bias_sc[h] = jnp.where(valid, pltpu.roll(pb_ref[h], (4 * i + 4) % n_keys, 1), NEG)

    rowvalid8 = _tile8(jnp.where(qpos[:, 0:1] >= CMP_BLOCK - 1, 1.0, 0.0))
    jj = lax.broadcasted_iota(jnp.int32, (LANE, LANE), 1)
    ov = ov_ref[...].astype(BF16)
    forced = (jj == 0) | (jj == i) | (jj == i - 1)
    for bb in range(q_ref.shape[0]):
        q = q_ref[bb] * LOG2E
        kv = kv_ref[bb]
        kc = kv[:, 0:LANE].astype(BF16)
        vc = kv[:, LANE:].astype(BF16)
        psum = []
        for g in range(NSA_KV_HEADS):
            s = _dot_nt(_stack_q(q, g), kc) + bias_sc[g * NSA_REP:(g + 1) * NSA_REP].reshape(ROWS, n_keys)
            p = jnp.exp2(s - jnp.max(s, axis=-1, keepdims=True))
            pc = p / jnp.sum(p, axis=-1, keepdims=True) * rowvalid8
            o_ref[bb, :, g * GROUP_W:(g + 1) * GROUP_W] = _unstack_o(
                jnp.dot(pc.astype(BF16), vc, preferred_element_type=F32), g)
            ps = pc[0:Q_BLOCK]
            for r in range(1, NSA_REP):
                ps = ps + pc[r * Q_BLOCK:(r + 1) * Q_BLOCK]
            psum.append(ps)
        ps = jnp.concatenate(psum, axis=0)
        hi = ps.astype(BF16)
        lo = (ps - hi.astype(F32)).astype(BF16)
        imp = (jnp.dot(hi, ov, preferred_element_type=F32)
               + jnp.dot(lo, ov, preferred_element_type=F32))
        imp = jnp.where(forced, 1e6, jnp.where(jj <= i, imp, -1e6))
        imp = jnp.where(jj < n_blocks, imp, -2e6)
        rank = _rank_rows(imp.T, n_blocks)
        sel_ref[bb, 0] = jnp.where(rank < SLC_TOP, 1.0, 0.0).astype(BF16)


def _cmp_prompt(q, kvc_cmp, pattern, ov):
    b, l, _ = q.shape
    n_keys = kvc_cmp.shape[1]
    n_blocks = l // SLC_BLOCK
    nb = 2 if b % 2 == 0 else 1
    return pl.pallas_call(
        functools.partial(_cmp_kernel, n_keys=n_keys, n_blocks=n_blocks),
        out_shape=(jax.ShapeDtypeStruct((b, l, NSA_WIDTH), F32),
                   jax.ShapeDtypeStruct((b, l // Q_BLOCK, LANE, LANE), BF16)),
        grid=(l // Q_BLOCK, b // nb),
        in_specs=[pl.BlockSpec((nb, Q_BLOCK, NSA_WIDTH), lambda i, bi: (bi, i, 0)),
                  pl.BlockSpec((nb, n_keys, KV_COLS), lambda i, bi: (bi, 0, 0)),
                  pl.BlockSpec(pattern.shape, lambda i, bi: (0, 0, 0)),
                  pl.BlockSpec(ov.shape, lambda i, bi: (0, 0))],
        out_specs=(pl.BlockSpec((nb, Q_BLOCK, NSA_WIDTH), lambda i, bi: (bi, i, 0)),
                   pl.BlockSpec((nb, 1, LANE, LANE), lambda i, bi: (bi, i, 0, 0))),
        scratch_shapes=[pltpu.VMEM(pattern.shape, F32)],
        compiler_params=pltpu.CompilerParams(dimension_semantics=("arbitrary", "arbitrary"),
                                             vmem_limit_bytes=VMEM_LIMIT),
        name="nsa_cmp_prompt",
    )(q, kvc_cmp, pattern, ov)


def _flash_update_t(st, vt, m, acc):
    m_new = jnp.maximum(m, jnp.max(st, axis=0, keepdims=True))
    p = jnp.exp2(st - m_new)
    acc = jnp.exp2(m - m_new) * acc + jnp.dot(vt, p.astype(BF16), preferred_element_type=F32)
    return m_new, acc


def _finish_t(acc):
    o = (acc[0:HEAD_DIM] / acc[HEAD_DIM:HEAD_DIM + 1]).T
    return jnp.concatenate([o[r * Q_BLOCK:(r + 1) * Q_BLOCK] for r in range(NSA_REP)], axis=1)


def _slc_kernel(q_ref, sel_ref, kx_ref, vt_ref, tnt_ref, tile_ref, o_ref, far_sc, near_sc, sta_sc, stb_sc, stn_sc):
    i = pl.program_id(1)
    q = q_ref[0] * LOG2E
    n_far = (jnp.maximum(i - 3, 0) + 3) // 4
    sel = sel_ref[0, 0]
    jrow = lax.broadcasted_iota(jnp.int32, (LANE, ROWS), 0)
    a = ((i + 1) // 2) * LANE
    delta = Q_BLOCK * (i + 1) - a
    groups = range(NSA_KV_HEADS)
    for g in groups:
        qs_t = _stack_q_f32(q, g).T.astype(BF16)
        hit = jnp.dot(sel, tile_ref[g], preferred_element_type=F32) > 0.5
        near_sc[g] = jnp.concatenate([qs_t, jnp.where(hit & (jrow != PAD_FLAG_COL), 0.0, NEG).astype(BF16)], axis=0)
        far_sc[g] = jnp.concatenate([qs_t, jnp.where(hit & (jrow < i - 3), 0.0, NEG).astype(BF16)], axis=0)

    def far_scores(t, dst):
        start = pl.multiple_of(SLC_PAD + SLC_NEAR * t, SLC_NEAR)
        kx = kx_ref[0, pl.ds(start, SLC_NEAR), :]
        for g in groups:
            dst[g] = jnp.dot(kx, far_sc[g], preferred_element_type=F32)

    def far_softmax(t, src, carry):
        start = pl.multiple_of(SLC_PAD + SLC_NEAR * t, SLC_NEAR)
        return tuple(_flash_update_t(src[g], vt_ref[0, g, :, pl.ds(start, SLC_NEAR)], *carry[g]) for g in groups)

    def far_pair(u, carry):
        far_scores(2 * u + 1, stb_sc)
        carry = far_softmax(2 * u, sta_sc, carry)
        far_scores(2 * u + 2, sta_sc)
        return far_softmax(2 * u + 1, stb_sc, carry)

    far_scores(0, sta_sc)
    start = pl.multiple_of(a, LANE)
    kx = kx_ref[0, pl.ds(start, SLC_NEARW), :]
    for g in groups:
        stn_sc[g] = jnp.dot(kx, near_sc[g], preferred_element_type=F32)
    init = (jnp.full((1, ROWS), NEG, F32), jnp.zeros((HEAD_DIM + ONES_ROWS, ROWS), F32))
    carry = lax.fori_loop(0, (n_far + 1) // 2, far_pair, (init, init))

    for g in groups:
        st = stn_sc[g] + tnt_ref[delta // Q_BLOCK, g]
        _, acc = _flash_update_t(st, vt_ref[0, g, :, pl.ds(start, SLC_NEARW)], *carry[g])
        o_ref[0, :, g * GROUP_W:(g + 1) * GROUP_W] = _finish_t(acc)


def _slc_prompt(q, sel, kx_pad, vt_pad, tnt, tile):
    b, l, _ = q.shape
    lp = kx_pad.shape[1]
    tq = Q_BLOCK
    n_chains = NSA_KV_HEADS
    return pl.pallas_call(
        _slc_kernel,
        out_shape=jax.ShapeDtypeStruct((b, l, NSA_WIDTH), F32),
        grid=(b, l // tq),
        in_specs=[pl.BlockSpec((1, tq, NSA_WIDTH), lambda bi, i: (bi, i, 0)),
                  pl.BlockSpec((1, 1, LANE, LANE), lambda bi, i: (bi, i, 0, 0)),
                  pl.BlockSpec((1, lp, 2 * LANE), lambda bi, i: (bi, 0, 0)),
                  pl.BlockSpec((1, NSA_KV_HEADS, HEAD_DIM + ONES_ROWS, lp), lambda bi, i: (bi, 0, 0, 0)),
                  pl.BlockSpec(tnt.shape, lambda bi, i: (0, 0, 0, 0)),
                  pl.BlockSpec(tile.shape, lambda bi, i: (0, 0, 0))],
        out_specs=pl.BlockSpec((1, tq, NSA_WIDTH), lambda bi, i: (bi, i, 0)),
        scratch_shapes=[pltpu.VMEM((n_chains, 2 * LANE, ROWS), BF16),
                        pltpu.VMEM((n_chains, 2 * LANE, ROWS), BF16),
                        pltpu.VMEM((n_chains, SLC_NEAR, ROWS), F32),
                        pltpu.VMEM((n_chains, SLC_NEAR, ROWS), F32),
                        pltpu.VMEM((n_chains, SLC_NEARW, ROWS), F32)],
        compiler_params=pltpu.CompilerParams(vmem_limit_bytes=VMEM_LIMIT),
        name="nsa_slc_prompt",
    )(q, sel, kx_pad, vt_pad, tnt, tile)


def _group_tile():
    t = np.zeros((NSA_KV_HEADS, LANE, ROWS), np.float32)
    for g in range(NSA_KV_HEADS):
        for r in range(NSA_REP):
            for qq in range(Q_BLOCK):
                t[g, g * Q_BLOCK + qq, r * Q_BLOCK + qq] = 1.0
    return jnp.asarray(t, BF16)


def _win_kernel(q_ref, kx_ref, vt_ref, tw_ref, o_ref, st_sc):
    i2 = pl.program_id(1)
    q = q_ref[0] * LOG2E
    start = pl.multiple_of(i2 * LANE, LANE)
    kx = kx_ref[0, pl.ds(start, WIN_W), :]
    groups = range(NSA_KV_HEADS)
    row = lax.broadcasted_iota(jnp.int32, (LANE, 2 * ROWS), 0)
    pad_rows = jnp.where(row == 0, NEG, 0.0).astype(BF16)
    for g in groups:
        qs_t = jnp.concatenate([_stack_q_f32(q[h * Q_BLOCK:(h + 1) * Q_BLOCK], g).T for h in range(2)], axis=1)
        qx = jnp.concatenate([qs_t.astype(BF16), pad_rows], axis=0)
        st_sc[g] = jnp.dot(kx, qx, preferred_element_type=F32)
    for g in groups:
        st = st_sc[g] + tw_ref[g]
        p = jnp.exp2(st - jnp.max(st, axis=0, keepdims=True))
        acc = jnp.dot(vt_ref[0, g, :, pl.ds(start, WIN_W)], p.astype(BF16), preferred_element_type=F32)
        for h in range(2):
            o_ref[0, h * Q_BLOCK:(h + 1) * Q_BLOCK, g * GROUP_W:(g + 1) * GROUP_W] = _finish_t(
                acc[:, h * ROWS:(h + 1) * ROWS])


def _win_prompt(q, kx_pad, vt_pad, tw):
    b, l, _ = q.shape
    lp = kx_pad.shape[1]
    tq = 2 * Q_BLOCK
    return pl.pallas_call(
        _win_kernel,
        out_shape=jax.ShapeDtypeStruct((b, l, NSA_WIDTH), F32),
        grid=(b, l // tq),
        in_specs=[pl.BlockSpec((1, tq, NSA_WIDTH), lambda bi, i: (bi, i, 0)),
                  pl.BlockSpec((1, lp, 2 * LANE), lambda bi, i: (bi, 0, 0)),
                  pl.BlockSpec((1, NSA_KV_HEADS, HEAD_DIM + ONES_ROWS, lp), lambda bi, i: (bi, 0, 0, 0)),
                  pl.BlockSpec(tw.shape, lambda bi, i: (0, 0, 0))],
        out_specs=pl.BlockSpec((1, tq, NSA_WIDTH), lambda bi, i: (bi, i, 0)),
        scratch_shapes=[pltpu.VMEM((NSA_KV_HEADS, WIN_W, 2 * ROWS), F32)],
        compiler_params=pltpu.CompilerParams(vmem_limit_bytes=VMEM_LIMIT),
        name="nsa_win_prompt",
    )(q, kx_pad, vt_pad, tw)


def _dense1_kernel(q_ref, k_ref, v_ref, bm_ref, o_ref, ps_ref):
    k = k_ref[0].astype(BF16)
    v = v_ref[0].astype(BF16)
    for g in range(NSA_KV_HEADS):
        rows = slice(g * NSA_REP, (g + 1) * NSA_REP)
        s = _dot_nt(q_ref[0, rows, :], k) + bm_ref[rows, :]
        p = jnp.exp(s - jnp.max(s, axis=-1, keepdims=True))
        pc = p / jnp.sum(p, axis=-1, keepdims=True)
        o_ref[0, rows, :] = jnp.dot(pc.astype(BF16), v, preferred_element_type=F32)
        ps_ref[0, g:g + 1, :] = jnp.sum(pc, axis=0, keepdims=True)


def _dense1(q_pad, kv, biasmask):
    nb, n, _ = kv.shape
    return pl.pallas_call(
        _dense1_kernel,
        out_shape=(jax.ShapeDtypeStruct((nb, NSA_HEADS, LANE), F32),
                   jax.ShapeDtypeStruct((nb, NSA_KV_HEADS, n), F32)),
        grid=(nb,),
        in_specs=[pl.BlockSpec((1, NSA_HEADS, LANE), lambda b: (b, 0, 0)),
                  pl.BlockSpec((1, n, LANE), lambda b: (b, 0, 0)),
                  pl.BlockSpec((1, n, LANE), lambda b: (b, 0, 1)),
                  pl.BlockSpec((NSA_HEADS, n), lambda b: (0, 0))],
        out_specs=(pl.BlockSpec((1, NSA_HEADS, LANE), lambda b: (b, 0, 0)),
                   pl.BlockSpec((1, NSA_KV_HEADS, n), lambda b: (b, 0, 0))),
        name="nsa_dense_sample",
    )(q_pad, kv, kv, biasmask)


def _take_group_half(o_pad):
    nb = o_pad.shape[0]
    o = o_pad.reshape(nb, NSA_KV_HEADS, NSA_REP, NSA_KV_HEADS, HEAD_DIM)
    o = jnp.stack([o[:, g, :, g, :] for g in range(NSA_KV_HEADS)], axis=1)
    return o.reshape(nb, NSA_WIDTH)


def _topk_kernel(ps_ref, ov_ref, idx_ref, imp_sc, *, n_blocks, cur):
    n_rows = ps_ref.shape[0]
    n_cols = ov_ref.shape[1]
    ps = jnp.concatenate([ps_ref[...], jnp.zeros((LANE - n_rows, ps_ref.shape[1]), F32)], axis=0)
    imp = _dot32(ps, ov_ref[...])
    jj = lax.broadcasted_iota(jnp.int32, (LANE, n_cols), 1)
    forced = (jj == 0) | (jj == cur) | (jj == cur - 1)
    imp = jnp.where(forced, 1e6, jnp.where(jj <= cur, imp, -1e6))
    imp = jnp.where(jj < n_blocks, imp, -2e6)
    imp_t = imp.T
    imp_sc[...] = imp_t
    jrow = lax.broadcasted_iota(jnp.int32, (n_cols, LANE), 0)

    def body(k, rank):
        rk = imp_sc[pl.ds(k, 1), :]
        ahead = (rk > imp_t) | ((rk == imp_t) & (jrow > k))
        return rank + jnp.where(ahead, 1.0, 0.0)

    rank = lax.fori_loop(0, n_blocks, body, jnp.zeros((n_cols, LANE), F32))
    jf = jrow.astype(F32)
    rows = [jnp.sum(jnp.where(rank == float(r), jf, 0.0), axis=0, keepdims=True) for r in range(SLC_TOP)]
    idx_ref[...] = jnp.concatenate(rows, axis=0).astype(jnp.int32)


def _topk_sample(psum, ov, n_blocks, cur):
    n_rows, n_keys = psum.shape
    n_cols = ov.shape[1]
    return pl.pallas_call(
        functools.partial(_topk_kernel, n_blocks=n_blocks, cur=cur),
        out_shape=jax.ShapeDtypeStruct((SLC_TOP, LANE), jnp.int32),
        grid=(1,),
        in_specs=[pl.BlockSpec((n_rows, n_keys), lambda i: (0, 0)),
                  pl.BlockSpec(ov.shape, lambda i: (0, 0))],
        out_specs=pl.BlockSpec((SLC_TOP, LANE), lambda i: (0, 0)),
        scratch_shapes=[pltpu.VMEM((n_cols, LANE), F32)],
        name="nsa_topk_sample",
    )(psum, ov)


def _slc1_kernel(phys_ref, jsel_ref, q_ref, *refs, cur, past):
    pages = refs[:SLC_TOP]
    new_ref, bb_ref, o_ref = refs[SLC_TOP:]
    b = pl.program_id(0)
    g = pl.program_id(1)
    goff = pl.multiple_of(g * HEAD_DIM, HEAD_DIM)
    q = q_ref[0]
    lane = lax.broadcasted_iota(jnp.int32, (NSA_REP, PAGE_SIZE), 1)
    first = lax.broadcasted_iota(jnp.int32, (HEAD_DIM, PAGE_SIZE), 1) == 0
    new_k = jnp.where(first, new_ref[0, pl.ds(goff, HEAD_DIM), :], 0.0)
    new_v = jnp.where(first, new_ref[0, pl.ds(LANE + goff, HEAD_DIM), :], 0.0)
    scores, values = [], []
    for n in range(SLC_TOP):
        j = jsel_ref[(b * NSA_KV_HEADS + g) * SLC_TOP + n]
        kt = jnp.where(j == cur, new_k, pages[n][0, pl.ds(goff, HEAD_DIM), :])
        vt = jnp.where(j == cur, new_v, pages[n][0, pl.ds(LANE + goff, HEAD_DIM), :])
        s = jnp.dot(q, kt.astype(BF16), preferred_element_type=F32)
        ok = (lane // SLC_BLOCK == j % 2) & ((j // 2) * PAGE_SIZE + lane <= past)
        bias = bb_ref[jnp.clip(j - (cur - 3), 0, 3), pl.ds(pl.multiple_of(g * NSA_REP, NSA_REP), NSA_REP), :]
        scores.append(jnp.where(ok, s + bias, NEG))
        values.append(vt.astype(BF16))
    s_all = jnp.concatenate(scores, axis=1)
    p = jnp.exp(s_all - jnp.max(s_all, axis=-1, keepdims=True))
    acc = jnp.zeros((NSA_REP, HEAD_DIM), F32)
    for n in range(SLC_TOP):
        acc = acc + _dot_nt(p[:, n * PAGE_SIZE:(n + 1) * PAGE_SIZE].astype(BF16), values[n])
    o_ref[0] = acc / jnp.sum(p, axis=-1, keepdims=True)


def _slc_sample(q, cache_pages_t, phys, jsel, new_cols, biasblk, cur, past):
    nb = q.shape[0]
    idx = lambda b, g, n: (b * NSA_KV_HEADS + g) * SLC_TOP + n
    page_specs = [pl.BlockSpec((1, KV_COLS, PAGE_SIZE), lambda b, g, ph, js, n=n: (ph[idx(b, g, n)], 0, 0))
                  for n in range(SLC_TOP)]
    return pl.pallas_call(
        functools.partial(_slc1_kernel, cur=cur, past=past),
        out_shape=jax.ShapeDtypeStruct((nb, NSA_HEADS, HEAD_DIM), F32),
        grid_spec=pltpu.PrefetchScalarGridSpec(
            num_scalar_prefetch=2,
            grid=(nb, NSA_KV_HEADS),
            in_specs=[pl.BlockSpec((1, NSA_REP, HEAD_DIM), lambda b, g, ph, js: (b, g, 0))] + page_specs
            + [pl.BlockSpec((1, KV_COLS, 1), lambda b, g, ph, js: (b, 0, 0)),
               pl.BlockSpec(biasblk.shape, lambda b, g, ph, js: (0, 0, 0))],
            out_specs=pl.BlockSpec((1, NSA_REP, HEAD_DIM), lambda b, g, ph, js: (b, g, 0))),
        name="nsa_slc_sample",
    )(phys, jsel, q, *([cache_pages_t] * SLC_TOP), new_cols, biasblk)


def _out_kernel(x_ref, gate_ref, yssd_ref, zs_ref, oc_ref, os_ref, ow_ref, gl_ref, za_ref,
                nw1_ref, nw2_ref, w_ref, eg_ref, fg_ref, o_ref, *, per_row, final):
    gates = _sigmoid(gl_ref[...])
    g2 = jnp.concatenate(_split_bf16(gates, 2), axis=1)
    expand = lambda br: jnp.dot(g2, eg_ref[br], preferred_element_type=F32)
    y_nsa = expand(0) * oc_ref[...] + expand(1) * os_ref[...] + expand(2) * ow_ref[...]

    def gated_norm(y, z, w):
        u = y * _silu(z)
        half = u.shape[1] // 2
        parts = []
        for g in range(2):
            ug = u[:, g * half:(g + 1) * half]
            parts.append(ug * lax.rsqrt(jnp.mean(ug * ug, axis=-1, keepdims=True) + NORM_EPS))
        return (jnp.concatenate(parts, axis=1) * w).astype(BF16)

    m1 = gated_norm(yssd_ref[...], zs_ref[...], nw1_ref[...])
    m2 = gated_norm(y_nsa, za_ref[...], nw2_ref[...])
    proj = (jnp.dot(m1, w_ref[0:SSD_WIDTH, :], preferred_element_type=F32)
            + jnp.dot(m2, w_ref[SSD_WIDTH:, :], preferred_element_type=F32))
    gate = gate_ref[...] if per_row else gate_ref[0]
    out = x_ref[...] + gate * proj
    if final:
        out = out * lax.rsqrt(jnp.mean(out * out, axis=-1, keepdims=True) + NORM_EPS) * fg_ref[...]
    o_ref[...] = out


def _gate_expand():
    e = np.zeros((3, LANE, NSA_WIDTH), np.float32)
    for br in range(3):
        for h in range(NSA_HEADS):
            e[br, br * NSA_HEADS + h, h * HEAD_DIM:(h + 1) * HEAD_DIM] = 1.0
    return jnp.asarray(np.concatenate([e, e], axis=1), BF16)


def _layer_out(x2d, gate, y_ssd, z_s, o_cmp, o_slc, o_win, gl, z_a, nw1, nw2, w_out_bf, final_g,
               rows_per_batch, final):
    m = x2d.shape[0]
    per_row = rows_per_batch == 1
    tm = m if per_row else 256
    if per_row:
        gate_spec = pl.BlockSpec((tm, D_MODEL), lambda i: (0, 0))
        gt = gate
    else:
        gate_spec = pl.BlockSpec((1, 1, D_MODEL), lambda i: ((i * tm) // rows_per_batch, 0, 0))
        gt = gate[:, None, :]
    row = lambda w: pl.BlockSpec((tm, w), lambda i: (i, 0))
    full = lambda shape: pl.BlockSpec(shape, lambda i: (0,) * len(shape))
    eg = _gate_expand()
    return pl.pallas_call(
        functools.partial(_out_kernel, per_row=per_row, final=final),
        out_shape=jax.ShapeDtypeStruct((m, D_MODEL), F32),
        grid=(m // tm,),
        in_specs=[row(D_MODEL), gate_spec, row(SSD_WIDTH), row(SSD_WIDTH), row(NSA_WIDTH), row(NSA_WIDTH),
                  row(NSA_WIDTH), row(LANE), row(NSA_WIDTH), full((1, SSD_WIDTH)), full((1, NSA_WIDTH)),
                  full(w_out_bf.shape), full(eg.shape), full((1, D_MODEL))],
        out_specs=row(D_MODEL),
        compiler_params=pltpu.CompilerParams(vmem_limit_bytes=VMEM_LIMIT),
        name="layer_out",
    )(x2d, gt, y_ssd, z_s, o_cmp, o_slc, o_win, gl, z_a, nw1.reshape(1, SSD_WIDTH), nw2.reshape(1, NSA_WIDTH),
      w_out_bf, eg, final_g.reshape(1, D_MODEL))


def _pad_in_weights(w_in):
    cols = []
    off = 0
    for size, width in zip(_SEG_SIZES, _SEG_PAD):
        seg = w_in[:, off:off + size]
        cols.append(jnp.pad(seg, ((0, 0), (0, width - size))))
        off += size
    return jnp.concatenate(cols, axis=1).astype(BF16)


def _values_t(v_pad):
    b, rows, _ = v_pad.shape
    vt = jnp.transpose(v_pad.reshape(b, rows, NSA_KV_HEADS, HEAD_DIM), (0, 2, 3, 1))
    return jnp.concatenate([vt, jnp.ones((b, NSA_KV_HEADS, ONES_ROWS, rows), BF16)], axis=2)


def _front_pad_bf16(kv, rows):
    return jnp.pad(kv, ((0, 0), (rows, 0), (0, 0))).astype(BF16)


def kernel(x_prompt, x_sample, cache_cmp_kv, cache_slc_kv, state_win_kv, state_conv, state_ssm, page_table,
           c_prompt, c_sample, norm_g, ada_w, ada_b, w_in, conv_w, conv_b, dt_bias, a_log, d_skip,
           ssd_norm_w, cmp_pe, cmp_w1, cmp_w2, nsa_norm_w, w_out, rel_bias, final_norm_g):
    nbp, lp, _ = x_prompt.shape
    nbs = x_sample.shape[0]
    depth = w_in.shape[0]
    n_pool = cache_cmp_kv.shape[1]
    n_pages = page_table.shape[1]
    past = n_pages * PAGE_SIZE
    w_buf = state_win_kv.shape[2]
    kv_row = (2, NSA_KV_HEADS, HEAD_DIM)

    n_ck = lp // CMP_STRIDE
    n_sb = lp // SLC_BLOCK
    qi = np.arange(Q_BLOCK)[:, None]
    c0 = n_ck - 4
    pattern = _bias_of_dist(rel_bias, qi - CMP_STRIDE * (np.arange(n_ck)[None, :] - c0) - (CMP_STRIDE - 1)) * LOG2E
    far = rel_bias.astype(F32)[REL_BUCKETS - 1][:, None, None]
    par = np.arange(2)[:, None, None]
    qq = np.arange(Q_BLOCK)[None, None, :]
    dist_n = qq - np.arange(SLC_NEARW)[None, :, None] + Q_BLOCK * par + (SLC_NEAR - Q_BLOCK)
    c_max = Q_BLOCK + WIN_PAD
    master = _toeplitz_bias(rel_bias, c_max - (SLC_NEAR - Q_BLOCK) + SLC_NEARW, Q_BLOCK, c_max)
    window = lambda c0, rows: master[:, c_max - c0:c_max - c0 + rows]
    tnt = jnp.stack([window(Q_BLOCK * p + SLC_NEAR - Q_BLOCK, SLC_NEARW) for p in range(2)], axis=1)
    tnt = jnp.where(jnp.asarray((dist_n >= 0) & (dist_n <= qq + SLC_NEAR - Q_BLOCK))[None],
                    (tnt - far[..., None]) * LOG2E, NEG)
    tnt = jnp.transpose(tnt.reshape(NSA_KV_HEADS, NSA_REP, 2, SLC_NEARW, Q_BLOCK), (2, 0, 3, 1, 4))
    tnt = tnt.reshape(2, NSA_KV_HEADS, SLC_NEARW, ROWS)
    tile = _group_tile()
    key_row = np.arange(SLC_PAD + lp + SLC_BACK) - SLC_PAD
    assert (key_row[-1] // SLC_BLOCK) < PAD_FLAG_COL
    cols = np.arange(LANE)[None, :]
    blk_onehot = jnp.asarray(((key_row[:, None] >= 0) & (key_row[:, None] // SLC_BLOCK == cols))
                             | ((key_row[:, None] < 0) & (cols == PAD_FLAG_COL)), BF16)
    dist_w = qq - np.arange(WIN_W)[None, :, None] + Q_BLOCK * par + WIN_PAD
    tw = jnp.stack([window(Q_BLOCK * p + WIN_PAD, WIN_W) for p in range(2)], axis=1)
    tw = jnp.where(jnp.asarray((dist_w >= 0) & (dist_w < WINDOW))[None], tw * LOG2E, NEG)
    tw = jnp.transpose(tw.reshape(NSA_KV_HEADS, NSA_REP, 2, WIN_W, Q_BLOCK), (2, 0, 3, 1, 4))
    tw = tw.reshape(2, NSA_KV_HEADS, WIN_W, ROWS)
    tw = jnp.concatenate([tw[0], tw[1]], axis=-1)
    win_flag = jnp.asarray((np.arange(WIN_PAD + lp)[:, None] < WIN_PAD) & (np.arange(LANE)[None, :] == 0), BF16)
    ov_p = _overlap_matrix(n_ck, n_sb, LANE)

    n_cs = past // CMP_STRIDE
    cur = past // SLC_BLOCK
    n_blk_s = cur + 1
    n_cols_s = -(-n_blk_s // LANE) * LANE
    ov_s = _overlap_matrix(n_cs, n_blk_s, n_cols_s)
    m_s = np.arange(n_cs)
    bm_c = jnp.where(jnp.asarray(m_s >= 1)[None, :],
                     _bias_of_dist(rel_bias, past - (CMP_STRIDE * m_s + CMP_STRIDE - 1)), NEG)
    n_w = -(-(w_buf + 1) // LANE) * LANE
    iw = np.arange(n_w)
    dw = w_buf - iw
    ok_w = (iw <= w_buf) & (dw >= 0) & (dw < WINDOW) & (past - w_buf + iw >= 0)
    bm_w = jnp.where(jnp.asarray(ok_w)[None, :], _bias_of_dist(rel_bias, dw), NEG)
    jb = (cur - 3 + np.arange(4))[:, None]
    biasblk = _bias_of_dist(rel_bias, past - SLC_BLOCK * jb - np.arange(SLC_BLOCK)[None, :])
    biasblk = jnp.transpose(biasblk, (1, 0, 2))
    biasblk = jnp.concatenate([biasblk, biasblk], axis=-1)

    pages_t = lambda c: jnp.transpose(c, (0, 1, 3, 4, 5, 2)).reshape(depth * n_pool, KV_COLS, PAGE_SIZE)
    cmp_pages_t = pages_t(cache_cmp_kv)
    slc_pages_t = pages_t(cache_slc_kv)
    prompt_pages = jnp.arange(nbp * (lp // PAGE_SIZE), dtype=jnp.int32).reshape(nbp, lp // PAGE_SIZE)

    c_all = jnp.concatenate([c_prompt, c_sample], axis=0)
    xp = x_prompt.reshape(nbp * lp, D_MODEL)
    xs = x_sample.reshape(nbs, D_MODEL)
    outs = {k: [] for k in ("pc", "ps", "pw", "pconv", "pssm", "sc", "ss", "sw", "sconv", "sssm")}

    for l in range(depth):
        final = l == depth - 1
        w_pad = _pad_in_weights(w_in[l])
        w_out_bf = w_out[l].astype(BF16)
        cweights = _compress_weights(cmp_pe[l], cmp_w1[l], cmp_w2[l])
        mod = _modulation(c_all, ada_w[l], ada_b[l])
        shift, scale, gate = mod[:, :D_MODEL], mod[:, D_MODEL:2 * D_MODEL], mod[:, 2 * D_MODEL:]

        z_s, xbc, dt, q, kvc, kvs, kvw, gl, z_a = _in_projection(xp, norm_g[l], scale[:nbp], shift[:nbp], w_pad, lp)
        xbc3 = xbc.reshape(nbp, lp, CONV_DIM)
        y_ssd, h_fin = _ssd_prompt(xbc3, dt.reshape(nbp, lp, LANE), conv_w[l], conv_b[l], dt_bias[l], a_log[l],
                                   d_skip[l])
        q3 = q.reshape(nbp, lp, NSA_WIDTH)
        kvc3, kvs3, kvw3 = (t.reshape(nbp, lp, KV_COLS) for t in (kvc, kvs, kvw))
        kc = _compress(kvc.reshape(nbp * (lp // PAGE_SIZE), PAGE_SIZE, KV_COLS), prompt_pages, cweights, False)
        o_cmp, sel = _cmp_prompt(q3, kc, pattern, ov_p)
        ks_pad = jnp.pad(kvs3, ((0, 0), (SLC_PAD, SLC_BACK), (0, 0))).astype(BF16)
        kx_pad = jnp.concatenate([ks_pad[:, :, :LANE], jnp.broadcast_to(blk_onehot, (nbp,) + blk_onehot.shape)],
                                 axis=-1)
        o_slc = _slc_prompt(q3, sel, kx_pad, _values_t(ks_pad[:, :, LANE:]), tnt, tile)
        kw_pad = _front_pad_bf16(kvw3, WIN_PAD)
        kwx_pad = jnp.concatenate([kw_pad[:, :, :LANE], jnp.broadcast_to(win_flag, (nbp,) + win_flag.shape)], axis=-1)
        o_win = _win_prompt(q3, kwx_pad, _values_t(kw_pad[:, :, LANE:]), tw)
        xp = _layer_out(xp, gate[:nbp], y_ssd.reshape(nbp * lp, SSD_WIDTH), z_s,
                        o_cmp.reshape(nbp * lp, NSA_WIDTH), o_slc.reshape(nbp * lp, NSA_WIDTH),
                        o_win.reshape(nbp * lp, NSA_WIDTH), gl, z_a, ssd_norm_w[l], nsa_norm_w[l], w_out_bf,
                        final_norm_g, lp, final)
        outs["pc"].append(kvc3.reshape((nbp, lp) + kv_row))
        outs["ps"].append(kvs3.reshape((nbp, lp) + kv_row))
        outs["pw"].append(kvw3[:, -min(WINDOW, lp):].reshape((nbp, min(WINDOW, lp)) + kv_row))
        outs["pconv"].append(xbc3[:, -(SSD_CONV - 1):])
        outs["pssm"].append(h_fin)

        z_s2, xbc2, dt2, q2, kvc2, kvs2, kvw2, gl2, z_a2 = _in_projection(
            xs, norm_g[l], scale[nbp:], shift[nbp:], w_pad, 1)
        y_ssd2, h2 = _ssd_step(xbc2, state_conv[l], dt2, state_ssm[l], conv_w[l], conv_b[l], dt_bias[l],
                               a_log[l], d_skip[l])
        qh = q2.reshape(nbs, NSA_KV_HEADS, NSA_REP, HEAD_DIM)
        zq = jnp.zeros((nbs, NSA_REP, HEAD_DIM), F32)
        q_pad = jnp.stack([jnp.concatenate([qh[:, 0], zq], axis=-1), jnp.concatenate([zq, qh[:, 1]], axis=-1)],
                          axis=1).reshape(nbs, NSA_HEADS, LANE).astype(BF16)
        kc2 = _compress(cmp_pages_t, page_table + l * n_pool, cweights, True)
        oc2, psum = _dense1(q_pad, kc2, bm_c)
        sel_idx = _topk_sample(psum.reshape(nbs * NSA_KV_HEADS, n_cs), ov_s, n_blk_s, cur)
        jsel = sel_idx[:, :nbs * NSA_KV_HEADS].T.reshape(nbs, NSA_KV_HEADS, SLC_TOP)
        jc = jnp.minimum(jsel, cur - 1)
        page = jnp.take_along_axis(page_table, (jc // 2).reshape(nbs, -1), axis=1).reshape(jsel.shape)
        phys = jnp.where(jsel < cur, page + l * n_pool, 0).astype(jnp.int32)
        os2 = _slc_sample(q2.reshape(nbs, NSA_HEADS, HEAD_DIM).astype(BF16), slc_pages_t, phys.reshape(-1),
                          jsel.reshape(-1), kvs2.reshape(nbs, KV_COLS, 1), biasblk, cur, past)
        kw_full = jnp.concatenate([state_win_kv[l].reshape(nbs, w_buf, KV_COLS), kvw2[:, None, :]], axis=1)
        kw_in = jnp.pad(kw_full, ((0, 0), (0, n_w - (w_buf + 1)), (0, 0)))
        ow2, _ = _dense1(q_pad, kw_in, bm_w)
        xs = _layer_out(xs, gate[nbp:], y_ssd2, z_s2, _take_group_half(oc2), os2.reshape(nbs, NSA_WIDTH),
                        _take_group_half(ow2), gl2, z_a2, ssd_norm_w[l], nsa_norm_w[l], w_out_bf,
                        final_norm_g, 1, final)
        outs["sc"].append(kvc2.reshape((nbs, 1) + kv_row))
        outs["ss"].append(kvs2.reshape((nbs, 1) + kv_row))
        outs["sw"].append(kw_full[:, -w_buf:].reshape((nbs, w_buf) + kv_row))
        outs["sconv"].append(jnp.concatenate([state_conv[l][:, 1:], xbc2[:, None, :]], axis=1))
        outs["sssm"].append(h2)

    st = lambda k: jnp.stack(outs[k])
    return (xp.reshape(nbp, lp, D_MODEL), xs.reshape(nbs, 1, D_MODEL),
            st("pc"), st("ps"), st("pw"), st("pconv"), st("pssm"),
            st("sc"), st("ss"), st("sw"), st("sconv"), st("sssm"))
```

```python
import functools
import math

import numpy as np
import jax
import jax.numpy as jnp
from jax import lax
from jax.experimental import pallas as pl
from jax.experimental.pallas import tpu as pltpu

F32 = jnp.float32
BF16 = jnp.bfloat16
HIGHEST = lax.Precision.HIGHEST

D_MODEL = 1024
HEAD_DIM = 64
SSD_WIDTH = 1024
SSD_HEADS = 16
SSD_GROUPS = 2
SSD_STATE = 128
SSD_CONV = 4
SSD_CHUNK = 256
CONV_DIM = SSD_WIDTH + 2 * SSD_GROUPS * SSD_STATE
NSA_WIDTH = 1024
NSA_HEADS = 16
NSA_KV_HEADS = 2
NSA_REP = NSA_HEADS // NSA_KV_HEADS
CMP_BLOCK = 32
CMP_STRIDE = 16
CMP_HID = 2 * HEAD_DIM
SLC_BLOCK = 64
SLC_TOP = 16
WINDOW = 512
Q_BLOCK = 64
REL_BUCKETS = 32
REL_MAX_DIST = 128
NORM_EPS = 1e-6
KV_COLS = 2 * NSA_KV_HEADS * HEAD_DIM
PAGE_SIZE = 128
NEG = -1e30
LOG2E = 1.4426950408889634

LANE = 128
HALF = LANE // 2
GROUP_W = NSA_REP * HEAD_DIM
ROWS = NSA_REP * Q_BLOCK
SLC_NEAR = 4 * SLC_BLOCK
SLC_PAD = SLC_NEAR
SLC_NEARW = SLC_NEAR + LANE
ONES_ROWS = 8
PAD_FLAG_COL = LANE - 1
SLC_BACK = 2 * SLC_NEAR
WIN_W = WINDOW + 2 * Q_BLOCK
WIN_PAD = WINDOW
MAX_PAGES_PER_STEP = 32
PROJ_ROWS = 256
MOD_COLS = 512
VMEM_LIMIT = 48 * 1024 * 1024

_SEG_NAMES = ("z_s", "xbc", "dt", "q", "kvc", "kvs", "kvw", "gl", "z_a")
_SEG_SIZES = (SSD_WIDTH, CONV_DIM, SSD_HEADS, NSA_WIDTH, KV_COLS, KV_COLS, KV_COLS, 3 * NSA_HEADS, NSA_WIDTH)
_SEG_PAD = tuple(-(-s // LANE) * LANE for s in _SEG_SIZES)
_SEG_OFF = tuple(int(o) for o in np.cumsum((0,) + _SEG_PAD[:-1]))
IN_PAD = int(sum(_SEG_PAD))


def _sigmoid(x):
    return 1.0 / (1.0 + jnp.exp(-x))


def _silu(x):
    return x * _sigmoid(x)


def _dot32(a, b):
    return jnp.dot(a, b, precision=HIGHEST, preferred_element_type=F32)


def _split_bf16(x, terms):
    parts = []
    for _ in range(terms):
        p = x.astype(BF16)
        parts.append(p)
        x = x - p.astype(F32)
    return parts


def _expand2(x, sel2_bf16):
    hi, lo = _split_bf16(x, 2)
    return jnp.dot(jnp.concatenate([hi, lo], axis=1), sel2_bf16, preferred_element_type=F32)


def _sel_dot(sel_bf16, x, terms):
    return sum(jnp.dot(sel_bf16, p, preferred_element_type=F32) for p in _split_bf16(x, terms))


def _dot_nt(a, b):
    return lax.dot_general(a, b, (((1,), (1,)), ((), ())), preferred_element_type=F32)


def _bucket_table():
    n = np.arange(REL_MAX_DIST + 1)
    max_exact = REL_BUCKETS // 2
    nf = np.maximum(n, 1).astype(np.float32)
    large = max_exact + (np.log(nf / np.float32(max_exact)) / np.float32(math.log(REL_MAX_DIST / max_exact))
                         * np.float32(REL_BUCKETS - max_exact)).astype(np.int32)
    large = np.minimum(large, REL_BUCKETS - 1)
    return np.where(n < max_exact, n, large).astype(np.int32)


_BUCKETS = _bucket_table()


def _bias_of_dist(rel_bias, dist):
    idx = _BUCKETS[np.clip(dist, 0, REL_MAX_DIST)]
    out = jnp.take(rel_bias.astype(F32), jnp.asarray(idx.reshape(-1)), axis=0)
    return out.T.reshape((NSA_HEADS,) + dist.shape)


def _toeplitz_bias(rel_bias, rows, cols, c0):
    n = rows + cols
    d = np.arange(n) + c0 - (rows - 1)
    v = jnp.take(rel_bias.astype(F32), jnp.asarray(_BUCKETS[np.clip(d, 0, REL_MAX_DIST)]), axis=0).T
    flat = jnp.tile(v, (1, rows))[:, :rows * (n - 1)]
    return flat.reshape(NSA_HEADS, rows, n - 1)[:, :, rows - 1:rows - 1 + cols]


def _mod_kernel(c_ref, w_ref, b_ref, o_ref):
    o_ref[...] = _dot32(_silu(c_ref[...]), w_ref[...]) + b_ref[...]


def _modulation(c, w, b):
    m, d = c.shape
    n = w.shape[1]
    tn = MOD_COLS
    return pl.pallas_call(
        _mod_kernel,
        out_shape=jax.ShapeDtypeStruct((m, n), F32),
        grid=(n // tn,),
        in_specs=[pl.BlockSpec((m, d), lambda j: (0, 0)),
                  pl.BlockSpec((d, tn), lambda j: (0, j)),
                  pl.BlockSpec((1, tn), lambda j: (0, j))],
        out_specs=pl.BlockSpec((m, tn), lambda j: (0, j)),
        name="adaln_mod",
    )(c, w, b.reshape(1, n))


def _inproj_kernel(x_ref, g_ref, sc_ref, sh_ref, w_ref, *out_refs, per_row):
    x = x_ref[...]
    xn = x * lax.rsqrt(jnp.mean(x * x, axis=-1, keepdims=True) + NORM_EPS)
    sc = sc_ref[...] if per_row else sc_ref[0]
    sh = sh_ref[...] if per_row else sh_ref[0]
    h = ((xn * g_ref[...]) * (1.0 + sc) + sh).astype(BF16)
    for name, off, width, ref in zip(_SEG_NAMES, _SEG_OFF, _SEG_PAD, out_refs):
        r = jnp.dot(h, w_ref[:, off:off + width], preferred_element_type=F32)
        if name == "q":
            r = r * (HEAD_DIM ** -0.5)
        ref[...] = r


def _in_projection(x2d, g, scale, shift, w_pad, rows_per_batch):
    m = x2d.shape[0]
    per_row = rows_per_batch == 1
    tm = m if per_row else PROJ_ROWS
    if per_row:
        mod_spec = pl.BlockSpec((tm, D_MODEL), lambda i: (0, 0))
        sc, sh = scale, shift
    else:
        mod_spec = pl.BlockSpec((1, 1, D_MODEL), lambda i: ((i * tm) // rows_per_batch, 0, 0))
        sc, sh = scale[:, None, :], shift[:, None, :]
    outs = tuple(jax.ShapeDtypeStruct((m, w), F32) for w in _SEG_PAD)
    return pl.pallas_call(
        functools.partial(_inproj_kernel, per_row=per_row),
        out_shape=outs,
        grid=(m // tm,),
        in_specs=[pl.BlockSpec((tm, D_MODEL), lambda i: (i, 0)),
                  pl.BlockSpec((1, D_MODEL), lambda i: (0, 0)),
                  mod_spec, mod_spec,
                  pl.BlockSpec((D_MODEL, IN_PAD), lambda i: (0, 0))],
        out_specs=tuple(pl.BlockSpec((tm, w), lambda i: (i, 0)) for w in _SEG_PAD),
        compiler_params=pltpu.CompilerParams(vmem_limit_bytes=VMEM_LIMIT),
        name="in_projection",
    )(x2d, g.reshape(1, D_MODEL), sc, sh, w_pad)


def _softplus(x):
    return jnp.maximum(x, 0.0) + jnp.log(1.0 + jnp.exp(-jnp.abs(x)))


def _ssd_kernel(xbc_ref, dt_ref, cw_ref, cb_ref, dtb_ref, alog_ref, dsk_ref, e_ref, tril_ref,
                y_ref, hfin_ref, xe_sc, st_sc):
    c = pl.program_id(1)
    q = SSD_CHUNK
    n_pairs = SSD_HEADS // 2

    @pl.when(c == 0)
    def _():
        xe_sc[0:8, :] = jnp.zeros((8, CONV_DIM), F32)
        st_sc[...] = jnp.zeros(st_sc.shape, F32)

    xe_sc[8:8 + q, :] = xbc_ref[0]
    acc = cb_ref[...] + cw_ref[0:1, :] * xe_sc[5:5 + q, :]
    for k in range(1, SSD_CONV):
        acc = acc + cw_ref[k:k + 1, :] * xe_sc[5 + k:5 + k + q, :]
    u = _silu(acc)
    xe_sc[0:8, :] = xe_sc[q:q + 8, :]

    xs = u[:, :SSD_WIDTH]
    gn = SSD_GROUPS * SSD_STATE
    bm = u[:, SSD_WIDTH:SSD_WIDTH + gn]
    cm = u[:, SSD_WIDTH + gn:]

    dt = _softplus(dt_ref[0] + dtb_ref[...])
    a = dt * (-jnp.exp(alog_ref[...]))
    cs = _sel_dot(tril_ref[...], a, 3)
    cs_t = cs.T
    cs_last = cs[q - 1:q, :]
    e = e_ref[...]
    dt_e = _expand2(dt, e)
    w_e = _expand2(dt * jnp.exp(cs_last - cs), e)
    ecs_e = _expand2(jnp.exp(cs), e)
    tot_e = _expand2(jnp.broadcast_to(jnp.exp(cs_last), (8, LANE)), e)[0:1, :]
    xdt = (xs * dt_e).astype(BF16)
    xw = (xs * w_e).astype(BF16)

    li = lax.broadcasted_iota(jnp.int32, (q, q), 0)
    si = lax.broadcasted_iota(jnp.int32, (q, q), 1)
    tri = li >= si
    lane = lax.broadcasted_iota(jnp.int32, (q, LANE), 1)

    for g in range(SSD_GROUPS):
        cg = cm[:, g * SSD_STATE:(g + 1) * SSD_STATE].astype(BF16)
        bg = bm[:, g * SSD_STATE:(g + 1) * SSD_STATE]
        cb = _dot_nt(cg, bg.astype(BF16))
        bg_t = bg.T.astype(BF16)
        for jp in range(n_pairs // SSD_GROUPS):
            j = g * (n_pairs // SSD_GROUPS) + jp
            sl = slice(j * LANE, (j + 1) * LANE)
            xdt_p = xdt[:, sl]
            ys = []
            for hh in (2 * j, 2 * j + 1):
                diff = cs[:, hh:hh + 1] - cs_t[hh:hh + 1, :]
                lmat = jnp.exp(jnp.where(tri, diff, NEG))
                ys.append(jnp.dot((cb * lmat).astype(BF16), xdt_p, preferred_element_type=F32))
            y_diag = jnp.where(lane < HALF, ys[0], ys[1])
            st = st_sc[j]
            y_off = jnp.dot(cg, st.astype(BF16), preferred_element_type=F32) * ecs_e[:, sl]
            y_ref[0, :, sl] = y_diag + y_off + xs[:, sl] * dsk_ref[:, sl]
            new = jnp.dot(bg_t, xw[:, sl], preferred_element_type=F32)
            st_sc[j] = st * tot_e[:, sl] + new

    @pl.when(c == pl.num_programs(1) - 1)
    def _():
        for j in range(n_pairs):
            hfin_ref[0, j * LANE:(j + 1) * LANE, :] = st_sc[j].T


def _head_expand():
    e = np.zeros((LANE, SSD_WIDTH), np.float32)
    for h in range(SSD_HEADS):
        e[h, h * HEAD_DIM:(h + 1) * HEAD_DIM] = 1.0
    return jnp.asarray(e)


def _pad_lanes(v):
    return jnp.pad(v.astype(F32), (0, LANE - v.shape[0])).reshape(1, LANE)


def _ssd_prompt(xbc, dt, conv_w, conv_b, dt_bias, a_log, d_skip):
    b, l, _ = xbc.shape
    nc = l // SSD_CHUNK
    full = lambda shape: pl.BlockSpec(shape, lambda i, c: (0,) * len(shape))
    y, hfin = pl.pallas_call(
        _ssd_kernel,
        out_shape=(jax.ShapeDtypeStruct((b, l, SSD_WIDTH), F32),
                   jax.ShapeDtypeStruct((b, SSD_HEADS * HEAD_DIM, SSD_STATE), F32)),
        grid=(b, nc),
        in_specs=[pl.BlockSpec((1, SSD_CHUNK, CONV_DIM), lambda i, c: (i, c, 0)),
                  pl.BlockSpec((1, SSD_CHUNK, LANE), lambda i, c: (i, c, 0)),
                  full((SSD_CONV, CONV_DIM)), full((1, CONV_DIM)), full((1, LANE)), full((1, LANE)),
                  full((1, SSD_WIDTH)), full((2 * LANE, SSD_WIDTH)), full((SSD_CHUNK, SSD_CHUNK))],
        out_specs=(pl.BlockSpec((1, SSD_CHUNK, SSD_WIDTH), lambda i, c: (i, c, 0)),
                   pl.BlockSpec((1, SSD_HEADS * HEAD_DIM, SSD_STATE), lambda i, c: (i, 0, 0))),
        scratch_shapes=[pltpu.VMEM((SSD_CHUNK + 8, CONV_DIM), F32),
                        pltpu.VMEM((SSD_HEADS // 2, SSD_STATE, LANE), F32)],
        compiler_params=pltpu.CompilerParams(dimension_semantics=("arbitrary", "arbitrary"),
                                             vmem_limit_bytes=VMEM_LIMIT),
        name="ssd_prompt",
    )(xbc, dt, conv_w, conv_b.reshape(1, CONV_DIM), _pad_lanes(dt_bias), _pad_lanes(a_log),
      jnp.repeat(d_skip.astype(F32), HEAD_DIM).reshape(1, SSD_WIDTH),
      jnp.concatenate([_head_expand(), _head_expand()], axis=0).astype(BF16),
      jnp.asarray(np.tril(np.ones((SSD_CHUNK, SSD_CHUNK), np.float32)), BF16))
    return y, hfin.reshape(b, SSD_HEADS, HEAD_DIM, SSD_STATE)


def _ssd_step_kernel(xbc_ref, c0_ref, c1_ref, c2_ref, dt_ref, h0_ref, cw_ref, cb_ref, dtb_ref, alog_ref,
                     dsk_ref, e_ref, y_ref, hout_ref, xt_sc, dect_sc, bc_sc, yt_sc, xs_sc):
    b = pl.program_id(0)
    nb = xbc_ref.shape[0]
    rows = SSD_HEADS * HEAD_DIM
    gn = SSD_GROUPS * SSD_STATE

    @pl.when(b == 0)
    def _():
        acc = (cb_ref[...] + cw_ref[0:1, :] * c0_ref[...] + cw_ref[1:2, :] * c1_ref[...]
               + cw_ref[2:3, :] * c2_ref[...] + cw_ref[3:4, :] * xbc_ref[...])
        u = _silu(acc)
        xs = u[:, :SSD_WIDTH]
        dt = _softplus(dt_ref[...] + dtb_ref[...])
        dec = jnp.exp(dt * (-jnp.exp(alog_ref[...])))
        e = e_ref[...]
        xdt = xs * _dot32(dt, e)
        dec_e = _dot32(dec, e)
        pad = jnp.zeros((LANE - nb, SSD_WIDTH), F32)
        xt_sc[...] = jnp.concatenate([xdt, pad], axis=0).T
        dect_sc[...] = jnp.concatenate([dec_e, pad], axis=0).T
        bc_sc[...] = u[:, SSD_WIDTH:]
        xs_sc[...] = xs
        yt_sc[...] = jnp.zeros(yt_sc.shape, F32)

    ri = lax.broadcasted_iota(jnp.int32, (LANE, LANE), 0)
    onehot = jnp.where(ri == b, 1.0, 0.0)
    xcol = _dot32(xt_sc[...], onehot)
    dcol = _dot32(dect_sc[...], onehot)
    bc = bc_sc[pl.ds(b, 1), :]
    row = lax.broadcasted_iota(jnp.int32, (rows, SSD_STATE), 0)
    first = row < rows // SSD_GROUPS
    b_full = jnp.where(first, bc[:, 0:SSD_STATE], bc[:, SSD_STATE:gn])
    c_full = jnp.where(first, bc[:, gn:gn + SSD_STATE], bc[:, gn + SSD_STATE:])
    new = dcol * h0_ref[0] + xcol * b_full
    hout_ref[0] = new
    ycol = _dot32(new * c_full, jnp.ones((SSD_STATE, LANE), F32))
    lane = lax.broadcasted_iota(jnp.int32, (rows, LANE), 1)
    yt_sc[...] = jnp.where(lane == b, ycol, yt_sc[...])

    @pl.when(b == nb - 1)
    def _():
        y_ref[...] = yt_sc[...].T[0:nb, :] + xs_sc[...] * dsk_ref[...]


def _ssd_step(xbc, conv_state, dt, h0, conv_w, conv_b, dt_bias, a_log, d_skip):
    nb = xbc.shape[0]
    rows = SSD_HEADS * HEAD_DIM
    full = lambda shape: pl.BlockSpec(shape, lambda i: (0,) * len(shape))
    y, hout = pl.pallas_call(
        _ssd_step_kernel,
        out_shape=(jax.ShapeDtypeStruct((nb, SSD_WIDTH), F32),
                   jax.ShapeDtypeStruct((nb, rows, SSD_STATE), F32)),
        grid=(nb,),
        in_specs=[full((nb, CONV_DIM)), full((nb, CONV_DIM)), full((nb, CONV_DIM)), full((nb, CONV_DIM)),
                  full((nb, LANE)),
                  pl.BlockSpec((1, rows, SSD_STATE), lambda i: (i, 0, 0)),
                  full((SSD_CONV, CONV_DIM)), full((1, CONV_DIM)), full((1, LANE)), full((1, LANE)),
                  full((1, SSD_WIDTH)), full((LANE, SSD_WIDTH))],
        out_specs=(full((nb, SSD_WIDTH)),
                   pl.BlockSpec((1, rows, SSD_STATE), lambda i: (i, 0, 0))),
        scratch_shapes=[pltpu.VMEM((rows, LANE), F32), pltpu.VMEM((rows, LANE), F32),
                        pltpu.VMEM((nb, 2 * SSD_GROUPS * SSD_STATE), F32),
                        pltpu.VMEM((rows, LANE), F32), pltpu.VMEM((nb, SSD_WIDTH), F32)],
        compiler_params=pltpu.CompilerParams(dimension_semantics=("arbitrary",)),
        name="ssd_step",
    )(xbc, conv_state[:, 0], conv_state[:, 1], conv_state[:, 2], dt, h0.reshape(nb, rows, SSD_STATE),
      conv_w, conv_b.reshape(1, CONV_DIM), _pad_lanes(dt_bias), _pad_lanes(a_log),
      jnp.repeat(d_skip.astype(F32), HEAD_DIM).reshape(1, SSD_WIDTH), _head_expand())
    return y, hout.reshape(nb, SSD_HEADS, HEAD_DIM, SSD_STATE)


def _compress_kernel(pt_ref, *refs, transposed, n_pg):
    segs = PAGE_SIZE // CMP_STRIDE
    if transposed:
        perm_ref, w1_ref, pe_ref, w2_ref, out_ref, sh_sc, pe_sc, xs_sc = refs[n_pg:]
        for i in range(n_pg):
            xs_sc[i] = _dot_nt(perm_ref[...], refs[i][0].astype(BF16))

        def token_rows(k, o):
            return jnp.concatenate([xs_sc[i, o * segs:(o + 1) * segs, k * LANE:(k + 1) * LANE]
                                    for i in range(n_pg)], axis=0)
    else:
        w1_ref, pe_ref, w2_ref, out_ref, sh_sc, pe_sc = refs[2 * n_pg:]

        def token_rows(k, o):
            return jnp.concatenate([refs[k * n_pg + i][0, pl.ds(o, segs, stride=CMP_STRIDE), :]
                                    for i in range(n_pg)], axis=0)
    s = pl.program_id(1)
    rows = n_pg * segs
    hid2 = NSA_KV_HEADS * CMP_HID

    @pl.when(s == 0)
    def _():
        sh_sc[:, 0:8, :] = jnp.zeros((2, 8, hid2), F32)
        for k in range(2):
            t = jnp.zeros((8, 2 * hid2), F32)
            for o in range(0, CMP_STRIDE, 2):
                pe2 = jnp.concatenate([pe_ref[o, k], pe_ref[o + 1, k]], axis=1)
                t = t + jnp.dot(pe2.astype(BF16), w1_ref[o // 2, k], preferred_element_type=F32)
            pe_sc[k] = jnp.broadcast_to(t[0:1, 0:hid2] + t[1:2, hid2:], (8, hid2))

    for k in range(2):
        acc = jnp.zeros((rows, 2 * hid2), F32)
        for o in range(0, CMP_STRIDE, 2):
            xo = jnp.concatenate([token_rows(k, o), token_rows(k, o + 1)], axis=1)
            acc = acc + jnp.dot(xo.astype(BF16), w1_ref[o // 2, k], preferred_element_type=F32)
        sh_sc[k, 8:8 + rows, :] = acc[:, 0:hid2]
        pre = acc[:, hid2:] + sh_sc[k, 7:7 + rows, :] + pe_sc[k, 0:1, :]
        sh_sc[k, 0:8, :] = sh_sc[k, rows:rows + 8, :]
        out_ref[0, :, k * LANE:(k + 1) * LANE] = jnp.dot(_silu(pre).astype(BF16), w2_ref[k],
                                                         preferred_element_type=F32)


def _compress_weights(cmp_pe, cmp_w1, cmp_w2):
    span = CMP_BLOCK // CMP_STRIDE
    w1s = cmp_w1.astype(F32).reshape(2, span, CMP_STRIDE, HEAD_DIM, CMP_HID)
    z = jnp.zeros((2, span, CMP_STRIDE, HEAD_DIM, CMP_HID), F32)
    top = jnp.concatenate([w1s, z], axis=-1)
    bot = jnp.concatenate([z, w1s], axis=-1)
    bd = jnp.concatenate([top, bot], axis=-2)
    w1 = jnp.transpose(bd, (2, 0, 3, 1, 4)).reshape(CMP_STRIDE, 2, LANE, span * 2 * CMP_HID).astype(BF16)
    w1 = jnp.transpose(w1.reshape(CMP_STRIDE // 2, 2, 2, LANE, span * 2 * CMP_HID), (0, 2, 1, 3, 4))
    w1 = w1.reshape(CMP_STRIDE // 2, 2, 2 * LANE, span * 2 * CMP_HID)
    pe = cmp_pe.astype(F32).reshape(2, span, CMP_STRIDE, HEAD_DIM)
    pe = jnp.transpose(pe, (2, 0, 1, 3))
    pe = jnp.concatenate([pe, pe], axis=-1)
    pe = jnp.pad(pe, ((0, 0), (0, 0), (0, 8 - span), (0, 0)))
    w2 = cmp_w2.astype(F32)
    z2 = jnp.zeros_like(w2)
    w2bd = jnp.concatenate([jnp.concatenate([w2, z2], axis=-1), jnp.concatenate([z2, w2], axis=-1)],
                           axis=-2).astype(BF16)
    return w1, pe, w2bd


def _compress(pages_arr, page_ids, cweights, transposed):
    nb, n_pages = page_ids.shape
    n_pg = math.gcd(n_pages, MAX_PAGES_PER_STEP)
    steps = n_pages // n_pg
    segs = PAGE_SIZE // CMP_STRIDE
    rows = n_pg * segs
    w1, pe, w2bd = cweights
    hid2 = NSA_KV_HEADS * CMP_HID
    page_of = lambda b, s, pt, i: pt[(b * steps + s) * n_pg + i]
    scratch = [pltpu.VMEM((2, rows + 8, hid2), F32), pltpu.VMEM((2, 8, hid2), F32)]
    if transposed:
        page_specs = [pl.BlockSpec((1, KV_COLS, PAGE_SIZE), lambda b, s, pt, i=i: (page_of(b, s, pt, i), 0, 0))
                      for i in range(n_pg)]
        scratch.append(pltpu.VMEM((n_pg, PAGE_SIZE, KV_COLS), F32))
        perm = np.zeros((PAGE_SIZE, PAGE_SIZE), np.float32)
        for o in range(CMP_STRIDE):
            for sg in range(segs):
                perm[o * segs + sg, sg * CMP_STRIDE + o] = 1.0
        extra, extra_specs = [jnp.asarray(perm, BF16)], [pl.BlockSpec((PAGE_SIZE, PAGE_SIZE), lambda b, s, pt: (0, 0))]
    else:
        extra, extra_specs = [], []
        page_specs = [pl.BlockSpec((1, PAGE_SIZE, LANE), lambda b, s, pt, i=i, k=k: (page_of(b, s, pt, i), 0, k))
                      for k in range(2) for i in range(n_pg)]
    full = lambda shape: pl.BlockSpec(shape, lambda b, s, pt: (0,) * len(shape))
    return pl.pallas_call(
        functools.partial(_compress_kernel, transposed=transposed, n_pg=n_pg),
        out_shape=jax.ShapeDtypeStruct((nb, n_pages * segs, KV_COLS), F32),
        grid_spec=pltpu.PrefetchScalarGridSpec(
            num_scalar_prefetch=1,
            grid=(nb, steps),
            in_specs=page_specs + extra_specs + [full(w1.shape), full(pe.shape), full(w2bd.shape)],
            out_specs=pl.BlockSpec((1, rows, KV_COLS), lambda b, s, pt: (b, s, 0)),
            scratch_shapes=scratch),
        compiler_params=pltpu.CompilerParams(dimension_semantics=("arbitrary", "arbitrary"),
                                             vmem_limit_bytes=VMEM_LIMIT),
        name="nsa_compress",
    )(page_ids.reshape(-1), *([pages_arr] * len(page_specs)), *extra, w1, pe, w2bd)


def _overlap_matrix(n_rows, n_blocks, n_cols):
    m = np.arange(n_rows)[:, None]
    j = np.arange(n_cols)[None, :]
    cs = (m - 1) * CMP_STRIDE
    ov = (m >= 1) & (j < n_blocks) & (cs < j * SLC_BLOCK + SLC_BLOCK) & (cs + CMP_BLOCK > j * SLC_BLOCK)
    return jnp.asarray(ov.astype(np.float32))


def _stack_q(q, g):
    return _stack_q_f32(q, g).astype(BF16)


def _stack_q_f32(q, g):
    lane = lax.broadcasted_iota(jnp.int32, (Q_BLOCK, LANE), 1)
    keep = (lane < HALF) if g == 0 else (lane >= HALF)
    parts = []
    for jp in range(NSA_REP // 2):
        j = g * (NSA_REP // 2) + jp
        slab = q[:, j * LANE:(j + 1) * LANE]
        rolled = pltpu.roll(slab, HALF, 1)
        first, second = (slab, rolled) if g == 0 else (rolled, slab)
        parts.append(jnp.where(keep, first, 0.0))
        parts.append(jnp.where(keep, second, 0.0))
    return jnp.concatenate(parts, axis=0)


def _unstack_o(acc, g):
    lane = lax.broadcasted_iota(jnp.int32, (Q_BLOCK, LANE), 1)
    outs = []
    for jp in range(NSA_REP // 2):
        a = acc[(2 * jp) * Q_BLOCK:(2 * jp + 1) * Q_BLOCK]
        b = acc[(2 * jp + 1) * Q_BLOCK:(2 * jp + 2) * Q_BLOCK]
        if g == 0:
            outs.append(jnp.where(lane < HALF, a, pltpu.roll(b, HALF, 1)))
        else:
            outs.append(jnp.where(lane < HALF, pltpu.roll(a, HALF, 1), b))
    return jnp.concatenate(outs, axis=1)


def _tile8(x):
    return jnp.concatenate([x] * NSA_REP, axis=0)


def _rank_rows(imp, n_valid):
    sub = 8
    n_rank = -(-n_valid // sub) * sub
    chunks = [imp[c:c + sub] for c in range(0, n_rank, sub)]
    ranks = [jnp.zeros(ch.shape, F32) for ch in chunks]
    jrow = lax.broadcasted_iota(jnp.int32, chunks[0].shape, 0)
    for k in range(n_valid):
        rk = imp[k:k + 1, :]
        for c, ch in enumerate(chunks):
            if c * sub > k:
                ahead = rk >= ch
            elif c * sub + sub - 1 < k:
                ahead = rk > ch
            else:
                ahead = (rk > ch) | ((rk == ch) & (jrow + c * sub > k))
            ranks[c] = ranks[c] + jnp.where(ahead, 1.0, 0.0)
    rest = jnp.full((imp.shape[0] - n_rank, imp.shape[1]), float(n_valid), F32)
    return jnp.concatenate(ranks + [rest], axis=0)


def _cmp_kernel(q_ref, kv_ref, pb_ref, ov_ref, o_ref, sel_ref, bias_sc, *, n_keys, n_blocks):
    i = pl.program_id(0)
    qi = lax.broadcasted_iota(jnp.int32, (Q_BLOCK, n_keys), 0)
    mi = lax.broadcasted_iota(jnp.int32, (Q_BLOCK, n_keys), 1)
    qpos = Q_BLOCK * i + qi

    @pl.when(pl.program_id(1) == 0)
    def _():
        valid = (mi >= 1) & (CMP_STRIDE * mi + CMP_STRIDE - 1 <= qpos)
        for h in range(NSA_HEADS):
            bias_sc[h] = jnp.where(valid, pltpu.roll(pb_ref[h], (4 * i + 4) % n_keys, 1), NEG)

    rowvalid8 = _tile8(jnp.where(qpos[:, 0:1] >= CMP_BLOCK - 1, 1.0, 0.0))
    jj = lax.broadcasted_iota(jnp.int32, (LANE, LANE), 1)
    ov = ov_ref[...].astype(BF16)
    forced = (jj == 0) | (jj == i) | (jj == i - 1)
    for bb in range(q_ref.shape[0]):
        q = q_ref[bb] * LOG2E
        kv = kv_ref[bb]
        kc = kv[:, 0:LANE].astype(BF16)
        vc = kv[:, LANE:].astype(BF16)
        psum = []
        for g in range(NSA_KV_HEADS):
            s = _dot_nt(_stack_q(q, g), kc) + bias_sc[g * NSA_REP:(g + 1) * NSA_REP].reshape(ROWS, n_keys)
            p = jnp.exp2(s - jnp.max(s, axis=-1, keepdims=True))
            pc = p / jnp.sum(p, axis=-1, keepdims=True) * rowvalid8
            o_ref[bb, :, g * GROUP_W:(g + 1) * GROUP_W] = _unstack_o(
                jnp.dot(pc.astype(BF16), vc, preferred_element_type=F32), g)
            ps = pc[0:Q_BLOCK]
            for r in range(1, NSA_REP):
                ps = ps + pc[r * Q_BLOCK:(r + 1) * Q_BLOCK]
            psum.append(ps)
        ps = jnp.concatenate(psum, axis=0)
        hi = ps.astype(BF16)
        lo = (ps - hi.astype(F32)).astype(BF16)
        imp = (jnp.dot(hi, ov, preferred_element_type=F32)
               + jnp.dot(lo, ov, preferred_element_type=F32))
        imp = jnp.where(forced, 1e6, jnp.where(jj <= i, imp, -1e6))
        imp = jnp.where(jj < n_blocks, imp, -2e6)
        rank = _rank_rows(imp.T, n_blocks)
        sel_ref[bb, 0] = jnp.where(rank < SLC_TOP, 1.0, 0.0).astype(BF16)


def _cmp_prompt(q, kvc_cmp, pattern, ov):
    b, l, _ = q.shape
    n_keys = kvc_cmp.shape[1]
    n_blocks = l // SLC_BLOCK
    nb = 2 if b % 2 == 0 else 1
    return pl.pallas_call(
        functools.partial(_cmp_kernel, n_keys=n_keys, n_blocks=n_blocks),
        out_shape=(jax.ShapeDtypeStruct((b, l, NSA_WIDTH), F32),
                   jax.ShapeDtypeStruct((b, l // Q_BLOCK, LANE, LANE), BF16)),
        grid=(l // Q_BLOCK, b // nb),
        in_specs=[pl.BlockSpec((nb, Q_BLOCK, NSA_WIDTH), lambda i, bi: (bi, i, 0)),
                  pl.BlockSpec((nb, n_keys, KV_COLS), lambda i, bi: (bi, 0, 0)),
                  pl.BlockSpec(pattern.shape, lambda i, bi: (0, 0, 0)),
                  pl.BlockSpec(ov.shape, lambda i, bi: (0, 0))],
        out_specs=(pl.BlockSpec((nb, Q_BLOCK, NSA_WIDTH), lambda i, bi: (bi, i, 0)),
                   pl.BlockSpec((nb, 1, LANE, LANE), lambda i, bi: (bi, i, 0, 0))),
        scratch_shapes=[pltpu.VMEM(pattern.shape, F32)],
        compiler_params=pltpu.CompilerParams(dimension_semantics=("arbitrary", "arbitrary"),
                                             vmem_limit_bytes=VMEM_LIMIT),
        name="nsa_cmp_prompt",
    )(q, kvc_cmp, pattern, ov)


def _flash_update_t(st, vt, m, acc):
    m_new = jnp.maximum(m, jnp.max(st, axis=0, keepdims=True))
    p = jnp.exp2(st - m_new)
    acc = jnp.exp2(m - m_new) * acc + jnp.dot(vt, p.astype(BF16), preferred_element_type=F32)
    return m_new, acc


def _finish_t(acc):
    o = (acc[0:HEAD_DIM] / acc[HEAD_DIM:HEAD_DIM + 1]).T
    return jnp.concatenate([o[r * Q_BLOCK:(r + 1) * Q_BLOCK] for r in range(NSA_REP)], axis=1)


def _slc_kernel(q_ref, sel_ref, kx_ref, vt_ref, tnt_ref, tile_ref, o_ref, far_sc, near_sc, sta_sc, stb_sc, stn_sc):
    i = pl.program_id(1)
    q = q_ref[0] * LOG2E
    n_far = (jnp.maximum(i - 3, 0) + 3) // 4
    sel = sel_ref[0, 0]
    jrow = lax.broadcasted_iota(jnp.int32, (LANE, ROWS), 0)
    a = ((i + 1) // 2) * LANE
    delta = Q_BLOCK * (i + 1) - a
    groups = range(NSA_KV_HEADS)
    for g in groups:
        qs_t = _stack_q_f32(q, g).T.astype(BF16)
        hit = jnp.dot(sel, tile_ref[g], preferred_element_type=F32) > 0.5
        near_sc[g] = jnp.concatenate([qs_t, jnp.where(hit & (jrow != PAD_FLAG_COL), 0.0, NEG).astype(BF16)], axis=0)
        far_sc[g] = jnp.concatenate([qs_t, jnp.where(hit & (jrow < i - 3), 0.0, NEG).astype(BF16)], axis=0)

    def far_scores(t, dst):
        start = pl.multiple_of(SLC_PAD + SLC_NEAR * t, SLC_NEAR)
        kx = kx_ref[0, pl.ds(start, SLC_NEAR), :]
        for g in groups:
            dst[g] = jnp.dot(kx, far_sc[g], preferred_element_type=F32)

    def far_softmax(t, src, carry):
        start = pl.multiple_of(SLC_PAD + SLC_NEAR * t, SLC_NEAR)
        return tuple(_flash_update_t(src[g], vt_ref[0, g, :, pl.ds(start, SLC_NEAR)], *carry[g]) for g in groups)

    def far_pair(u, carry):
        far_scores(2 * u + 1, stb_sc)
        carry = far_softmax(2 * u, sta_sc, carry)
        far_scores(2 * u + 2, sta_sc)
        return far_softmax(2 * u + 1, stb_sc, carry)

    far_scores(0, sta_sc)
    start = pl.multiple_of(a, LANE)
    kx = kx_ref[0, pl.ds(start, SLC_NEARW), :]
    for g in groups:
        stn_sc[g] = jnp.dot(kx, near_sc[g], preferred_element_type=F32)
    init = (jnp.full((1, ROWS), NEG, F32), jnp.zeros((HEAD_DIM + ONES_ROWS, ROWS), F32))
    carry = lax.fori_loop(0, (n_far + 1) // 2, far_pair, (init, init))

    for g in groups:
        st = stn_sc[g] + tnt_ref[delta // Q_BLOCK, g]
        _, acc = _flash_update_t(st, vt_ref[0, g, :, pl.ds(start, SLC_NEARW)], *carry[g])
        o_ref[0, :, g * GROUP_W:(g + 1) * GROUP_W] = _finish_t(acc)


def _slc_prompt(q, sel, kx_pad, vt_pad, tnt, tile):
    b, l, _ = q.shape
    lp = kx_pad.shape[1]
    tq = Q_BLOCK
    n_chains = NSA_KV_HEADS
    return pl.pallas_call(
        _slc_kernel,
        out_shape=jax.ShapeDtypeStruct((b, l, NSA_WIDTH), F32),
        grid=(b, l // tq),
        in_specs=[pl.BlockSpec((1, tq, NSA_WIDTH), lambda bi, i: (bi, i, 0)),
                  pl.BlockSpec((1, 1, LANE, LANE), lambda bi, i: (bi, i, 0, 0)),
                  pl.BlockSpec((1, lp, 2 * LANE), lambda bi, i: (bi, 0, 0)),
                  pl.BlockSpec((1, NSA_KV_HEADS, HEAD_DIM + ONES_ROWS, lp), lambda bi, i: (bi, 0, 0, 0)),
                  pl.BlockSpec(tnt.shape, lambda bi, i: (0, 0, 0, 0)),
                  pl.BlockSpec(tile.shape, lambda bi, i: (0, 0, 0))],
        out_specs=pl.BlockSpec((1, tq, NSA_WIDTH), lambda bi, i: (bi, i, 0)),
        scratch_shapes=[pltpu.VMEM((n_chains, 2 * LANE, ROWS), BF16),
                        pltpu.VMEM((n_chains, 2 * LANE, ROWS), BF16),
                        pltpu.VMEM((n_chains, SLC_NEAR, ROWS), F32),
                        pltpu.VMEM((n_chains, SLC_NEAR, ROWS), F32),
                        pltpu.VMEM((n_chains, SLC_NEARW, ROWS), F32)],
        compiler_params=pltpu.CompilerParams(vmem_limit_bytes=VMEM_LIMIT),
        name="nsa_slc_prompt",
    )(q, sel, kx_pad, vt_pad, tnt, tile)


def _group_tile():
    t = np.zeros((NSA_KV_HEADS, LANE, ROWS), np.float32)
    for g in range(NSA_KV_HEADS):
        for r in range(NSA_REP):
            for qq in range(Q_BLOCK):
                t[g, g * Q_BLOCK + qq, r * Q_BLOCK + qq] = 1.0
    return jnp.asarray(t, BF16)


def _win_kernel(q_ref, kx_ref, vt_ref, tw_ref, o_ref, st_sc):
    i2 = pl.program_id(1)
    q = q_ref[0] * LOG2E
    start = pl.multiple_of(i2 * LANE, LANE)
    kx = kx_ref[0, pl.ds(start, WIN_W), :]
    groups = range(NSA_KV_HEADS)
    row = lax.broadcasted_iota(jnp.int32, (LANE, 2 * ROWS), 0)
    pad_rows = jnp.where(row == 0, NEG, 0.0).astype(BF16)
    for g in groups:
        qs_t = jnp.concatenate([_stack_q_f32(q[h * Q_BLOCK:(h + 1) * Q_BLOCK], g).T for h in range(2)], axis=1)
        qx = jnp.concatenate([qs_t.astype(BF16), pad_rows], axis=0)
        st_sc[g] = jnp.dot(kx, qx, preferred_element_type=F32)
    for g in groups:
        st = st_sc[g] + tw_ref[g]
        p = jnp.exp2(st - jnp.max(st, axis=0, keepdims=True))
        acc = jnp.dot(vt_ref[0, g, :, pl.ds(start, WIN_W)], p.astype(BF16), preferred_element_type=F32)
        for h in range(2):
            o_ref[0, h * Q_BLOCK:(h + 1) * Q_BLOCK, g * GROUP_W:(g + 1) * GROUP_W] = _finish_t(
                acc[:, h * ROWS:(h + 1) * ROWS])


def _win_prompt(q, kx_pad, vt_pad, tw):
    b, l, _ = q.shape
    lp = kx_pad.shape[1]
    tq = 2 * Q_BLOCK
    return pl.pallas_call(
        _win_kernel,
        out_shape=jax.ShapeDtypeStruct((b, l, NSA_WIDTH), F32),
        grid=(b, l // tq),
        in_specs=[pl.BlockSpec((1, tq, NSA_WIDTH), lambda bi, i: (bi, i, 0)),
                  pl.BlockSpec((1, lp, 2 * LANE), lambda bi, i: (bi, 0, 0)),
                  pl.BlockSpec((1, NSA_KV_HEADS, HEAD_DIM + ONES_ROWS, lp), lambda bi, i: (bi, 0, 0, 0)),
                  pl.BlockSpec(tw.shape, lambda bi, i: (0, 0, 0))],
        out_specs=pl.BlockSpec((1, tq, NSA_WIDTH), lambda bi, i: (bi, i, 0)),
        scratch_shapes=[pltpu.VMEM((NSA_KV_HEADS, WIN_W, 2 * ROWS), F32)],
        compiler_params=pltpu.CompilerParams(vmem_limit_bytes=VMEM_LIMIT),
        name="nsa_win_prompt",
    )(q, kx_pad, vt_pad, tw)


def _dense1_kernel(q_ref, k_ref, v_ref, bm_ref, o_ref, ps_ref):
    k = k_ref[0].astype(BF16)
    v = v_ref[0].astype(BF16)
    for g in range(NSA_KV_HEADS):
        rows = slice(g * NSA_REP, (g + 1) * NSA_REP)
        s = _dot_nt(q_ref[0, rows, :], k) + bm_ref[rows, :]
        p = jnp.exp(s - jnp.max(s, axis=-1, keepdims=True))
        pc = p / jnp.sum(p, axis=-1, keepdims=True)
        o_ref[0, rows, :] = jnp.dot(pc.astype(BF16), v, preferred_element_type=F32)
        ps_ref[0, g:g + 1, :] = jnp.sum(pc, axis=0, keepdims=True)


def _dense1(q_pad, kv, biasmask):
    nb, n, _ = kv.shape
    return pl.pallas_call(
        _dense1_kernel,
        out_shape=(jax.ShapeDtypeStruct((nb, NSA_HEADS, LANE), F32),
                   jax.ShapeDtypeStruct((nb, NSA_KV_HEADS, n), F32)),
        grid=(nb,),
        in_specs=[pl.BlockSpec((1, NSA_HEADS, LANE), lambda b: (b, 0, 0)),
                  pl.BlockSpec((1, n, LANE), lambda b: (b, 0, 0)),
                  pl.BlockSpec((1, n, LANE), lambda b: (b, 0, 1)),
                  pl.BlockSpec((NSA_HEADS, n), lambda b: (0, 0))],
        out_specs=(pl.BlockSpec((1, NSA_HEADS, LANE), lambda b: (b, 0, 0)),
                   pl.BlockSpec((1, NSA_KV_HEADS, n), lambda b: (b, 0, 0))),
        name="nsa_dense_sample",
    )(q_pad, kv, kv, biasmask)


def _take_group_half(o_pad):
    nb = o_pad.shape[0]
    o = o_pad.reshape(nb, NSA_KV_HEADS, NSA_REP, NSA_KV_HEADS, HEAD_DIM)
    o = jnp.stack([o[:, g, :, g, :] for g in range(NSA_KV_HEADS)], axis=1)
    return o.reshape(nb, NSA_WIDTH)


def _topk_kernel(ps_ref, ov_ref, idx_ref, imp_sc, *, n_blocks, cur):
    n_rows = ps_ref.shape[0]
    n_cols = ov_ref.shape[1]
    ps = jnp.concatenate([ps_ref[...], jnp.zeros((LANE - n_rows, ps_ref.shape[1]), F32)], axis=0)
    imp = _dot32(ps, ov_ref[...])
    jj = lax.broadcasted_iota(jnp.int32, (LANE, n_cols), 1)
    forced = (jj == 0) | (jj == cur) | (jj == cur - 1)
    imp = jnp.where(forced, 1e6, jnp.where(jj <= cur, imp, -1e6))
    imp = jnp.where(jj < n_blocks, imp, -2e6)
    imp_t = imp.T
    imp_sc[...] = imp_t
    jrow = lax.broadcasted_iota(jnp.int32, (n_cols, LANE), 0)

    def body(k, rank):
        rk = imp_sc[pl.ds(k, 1), :]
        ahead = (rk > imp_t) | ((rk == imp_t) & (jrow > k))
        return rank + jnp.where(ahead, 1.0, 0.0)

    rank = lax.fori_loop(0, n_blocks, body, jnp.zeros((n_cols, LANE), F32))
    jf = jrow.astype(F32)
    rows = [jnp.sum(jnp.where(rank == float(r), jf, 0.0), axis=0, keepdims=True) for r in range(SLC_TOP)]
    idx_ref[...] = jnp.concatenate(rows, axis=0).astype(jnp.int32)


def _topk_sample(psum, ov, n_blocks, cur):
    n_rows, n_keys = psum.shape
    n_cols = ov.shape[1]
    return pl.pallas_call(
        functools.partial(_topk_kernel, n_blocks=n_blocks, cur=cur),
        out_shape=jax.ShapeDtypeStruct((SLC_TOP, LANE), jnp.int32),
        grid=(1,),
        in_specs=[pl.BlockSpec((n_rows, n_keys), lambda i: (0, 0)),
                  pl.BlockSpec(ov.shape, lambda i: (0, 0))],
        out_specs=pl.BlockSpec((SLC_TOP, LANE), lambda i: (0, 0)),
        scratch_shapes=[pltpu.VMEM((n_cols, LANE), F32)],
        name="nsa_topk_sample",
    )(psum, ov)


def _slc1_kernel(phys_ref, jsel_ref, q_ref, *refs, cur, past):
    pages = refs[:SLC_TOP]
    new_ref, bb_ref, o_ref = refs[SLC_TOP:]
    b = pl.program_id(0)
    g = pl.program_id(1)
    goff = pl.multiple_of(g * HEAD_DIM, HEAD_DIM)
    q = q_ref[0]
    lane = lax.broadcasted_iota(jnp.int32, (NSA_REP, PAGE_SIZE), 1)
    first = lax.broadcasted_iota(jnp.int32, (HEAD_DIM, PAGE_SIZE), 1) == 0
    new_k = jnp.where(first, new_ref[0, pl.ds(goff, HEAD_DIM), :], 0.0)
    new_v = jnp.where(first, new_ref[0, pl.ds(LANE + goff, HEAD_DIM), :], 0.0)
    scores, values = [], []
    for n in range(SLC_TOP):
        j = jsel_ref[(b * NSA_KV_HEADS + g) * SLC_TOP + n]
        kt = jnp.where(j == cur, new_k, pages[n][0, pl.ds(goff, HEAD_DIM), :])
        vt = jnp.where(j == cur, new_v, pages[n][0, pl.ds(LANE + goff, HEAD_DIM), :])
        s = jnp.dot(q, kt.astype(BF16), preferred_element_type=F32)
        ok = (lane // SLC_BLOCK == j % 2) & ((j // 2) * PAGE_SIZE + lane <= past)
        bias = bb_ref[jnp.clip(j - (cur - 3), 0, 3), pl.ds(pl.multiple_of(g * NSA_REP, NSA_REP), NSA_REP), :]
        scores.append(jnp.where(ok, s + bias, NEG))
        values.append(vt.astype(BF16))
    s_all = jnp.concatenate(scores, axis=1)
    p = jnp.exp(s_all - jnp.max(s_all, axis=-1, keepdims=True))
    acc = jnp.zeros((NSA_REP, HEAD_DIM), F32)
    for n in range(SLC_TOP):
        acc = acc + _dot_nt(p[:, n * PAGE_SIZE:(n + 1) * PAGE_SIZE].astype(BF16), values[n])
    o_ref[0] = acc / jnp.sum(p, axis=-1, keepdims=True)


def _slc_sample(q, cache_pages_t, phys, jsel, new_cols, biasblk, cur, past):
    nb = q.shape[0]
    idx = lambda b, g, n: (b * NSA_KV_HEADS + g) * SLC_TOP + n
    page_specs = [pl.BlockSpec((1, KV_COLS, PAGE_SIZE), lambda b, g, ph, js, n=n: (ph[idx(b, g, n)], 0, 0))
                  for n in range(SLC_TOP)]
    return pl.pallas_call(
        functools.partial(_slc1_kernel, cur=cur, past=past),
        out_shape=jax.ShapeDtypeStruct((nb, NSA_HEADS, HEAD_DIM), F32),
        grid_spec=pltpu.PrefetchScalarGridSpec(
            num_scalar_prefetch=2,
            grid=(nb, NSA_KV_HEADS),
            in_specs=[pl.BlockSpec((1, NSA_REP, HEAD_DIM), lambda b, g, ph, js: (b, g, 0))] + page_specs
            + [pl.BlockSpec((1, KV_COLS, 1), lambda b, g, ph, js: (b, 0, 0)),
               pl.BlockSpec(biasblk.shape, lambda b, g, ph, js: (0, 0, 0))],
            out_specs=pl.BlockSpec((1, NSA_REP, HEAD_DIM), lambda b, g, ph, js: (b, g, 0))),
        name="nsa_slc_sample",
    )(phys, jsel, q, *([cache_pages_t] * SLC_TOP), new_cols, biasblk)


def _out_kernel(x_ref, gate_ref, yssd_ref, zs_ref, oc_ref, os_ref, ow_ref, gl_ref, za_ref,
                nw1_ref, nw2_ref, w_ref, eg_ref, fg_ref, o_ref, *, per_row, final):
    gates = _sigmoid(gl_ref[...])
    g2 = jnp.concatenate(_split_bf16(gates, 2), axis=1)
    expand = lambda br: jnp.dot(g2, eg_ref[br], preferred_element_type=F32)
    y_nsa = expand(0) * oc_ref[...] + expand(1) * os_ref[...] + expand(2) * ow_ref[...]

    def gated_norm(y, z, w):
        u = y * _silu(z)
        half = u.shape[1] // 2
        parts = []
        for g in range(2):
            ug = u[:, g * half:(g + 1) * half]
            parts.append(ug * lax.rsqrt(jnp.mean(ug * ug, axis=-1, keepdims=True) + NORM_EPS))
        return (jnp.concatenate(parts, axis=1) * w).astype(BF16)

    m1 = gated_norm(yssd_ref[...], zs_ref[...], nw1_ref[...])
    m2 = gated_norm(y_nsa, za_ref[...], nw2_ref[...])
    proj = (jnp.dot(m1, w_ref[0:SSD_WIDTH, :], preferred_element_type=F32)
            + jnp.dot(m2, w_ref[SSD_WIDTH:, :], preferred_element_type=F32))
    gate = gate_ref[...] if per_row else gate_ref[0]
    out = x_ref[...] + gate * proj
    if final:
        out = out * lax.rsqrt(jnp.mean(out * out, axis=-1, keepdims=True) + NORM_EPS) * fg_ref[...]
    o_ref[...] = out


def _gate_expand():
    e = np.zeros((3, LANE, NSA_WIDTH), np.float32)
    for br in range(3):
        for h in range(NSA_HEADS):
            e[br, br * NSA_HEADS + h, h * HEAD_DIM:(h + 1) * HEAD_DIM] = 1.0
    return jnp.asarray(np.concatenate([e, e], axis=1), BF16)


def _layer_out(x2d, gate, y_ssd, z_s, o_cmp, o_slc, o_win, gl, z_a, nw1, nw2, w_out_bf, final_g,
               rows_per_batch, final):
    m = x2d.shape[0]
    per_row = rows_per_batch == 1
    tm = m if per_row else PROJ_ROWS
    if per_row:
        gate_spec = pl.BlockSpec((tm, D_MODEL), lambda i: (0, 0))
        gt = gate
    else:
        gate_spec = pl.BlockSpec((1, 1, D_MODEL), lambda i: ((i * tm) // rows_per_batch, 0, 0))
        gt = gate[:, None, :]
    row = lambda w: pl.BlockSpec((tm, w), lambda i: (i, 0))
    full = lambda shape: pl.BlockSpec(shape, lambda i: (0,) * len(shape))
    eg = _gate_expand()
    return pl.pallas_call(
        functools.partial(_out_kernel, per_row=per_row, final=final),
        out_shape=jax.ShapeDtypeStruct((m, D_MODEL), F32),
        grid=(m // tm,),
        in_specs=[row(D_MODEL), gate_spec, row(SSD_WIDTH), row(SSD_WIDTH), row(NSA_WIDTH), row(NSA_WIDTH),
                  row(NSA_WIDTH), row(LANE), row(NSA_WIDTH), full((1, SSD_WIDTH)), full((1, NSA_WIDTH)),
                  full(w_out_bf.shape), full(eg.shape), full((1, D_MODEL))],
        out_specs=row(D_MODEL),
        compiler_params=pltpu.CompilerParams(vmem_limit_bytes=VMEM_LIMIT),
        name="layer_out",
    )(x2d, gt, y_ssd, z_s, o_cmp, o_slc, o_win, gl, z_a, nw1.reshape(1, SSD_WIDTH), nw2.reshape(1, NSA_WIDTH),
      w_out_bf, eg, final_g.reshape(1, D_MODEL))


def _pad_in_weights(w_in):
    cols = []
    off = 0
    for size, width in zip(_SEG_SIZES, _SEG_PAD):
        seg = w_in[:, off:off + size]
        cols.append(jnp.pad(seg, ((0, 0), (0, width - size))))
        off += size
    return jnp.concatenate(cols, axis=1).astype(BF16)


def _values_t(v_pad):
    b, rows, _ = v_pad.shape
    vt = jnp.transpose(v_pad.reshape(b, rows, NSA_KV_HEADS, HEAD_DIM), (0, 2, 3, 1))
    return jnp.concatenate([vt, jnp.ones((b, NSA_KV_HEADS, ONES_ROWS, rows), BF16)], axis=2)


def _front_pad_bf16(kv, rows):
    return jnp.pad(kv, ((0, 0), (rows, 0), (0, 0))).astype(BF16)


def kernel(x_prompt, x_sample, cache_cmp_kv, cache_slc_kv, state_win_kv, state_conv, state_ssm, page_table,
           c_prompt, c_sample, norm_g, ada_w, ada_b, w_in, conv_w, conv_b, dt_bias, a_log, d_skip,
           ssd_norm_w, cmp_pe, cmp_w1, cmp_w2, nsa_norm_w, w_out, rel_bias, final_norm_g):
    nbp, lp, _ = x_prompt.shape
    nbs = x_sample.shape[0]
    depth = w_in.shape[0]
    n_pool = cache_cmp_kv.shape[1]
    n_pages = page_table.shape[1]
    past = n_pages * PAGE_SIZE
    w_buf = state_win_kv.shape[2]
    kv_row = (2, NSA_KV_HEADS, HEAD_DIM)

    n_ck = lp // CMP_STRIDE
    n_sb = lp // SLC_BLOCK
    qi = np.arange(Q_BLOCK)[:, None]
    c0 = n_ck - 4
    pattern = _bias_of_dist(rel_bias, qi - CMP_STRIDE * (np.arange(n_ck)[None, :] - c0) - (CMP_STRIDE - 1)) * LOG2E
    far = rel_bias.astype(F32)[REL_BUCKETS - 1][:, None, None]
    par = np.arange(2)[:, None, None]
    qq = np.arange(Q_BLOCK)[None, None, :]
    dist_n = qq - np.arange(SLC_NEARW)[None, :, None] + Q_BLOCK * par + (SLC_NEAR - Q_BLOCK)
    c_max = Q_BLOCK + WIN_PAD
    master = _toeplitz_bias(rel_bias, c_max - (SLC_NEAR - Q_BLOCK) + SLC_NEARW, Q_BLOCK, c_max)
    window = lambda c0, rows: master[:, c_max - c0:c_max - c0 + rows]
    tnt = jnp.stack([window(Q_BLOCK * p + SLC_NEAR - Q_BLOCK, SLC_NEARW) for p in range(2)], axis=1)
    tnt = jnp.where(jnp.asarray((dist_n >= 0) & (dist_n <= qq + SLC_NEAR - Q_BLOCK))[None],
                    (tnt - far[..., None]) * LOG2E, NEG)
    tnt = jnp.transpose(tnt.reshape(NSA_KV_HEADS, NSA_REP, 2, SLC_NEARW, Q_BLOCK), (2, 0, 3, 1, 4))
    tnt = tnt.reshape(2, NSA_KV_HEADS, SLC_NEARW, ROWS)
    tile = _group_tile()
    key_row = np.arange(SLC_PAD + lp + SLC_BACK) - SLC_PAD
    assert (key_row[-1] // SLC_BLOCK) < PAD_FLAG_COL
    cols = np.arange(LANE)[None, :]
    blk_onehot = jnp.asarray(((key_row[:, None] >= 0) & (key_row[:, None] // SLC_BLOCK == cols))
                             | ((key_row[:, None] < 0) & (cols == PAD_FLAG_COL)), BF16)
    dist_w = qq - np.arange(WIN_W)[None, :, None] + Q_BLOCK * par + WIN_PAD
    tw = jnp.stack([window(Q_BLOCK * p + WIN_PAD, WIN_W) for p in range(2)], axis=1)
    tw = jnp.where(jnp.asarray((dist_w >= 0) & (dist_w < WINDOW))[None], tw * LOG2E, NEG)
    tw = jnp.transpose(tw.reshape(NSA_KV_HEADS, NSA_REP, 2, WIN_W, Q_BLOCK), (2, 0, 3, 1, 4))
    tw = tw.reshape(2, NSA_KV_HEADS, WIN_W, ROWS)
    tw = jnp.concatenate([tw[0], tw[1]], axis=-1)
    win_flag = jnp.asarray((np.arange(WIN_PAD + lp)[:, None] < WIN_PAD) & (np.arange(LANE)[None, :] == 0), BF16)
    ov_p = _overlap_matrix(n_ck, n_sb, LANE)

    n_cs = past // CMP_STRIDE
    cur = past // SLC_BLOCK
    n_blk_s = cur + 1
    n_cols_s = -(-n_blk_s // LANE) * LANE
    ov_s = _overlap_matrix(n_cs, n_blk_s, n_cols_s)
    m_s = np.arange(n_cs)
    bm_c = jnp.where(jnp.asarray(m_s >= 1)[None, :],
                     _bias_of_dist(rel_bias, past - (CMP_STRIDE * m_s + CMP_STRIDE - 1)), NEG)
    n_w = -(-(w_buf + 1) // LANE) * LANE
    iw = np.arange(n_w)
    dw = w_buf - iw
    ok_w = (iw <= w_buf) & (dw >= 0) & (dw < WINDOW) & (past - w_buf + iw >= 0)
    bm_w = jnp.where(jnp.asarray(ok_w)[None, :], _bias_of_dist(rel_bias, dw), NEG)
    jb = (cur - 3 + np.arange(4))[:, None]
    biasblk = _bias_of_dist(rel_bias, past - SLC_BLOCK * jb - np.arange(SLC_BLOCK)[None, :])
    biasblk = jnp.transpose(biasblk, (1, 0, 2))
    biasblk = jnp.concatenate([biasblk, biasblk], axis=-1)

    pages_t = lambda c: jnp.transpose(c, (0, 1, 3, 4, 5, 2)).reshape(depth * n_pool, KV_COLS, PAGE_SIZE)
    cmp_pages_t = pages_t(cache_cmp_kv)
    slc_pages_t = pages_t(cache_slc_kv)
    prompt_pages = jnp.arange(nbp * (lp // PAGE_SIZE), dtype=jnp.int32).reshape(nbp, lp // PAGE_SIZE)

    c_all = jnp.concatenate([c_prompt, c_sample], axis=0)
    xp = x_prompt.reshape(nbp * lp, D_MODEL)
    xs = x_sample.reshape(nbs, D_MODEL)
    outs = {k: [] for k in ("pc", "ps", "pw", "pconv", "pssm", "sc", "ss", "sw", "sconv", "sssm")}

    for l in range(depth):
        final = l == depth - 1
        w_pad = _pad_in_weights(w_in[l])
        w_out_bf = w_out[l].astype(BF16)
        cweights = _compress_weights(cmp_pe[l], cmp_w1[l], cmp_w2[l])
        mod = _modulation(c_all, ada_w[l], ada_b[l])
        shift, scale, gate = mod[:, :D_MODEL], mod[:, D_MODEL:2 * D_MODEL], mod[:, 2 * D_MODEL:]

        z_s, xbc, dt, q, kvc, kvs, kvw, gl, z_a = _in_projection(xp, norm_g[l], scale[:nbp], shift[:nbp], w_pad, lp)
        xbc3 = xbc.reshape(nbp, lp, CONV_DIM)
        y_ssd, h_fin = _ssd_prompt(xbc3, dt.reshape(nbp, lp, LANE), conv_w[l], conv_b[l], dt_bias[l], a_log[l],
                                   d_skip[l])
        q3 = q.reshape(nbp, lp, NSA_WIDTH)
        kvc3, kvs3, kvw3 = (t.reshape(nbp, lp, KV_COLS) for t in (kvc, kvs, kvw))
        kc = _compress(kvc.reshape(nbp * (lp // PAGE_SIZE), PAGE_SIZE, KV_COLS), prompt_pages, cweights, False)
        o_cmp, sel = _cmp_prompt(q3, kc, pattern, ov_p)
        ks_pad = jnp.pad(kvs3, ((0, 0), (SLC_PAD, SLC_BACK), (0, 0))).astype(BF16)
        kx_pad = jnp.concatenate([ks_pad[:, :, :LANE], jnp.broadcast_to(blk_onehot, (nbp,) + blk_onehot.shape)],
                                 axis=-1)
        o_slc = _slc_prompt(q3, sel, kx_pad, _values_t(ks_pad[:, :, LANE:]), tnt, tile)
        kw_pad = _front_pad_bf16(kvw3, WIN_PAD)
        kwx_pad = jnp.concatenate([kw_pad[:, :, :LANE], jnp.broadcast_to(win_flag, (nbp,) + win_flag.shape)], axis=-1)
        o_win = _win_prompt(q3, kwx_pad, _values_t(kw_pad[:, :, LANE:]), tw)
        xp = _layer_out(xp, gate[:nbp], y_ssd.reshape(nbp * lp, SSD_WIDTH), z_s,
                        o_cmp.reshape(nbp * lp, NSA_WIDTH), o_slc.reshape(nbp * lp, NSA_WIDTH),
                        o_win.reshape(nbp * lp, NSA_WIDTH), gl, z_a, ssd_norm_w[l], nsa_norm_w[l], w_out_bf,
                        final_norm_g, lp, final)
        outs["pc"].append(kvc3.reshape((nbp, lp) + kv_row))
        outs["ps"].append(kvs3.reshape((nbp, lp) + kv_row))
        outs["pw"].append(kvw3[:, -min(WINDOW, lp):].reshape((nbp, min(WINDOW, lp)) + kv_row))
        outs["pconv"].append(xbc3[:, -(SSD_CONV - 1):])
        outs["pssm"].append(h_fin)

        z_s2, xbc2, dt2, q2, kvc2, kvs2, kvw2, gl2, z_a2 = _in_projection(
            xs, norm_g[l], scale[nbp:], shift[nbp:], w_pad, 1)
        y_ssd2, h2 = _ssd_step(xbc2, state_conv[l], dt2, state_ssm[l], conv_w[l], conv_b[l], dt_bias[l],
                               a_log[l], d_skip[l])
        qh = q2.reshape(nbs, NSA_KV_HEADS, NSA_REP, HEAD_DIM)
        zq = jnp.zeros((nbs, NSA_REP, HEAD_DIM), F32)
        q_pad = jnp.stack([jnp.concatenate([qh[:, 0], zq], axis=-1), jnp.concatenate([zq, qh[:, 1]], axis=-1)],
                          axis=1).reshape(nbs, NSA_HEADS, LANE).astype(BF16)
        kc2 = _compress(cmp_pages_t, page_table + l * n_pool, cweights, True)
        oc2, psum = _dense1(q_pad, kc2, bm_c)
        sel_idx = _topk_sample(psum.reshape(nbs * NSA_KV_HEADS, n_cs), ov_s, n_blk_s, cur)
        jsel = sel_idx[:, :nbs * NSA_KV_HEADS].T.reshape(nbs, NSA_KV_HEADS, SLC_TOP)
        jc = jnp.minimum(jsel, cur - 1)
        page = jnp.take_along_axis(page_table, (jc // 2).reshape(nbs, -1), axis=1).reshape(jsel.shape)
        phys = jnp.where(jsel < cur, page + l * n_pool, 0).astype(jnp.int32)
        os2 = _slc_sample(q2.reshape(nbs, NSA_HEADS, HEAD_DIM).astype(BF16), slc_pages_t, phys.reshape(-1),
                          jsel.reshape(-1), kvs2.reshape(nbs, KV_COLS, 1), biasblk, cur, past)
        kw_full = jnp.concatenate([state_win_kv[l].reshape(nbs, w_buf, KV_COLS), kvw2[:, None, :]], axis=1)
        kw_in = jnp.pad(kw_full, ((0, 0), (0, n_w - (w_buf + 1)), (0, 0)))
        ow2, _ = _dense1(q_pad, kw_in, bm_w)
        xs = _layer_out(xs, gate[nbp:], y_ssd2, z_s2, _take_group_half(oc2), os2.reshape(nbs, NSA_WIDTH),
                        _take_group_half(ow2), gl2, z_a2, ssd_norm_w[l], nsa_norm_w[l], w_out_bf,
                        final_norm_g, 1, final)
        outs["sc"].append(kvc2.reshape((nbs, 1) + kv_row))
        outs["ss"].append(kvs2.reshape((nbs, 1) + kv_row))
        outs["sw"].append(kw_full[:, -w_buf:].reshape((nbs, w_buf) + kv_row))
        outs["sconv"].append(jnp.concatenate([state_conv[l][:, 1:], xbc2[:, None, :]], axis=1))
        outs["sssm"].append(h2)

    st = lambda k: jnp.stack(outs[k])
    return (xp.reshape(nbp, lp, D_MODEL), xs.reshape(nbs, 1, D_MODEL),
            st("pc"), st("ps"), st("pw"), st("pconv"), st("pssm"),
            st("sc"), st("ss"), st("sw"), st("sconv"), st("sssm"))
```

```python
import functools
import math

import numpy as np
import jax
import jax.numpy as jnp
from jax import lax
from jax.experimental import pallas as pl
from jax.experimental.pallas import tpu as pltpu

F32 = jnp.float32
BF16 = jnp.bfloat16
HIGHEST = lax.Precision.HIGHEST

D_MODEL = 1024
HEAD_DIM = 64
SSD_WIDTH = 1024
SSD_HEADS = 16
SSD_GROUPS = 2
SSD_STATE = 128
SSD_CONV = 4
SSD_CHUNK = 256
CONV_DIM = SSD_WIDTH + 2 * SSD_GROUPS * SSD_STATE
NSA_WIDTH = 1024
NSA_HEADS = 16
NSA_KV_HEADS = 2
NSA_REP = NSA_HEADS // NSA_KV_HEADS
CMP_BLOCK = 32
CMP_STRIDE = 16
CMP_HID = 2 * HEAD_DIM
SLC_BLOCK = 64
SLC_TOP = 16
WINDOW = 512
Q_BLOCK = 64
REL_BUCKETS = 32
REL_MAX_DIST = 128
NORM_EPS = 1e-6
KV_COLS = 2 * NSA_KV_HEADS * HEAD_DIM
PAGE_SIZE = 128
NEG = -1e30
LOG2E = 1.4426950408889634

LANE = 128
HALF = LANE // 2
GROUP_W = NSA_REP * HEAD_DIM
ROWS = NSA_REP * Q_BLOCK
SLC_NEAR = 4 * SLC_BLOCK
SLC_PAD = SLC_NEAR
SLC_NEARW = SLC_NEAR + LANE
ONES_ROWS = 8
PAD_FLAG_COL = LANE - 1
SLC_BACK = 2 * SLC_NEAR
WIN_W = WINDOW + 2 * Q_BLOCK
WIN_PAD = WINDOW
MAX_PAGES_PER_STEP = 32
PROJ_ROWS = 256
MOD_COLS = 512
VMEM_LIMIT = 48 * 1024 * 1024

_SEG_NAMES = ("z_s", "xbc", "dt", "q", "kvc", "kvs", "kvw", "gl", "z_a")
_SEG_SIZES = (SSD_WIDTH, CONV_DIM, SSD_HEADS, NSA_WIDTH, KV_COLS, KV_COLS, KV_COLS, 3 * NSA_HEADS, NSA_WIDTH)
_SEG_PAD = tuple(-(-s // LANE) * LANE for s in _SEG_SIZES)
_SEG_OFF = tuple(int(o) for o in np.cumsum((0,) + _SEG_PAD[:-1]))
IN_PAD = int(sum(_SEG_PAD))


def _sigmoid(x):
    return 1.0 / (1.0 + jnp.exp(-x))


def _silu(x):
    return x * _sigmoid(x)


def _dot32(a, b):
    return jnp.dot(a, b, precision=HIGHEST, preferred_element_type=F32)


def _split_bf16(x, terms):
    parts = []
    for _ in range(terms):
        p = x.astype(BF16)
        parts.append(p)
        x = x - p.astype(F32)
    return parts


def _expand2(x, sel2_bf16):
    hi, lo = _split_bf16(x, 2)
    return jnp.dot(jnp.concatenate([hi, lo], axis=1), sel2_bf16, preferred_element_type=F32)


def _sel_dot(sel_bf16, x, terms):
    return sum(jnp.dot(sel_bf16, p, preferred_element_type=F32) for p in _split_bf16(x, terms))


def _dot_nt(a, b):
    return lax.dot_general(a, b, (((1,), (1,)), ((), ())), preferred_element_type=F32)


def _bucket_table():
    n = np.arange(REL_MAX_DIST + 1)
    max_exact = REL_BUCKETS // 2
    nf = np.maximum(n, 1).astype(np.float32)
    large = max_exact + (np.log(nf / np.float32(max_exact)) / np.float32(math.log(REL_MAX_DIST / max_exact))
                         * np.float32(REL_BUCKETS - max_exact)).astype(np.int32)
    large = np.minimum(large, REL_BUCKETS - 1)
    return np.where(n < max_exact, n, large).astype(np.int32)


_BUCKETS = _bucket_table()


def _bias_of_dist(rel_bias, dist):
    idx = _BUCKETS[np.clip(dist, 0, REL_MAX_DIST)]
    out = jnp.take(rel_bias.astype(F32), jnp.asarray(idx.reshape(-1)), axis=0)
    return out.T.reshape((NSA_HEADS,) + dist.shape)


def _toeplitz_bias(rel_bias, rows, cols, c0):
    n = rows + cols
    d = np.arange(n) + c0 - (rows - 1)
    v = jnp.take(rel_bias.astype(F32), jnp.asarray(_BUCKETS[np.clip(d, 0, REL_MAX_DIST)]), axis=0).T
    flat = jnp.tile(v, (1, rows))[:, :rows * (n - 1)]
    return flat.reshape(NSA_HEADS, rows, n - 1)[:, :, rows - 1:rows - 1 + cols]


def _mod_kernel(c_ref, w_ref, b_ref, o_ref):
    o_ref[...] = _dot32(_silu(c_ref[...]), w_ref[...]) + b_ref[...]


def _modulation(c, w, b):
    m, d = c.shape
    n = w.shape[1]
    tn = MOD_COLS
    return pl.pallas_call(
        _mod_kernel,
        out_shape=jax.ShapeDtypeStruct((m, n), F32),
        grid=(n // tn,),
        in_specs=[pl.BlockSpec((m, d), lambda j: (0, 0)),
                  pl.BlockSpec((d, tn), lambda j: (0, j)),
                  pl.BlockSpec((1, tn), lambda j: (0, j))],
        out_specs=pl.BlockSpec((m, tn), lambda j: (0, j)),
        name="adaln_mod",
    )(c, w, b.reshape(1, n))


def _inproj_kernel(x_ref, g_ref, sc_ref, sh_ref, w_ref, *out_refs, per_row):
    x = x_ref[...]
    xn = x * lax.rsqrt(jnp.mean(x * x, axis=-1, keepdims=True) + NORM_EPS)
    sc = sc_ref[...] if per_row else sc_ref[0]
    sh = sh_ref[...] if per_row else sh_ref[0]
    h = ((xn * g_ref[...]) * (1.0 + sc) + sh).astype(BF16)
    for name, off, width, ref in zip(_SEG_NAMES, _SEG_OFF, _SEG_PAD, out_refs):
        r = jnp.dot(h, w_ref[:, off:off + width], preferred_element_type=F32)
        if name == "q":
            r = r * (HEAD_DIM ** -0.5)
            if not per_row:
                qt_ref = out_refs[-1]
                for t in range(r.shape[0] // Q_BLOCK):
                    for g in range(NSA_KV_HEADS):
                        qt_ref[t, g] = _stack_q_f32(r[t * Q_BLOCK:(t + 1) * Q_BLOCK] * LOG2E, g).T.astype(BF16)
        ref[...] = r


def _in_projection(x2d, g, scale, shift, w_pad, rows_per_batch):
    m = x2d.shape[0]
    per_row = rows_per_batch == 1
    tm = m if per_row else PROJ_ROWS
    outs = tuple(jax.ShapeDtypeStruct((m, w), F32) for w in _SEG_PAD)
    out_specs = tuple(pl.BlockSpec((tm, w), lambda i: (i, 0)) for w in _SEG_PAD)
    if per_row:
        mod_spec = pl.BlockSpec((tm, D_MODEL), lambda i: (0, 0))
        sc, sh = scale, shift
    else:
        mod_spec = pl.BlockSpec((1, 1, D_MODEL), lambda i: ((i * tm) // rows_per_batch, 0, 0))
        sc, sh = scale[:, None, :], shift[:, None, :]
        outs += (jax.ShapeDtypeStruct((m // Q_BLOCK, NSA_KV_HEADS, LANE, ROWS), BF16),)
        out_specs += (pl.BlockSpec((tm // Q_BLOCK, NSA_KV_HEADS, LANE, ROWS), lambda i: (i, 0, 0, 0)),)
    return pl.pallas_call(
        functools.partial(_inproj_kernel, per_row=per_row),
        out_shape=outs,
        grid=(m // tm,),
        in_specs=[pl.BlockSpec((tm, D_MODEL), lambda i: (i, 0)),
                  pl.BlockSpec((1, D_MODEL), lambda i: (0, 0)),
                  mod_spec, mod_spec,
                  pl.BlockSpec((D_MODEL, IN_PAD), lambda i: (0, 0))],
        out_specs=out_specs,
        compiler_params=pltpu.CompilerParams(vmem_limit_bytes=VMEM_LIMIT),
        name="in_projection",
    )(x2d, g.reshape(1, D_MODEL), sc, sh, w_pad)


def _softplus(x):
    return jnp.maximum(x, 0.0) + jnp.log(1.0 + jnp.exp(-jnp.abs(x)))


def _ssd_kernel(xbc_ref, dt_ref, cw_ref, cb_ref, dtb_ref, alog_ref, dsk_ref, e_ref, tril_ref,
                y_ref, hfin_ref, xe_sc, st_sc):
    c = pl.program_id(1)
    q = SSD_CHUNK
    n_pairs = SSD_HEADS // 2

    @pl.when(c == 0)
    def _():
        xe_sc[0:8, :] = jnp.zeros((8, CONV_DIM), F32)
        st_sc[...] = jnp.zeros(st_sc.shape, F32)

    xe_sc[8:8 + q, :] = xbc_ref[0]
    acc = cb_ref[...] + cw_ref[0:1, :] * xe_sc[5:5 + q, :]
    for k in range(1, SSD_CONV):
        acc = acc + cw_ref[k:k + 1, :] * xe_sc[5 + k:5 + k + q, :]
    u = _silu(acc)
    xe_sc[0:8, :] = xe_sc[q:q + 8, :]

    xs = u[:, :SSD_WIDTH]
    gn = SSD_GROUPS * SSD_STATE
    bm = u[:, SSD_WIDTH:SSD_WIDTH + gn]
    cm = u[:, SSD_WIDTH + gn:]

    dt = _softplus(dt_ref[0] + dtb_ref[...])
    a = dt * (-jnp.exp(alog_ref[...]))
    cs = _sel_dot(tril_ref[...], a, 3)
    cs_t = cs.T
    cs_last = cs[q - 1:q, :]
    e = e_ref[...]
    dt_e = _expand2(dt, e)
    w_e = _expand2(dt * jnp.exp(cs_last - cs), e)
    ecs_e = _expand2(jnp.exp(cs), e)
    tot_e = _expand2(jnp.broadcast_to(jnp.exp(cs_last), (8, LANE)), e)[0:1, :]
    xdt = (xs * dt_e).astype(BF16)
    xw = (xs * w_e).astype(BF16)

    li = lax.broadcasted_iota(jnp.int32, (q, q), 0)
    si = lax.broadcasted_iota(jnp.int32, (q, q), 1)
    tri = li >= si
    lane = lax.broadcasted_iota(jnp.int32, (q, LANE), 1)

    for g in range(SSD_GROUPS):
        cg = cm[:, g * SSD_STATE:(g + 1) * SSD_STATE].astype(BF16)
        bg = bm[:, g * SSD_STATE:(g + 1) * SSD_STATE]
        cb = _dot_nt(cg, bg.astype(BF16))
        bg_t = bg.T.astype(BF16)
        for jp in range(n_pairs // SSD_GROUPS):
            j = g * (n_pairs // SSD_GROUPS) + jp
            sl = slice(j * LANE, (j + 1) * LANE)
            xdt_p = xdt[:, sl]
            ys = []
            for hh in (2 * j, 2 * j + 1):
                diff = cs[:, hh:hh + 1] - cs_t[hh:hh + 1, :]
                lmat = jnp.exp(jnp.where(tri, diff, NEG))
                ys.append(jnp.dot((cb * lmat).astype(BF16), xdt_p, preferred_element_type=F32))
            y_diag = jnp.where(lane < HALF, ys[0], ys[1])
            st = st_sc[j]
            y_off = jnp.dot(cg, st.astype(BF16), preferred_element_type=F32) * ecs_e[:, sl]
            y_ref[0, :, sl] = y_diag + y_off + xs[:, sl] * dsk_ref[:, sl]
            new = jnp.dot(bg_t, xw[:, sl], preferred_element_type=F32)
            st_sc[j] = st * tot_e[:, sl] + new

    @pl.when(c == pl.num_programs(1) - 1)
    def _():
        for j in range(n_pairs):
            hfin_ref[0, j * LANE:(j + 1) * LANE, :] = st_sc[j].T


def _head_expand():
    e = np.zeros((LANE, SSD_WIDTH), np.float32)
    for h in range(SSD_HEADS):
        e[h, h * HEAD_DIM:(h + 1) * HEAD_DIM] = 1.0
    return jnp.asarray(e)


def _pad_lanes(v):
    return jnp.pad(v.astype(F32), (0, LANE - v.shape[0])).reshape(1, LANE)


def _ssd_prompt(xbc, dt, conv_w, conv_b, dt_bias, a_log, d_skip):
    b, l, _ = xbc.shape
    nc = l // SSD_CHUNK
    full = lambda shape: pl.BlockSpec(shape, lambda i, c: (0,) * len(shape))
    y, hfin = pl.pallas_call(
        _ssd_kernel,
        out_shape=(jax.ShapeDtypeStruct((b, l, SSD_WIDTH), F32),
                   jax.ShapeDtypeStruct((b, SSD_HEADS * HEAD_DIM, SSD_STATE), F32)),
        grid=(b, nc),
        in_specs=[pl.BlockSpec((1, SSD_CHUNK, CONV_DIM), lambda i, c: (i, c, 0)),
                  pl.BlockSpec((1, SSD_CHUNK, LANE), lambda i, c: (i, c, 0)),
                  full((SSD_CONV, CONV_DIM)), full((1, CONV_DIM)), full((1, LANE)), full((1, LANE)),
                  full((1, SSD_WIDTH)), full((2 * LANE, SSD_WIDTH)), full((SSD_CHUNK, SSD_CHUNK))],
        out_specs=(pl.BlockSpec((1, SSD_CHUNK, SSD_WIDTH), lambda i, c: (i, c, 0)),
                   pl.BlockSpec((1, SSD_HEADS * HEAD_DIM, SSD_STATE), lambda i, c: (i, 0, 0))),
        scratch_shapes=[pltpu.VMEM((SSD_CHUNK + 8, CONV_DIM), F32),
                        pltpu.VMEM((SSD_HEADS // 2, SSD_STATE, LANE), F32)],
        compiler_params=pltpu.CompilerParams(dimension_semantics=("arbitrary", "arbitrary"),
                                             vmem_limit_bytes=VMEM_LIMIT),
        name="ssd_prompt",
    )(xbc, dt, conv_w, conv_b.reshape(1, CONV_DIM), _pad_lanes(dt_bias), _pad_lanes(a_log),
      jnp.repeat(d_skip.astype(F32), HEAD_DIM).reshape(1, SSD_WIDTH),
      jnp.concatenate([_head_expand(), _head_expand()], axis=0).astype(BF16),
      jnp.asarray(np.tril(np.ones((SSD_CHUNK, SSD_CHUNK), np.float32)), BF16))
    return y, hfin.reshape(b, SSD_HEADS, HEAD_DIM, SSD_STATE)


def _ssd_step_kernel(xbc_ref, c0_ref, c1_ref, c2_ref, dt_ref, h0_ref, cw_ref, cb_ref, dtb_ref, alog_ref,
                     dsk_ref, e_ref, y_ref, hout_ref, xt_sc, dect_sc, bc_sc, yt_sc, xs_sc):
    b = pl.program_id(0)
    nb = xbc_ref.shape[0]
    rows = SSD_HEADS * HEAD_DIM
    gn = SSD_GROUPS * SSD_STATE

    @pl.when(b == 0)
    def _():
        acc = (cb_ref[...] + cw_ref[0:1, :] * c0_ref[...] + cw_ref[1:2, :] * c1_ref[...]
               + cw_ref[2:3, :] * c2_ref[...] + cw_ref[3:4, :] * xbc_ref[...])
        u = _silu(acc)
        xs = u[:, :SSD_WIDTH]
        dt = _softplus(dt_ref[...] + dtb_ref[...])
        dec = jnp.exp(dt * (-jnp.exp(alog_ref[...])))
        e = e_ref[...]
        xdt = xs * _dot32(dt, e)
        dec_e = _dot32(dec, e)
        pad = jnp.zeros((LANE - nb, SSD_WIDTH), F32)
        xt_sc[...] = jnp.concatenate([xdt, pad], axis=0).T
        dect_sc[...] = jnp.concatenate([dec_e, pad], axis=0).T
        bc_sc[...] = u[:, SSD_WIDTH:]
        xs_sc[...] = xs
        yt_sc[...] = jnp.zeros(yt_sc.shape, F32)

    ri = lax.broadcasted_iota(jnp.int32, (LANE, LANE), 0)
    onehot = jnp.where(ri == b, 1.0, 0.0)
    xcol = _dot32(xt_sc[...], onehot)
    dcol = _dot32(dect_sc[...], onehot)
    bc = bc_sc[pl.ds(b, 1), :]
    row = lax.broadcasted_iota(jnp.int32, (rows, SSD_STATE), 0)
    first = row < rows // SSD_GROUPS
    b_full = jnp.where(first, bc[:, 0:SSD_STATE], bc[:, SSD_STATE:gn])
    c_full = jnp.where(first, bc[:, gn:gn + SSD_STATE], bc[:, gn + SSD_STATE:])
    new = dcol * h0_ref[0] + xcol * b_full
    hout_ref[0] = new
    ycol = _dot32(new * c_full, jnp.ones((SSD_STATE, LANE), F32))
    lane = lax.broadcasted_iota(jnp.int32, (rows, LANE), 1)
    yt_sc[...] = jnp.where(lane == b, ycol, yt_sc[...])

    @pl.when(b == nb - 1)
    def _():
        y_ref[...] = yt_sc[...].T[0:nb, :] + xs_sc[...] * dsk_ref[...]


def _ssd_step(xbc, conv_state, dt, h0, conv_w, conv_b, dt_bias, a_log, d_skip):
    nb = xbc.shape[0]
    rows = SSD_HEADS * HEAD_DIM
    full = lambda shape: pl.BlockSpec(shape, lambda i: (0,) * len(shape))
    y, hout = pl.pallas_call(
        _ssd_step_kernel,
        out_shape=(jax.ShapeDtypeStruct((nb, SSD_WIDTH), F32),
                   jax.ShapeDtypeStruct((nb, rows, SSD_STATE), F32)),
        grid=(nb,),
        in_specs=[full((nb, CONV_DIM)), full((nb, CONV_DIM)), full((nb, CONV_DIM)), full((nb, CONV_DIM)),
                  full((nb, LANE)),
                  pl.BlockSpec((1, rows, SSD_STATE), lambda i: (i, 0, 0)),
                  full((SSD_CONV, CONV_DIM)), full((1, CONV_DIM)), full((1, LANE)), full((1, LANE)),
                  full((1, SSD_WIDTH)), full((LANE, SSD_WIDTH))],
        out_specs=(full((nb, SSD_WIDTH)),
                   pl.BlockSpec((1, rows, SSD_STATE), lambda i: (i, 0, 0))),
        scratch_shapes=[pltpu.VMEM((rows, LANE), F32), pltpu.VMEM((rows, LANE), F32),
                        pltpu.VMEM((nb, 2 * SSD_GROUPS * SSD_STATE), F32),
                        pltpu.VMEM((rows, LANE), F32), pltpu.VMEM((nb, SSD_WIDTH), F32)],
        compiler_params=pltpu.CompilerParams(dimension_semantics=("arbitrary",)),
        name="ssd_step",
    )(xbc, conv_state[:, 0], conv_state[:, 1], conv_state[:, 2], dt, h0.reshape(nb, rows, SSD_STATE),
      conv_w, conv_b.reshape(1, CONV_DIM), _pad_lanes(dt_bias), _pad_lanes(a_log),
      jnp.repeat(d_skip.astype(F32), HEAD_DIM).reshape(1, SSD_WIDTH), _head_expand())
    return y, hout.reshape(nb, SSD_HEADS, HEAD_DIM, SSD_STATE)


def _compress_kernel(pt_ref, *refs, transposed, n_pg):
    segs = PAGE_SIZE // CMP_STRIDE
    if transposed:
        perm_ref, w1_ref, pe_ref, w2_ref, out_ref, sh_sc, pe_sc, xs_sc = refs[n_pg:]
        for i in range(n_pg):
            xs_sc[i] = _dot_nt(perm_ref[...], refs[i][0].astype(BF16))

        def token_rows(k, o):
            return jnp.concatenate([xs_sc[i, o * segs:(o + 1) * segs, k * LANE:(k + 1) * LANE]
                                    for i in range(n_pg)], axis=0)
    else:
        w1_ref, pe_ref, w2_ref, out_ref, sh_sc, pe_sc = refs[2 * n_pg:]

        def token_rows(k, o):
            return jnp.concatenate([refs[k * n_pg + i][0, pl.ds(o, segs, stride=CMP_STRIDE), :]
                                    for i in range(n_pg)], axis=0)
    s = pl.program_id(1)
    rows = n_pg * segs
    hid2 = NSA_KV_HEADS * CMP_HID

    @pl.when(s == 0)
    def _():
        sh_sc[:, 0:8, :] = jnp.zeros((2, 8, hid2), F32)
        for k in range(2):
            t = jnp.zeros((8, 2 * hid2), F32)
            for o in range(0, CMP_STRIDE, 2):
                pe2 = jnp.concatenate([pe_ref[o, k], pe_ref[o + 1, k]], axis=1)
                t = t + jnp.dot(pe2.astype(BF16), w1_ref[o // 2, k], preferred_element_type=F32)
            pe_sc[k] = jnp.broadcast_to(t[0:1, 0:hid2] + t[1:2, hid2:], (8, hid2))

    for k in range(2):
        acc = jnp.zeros((rows, 2 * hid2), F32)
        for o in range(0, CMP_STRIDE, 2):
            xo = jnp.concatenate([token_rows(k, o), token_rows(k, o + 1)], axis=1)
            acc = acc + jnp.dot(xo.astype(BF16), w1_ref[o // 2, k], preferred_element_type=F32)
        sh_sc[k, 8:8 + rows, :] = acc[:, 0:hid2]
        pre = acc[:, hid2:] + sh_sc[k, 7:7 + rows, :] + pe_sc[k, 0:1, :]
        sh_sc[k, 0:8, :] = sh_sc[k, rows:rows + 8, :]
        out_ref[0, :, k * LANE:(k + 1) * LANE] = jnp.dot(_silu(pre).astype(BF16), w2_ref[k],
                                                         preferred_element_type=F32)


def _compress_weights(cmp_pe, cmp_w1, cmp_w2):
    span = CMP_BLOCK // CMP_STRIDE
    w1s = cmp_w1.astype(F32).reshape(2, span, CMP_STRIDE, HEAD_DIM, CMP_HID)
    z = jnp.zeros((2, span, CMP_STRIDE, HEAD_DIM, CMP_HID), F32)
    top = jnp.concatenate([w1s, z], axis=-1)
    bot = jnp.concatenate([z, w1s], axis=-1)
    bd = jnp.concatenate([top, bot], axis=-2)
    w1 = jnp.transpose(bd, (2, 0, 3, 1, 4)).reshape(CMP_STRIDE, 2, LANE, span * 2 * CMP_HID).astype(BF16)
    w1 = jnp.transpose(w1.reshape(CMP_STRIDE // 2, 2, 2, LANE, span * 2 * CMP_HID), (0, 2, 1, 3, 4))
    w1 = w1.reshape(CMP_STRIDE // 2, 2, 2 * LANE, span * 2 * CMP_HID)
    pe = cmp_pe.astype(F32).reshape(2, span, CMP_STRIDE, HEAD_DIM)
    pe = jnp.transpose(pe, (2, 0, 1, 3))
    pe = jnp.concatenate([pe, pe], axis=-1)
    pe = jnp.pad(pe, ((0, 0), (0, 0), (0, 8 - span), (0, 0)))
    w2 = cmp_w2.astype(F32)
    z2 = jnp.zeros_like(w2)
    w2bd = jnp.concatenate([jnp.concatenate([w2, z2], axis=-1), jnp.concatenate([z2, w2], axis=-1)],
                           axis=-2).astype(BF16)
    return w1, pe, w2bd


def _compress(pages_arr, page_ids, cweights, transposed):
    nb, n_pages = page_ids.shape
    n_pg = math.gcd(n_pages, MAX_PAGES_PER_STEP)
    steps = n_pages // n_pg
    segs = PAGE_SIZE // CMP_STRIDE
    rows = n_pg * segs
    w1, pe, w2bd = cweights
    hid2 = NSA_KV_HEADS * CMP_HID
    page_of = lambda b, s, pt, i: pt[(b * steps + s) * n_pg + i]
    scratch = [pltpu.VMEM((2, rows + 8, hid2), F32), pltpu.VMEM((2, 8, hid2), F32)]
    if transposed:
        page_specs = [pl.BlockSpec((1, KV_COLS, PAGE_SIZE), lambda b, s, pt, i=i: (page_of(b, s, pt, i), 0, 0))
                      for i in range(n_pg)]
        scratch.append(pltpu.VMEM((n_pg, PAGE_SIZE, KV_COLS), F32))
        perm = np.zeros((PAGE_SIZE, PAGE_SIZE), np.float32)
        for o in range(CMP_STRIDE):
            for sg in range(segs):
                perm[o * segs + sg, sg * CMP_STRIDE + o] = 1.0
        extra, extra_specs = [jnp.asarray(perm, BF16)], [pl.BlockSpec((PAGE_SIZE, PAGE_SIZE), lambda b, s, pt: (0, 0))]
    else:
        extra, extra_specs = [], []
        page_specs = [pl.BlockSpec((1, PAGE_SIZE, LANE), lambda b, s, pt, i=i, k=k: (page_of(b, s, pt, i), 0, k))
                      for k in range(2) for i in range(n_pg)]
    full = lambda shape: pl.BlockSpec(shape, lambda b, s, pt: (0,) * len(shape))
    return pl.pallas_call(
        functools.partial(_compress_kernel, transposed=transposed, n_pg=n_pg),
        out_shape=jax.ShapeDtypeStruct((nb, n_pages * segs, KV_COLS), F32),
        grid_spec=pltpu.PrefetchScalarGridSpec(
            num_scalar_prefetch=1,
            grid=(nb, steps),
            in_specs=page_specs + extra_specs + [full(w1.shape), full(pe.shape), full(w2bd.shape)],
            out_specs=pl.BlockSpec((1, rows, KV_COLS), lambda b, s, pt: (b, s, 0)),
            scratch_shapes=scratch),
        compiler_params=pltpu.CompilerParams(dimension_semantics=("arbitrary", "arbitrary"),
                                             vmem_limit_bytes=VMEM_LIMIT),
        name="nsa_compress",
    )(page_ids.reshape(-1), *([pages_arr] * len(page_specs)), *extra, w1, pe, w2bd)


def _overlap_matrix(n_rows, n_blocks, n_cols):
    m = np.arange(n_rows)[:, None]
    j = np.arange(n_cols)[None, :]
    cs = (m - 1) * CMP_STRIDE
    ov = (m >= 1) & (j < n_blocks) & (cs < j * SLC_BLOCK + SLC_BLOCK) & (cs + CMP_BLOCK > j * SLC_BLOCK)
    return jnp.asarray(ov.astype(np.float32))


def _stack_q(q, g):
    return _stack_q_f32(q, g).astype(BF16)


def _stack_q_f32(q, g):
    lane = lax.broadcasted_iota(jnp.int32, (Q_BLOCK, LANE), 1)
    keep = (lane < HALF) if g == 0 else (lane >= HALF)
    parts = []
    for jp in range(NSA_REP // 2):
        j = g * (NSA_REP // 2) + jp
        slab = q[:, j * LANE:(j + 1) * LANE]
        rolled = pltpu.roll(slab, HALF, 1)
        first, second = (slab, rolled) if g == 0 else (rolled, slab)
        parts.append(jnp.where(keep, first, 0.0))
        parts.append(jnp.where(keep, second, 0.0))
    return jnp.concatenate(parts, axis=0)


def _unstack_o(acc, g):
    lane = lax.broadcasted_iota(jnp.int32, (Q_BLOCK, LANE), 1)
    outs = []
    for jp in range(NSA_REP // 2):
        a = acc[(2 * jp) * Q_BLOCK:(2 * jp + 1) * Q_BLOCK]
        b = acc[(2 * jp + 1) * Q_BLOCK:(2 * jp + 2) * Q_BLOCK]
        if g == 0:
            outs.append(jnp.where(lane < HALF, a, pltpu.roll(b, HALF, 1)))
        else:
            outs.append(jnp.where(lane < HALF, pltpu.roll(a, HALF, 1), b))
    return jnp.concatenate(outs, axis=1)


def _tile8(x):
    return jnp.concatenate([x] * NSA_REP, axis=0)


def _rank_rows(imp, n_valid):
    sub = 8
    n_rank = -(-n_valid // sub) * sub
    chunks = [imp[c:c + sub] for c in range(0, n_rank, sub)]
    ranks = [jnp.zeros(ch.shape, F32) for ch in chunks]
    jrow = lax.broadcasted_iota(jnp.int32, chunks[0].shape, 0)
    for k in range(n_valid):
        rk = imp[k:k + 1, :]
        for c, ch in enumerate(chunks):
            if c * sub > k:
                ahead = rk >= ch
            elif c * sub + sub - 1 < k:
                ahead = rk > ch
            else:
                ahead = (rk > ch) | ((rk == ch) & (jrow + c * sub > k))
            ranks[c] = ranks[c] + jnp.where(ahead, 1.0, 0.0)
    rest = jnp.full((imp.shape[0] - n_rank, imp.shape[1]), float(n_valid), F32)
    return jnp.concatenate(ranks + [rest], axis=0)


def _cmp_kernel(q_ref, kv_ref, pb_ref, ov_ref, o_ref, sel_ref, bias_sc, *, n_keys, n_blocks):
    i = pl.program_id(0)
    qi = lax.broadcasted_iota(jnp.int32, (Q_BLOCK, n_keys), 0)
    mi = lax.broadcasted_iota(jnp.int32, (Q_BLOCK, n_keys), 1)
    qpos = Q_BLOCK * i + qi

    @pl.when(pl.program_id(1) == 0)
    def _():
        valid = (mi >= 1) & (CMP_STRIDE * mi + CMP_STRIDE - 1 <= qpos)
        for h in range(NSA_HEADS):
            bias_sc[h] = jnp.where(valid, pltpu.roll(pb_ref[h], (4 * i + 4) % n_keys, 1), NEG)

    rowvalid8 = _tile8(jnp.where(qpos[:, 0:1] >= CMP_BLOCK - 1, 1.0, 0.0))
    jj = lax.broadcasted_iota(jnp.int32, (LANE, LANE), 1)
    ov = ov_ref[...].astype(BF16)
    forced = (jj == 0) | (jj == i) | (jj == i - 1)
    for bb in range(q_ref.shape[0]):
        q = q_ref[bb] * LOG2E
        kv = kv_ref[bb]
        kc = kv[:, 0:LANE].astype(BF16)
        vc = kv[:, LANE:].astype(BF16)
        psum = []
        for g in range(NSA_KV_HEADS):
            s = _dot_nt(_stack_q(q, g), kc) + bias_sc[g * NSA_REP:(g + 1) * NSA_REP].reshape(ROWS, n_keys)
            p = jnp.exp2(s - jnp.max(s, axis=-1, keepdims=True))
            pc = p / jnp.sum(p, axis=-1, keepdims=True) * rowvalid8
            o_ref[bb, :, g * GROUP_W:(g + 1) * GROUP_W] = _unstack_o(
                jnp.dot(pc.astype(BF16), vc, preferred_element_type=F32), g)
            ps = pc[0:Q_BLOCK]
            for r in range(1, NSA_REP):
                ps = ps + pc[r * Q_BLOCK:(r + 1) * Q_BLOCK]
            psum.append(ps)
        ps = jnp.concatenate(psum, axis=0)
        hi = ps.astype(BF16)
        lo = (ps - hi.astype(F32)).astype(BF16)
        imp = (jnp.dot(hi, ov, preferred_element_type=F32)
               + jnp.dot(lo, ov, preferred_element_type=F32))
        imp = jnp.where(forced, 1e6, jnp.where(jj <= i, imp, -1e6))
        imp = jnp.where(jj < n_blocks, imp, -2e6)
        rank = _rank_rows(imp.T, n_blocks)
        sel_ref[bb, 0] = jnp.where(rank < SLC_TOP, 1.0, 0.0).astype(BF16)


def _cmp_prompt(q, kvc_cmp, pattern, ov):
    b, l, _ = q.shape
    n_keys = kvc_cmp.shape[1]
    n_blocks = l // SLC_BLOCK
    nb = 2 if b % 2 == 0 else 1
    return pl.pallas_call(
        functools.partial(_cmp_kernel, n_keys=n_keys, n_blocks=n_blocks),
        out_shape=(jax.ShapeDtypeStruct((b, l, NSA_WIDTH), F32),
                   jax.ShapeDtypeStruct((b, l // Q_BLOCK, LANE, LANE), BF16)),
        grid=(l // Q_BLOCK, b // nb),
        in_specs=[pl.BlockSpec((nb, Q_BLOCK, NSA_WIDTH), lambda i, bi: (bi, i, 0)),
                  pl.BlockSpec((nb, n_keys, KV_COLS), lambda i, bi: (bi, 0, 0)),
                  pl.BlockSpec(pattern.shape, lambda i, bi: (0, 0, 0)),
                  pl.BlockSpec(ov.shape, lambda i, bi: (0, 0))],
        out_specs=(pl.BlockSpec((nb, Q_BLOCK, NSA_WIDTH), lambda i, bi: (bi, i, 0)),
                   pl.BlockSpec((nb, 1, LANE, LANE), lambda i, bi: (bi, i, 0, 0))),
        scratch_shapes=[pltpu.VMEM(pattern.shape, F32)],
        compiler_params=pltpu.CompilerParams(dimension_semantics=("arbitrary", "arbitrary"),
                                             vmem_limit_bytes=VMEM_LIMIT),
        name="nsa_cmp_prompt",
    )(q, kvc_cmp, pattern, ov)


def _flash_update_t(st, vt, m, acc):
    m_new = jnp.maximum(m, jnp.max(st, axis=0, keepdims=True))
    p = jnp.exp2(st - m_new)
    acc = jnp.exp2(m - m_new) * acc + jnp.dot(vt, p.astype(BF16), preferred_element_type=F32)
    return m_new, acc


def _finish_t(acc):
    o = (acc[0:HEAD_DIM] / acc[HEAD_DIM:HEAD_DIM + 1]).T
    return jnp.concatenate([o[r * Q_BLOCK:(r + 1) * Q_BLOCK] for r in range(NSA_REP)], axis=1)


def _slc_kernel(qt_ref, sel_ref, kx_ref, vt_ref, tnt_ref, tile_ref, o_ref, far_sc, near_sc, sta_sc, stb_sc, stn_sc):
    i = pl.program_id(1)
    n_far = (jnp.maximum(i - 3, 0) + 3) // 4
    sel = sel_ref[0, 0]
    jrow = lax.broadcasted_iota(jnp.int32, (LANE, ROWS), 0)
    a = ((i + 1) // 2) * LANE
    delta = Q_BLOCK * (i + 1) - a
    groups = range(NSA_KV_HEADS)
    for g in groups:
        qs_t = qt_ref[0, g]
        hit = jnp.dot(sel, tile_ref[g], preferred_element_type=F32) > 0.5
        near_sc[g] = jnp.concatenate([qs_t, jnp.where(hit & (jrow != PAD_FLAG_COL), 0.0, NEG).astype(BF16)], axis=0)
        far_sc[g] = jnp.concatenate([qs_t, jnp.where(hit & (jrow < i - 3), 0.0, NEG).astype(BF16)], axis=0)

    def far_scores(t, dst):
        start = pl.multiple_of(SLC_PAD + SLC_NEAR * t, SLC_NEAR)
        kx = kx_ref[0, pl.ds(start, SLC_NEAR), :]
        for g in groups:
            dst[g] = jnp.dot(kx, far_sc[g], preferred_element_type=F32)

    def far_softmax(t, src, carry):
        start = pl.multiple_of(SLC_PAD + SLC_NEAR * t, SLC_NEAR)
        return tuple(_flash_update_t(src[g], vt_ref[0, g, :, pl.ds(start, SLC_NEAR)], *carry[g]) for g in groups)

    def far_pair(u, carry):
        far_scores(2 * u + 1, stb_sc)
        carry = far_softmax(2 * u, sta_sc, carry)
        far_scores(2 * u + 2, sta_sc)
        return far_softmax(2 * u + 1, stb_sc, carry)

    far_scores(0, sta_sc)
    start = pl.multiple_of(a, LANE)
    kx = kx_ref[0, pl.ds(start, SLC_NEARW), :]
    for g in groups:
        stn_sc[g] = jnp.dot(kx, near_sc[g], preferred_element_type=F32)
    init = (jnp.full((1, ROWS), NEG, F32), jnp.zeros((HEAD_DIM + ONES_ROWS, ROWS), F32))
    carry = lax.fori_loop(0, (n_far + 1) // 2, far_pair, (init, init))

    for g in groups:
        st = stn_sc[g] + tnt_ref[delta // Q_BLOCK, g]
        _, acc = _flash_update_t(st, vt_ref[0, g, :, pl.ds(start, SLC_NEARW)], *carry[g])
        o_ref[0, :, g * GROUP_W:(g + 1) * GROUP_W] = _finish_t(acc)


def _slc_prompt(qt, sel, kx_pad, vt_pad, tnt, tile):
    b, n_tiles = sel.shape[:2]
    l = n_tiles * Q_BLOCK
    lp = kx_pad.shape[1]
    tq = Q_BLOCK
    n_chains = NSA_KV_HEADS
    return pl.pallas_call(
        _slc_kernel,
        out_shape=jax.ShapeDtypeStruct((b, l, NSA_WIDTH), F32),
        grid=(b, n_tiles),
        in_specs=[pl.BlockSpec((1, NSA_KV_HEADS, LANE, ROWS), lambda bi, i: (bi * n_tiles + i, 0, 0, 0)),
                  pl.BlockSpec((1, 1, LANE, LANE), lambda bi, i: (bi, i, 0, 0)),
                  pl.BlockSpec((1, lp, 2 * LANE), lambda bi, i: (bi, 0, 0)),
                  pl.BlockSpec((1, NSA_KV_HEADS, HEAD_DIM + ONES_ROWS, lp), lambda bi, i: (bi, 0, 0, 0)),
                  pl.BlockSpec(tnt.shape, lambda bi, i: (0, 0, 0, 0)),
                  pl.BlockSpec(tile.shape, lambda bi, i: (0, 0, 0))],
        out_specs=pl.BlockSpec((1, tq, NSA_WIDTH), lambda bi, i: (bi, i, 0)),
        scratch_shapes=[pltpu.VMEM((n_chains, 2 * LANE, ROWS), BF16),
                        pltpu.VMEM((n_chains, 2 * LANE, ROWS), BF16),
                        pltpu.VMEM((n_chains, SLC_NEAR, ROWS), F32),
                        pltpu.VMEM((n_chains, SLC_NEAR, ROWS), F32),
                        pltpu.VMEM((n_chains, SLC_NEARW, ROWS), F32)],
        compiler_params=pltpu.CompilerParams(vmem_limit_bytes=VMEM_LIMIT),
        name="nsa_slc_prompt",
    )(qt, sel, kx_pad, vt_pad, tnt, tile)


def _group_tile():
    t = np.zeros((NSA_KV_HEADS, LANE, ROWS), np.float32)
    for g in range(NSA_KV_HEADS):
        for r in range(NSA_REP):
            for qq in range(Q_BLOCK):
                t[g, g * Q_BLOCK + qq, r * Q_BLOCK + qq] = 1.0
    return jnp.asarray(t, BF16)


def _win_kernel(qt_ref, kx_ref, vt_ref, tw_ref, o_ref, st_sc):
    i2 = pl.program_id(1)
    start = pl.multiple_of(i2 * LANE, LANE)
    kx = kx_ref[0, pl.ds(start, WIN_W), :]
    groups = range(NSA_KV_HEADS)
    row = lax.broadcasted_iota(jnp.int32, (LANE, 2 * ROWS), 0)
    pad_rows = jnp.where(row == 0, NEG, 0.0).astype(BF16)
    for g in groups:
        qs_t = jnp.concatenate([qt_ref[h, g] for h in range(2)], axis=1)
        qx = jnp.concatenate([qs_t, pad_rows], axis=0)
        st_sc[g] = jnp.dot(kx, qx, preferred_element_type=F32)
    for g in groups:
        st = st_sc[g] + tw_ref[g]
        p = jnp.exp2(st - jnp.max(st, axis=0, keepdims=True))
        acc = jnp.dot(vt_ref[0, g, :, pl.ds(start, WIN_W)], p.astype(BF16), preferred_element_type=F32)
        for h in range(2):
            o_ref[0, h * Q_BLOCK:(h + 1) * Q_BLOCK, g * GROUP_W:(g + 1) * GROUP_W] = _finish_t(
                acc[:, h * ROWS:(h + 1) * ROWS])


def _win_prompt(qt, kx_pad, vt_pad, tw):
    b = kx_pad.shape[0]
    l = qt.shape[0] // b * Q_BLOCK
    lp = kx_pad.shape[1]
    tq = 2 * Q_BLOCK
    steps = l // tq
    return pl.pallas_call(
        _win_kernel,
        out_shape=jax.ShapeDtypeStruct((b, l, NSA_WIDTH), F32),
        grid=(b, steps),
        in_specs=[pl.BlockSpec((2, NSA_KV_HEADS, LANE, ROWS), lambda bi, i: (bi * steps + i, 0, 0, 0)),
                  pl.BlockSpec((1, lp, 2 * LANE), lambda bi, i: (bi, 0, 0)),
                  pl.BlockSpec((1, NSA_KV_HEADS, HEAD_DIM + ONES_ROWS, lp), lambda bi, i: (bi, 0, 0, 0)),
                  pl.BlockSpec(tw.shape, lambda bi, i: (0, 0, 0))],
        out_specs=pl.BlockSpec((1, tq, NSA_WIDTH), lambda bi, i: (bi, i, 0)),
        scratch_shapes=[pltpu.VMEM((NSA_KV_HEADS, WIN_W, 2 * ROWS), F32)],
        compiler_params=pltpu.CompilerParams(vmem_limit_bytes=VMEM_LIMIT),
        name="nsa_win_prompt",
    )(qt, kx_pad, vt_pad, tw)


def _dense1_kernel(q_ref, k_ref, v_ref, bm_ref, o_ref, ps_ref):
    k = k_ref[0].astype(BF16)
    v = v_ref[0].astype(BF16)
    for g in range(NSA_KV_HEADS):
        rows = slice(g * NSA_REP, (g + 1) * NSA_REP)
        s = _dot_nt(q_ref[0, rows, :], k) + bm_ref[rows, :]
        p = jnp.exp(s - jnp.max(s, axis=-1, keepdims=True))
        pc = p / jnp.sum(p, axis=-1, keepdims=True)
        o_ref[0, rows, :] = jnp.dot(pc.astype(BF16), v, preferred_element_type=F32)
        ps_ref[0, g:g + 1, :] = jnp.sum(pc, axis=0, keepdims=True)


def _dense1(q_pad, kv, biasmask):
    nb, n, _ = kv.shape
    return pl.pallas_call(
        _dense1_kernel,
        out_shape=(jax.ShapeDtypeStruct((nb, NSA_HEADS, LANE), F32),
                   jax.ShapeDtypeStruct((nb, NSA_KV_HEADS, n), F32)),
        grid=(nb,),
        in_specs=[pl.BlockSpec((1, NSA_HEADS, LANE), lambda b: (b, 0, 0)),
                  pl.BlockSpec((1, n, LANE), lambda b: (b, 0, 0)),
                  pl.BlockSpec((1, n, LANE), lambda b: (b, 0, 1)),
                  pl.BlockSpec((NSA_HEADS, n), lambda b: (0, 0))],
        out_specs=(pl.BlockSpec((1, NSA_HEADS, LANE), lambda b: (b, 0, 0)),
                   pl.BlockSpec((1, NSA_KV_HEADS, n), lambda b: (b, 0, 0))),
        name="nsa_dense_sample",
    )(q_pad, kv, kv, biasmask)


def _take_group_half(o_pad):
    nb = o_pad.shape[0]
    o = o_pad.reshape(nb, NSA_KV_HEADS, NSA_REP, NSA_KV_HEADS, HEAD_DIM)
    o = jnp.stack([o[:, g, :, g, :] for g in range(NSA_KV_HEADS)], axis=1)
    return o.reshape(nb, NSA_WIDTH)


def _topk_kernel(ps_ref, ov_ref, idx_ref, imp_sc, *, n_blocks, cur):
    n_rows = ps_ref.shape[0]
    n_cols = ov_ref.shape[1]
    ps = jnp.concatenate([ps_ref[...], jnp.zeros((LANE - n_rows, ps_ref.shape[1]), F32)], axis=0)
    imp = _dot32(ps, ov_ref[...])
    jj = lax.broadcasted_iota(jnp.int32, (LANE, n_cols), 1)
    forced = (jj == 0) | (jj == cur) | (jj == cur - 1)
    imp = jnp.where(forced, 1e6, jnp.where(jj <= cur, imp, -1e6))
    imp = jnp.where(jj < n_blocks, imp, -2e6)
    imp_t = imp.T
    imp_sc[...] = imp_t
    jrow = lax.broadcasted_iota(jnp.int32, (n_cols, LANE), 0)

    def body(k, rank):
        rk = imp_sc[pl.ds(k, 1), :]
        ahead = (rk > imp_t) | ((rk == imp_t) & (jrow > k))
        return rank + jnp.where(ahead, 1.0, 0.0)

    rank = lax.fori_loop(0, n_blocks, body, jnp.zeros((n_cols, LANE), F32))
    jf = jrow.astype(F32)
    rows = [jnp.sum(jnp.where(rank == float(r), jf, 0.0), axis=0, keepdims=True) for r in range(SLC_TOP)]
    idx_ref[...] = jnp.concatenate(rows, axis=0).astype(jnp.int32)


def _topk_sample(psum, ov, n_blocks, cur):
    n_rows, n_keys = psum.shape
    n_cols = ov.shape[1]
    return pl.pallas_call(
        functools.partial(_topk_kernel, n_blocks=n_blocks, cur=cur),
        out_shape=jax.ShapeDtypeStruct((SLC_TOP, LANE), jnp.int32),
        grid=(1,),
        in_specs=[pl.BlockSpec((n_rows, n_keys), lambda i: (0, 0)),
                  pl.BlockSpec(ov.shape, lambda i: (0, 0))],
        out_specs=pl.BlockSpec((SLC_TOP, LANE), lambda i: (0, 0)),
        scratch_shapes=[pltpu.VMEM((n_cols, LANE), F32)],
        name="nsa_topk_sample",
    )(psum, ov)


def _slc1_kernel(phys_ref, jsel_ref, q_ref, *refs, cur, past):
    pages = refs[:SLC_TOP]
    new_ref, bb_ref, o_ref = refs[SLC_TOP:]
    b = pl.program_id(0)
    g = pl.program_id(1)
    goff = pl.multiple_of(g * HEAD_DIM, HEAD_DIM)
    q = q_ref[0]
    lane = lax.broadcasted_iota(jnp.int32, (NSA_REP, PAGE_SIZE), 1)
    first = lax.broadcasted_iota(jnp.int32, (HEAD_DIM, PAGE_SIZE), 1) == 0
    new_k = jnp.where(first, new_ref[0, pl.ds(goff, HEAD_DIM), :], 0.0)
    new_v = jnp.where(first, new_ref[0, pl.ds(LANE + goff, HEAD_DIM), :], 0.0)
    scores, values = [], []
    for n in range(SLC_TOP):
        j = jsel_ref[(b * NSA_KV_HEADS + g) * SLC_TOP + n]
        kt = jnp.where(j == cur, new_k, pages[n][0, pl.ds(goff, HEAD_DIM), :])
        vt = jnp.where(j == cur, new_v, pages[n][0, pl.ds(LANE + goff, HEAD_DIM), :])
        s = jnp.dot(q, kt.astype(BF16), preferred_element_type=F32)
        ok = (lane // SLC_BLOCK == j % 2) & ((j // 2) * PAGE_SIZE + lane <= past)
        bias = bb_ref[jnp.clip(j - (cur - 3), 0, 3), pl.ds(pl.multiple_of(g * NSA_REP, NSA_REP), NSA_REP), :]
        scores.append(jnp.where(ok, s + bias, NEG))
        values.append(vt.astype(BF16))
    s_all = jnp.concatenate(scores, axis=1)
    p = jnp.exp(s_all - jnp.max(s_all, axis=-1, keepdims=True))
    acc = jnp.zeros((NSA_REP, HEAD_DIM), F32)
    for n in range(SLC_TOP):
        acc = acc + _dot_nt(p[:, n * PAGE_SIZE:(n + 1) * PAGE_SIZE].astype(BF16), values[n])
    o_ref[0] = acc / jnp.sum(p, axis=-1, keepdims=True)


def _slc_sample(q, cache_pages_t, phys, jsel, new_cols, biasblk, cur, past):
    nb = q.shape[0]
    idx = lambda b, g, n: (b * NSA_KV_HEADS + g) * SLC_TOP + n
    page_specs = [pl.BlockSpec((1, KV_COLS, PAGE_SIZE), lambda b, g, ph, js, n=n: (ph[idx(b, g, n)], 0, 0))
                  for n in range(SLC_TOP)]
    return pl.pallas_call(
        functools.partial(_slc1_kernel, cur=cur, past=past),
        out_shape=jax.ShapeDtypeStruct((nb, NSA_HEADS, HEAD_DIM), F32),
        grid_spec=pltpu.PrefetchScalarGridSpec(
            num_scalar_prefetch=2,
            grid=(nb, NSA_KV_HEADS),
            in_specs=[pl.BlockSpec((1, NSA_REP, HEAD_DIM), lambda b, g, ph, js: (b, g, 0))] + page_specs
            + [pl.BlockSpec((1, KV_COLS, 1), lambda b, g, ph, js: (b, 0, 0)),
               pl.BlockSpec(biasblk.shape, lambda b, g, ph, js: (0, 0, 0))],
            out_specs=pl.BlockSpec((1, NSA_REP, HEAD_DIM), lambda b, g, ph, js: (b, g, 0))),
        name="nsa_slc_sample",
    )(phys, jsel, q, *([cache_pages_t] * SLC_TOP), new_cols, biasblk)


def _out_kernel(x_ref, gate_ref, yssd_ref, zs_ref, oc_ref, os_ref, ow_ref, gl_ref, za_ref,
                nw1_ref, nw2_ref, w_ref, eg_ref, fg_ref, o_ref, *, per_row, final):
    gates = _sigmoid(gl_ref[...])
    g2 = jnp.concatenate(_split_bf16(gates, 2), axis=1)
    expand = lambda br: jnp.dot(g2, eg_ref[br], preferred_element_type=F32)
    y_nsa = expand(0) * oc_ref[...] + expand(1) * os_ref[...] + expand(2) * ow_ref[...]

    def gated_norm(y, z, w):
        u = y * _silu(z)
        half = u.shape[1] // 2
        parts = []
        for g in range(2):
            ug = u[:, g * half:(g + 1) * half]
            parts.append(ug * lax.rsqrt(jnp.mean(ug * ug, axis=-1, keepdims=True) + NORM_EPS))
        return (jnp.concatenate(parts, axis=1) * w).astype(BF16)

    m1 = gated_norm(yssd_ref[...], zs_ref[...], nw1_ref[...])
    m2 = gated_norm(y_nsa, za_ref[...], nw2_ref[...])
    proj = (jnp.dot(m1, w_ref[0:SSD_WIDTH, :], preferred_element_type=F32)
            + jnp.dot(m2, w_ref[SSD_WIDTH:, :], preferred_element_type=F32))
    gate = gate_ref[...] if per_row else gate_ref[0]
    out = x_ref[...] + gate * proj
    if final:
        out = out * lax.rsqrt(jnp.mean(out * out, axis=-1, keepdims=True) + NORM_EPS) * fg_ref[...]
    o_ref[...] = out


def _gate_expand():
    e = np.zeros((3, LANE, NSA_WIDTH), np.float32)
    for br in range(3):
        for h in range(NSA_HEADS):
            e[br, br * NSA_HEADS + h, h * HEAD_DIM:(h + 1) * HEAD_DIM] = 1.0
    return jnp.asarray(np.concatenate([e, e], axis=1), BF16)


def _layer_out(x2d, gate, y_ssd, z_s, o_cmp, o_slc, o_win, gl, z_a, nw1, nw2, w_out_bf, final_g,
               rows_per_batch, final):
    m = x2d.shape[0]
    per_row = rows_per_batch == 1
    tm = m if per_row else PROJ_ROWS
    if per_row:
        gate_spec = pl.BlockSpec((tm, D_MODEL), lambda i: (0, 0))
        gt = gate
    else:
        gate_spec = pl.BlockSpec((1, 1, D_MODEL), lambda i: ((i * tm) // rows_per_batch, 0, 0))
        gt = gate[:, None, :]
    row = lambda w: pl.BlockSpec((tm, w), lambda i: (i, 0))
    full = lambda shape: pl.BlockSpec(shape, lambda i: (0,) * len(shape))
    eg = _gate_expand()
    return pl.pallas_call(
        functools.partial(_out_kernel, per_row=per_row, final=final),
        out_shape=jax.ShapeDtypeStruct((m, D_MODEL), F32),
        grid=(m // tm,),
        in_specs=[row(D_MODEL), gate_spec, row(SSD_WIDTH), row(SSD_WIDTH), row(NSA_WIDTH), row(NSA_WIDTH),
                  row(NSA_WIDTH), row(LANE), row(NSA_WIDTH), full((1, SSD_WIDTH)), full((1, NSA_WIDTH)),
                  full(w_out_bf.shape), full(eg.shape), full((1, D_MODEL))],
        out_specs=row(D_MODEL),
        compiler_params=pltpu.CompilerParams(vmem_limit_bytes=VMEM_LIMIT),
        name="layer_out",
    )(x2d, gt, y_ssd, z_s, o_cmp, o_slc, o_win, gl, z_a, nw1.reshape(1, SSD_WIDTH), nw2.reshape(1, NSA_WIDTH),
      w_out_bf, eg, final_g.reshape(1, D_MODEL))


def _pad_in_weights(w_in):
    cols = []
    off = 0
    for size, width in zip(_SEG_SIZES, _SEG_PAD):
        seg = w_in[:, off:off + size]
        cols.append(jnp.pad(seg, ((0, 0), (0, width - size))))
        off += size
    return jnp.concatenate(cols, axis=1).astype(BF16)


def _values_t(v_pad):
    b, rows, _ = v_pad.shape
    vt = jnp.transpose(v_pad.reshape(b, rows, NSA_KV_HEADS, HEAD_DIM), (0, 2, 3, 1))
    return jnp.concatenate([vt, jnp.ones((b, NSA_KV_HEADS, ONES_ROWS, rows), BF16)], axis=2)


def _front_pad_bf16(kv, rows):
    return jnp.pad(kv, ((0, 0), (rows, 0), (0, 0))).astype(BF16)


def kernel(x_prompt, x_sample, cache_cmp_kv, cache_slc_kv, state_win_kv, state_conv, state_ssm, page_table,
           c_prompt, c_sample, norm_g, ada_w, ada_b, w_in, conv_w, conv_b, dt_bias, a_log, d_skip,
           ssd_norm_w, cmp_pe, cmp_w1, cmp_w2, nsa_norm_w, w_out, rel_bias, final_norm_g):
    nbp, lp, _ = x_prompt.shape
    nbs = x_sample.shape[0]
    depth = w_in.shape[0]
    n_pool = cache_cmp_kv.shape[1]
    n_pages = page_table.shape[1]
    past = n_pages * PAGE_SIZE
    w_buf = state_win_kv.shape[2]
    kv_row = (2, NSA_KV_HEADS, HEAD_DIM)

    n_ck = lp // CMP_STRIDE
    n_sb = lp // SLC_BLOCK
    qi = np.arange(Q_BLOCK)[:, None]
    c0 = n_ck - 4
    pattern = _bias_of_dist(rel_bias, qi - CMP_STRIDE * (np.arange(n_ck)[None, :] - c0) - (CMP_STRIDE - 1)) * LOG2E
    far = rel_bias.astype(F32)[REL_BUCKETS - 1][:, None, None]
    par = np.arange(2)[:, None, None]
    qq = np.arange(Q_BLOCK)[None, None, :]
    dist_n = qq - np.arange(SLC_NEARW)[None, :, None] + Q_BLOCK * par + (SLC_NEAR - Q_BLOCK)
    c_max = Q_BLOCK + WIN_PAD
    master = _toeplitz_bias(rel_bias, c_max - (SLC_NEAR - Q_BLOCK) + SLC_NEARW, Q_BLOCK, c_max)
    window = lambda c0, rows: master[:, c_max - c0:c_max - c0 + rows]
    tnt = jnp.stack([window(Q_BLOCK * p + SLC_NEAR - Q_BLOCK, SLC_NEARW) for p in range(2)], axis=1)
    tnt = jnp.where(jnp.asarray((dist_n >= 0) & (dist_n <= qq + SLC_NEAR - Q_BLOCK))[None],
                    (tnt - far[..., None]) * LOG2E, NEG)
    tnt = jnp.transpose(tnt.reshape(NSA_KV_HEADS, NSA_REP, 2, SLC_NEARW, Q_BLOCK), (2, 0, 3, 1, 4))
    tnt = tnt.reshape(2, NSA_KV_HEADS, SLC_NEARW, ROWS)
    tile = _group_tile()
    key_row = np.arange(SLC_PAD + lp + SLC_BACK) - SLC_PAD
    assert (key_row[-1] // SLC_BLOCK) < PAD_FLAG_COL
    cols = np.arange(LANE)[None, :]
    blk_onehot = jnp.asarray(((key_row[:, None] >= 0) & (key_row[:, None] // SLC_BLOCK == cols))
                             | ((key_row[:, None] < 0) & (cols == PAD_FLAG_COL)), BF16)
    dist_w = qq - np.arange(WIN_W)[None, :, None] + Q_BLOCK * par + WIN_PAD
    tw = jnp.stack([window(Q_BLOCK * p + WIN_PAD, WIN_W) for p in range(2)], axis=1)
    tw = jnp.where(jnp.asarray((dist_w >= 0) & (dist_w < WINDOW))[None], tw * LOG2E, NEG)
    tw = jnp.transpose(tw.reshape(NSA_KV_HEADS, NSA_REP, 2, WIN_W, Q_BLOCK), (2, 0, 3, 1, 4))
    tw = tw.reshape(2, NSA_KV_HEADS, WIN_W, ROWS)
    tw = jnp.concatenate([tw[0], tw[1]], axis=-1)
    win_flag = jnp.asarray((np.arange(WIN_PAD + lp)[:, None] < WIN_PAD) & (np.arange(LANE)[None, :] == 0), BF16)
    ov_p = _overlap_matrix(n_ck, n_sb, LANE)

    n_cs = past // CMP_STRIDE
    cur = past // SLC_BLOCK
    n_blk_s = cur + 1
    n_cols_s = -(-n_blk_s // LANE) * LANE
    ov_s = _overlap_matrix(n_cs, n_blk_s, n_cols_s)
    m_s = np.arange(n_cs)
    bm_c = jnp.where(jnp.asarray(m_s >= 1)[None, :],
                     _bias_of_dist(rel_bias, past - (CMP_STRIDE * m_s + CMP_STRIDE - 1)), NEG)
    n_w = -(-(w_buf + 1) // LANE) * LANE
    iw = np.arange(n_w)
    dw = w_buf - iw
    ok_w = (iw <= w_buf) & (dw >= 0) & (dw < WINDOW) & (past - w_buf + iw >= 0)
    bm_w = jnp.where(jnp.asarray(ok_w)[None, :], _bias_of_dist(rel_bias, dw), NEG)
    jb = (cur - 3 + np.arange(4))[:, None]
    biasblk = _bias_of_dist(rel_bias, past - SLC_BLOCK * jb - np.arange(SLC_BLOCK)[None, :])
    biasblk = jnp.transpose(biasblk, (1, 0, 2))
    biasblk = jnp.concatenate([biasblk, biasblk], axis=-1)

    pages_t = lambda c: jnp.transpose(c, (0, 1, 3, 4, 5, 2)).reshape(depth * n_pool, KV_COLS, PAGE_SIZE)
    cmp_pages_t = pages_t(cache_cmp_kv)
    slc_pages_t = pages_t(cache_slc_kv)
    prompt_pages = jnp.arange(nbp * (lp // PAGE_SIZE), dtype=jnp.int32).reshape(nbp, lp // PAGE_SIZE)

    c_all = jnp.concatenate([c_prompt, c_sample], axis=0)
    xp = x_prompt.reshape(nbp * lp, D_MODEL)
    xs = x_sample.reshape(nbs, D_MODEL)
    outs = {k: [] for k in ("pc", "ps", "pw", "pconv", "pssm", "sc", "ss", "sw", "sconv", "sssm")}

    for l in range(depth):
        final = l == depth - 1
        w_pad = _pad_in_weights(w_in[l])
        w_out_bf = w_out[l].astype(BF16)
        cweights = _compress_weights(cmp_pe[l], cmp_w1[l], cmp_w2[l])
        mod = _modulation(c_all, ada_w[l], ada_b[l])
        shift, scale, gate = mod[:, :D_MODEL], mod[:, D_MODEL:2 * D_MODEL], mod[:, 2 * D_MODEL:]

        z_s, xbc, dt, q, kvc, kvs, kvw, gl, z_a, qt = _in_projection(xp, norm_g[l], scale[:nbp], shift[:nbp], w_pad,
                                                                     lp)
        xbc3 = xbc.reshape(nbp, lp, CONV_DIM)
        y_ssd, h_fin = _ssd_prompt(xbc3, dt.reshape(nbp, lp, LANE), conv_w[l], conv_b[l], dt_bias[l], a_log[l],
                                   d_skip[l])
        q3 = q.reshape(nbp, lp, NSA_WIDTH)
        kvc3, kvs3, kvw3 = (t.reshape(nbp, lp, KV_COLS) for t in (kvc, kvs, kvw))
        kc = _compress(kvc.reshape(nbp * (lp // PAGE_SIZE), PAGE_SIZE, KV_COLS), prompt_pages, cweights, False)
        o_cmp, sel = _cmp_prompt(q3, kc, pattern, ov_p)
        ks_pad = jnp.pad(kvs3, ((0, 0), (SLC_PAD, SLC_BACK), (0, 0))).astype(BF16)
        kx_pad = jnp.concatenate([ks_pad[:, :, :LANE], jnp.broadcast_to(blk_onehot, (nbp,) + blk_onehot.shape)],
                                 axis=-1)
        o_slc = _slc_prompt(qt, sel, kx_pad, _values_t(ks_pad[:, :, LANE:]), tnt, tile)
        kw_pad = _front_pad_bf16(kvw3, WIN_PAD)
        kwx_pad = jnp.concatenate([kw_pad[:, :, :LANE], jnp.broadcast_to(win_flag, (nbp,) + win_flag.shape)], axis=-1)
        o_win = _win_prompt(qt, kwx_pad, _values_t(kw_pad[:, :, LANE:]), tw)
        xp = _layer_out(xp, gate[:nbp], y_ssd.reshape(nbp * lp, SSD_WIDTH), z_s,
                        o_cmp.reshape(nbp * lp, NSA_WIDTH), o_slc.reshape(nbp * lp, NSA_WIDTH),
                        o_win.reshape(nbp * lp, NSA_WIDTH), gl, z_a, ssd_norm_w[l], nsa_norm_w[l], w_out_bf,
                        final_norm_g, lp, final)
        outs["pc"].append(kvc3.reshape((nbp, lp) + kv_row))
        outs["ps"].append(kvs3.reshape((nbp, lp) + kv_row))
        outs["pw"].append(kvw3[:, -min(WINDOW, lp):].reshape((nbp, min(WINDOW, lp)) + kv_row))
        outs["pconv"].append(xbc3[:, -(SSD_CONV - 1):])
        outs["pssm"].append(h_fin)

        z_s2, xbc2, dt2, q2, kvc2, kvs2, kvw2, gl2, z_a2 = _in_projection(
            xs, norm_g[l], scale[nbp:], shift[nbp:], w_pad, 1)
        y_ssd2, h2 = _ssd_step(xbc2, state_conv[l], dt2, state_ssm[l], conv_w[l], conv_b[l], dt_bias[l],
                               a_log[l], d_skip[l])
        qh = q2.reshape(nbs, NSA_KV_HEADS, NSA_REP, HEAD_DIM)
        zq = jnp.zeros((nbs, NSA_REP, HEAD_DIM), F32)
        q_pad = jnp.stack([jnp.concatenate([qh[:, 0], zq], axis=-1), jnp.concatenate([zq, qh[:, 1]], axis=-1)],
                          axis=1).reshape(nbs, NSA_HEADS, LANE).astype(BF16)
        kc2 = _compress(cmp_pages_t, page_table + l * n_pool, cweights, True)
        oc2, psum = _dense1(q_pad, kc2, bm_c)
        sel_idx = _topk_sample(psum.reshape(nbs * NSA_KV_HEADS, n_cs), ov_s, n_blk_s, cur)
        jsel = sel_idx[:, :nbs * NSA_KV_HEADS].T.reshape(nbs, NSA_KV_HEADS, SLC_TOP)
        jc = jnp.minimum(jsel, cur - 1)
        page = jnp.take_along_axis(page_table, (jc // 2).reshape(nbs, -1), axis=1).reshape(jsel.shape)
        phys = jnp.where(jsel < cur, page + l * n_pool, 0).astype(jnp.int32)
        os2 = _slc_sample(q2.reshape(nbs, NSA_HEADS, HEAD_DIM).astype(BF16), slc_pages_t, phys.reshape(-1),
                          jsel.reshape(-1), kvs2.reshape(nbs, KV_COLS, 1), biasblk, cur, past)
        kw_full = jnp.concatenate([state_win_kv[l].reshape(nbs, w_buf, KV_COLS), kvw2[:, None, :]], axis=1)
        kw_in = jnp.pad(kw_full, ((0, 0), (0, n_w - (w_buf + 1)), (0, 0)))
        ow2, _ = _dense1(q_pad, kw_in, bm_w)
        xs = _layer_out(xs, gate[nbp:], y_ssd2, z_s2, _take_group_half(oc2), os2.reshape(nbs, NSA_WIDTH),
                        _take_group_half(ow2), gl2, z_a2, ssd_norm_w[l], nsa_norm_w[l], w_out_bf,
                        final_norm_g, 1, final)
        outs["sc"].append(kvc2.reshape((nbs, 1) + kv_row))
        outs["ss"].append(kvs2.reshape((nbs, 1) + kv_row))
        outs["sw"].append(kw_full[:, -w_buf:].reshape((nbs, w_buf) + kv_row))
        outs["sconv"].append(jnp.concatenate([state_conv[l][:, 1:], xbc2[:, None, :]], axis=1))
        outs["sssm"].append(h2)

    st = lambda k: jnp.stack(outs[k])
    return (xp.reshape(nbp, lp, D_MODEL), xs.reshape(nbs, 1, D_MODEL),
            st("pc"), st("ps"), st("pw"), st("pconv"), st("pssm"),
            st("sc"), st("ss"), st("sw"), st("sconv"), st("sssm"))
```

```python
import functools
import math

import numpy as np
import jax
import jax.numpy as jnp
from jax import lax
from jax.experimental import pallas as pl
from jax.experimental.pallas import tpu as pltpu

F32 = jnp.float32
BF16 = jnp.bfloat16
HIGHEST = lax.Precision.HIGHEST

D_MODEL = 1024
HEAD_DIM = 64
SSD_WIDTH = 1024
SSD_HEADS = 16
SSD_GROUPS = 2
SSD_STATE = 128
SSD_CONV = 4
SSD_CHUNK = 256
CONV_DIM = SSD_WIDTH + 2 * SSD_GROUPS * SSD_STATE
NSA_WIDTH = 1024
NSA_HEADS = 16
NSA_KV_HEADS = 2
NSA_REP = NSA_HEADS // NSA_KV_HEADS
CMP_BLOCK = 32
CMP_STRIDE = 16
CMP_HID = 2 * HEAD_DIM
SLC_BLOCK = 64
SLC_TOP = 16
WINDOW = 512
Q_BLOCK = 64
REL_BUCKETS = 32
REL_MAX_DIST = 128
NORM_EPS = 1e-6
KV_COLS = 2 * NSA_KV_HEADS * HEAD_DIM
PAGE_SIZE = 128
NEG = -1e30
LOG2E = 1.4426950408889634

LANE = 128
HALF = LANE // 2
GROUP_W = NSA_REP * HEAD_DIM
ROWS = NSA_REP * Q_BLOCK
SLC_NEAR = 4 * SLC_BLOCK
SLC_PAD = SLC_NEAR
SLC_NEARW = SLC_NEAR + LANE
ONES_ROWS = 8
PAD_FLAG_COL = LANE - 1
SLC_BACK = 2 * SLC_NEAR
WIN_W = WINDOW + 2 * Q_BLOCK
WIN_PAD = WINDOW
MAX_PAGES_PER_STEP = 32
PROJ_ROWS = 256
MOD_COLS = 512
VMEM_LIMIT = 48 * 1024 * 1024

_SEG_NAMES = ("z_s", "xbc", "dt", "q", "kvc", "kvs", "kvw", "gl", "z_a")
_SEG_SIZES = (SSD_WIDTH, CONV_DIM, SSD_HEADS, NSA_WIDTH, KV_COLS, KV_COLS, KV_COLS, 3 * NSA_HEADS, NSA_WIDTH)
_SEG_PAD = tuple(-(-s // LANE) * LANE for s in _SEG_SIZES)
_SEG_OFF = tuple(int(o) for o in np.cumsum((0,) + _SEG_PAD[:-1]))
IN_PAD = int(sum(_SEG_PAD))


def _sigmoid(x):
    return 1.0 / (1.0 + jnp.exp(-x))


def _silu(x):
    return x * _sigmoid(x)


def _dot32(a, b):
    return jnp.dot(a, b, precision=HIGHEST, preferred_element_type=F32)


def _split_bf16(x, terms):
    parts = []
    for _ in range(terms):
        p = x.astype(BF16)
        parts.append(p)
        x = x - p.astype(F32)
    return parts


def _expand2(x, sel2_bf16):
    hi, lo = _split_bf16(x, 2)
    return jnp.dot(jnp.concatenate([hi, lo], axis=1), sel2_bf16, preferred_element_type=F32)


def _sel_dot(sel_bf16, x, terms):
    return sum(jnp.dot(sel_bf16, p, preferred_element_type=F32) for p in _split_bf16(x, terms))


def _dot_nt(a, b):
    return lax.dot_general(a, b, (((1,), (1,)), ((), ())), preferred_element_type=F32)


def _bucket_table():
    n = np.arange(REL_MAX_DIST + 1)
    max_exact = REL_BUCKETS // 2
    nf = np.maximum(n, 1).astype(np.float32)
    large = max_exact + (np.log(nf / np.float32(max_exact)) / np.float32(math.log(REL_MAX_DIST / max_exact))
                         * np.float32(REL_BUCKETS - max_exact)).astype(np.int32)
    large = np.minimum(large, REL_BUCKETS - 1)
    return np.where(n < max_exact, n, large).astype(np.int32)


_BUCKETS = _bucket_table()


def _bias_of_dist(rel_bias, dist):
    idx = _BUCKETS[np.clip(dist, 0, REL_MAX_DIST)]
    out = jnp.take(rel_bias.astype(F32), jnp.asarray(idx.reshape(-1)), axis=0)
    return out.T.reshape((NSA_HEADS,) + dist.shape)


def _toeplitz_bias(rel_bias, rows, cols, c0):
    n = rows + cols
    d = np.arange(n) + c0 - (rows - 1)
    v = jnp.take(rel_bias.astype(F32), jnp.asarray(_BUCKETS[np.clip(d, 0, REL_MAX_DIST)]), axis=0).T
    flat = jnp.tile(v, (1, rows))[:, :rows * (n - 1)]
    return flat.reshape(NSA_HEADS, rows, n - 1)[:, :, rows - 1:rows - 1 + cols]


def _mod_kernel(c_ref, w_ref, b_ref, o_ref):
    o_ref[...] = _dot32(_silu(c_ref[...]), w_ref[...]) + b_ref[...]


def _modulation(c, w, b):
    m, d = c.shape
    n = w.shape[1]
    tn = MOD_COLS
    return pl.pallas_call(
        _mod_kernel,
        out_shape=jax.ShapeDtypeStruct((m, n), F32),
        grid=(n // tn,),
        in_specs=[pl.BlockSpec((m, d), lambda j: (0, 0)),
                  pl.BlockSpec((d, tn), lambda j: (0, j)),
                  pl.BlockSpec((1, tn), lambda j: (0, j))],
        out_specs=pl.BlockSpec((m, tn), lambda j: (0, j)),
        name="adaln_mod",
    )(c, w, b.reshape(1, n))


def _inproj_kernel(x_ref, g_ref, sc_ref, sh_ref, w_ref, *out_refs, per_row):
    x = x_ref[...]
    xn = x * lax.rsqrt(jnp.mean(x * x, axis=-1, keepdims=True) + NORM_EPS)
    sc = sc_ref[...] if per_row else sc_ref[0]
    sh = sh_ref[...] if per_row else sh_ref[0]
    h = ((xn * g_ref[...]) * (1.0 + sc) + sh).astype(BF16)
    for name, off, width, ref in zip(_SEG_NAMES, _SEG_OFF, _SEG_PAD, out_refs):
        r = jnp.dot(h, w_ref[:, off:off + width], preferred_element_type=F32)
        if name == "q":
            r = r * (HEAD_DIM ** -0.5)
            if not per_row:
                qt_ref = out_refs[-1]
                for t in range(r.shape[0] // Q_BLOCK):
                    for g in range(NSA_KV_HEADS):
                        qt_ref[t, g] = _stack_q_f32(r[t * Q_BLOCK:(t + 1) * Q_BLOCK] * LOG2E, g).T.astype(BF16)
        ref[...] = r


def _in_projection(x2d, g, scale, shift, w_pad, rows_per_batch):
    m = x2d.shape[0]
    per_row = rows_per_batch == 1
    tm = m if per_row else PROJ_ROWS
    outs = tuple(jax.ShapeDtypeStruct((m, w), F32) for w in _SEG_PAD)
    out_specs = tuple(pl.BlockSpec((tm, w), lambda i: (i, 0)) for w in _SEG_PAD)
    if per_row:
        mod_spec = pl.BlockSpec((tm, D_MODEL), lambda i: (0, 0))
        sc, sh = scale, shift
    else:
        mod_spec = pl.BlockSpec((1, 1, D_MODEL), lambda i: ((i * tm) // rows_per_batch, 0, 0))
        sc, sh = scale[:, None, :], shift[:, None, :]
        outs += (jax.ShapeDtypeStruct((m // Q_BLOCK, NSA_KV_HEADS, LANE, ROWS), BF16),)
        out_specs += (pl.BlockSpec((tm // Q_BLOCK, NSA_KV_HEADS, LANE, ROWS), lambda i: (i, 0, 0, 0)),)
    return pl.pallas_call(
        functools.partial(_inproj_kernel, per_row=per_row),
        out_shape=outs,
        grid=(m // tm,),
        in_specs=[pl.BlockSpec((tm, D_MODEL), lambda i: (i, 0)),
                  pl.BlockSpec((1, D_MODEL), lambda i: (0, 0)),
                  mod_spec, mod_spec,
                  pl.BlockSpec((D_MODEL, IN_PAD), lambda i: (0, 0))],
        out_specs=out_specs,
        compiler_params=pltpu.CompilerParams(vmem_limit_bytes=VMEM_LIMIT),
        name="in_projection",
    )(x2d, g.reshape(1, D_MODEL), sc, sh, w_pad)


def _softplus(x):
    return jnp.maximum(x, 0.0) + jnp.log(1.0 + jnp.exp(-jnp.abs(x)))


def _ssd_kernel(xbc_ref, dt_ref, cw_ref, cb_ref, dtb_ref, alog_ref, dsk_ref, e_ref, tril_ref,
                y_ref, hfin_ref, xe_sc, st_sc):
    c = pl.program_id(1)
    q = SSD_CHUNK
    n_pairs = SSD_HEADS // 2

    @pl.when(c == 0)
    def _():
        xe_sc[0:8, :] = jnp.zeros((8, CONV_DIM), F32)
        st_sc[...] = jnp.zeros(st_sc.shape, F32)

    xe_sc[8:8 + q, :] = xbc_ref[0]
    acc = cb_ref[...] + cw_ref[0:1, :] * xe_sc[5:5 + q, :]
    for k in range(1, SSD_CONV):
        acc = acc + cw_ref[k:k + 1, :] * xe_sc[5 + k:5 + k + q, :]
    u = _silu(acc)
    xe_sc[0:8, :] = xe_sc[q:q + 8, :]

    xs = u[:, :SSD_WIDTH]
    gn = SSD_GROUPS * SSD_STATE
    bm = u[:, SSD_WIDTH:SSD_WIDTH + gn]
    cm = u[:, SSD_WIDTH + gn:]

    dt = _softplus(dt_ref[0] + dtb_ref[...])
    a = dt * (-jnp.exp(alog_ref[...]))
    cs = _sel_dot(tril_ref[...], a, 3)
    cs_t = cs.T
    cs_last = cs[q - 1:q, :]
    e = e_ref[...]
    dt_e = _expand2(dt, e)
    w_e = _expand2(dt * jnp.exp(cs_last - cs), e)
    ecs_e = _expand2(jnp.exp(cs), e)
    tot_e = _expand2(jnp.broadcast_to(jnp.exp(cs_last), (8, LANE)), e)[0:1, :]
    xdt = (xs * dt_e).astype(BF16)
    xw = (xs * w_e).astype(BF16)

    li = lax.broadcasted_iota(jnp.int32, (q, q), 0)
    si = lax.broadcasted_iota(jnp.int32, (q, q), 1)
    tri = li >= si
    lane = lax.broadcasted_iota(jnp.int32, (q, LANE), 1)

    for g in range(SSD_GROUPS):
        cg = cm[:, g * SSD_STATE:(g + 1) * SSD_STATE].astype(BF16)
        bg = bm[:, g * SSD_STATE:(g + 1) * SSD_STATE]
        cb = _dot_nt(cg, bg.astype(BF16))
        bg_t = bg.T.astype(BF16)
        for jp in range(n_pairs // SSD_GROUPS):
            j = g * (n_pairs // SSD_GROUPS) + jp
            sl = slice(j * LANE, (j + 1) * LANE)
            xdt_p = xdt[:, sl]
            ys = []
            for hh in (2 * j, 2 * j + 1):
                diff = cs[:, hh:hh + 1] - cs_t[hh:hh + 1, :]
                lmat = jnp.exp(jnp.where(tri, diff, NEG))
                ys.append(jnp.dot((cb * lmat).astype(BF16), xdt_p, preferred_element_type=F32))
            y_diag = jnp.where(lane < HALF, ys[0], ys[1])
            st = st_sc[j]
            y_off = jnp.dot(cg, st.astype(BF16), preferred_element_type=F32) * ecs_e[:, sl]
            y_ref[0, :, sl] = y_diag + y_off + xs[:, sl] * dsk_ref[:, sl]
            new = jnp.dot(bg_t, xw[:, sl], preferred_element_type=F32)
            st_sc[j] = st * tot_e[:, sl] + new

    @pl.when(c == pl.num_programs(1) - 1)
    def _():
        for j in range(n_pairs):
            hfin_ref[0, j * LANE:(j + 1) * LANE, :] = st_sc[j].T


def _head_expand():
    e = np.zeros((LANE, SSD_WIDTH), np.float32)
    for h in range(SSD_HEADS):
        e[h, h * HEAD_DIM:(h + 1) * HEAD_DIM] = 1.0
    return jnp.asarray(e)


def _pad_lanes(v):
    return jnp.pad(v.astype(F32), (0, LANE - v.shape[0])).reshape(1, LANE)


def _ssd_prompt(xbc, dt, conv_w, conv_b, dt_bias, a_log, d_skip):
    b, l, _ = xbc.shape
    nc = l // SSD_CHUNK
    full = lambda shape: pl.BlockSpec(shape, lambda i, c: (0,) * len(shape))
    y, hfin = pl.pallas_call(
        _ssd_kernel,
        out_shape=(jax.ShapeDtypeStruct((b, l, SSD_WIDTH), F32),
                   jax.ShapeDtypeStruct((b, SSD_HEADS * HEAD_DIM, SSD_STATE), F32)),
        grid=(b, nc),
        in_specs=[pl.BlockSpec((1, SSD_CHUNK, CONV_DIM), lambda i, c: (i, c, 0)),
                  pl.BlockSpec((1, SSD_CHUNK, LANE), lambda i, c: (i, c, 0)),
                  full((SSD_CONV, CONV_DIM)), full((1, CONV_DIM)), full((1, LANE)), full((1, LANE)),
                  full((1, SSD_WIDTH)), full((2 * LANE, SSD_WIDTH)), full((SSD_CHUNK, SSD_CHUNK))],
        out_specs=(pl.BlockSpec((1, SSD_CHUNK, SSD_WIDTH), lambda i, c: (i, c, 0)),
                   pl.BlockSpec((1, SSD_HEADS * HEAD_DIM, SSD_STATE), lambda i, c: (i, 0, 0))),
        scratch_shapes=[pltpu.VMEM((SSD_CHUNK + 8, CONV_DIM), F32),
                        pltpu.VMEM((SSD_HEADS // 2, SSD_STATE, LANE), F32)],
        compiler_params=pltpu.CompilerParams(dimension_semantics=("arbitrary", "arbitrary"),
                                             vmem_limit_bytes=VMEM_LIMIT),
        name="ssd_prompt",
    )(xbc, dt, conv_w, conv_b.reshape(1, CONV_DIM), _pad_lanes(dt_bias), _pad_lanes(a_log),
      jnp.repeat(d_skip.astype(F32), HEAD_DIM).reshape(1, SSD_WIDTH),
      jnp.concatenate([_head_expand(), _head_expand()], axis=0).astype(BF16),
      jnp.asarray(np.tril(np.ones((SSD_CHUNK, SSD_CHUNK), np.float32)), BF16))
    return y, hfin.reshape(b, SSD_HEADS, HEAD_DIM, SSD_STATE)


def _ssd_step_kernel(xbc_ref, c0_ref, c1_ref, c2_ref, dt_ref, h0_ref, cw_ref, cb_ref, dtb_ref, alog_ref,
                     dsk_ref, e_ref, y_ref, hout_ref, xt_sc, dect_sc, bc_sc, yt_sc, xs_sc):
    b = pl.program_id(0)
    nb = xbc_ref.shape[0]
    rows = SSD_HEADS * HEAD_DIM
    gn = SSD_GROUPS * SSD_STATE

    @pl.when(b == 0)
    def _():
        acc = (cb_ref[...] + cw_ref[0:1, :] * c0_ref[...] + cw_ref[1:2, :] * c1_ref[...]
               + cw_ref[2:3, :] * c2_ref[...] + cw_ref[3:4, :] * xbc_ref[...])
        u = _silu(acc)
        xs = u[:, :SSD_WIDTH]
        dt = _softplus(dt_ref[...] + dtb_ref[...])
        dec = jnp.exp(dt * (-jnp.exp(alog_ref[...])))
        e = e_ref[...]
        xdt = xs * _dot32(dt, e)
        dec_e = _dot32(dec, e)
        pad = jnp.zeros((LANE - nb, SSD_WIDTH), F32)
        xt_sc[...] = jnp.concatenate([xdt, pad], axis=0).T
        dect_sc[...] = jnp.concatenate([dec_e, pad], axis=0).T
        bc_sc[...] = u[:, SSD_WIDTH:]
        xs_sc[...] = xs
        yt_sc[...] = jnp.zeros(yt_sc.shape, F32)

    ri = lax.broadcasted_iota(jnp.int32, (LANE, LANE), 0)
    onehot = jnp.where(ri == b, 1.0, 0.0)
    xcol = _dot32(xt_sc[...], onehot)
    dcol = _dot32(dect_sc[...], onehot)
    bc = bc_sc[pl.ds(b, 1), :]
    row = lax.broadcasted_iota(jnp.int32, (rows, SSD_STATE), 0)
    first = row < rows // SSD_GROUPS
    b_full = jnp.where(first, bc[:, 0:SSD_STATE], bc[:, SSD_STATE:gn])
    c_full = jnp.where(first, bc[:, gn:gn + SSD_STATE], bc[:, gn + SSD_STATE:])
    new = dcol * h0_ref[0] + xcol * b_full
    hout_ref[0] = new
    ycol = _dot32(new * c_full, jnp.ones((SSD_STATE, LANE), F32))
    lane = lax.broadcasted_iota(jnp.int32, (rows, LANE), 1)
    yt_sc[...] = jnp.where(lane == b, ycol, yt_sc[...])

    @pl.when(b == nb - 1)
    def _():
        y_ref[...] = yt_sc[...].T[0:nb, :] + xs_sc[...] * dsk_ref[...]


def _ssd_step(xbc, conv_state, dt, h0, conv_w, conv_b, dt_bias, a_log, d_skip):
    nb = xbc.shape[0]
    rows = SSD_HEADS * HEAD_DIM
    full = lambda shape: pl.BlockSpec(shape, lambda i: (0,) * len(shape))
    y, hout = pl.pallas_call(
        _ssd_step_kernel,
        out_shape=(jax.ShapeDtypeStruct((nb, SSD_WIDTH), F32),
                   jax.ShapeDtypeStruct((nb, rows, SSD_STATE), F32)),
        grid=(nb,),
        in_specs=[full((nb, CONV_DIM)), full((nb, CONV_DIM)), full((nb, CONV_DIM)), full((nb, CONV_DIM)),
                  full((nb, LANE)),
                  pl.BlockSpec((1, rows, SSD_STATE), lambda i: (i, 0, 0)),
                  full((SSD_CONV, CONV_DIM)), full((1, CONV_DIM)), full((1, LANE)), full((1, LANE)),
                  full((1, SSD_WIDTH)), full((LANE, SSD_WIDTH))],
        out_specs=(full((nb, SSD_WIDTH)),
                   pl.BlockSpec((1, rows, SSD_STATE), lambda i: (i, 0, 0))),
        scratch_shapes=[pltpu.VMEM((rows, LANE), F32), pltpu.VMEM((rows, LANE), F32),
                        pltpu.VMEM((nb, 2 * SSD_GROUPS * SSD_STATE), F32),
                        pltpu.VMEM((rows, LANE), F32), pltpu.VMEM((nb, SSD_WIDTH), F32)],
        compiler_params=pltpu.CompilerParams(dimension_semantics=("arbitrary",)),
        name="ssd_step",
    )(xbc, conv_state[:, 0], conv_state[:, 1], conv_state[:, 2], dt, h0.reshape(nb, rows, SSD_STATE),
      conv_w, conv_b.reshape(1, CONV_DIM), _pad_lanes(dt_bias), _pad_lanes(a_log),
      jnp.repeat(d_skip.astype(F32), HEAD_DIM).reshape(1, SSD_WIDTH), _head_expand())
    return y, hout.reshape(nb, SSD_HEADS, HEAD_DIM, SSD_STATE)


def _compress_kernel(pt_ref, *refs, transposed, n_pg):
    segs = PAGE_SIZE // CMP_STRIDE
    if transposed:
        perm_ref, w1_ref, pe_ref, w2_ref, out_ref, sh_sc, pe_sc, xs_sc = refs[n_pg:]
        for i in range(n_pg):
            xs_sc[i] = _dot_nt(perm_ref[...], refs[i][0].astype(BF16))

        def token_rows(k, o):
            return jnp.concatenate([xs_sc[i, o * segs:(o + 1) * segs, k * LANE:(k + 1) * LANE]
                                    for i in range(n_pg)], axis=0)
    else:
        w1_ref, pe_ref, w2_ref, out_ref, sh_sc, pe_sc = refs[2 * n_pg:]

        def token_rows(k, o):
            return jnp.concatenate([refs[k * n_pg + i][0, pl.ds(o, segs, stride=CMP_STRIDE), :]
                                    for i in range(n_pg)], axis=0)
    s = pl.program_id(1)
    rows = n_pg * segs
    hid2 = NSA_KV_HEADS * CMP_HID

    @pl.when(s == 0)
    def _():
        sh_sc[:, 0:8, :] = jnp.zeros((2, 8, hid2), F32)
        for k in range(2):
            t = jnp.zeros((8, 2 * hid2), F32)
            for o in range(0, CMP_STRIDE, 2):
                pe2 = jnp.concatenate([pe_ref[o, k], pe_ref[o + 1, k]], axis=1)
                t = t + jnp.dot(pe2.astype(BF16), w1_ref[o // 2, k], preferred_element_type=F32)
            pe_sc[k] = jnp.broadcast_to(t[0:1, 0:hid2] + t[1:2, hid2:], (8, hid2))

    for k in range(2):
        acc = jnp.zeros((rows, 2 * hid2), F32)
        for o in range(0, CMP_STRIDE, 2):
            xo = jnp.concatenate([token_rows(k, o), token_rows(k, o + 1)], axis=1)
            acc = acc + jnp.dot(xo.astype(BF16), w1_ref[o // 2, k], preferred_element_type=F32)
        sh_sc[k, 8:8 + rows, :] = acc[:, 0:hid2]
        pre = acc[:, hid2:] + sh_sc[k, 7:7 + rows, :] + pe_sc[k, 0:1, :]
        sh_sc[k, 0:8, :] = sh_sc[k, rows:rows + 8, :]
        out_ref[0, :, k * LANE:(k + 1) * LANE] = jnp.dot(_silu(pre).astype(BF16), w2_ref[k],
                                                         preferred_element_type=F32)


def _compress_weights(cmp_pe, cmp_w1, cmp_w2):
    span = CMP_BLOCK // CMP_STRIDE
    w1s = cmp_w1.astype(F32).reshape(2, span, CMP_STRIDE, HEAD_DIM, CMP_HID)
    z = jnp.zeros((2, span, CMP_STRIDE, HEAD_DIM, CMP_HID), F32)
    top = jnp.concatenate([w1s, z], axis=-1)
    bot = jnp.concatenate([z, w1s], axis=-1)
    bd = jnp.concatenate([top, bot], axis=-2)
    w1 = jnp.transpose(bd, (2, 0, 3, 1, 4)).reshape(CMP_STRIDE, 2, LANE, span * 2 * CMP_HID).astype(BF16)
    w1 = jnp.transpose(w1.reshape(CMP_STRIDE // 2, 2, 2, LANE, span * 2 * CMP_HID), (0, 2, 1, 3, 4))
    w1 = w1.reshape(CMP_STRIDE // 2, 2, 2 * LANE, span * 2 * CMP_HID)
    pe = cmp_pe.astype(F32).reshape(2, span, CMP_STRIDE, HEAD_DIM)
    pe = jnp.transpose(pe, (2, 0, 1, 3))
    pe = jnp.concatenate([pe, pe], axis=-1)
    pe = jnp.pad(pe, ((0, 0), (0, 0), (0, 8 - span), (0, 0)))
    w2 = cmp_w2.astype(F32)
    z2 = jnp.zeros_like(w2)
    w2bd = jnp.concatenate([jnp.concatenate([w2, z2], axis=-1), jnp.concatenate([z2, w2], axis=-1)],
                           axis=-2).astype(BF16)
    return w1, pe, w2bd


def _compress(pages_arr, page_ids, cweights, transposed):
    nb, n_pages = page_ids.shape
    n_pg = math.gcd(n_pages, MAX_PAGES_PER_STEP)
    steps = n_pages // n_pg
    segs = PAGE_SIZE // CMP_STRIDE
    rows = n_pg * segs
    w1, pe, w2bd = cweights
    hid2 = NSA_KV_HEADS * CMP_HID
    page_of = lambda b, s, pt, i: pt[(b * steps + s) * n_pg + i]
    scratch = [pltpu.VMEM((2, rows + 8, hid2), F32), pltpu.VMEM((2, 8, hid2), F32)]
    if transposed:
        page_specs = [pl.BlockSpec((1, KV_COLS, PAGE_SIZE), lambda b, s, pt, i=i: (page_of(b, s, pt, i), 0, 0))
                      for i in range(n_pg)]
        scratch.append(pltpu.VMEM((n_pg, PAGE_SIZE, KV_COLS), F32))
        perm = np.zeros((PAGE_SIZE, PAGE_SIZE), np.float32)
        for o in range(CMP_STRIDE):
            for sg in range(segs):
                perm[o * segs + sg, sg * CMP_STRIDE + o] = 1.0
        extra, extra_specs = [jnp.asarray(perm, BF16)], [pl.BlockSpec((PAGE_SIZE, PAGE_SIZE), lambda b, s, pt: (0, 0))]
    else:
        extra, extra_specs = [], []
        page_specs = [pl.BlockSpec((1, PAGE_SIZE, LANE), lambda b, s, pt, i=i, k=k: (page_of(b, s, pt, i), 0, k))
                      for k in range(2) for i in range(n_pg)]
    full = lambda shape: pl.BlockSpec(shape, lambda b, s, pt: (0,) * len(shape))
    return pl.pallas_call(
        functools.partial(_compress_kernel, transposed=transposed, n_pg=n_pg),
        out_shape=jax.ShapeDtypeStruct((nb, n_pages * segs, KV_COLS), F32),
        grid_spec=pltpu.PrefetchScalarGridSpec(
            num_scalar_prefetch=1,
            grid=(nb, steps),
            in_specs=page_specs + extra_specs + [full(w1.shape), full(pe.shape), full(w2bd.shape)],
            out_specs=pl.BlockSpec((1, rows, KV_COLS), lambda b, s, pt: (b, s, 0)),
            scratch_shapes=scratch),
        compiler_params=pltpu.CompilerParams(dimension_semantics=("arbitrary", "arbitrary"),
                                             vmem_limit_bytes=VMEM_LIMIT),
        name="nsa_compress",
    )(page_ids.reshape(-1), *([pages_arr] * len(page_specs)), *extra, w1, pe, w2bd)


def _overlap_matrix(n_rows, n_blocks, n_cols):
    m = np.arange(n_rows)[:, None]
    j = np.arange(n_cols)[None, :]
    cs = (m - 1) * CMP_STRIDE
    ov = (m >= 1) & (j < n_blocks) & (cs < j * SLC_BLOCK + SLC_BLOCK) & (cs + CMP_BLOCK > j * SLC_BLOCK)
    return jnp.asarray(ov.astype(np.float32))


def _stack_q(q, g):
    return _stack_q_f32(q, g).astype(BF16)


def _stack_q_f32(q, g):
    lane = lax.broadcasted_iota(jnp.int32, (Q_BLOCK, LANE), 1)
    keep = (lane < HALF) if g == 0 else (lane >= HALF)
    parts = []
    for jp in range(NSA_REP // 2):
        j = g * (NSA_REP // 2) + jp
        slab = q[:, j * LANE:(j + 1) * LANE]
        rolled = pltpu.roll(slab, HALF, 1)
        first, second = (slab, rolled) if g == 0 else (rolled, slab)
        parts.append(jnp.where(keep, first, 0.0))
        parts.append(jnp.where(keep, second, 0.0))
    return jnp.concatenate(parts, axis=0)


def _unstack_o(acc, g):
    lane = lax.broadcasted_iota(jnp.int32, (Q_BLOCK, LANE), 1)
    outs = []
    for jp in range(NSA_REP // 2):
        a = acc[(2 * jp) * Q_BLOCK:(2 * jp + 1) * Q_BLOCK]
        b = acc[(2 * jp + 1) * Q_BLOCK:(2 * jp + 2) * Q_BLOCK]
        if g == 0:
            outs.append(jnp.where(lane < HALF, a, pltpu.roll(b, HALF, 1)))
        else:
            outs.append(jnp.where(lane < HALF, pltpu.roll(a, HALF, 1), b))
    return jnp.concatenate(outs, axis=1)


def _tile8(x):
    return jnp.concatenate([x] * NSA_REP, axis=0)


def _rank_rows(imp, n_valid):
    sub = 8
    n_rank = -(-n_valid // sub) * sub
    chunks = [imp[c:c + sub] for c in range(0, n_rank, sub)]
    ranks = [jnp.zeros(ch.shape, F32) for ch in chunks]
    jrow = lax.broadcasted_iota(jnp.int32, chunks[0].shape, 0)
    for k in range(n_valid):
        rk = imp[k:k + 1, :]
        for c, ch in enumerate(chunks):
            if c * sub > k:
                ahead = rk >= ch
            elif c * sub + sub - 1 < k:
                ahead = rk > ch
            else:
                ahead = (rk > ch) | ((rk == ch) & (jrow + c * sub > k))
            ranks[c] = ranks[c] + jnp.where(ahead, 1.0, 0.0)
    rest = jnp.full((imp.shape[0] - n_rank, imp.shape[1]), float(n_valid), F32)
    return jnp.concatenate(ranks + [rest], axis=0)


def _cmp_kernel(q_ref, kv_ref, pb_ref, ov_ref, o_ref, sel_ref, bias_sc, *, n_keys, n_blocks):
    i = pl.program_id(0)
    qi = lax.broadcasted_iota(jnp.int32, (Q_BLOCK, n_keys), 0)
    mi = lax.broadcasted_iota(jnp.int32, (Q_BLOCK, n_keys), 1)
    qpos = Q_BLOCK * i + qi

    @pl.when(pl.program_id(1) == 0)
    def _():
        valid = (mi >= 1) & (CMP_STRIDE * mi + CMP_STRIDE - 1 <= qpos)
        for h in range(NSA_HEADS):
            bias_sc[h] = jnp.where(valid, pltpu.roll(pb_ref[h], (4 * i + 4) % n_keys, 1), NEG)

    rowvalid8 = _tile8(jnp.where(qpos[:, 0:1] >= CMP_BLOCK - 1, 1.0, 0.0))
    jj = lax.broadcasted_iota(jnp.int32, (LANE, LANE), 1)
    ov = ov_ref[...].astype(BF16)
    forced = (jj == 0) | (jj == i) | (jj == i - 1)
    for bb in range(q_ref.shape[0]):
        q = q_ref[bb] * LOG2E
        kv = kv_ref[bb]
        kc = kv[:, 0:LANE].astype(BF16)
        vc = kv[:, LANE:].astype(BF16)
        psum = []
        for g in range(NSA_KV_HEADS):
            s = _dot_nt(_stack_q(q, g), kc) + bias_sc[g * NSA_REP:(g + 1) * NSA_REP].reshape(ROWS, n_keys)
            p = jnp.exp2(s - jnp.max(s, axis=-1, keepdims=True))
            pc = p / jnp.sum(p, axis=-1, keepdims=True) * rowvalid8
            o_ref[bb, :, g * GROUP_W:(g + 1) * GROUP_W] = _unstack_o(
                jnp.dot(pc.astype(BF16), vc, preferred_element_type=F32), g).astype(BF16)
            ps = pc[0:Q_BLOCK]
            for r in range(1, NSA_REP):
                ps = ps + pc[r * Q_BLOCK:(r + 1) * Q_BLOCK]
            psum.append(ps)
        ps = jnp.concatenate(psum, axis=0)
        hi = ps.astype(BF16)
        lo = (ps - hi.astype(F32)).astype(BF16)
        imp = (jnp.dot(hi, ov, preferred_element_type=F32)
               + jnp.dot(lo, ov, preferred_element_type=F32))
        imp = jnp.where(forced, 1e6, jnp.where(jj <= i, imp, -1e6))
        imp = jnp.where(jj < n_blocks, imp, -2e6)
        rank = _rank_rows(imp.T, n_blocks)
        sel_ref[bb, 0] = jnp.where(rank < SLC_TOP, 1.0, 0.0).astype(BF16)


def _cmp_prompt(q, kvc_cmp, pattern, ov):
    b, l, _ = q.shape
    n_keys = kvc_cmp.shape[1]
    n_blocks = l // SLC_BLOCK
    nb = 2 if b % 2 == 0 else 1
    return pl.pallas_call(
        functools.partial(_cmp_kernel, n_keys=n_keys, n_blocks=n_blocks),
        out_shape=(jax.ShapeDtypeStruct((b, l, NSA_WIDTH), BF16),
                   jax.ShapeDtypeStruct((b, l // Q_BLOCK, LANE, LANE), BF16)),
        grid=(l // Q_BLOCK, b // nb),
        in_specs=[pl.BlockSpec((nb, Q_BLOCK, NSA_WIDTH), lambda i, bi: (bi, i, 0)),
                  pl.BlockSpec((nb, n_keys, KV_COLS), lambda i, bi: (bi, 0, 0)),
                  pl.BlockSpec(pattern.shape, lambda i, bi: (0, 0, 0)),
                  pl.BlockSpec(ov.shape, lambda i, bi: (0, 0))],
        out_specs=(pl.BlockSpec((nb, Q_BLOCK, NSA_WIDTH), lambda i, bi: (bi, i, 0)),
                   pl.BlockSpec((nb, 1, LANE, LANE), lambda i, bi: (bi, i, 0, 0))),
        scratch_shapes=[pltpu.VMEM(pattern.shape, F32)],
        compiler_params=pltpu.CompilerParams(dimension_semantics=("arbitrary", "arbitrary"),
                                             vmem_limit_bytes=VMEM_LIMIT),
        name="nsa_cmp_prompt",
    )(q, kvc_cmp, pattern, ov)


def _flash_update_t(st, vt, m, acc):
    m_new = jnp.maximum(m, jnp.max(st, axis=0, keepdims=True))
    p = jnp.exp2(st - m_new)
    acc = jnp.exp2(m - m_new) * acc + jnp.dot(vt, p.astype(BF16), preferred_element_type=F32)
    return m_new, acc


def _finish_t(acc):
    o = (acc[0:HEAD_DIM] / acc[HEAD_DIM:HEAD_DIM + 1]).T
    return jnp.concatenate([o[r * Q_BLOCK:(r + 1) * Q_BLOCK] for r in range(NSA_REP)], axis=1).astype(BF16)


def _slc_kernel(qt_ref, sel_ref, kx_ref, vt_ref, tnt_ref, tile_ref, o_ref, far_sc, near_sc, sta_sc, stb_sc, stn_sc):
    i = pl.program_id(1)
    n_far = (jnp.maximum(i - 3, 0) + 3) // 4
    sel = sel_ref[0, 0]
    jrow = lax.broadcasted_iota(jnp.int32, (LANE, ROWS), 0)
    a = ((i + 1) // 2) * LANE
    delta = Q_BLOCK * (i + 1) - a
    groups = range(NSA_KV_HEADS)
    for g in groups:
        qs_t = qt_ref[0, g]
        hit = jnp.dot(sel, tile_ref[g], preferred_element_type=F32) > 0.5
        near_sc[g] = jnp.concatenate([qs_t, jnp.where(hit & (jrow != PAD_FLAG_COL), 0.0, NEG).astype(BF16)], axis=0)
        far_sc[g] = jnp.concatenate([qs_t, jnp.where(hit & (jrow < i - 3), 0.0, NEG).astype(BF16)], axis=0)

    def far_scores(t, dst):
        start = pl.multiple_of(SLC_PAD + SLC_NEAR * t, SLC_NEAR)
        kx = kx_ref[0, pl.ds(start, SLC_NEAR), :]
        for g in groups:
            dst[g] = jnp.dot(kx, far_sc[g], preferred_element_type=F32)

    def far_softmax(t, src, carry):
        start = pl.multiple_of(SLC_PAD + SLC_NEAR * t, SLC_NEAR)
        return tuple(_flash_update_t(src[g], vt_ref[0, g, :, pl.ds(start, SLC_NEAR)], *carry[g]) for g in groups)

    def far_pair(u, carry):
        far_scores(2 * u + 1, stb_sc)
        carry = far_softmax(2 * u, sta_sc, carry)
        far_scores(2 * u + 2, sta_sc)
        return far_softmax(2 * u + 1, stb_sc, carry)

    far_scores(0, sta_sc)
    start = pl.multiple_of(a, LANE)
    kx = kx_ref[0, pl.ds(start, SLC_NEARW), :]
    for g in groups:
        stn_sc[g] = jnp.dot(kx, near_sc[g], preferred_element_type=F32)
    init = (jnp.full((1, ROWS), NEG, F32), jnp.zeros((HEAD_DIM + ONES_ROWS, ROWS), F32))
    carry = lax.fori_loop(0, (n_far + 1) // 2, far_pair, (init, init))

    for g in groups:
        st = stn_sc[g] + tnt_ref[delta // Q_BLOCK, g]
        _, acc = _flash_update_t(st, vt_ref[0, g, :, pl.ds(start, SLC_NEARW)], *carry[g])
        o_ref[0, :, g * GROUP_W:(g + 1) * GROUP_W] = _finish_t(acc)


def _slc_prompt(qt, sel, kx_pad, vt_pad, tnt, tile):
    b, n_tiles = sel.shape[:2]
    l = n_tiles * Q_BLOCK
    lp = kx_pad.shape[1]
    tq = Q_BLOCK
    n_chains = NSA_KV_HEADS
    return pl.pallas_call(
        _slc_kernel,
        out_shape=jax.ShapeDtypeStruct((b, l, NSA_WIDTH), BF16),
        grid=(b, n_tiles),
        in_specs=[pl.BlockSpec((1, NSA_KV_HEADS, LANE, ROWS), lambda bi, i: (bi * n_tiles + i, 0, 0, 0)),
                  pl.BlockSpec((1, 1, LANE, LANE), lambda bi, i: (bi, i, 0, 0)),
                  pl.BlockSpec((1, lp, 2 * LANE), lambda bi, i: (bi, 0, 0)),
                  pl.BlockSpec((1, NSA_KV_HEADS, HEAD_DIM + ONES_ROWS, lp), lambda bi, i: (bi, 0, 0, 0)),
                  pl.BlockSpec(tnt.shape, lambda bi, i: (0, 0, 0, 0)),
                  pl.BlockSpec(tile.shape, lambda bi, i: (0, 0, 0))],
        out_specs=pl.BlockSpec((1, tq, NSA_WIDTH), lambda bi, i: (bi, i, 0)),
        scratch_shapes=[pltpu.VMEM((n_chains, 2 * LANE, ROWS), BF16),
                        pltpu.VMEM((n_chains, 2 * LANE, ROWS), BF16),
                        pltpu.VMEM((n_chains, SLC_NEAR, ROWS), F32),
                        pltpu.VMEM((n_chains, SLC_NEAR, ROWS), F32),
                        pltpu.VMEM((n_chains, SLC_NEARW, ROWS), F32)],
        compiler_params=pltpu.CompilerParams(vmem_limit_bytes=VMEM_LIMIT),
        name="nsa_slc_prompt",
    )(qt, sel, kx_pad, vt_pad, tnt, tile)


def _group_tile():
    t = np.zeros((NSA_KV_HEADS, LANE, ROWS), np.float32)
    for g in range(NSA_KV_HEADS):
        for r in range(NSA_REP):
            for qq in range(Q_BLOCK):
                t[g, g * Q_BLOCK + qq, r * Q_BLOCK + qq] = 1.0
    return jnp.asarray(t, BF16)


def _win_kernel(qt_ref, kx_ref, vt_ref, tw_ref, o_ref, st_sc):
    i2 = pl.program_id(1)
    start = pl.multiple_of(i2 * LANE, LANE)
    kx = kx_ref[0, pl.ds(start, WIN_W), :]
    groups = range(NSA_KV_HEADS)
    row = lax.broadcasted_iota(jnp.int32, (LANE, 2 * ROWS), 0)
    pad_rows = jnp.where(row == 0, NEG, 0.0).astype(BF16)
    for g in groups:
        qs_t = jnp.concatenate([qt_ref[h, g] for h in range(2)], axis=1)
        qx = jnp.concatenate([qs_t, pad_rows], axis=0)
        st_sc[g] = jnp.dot(kx, qx, preferred_element_type=F32)
    for g in groups:
        st = st_sc[g] + tw_ref[g]
        p = jnp.exp2(st - jnp.max(st, axis=0, keepdims=True))
        acc = jnp.dot(vt_ref[0, g, :, pl.ds(start, WIN_W)], p.astype(BF16), preferred_element_type=F32)
        for h in range(2):
            o_ref[0, h * Q_BLOCK:(h + 1) * Q_BLOCK, g * GROUP_W:(g + 1) * GROUP_W] = _finish_t(
                acc[:, h * ROWS:(h + 1) * ROWS])


def _win_prompt(qt, kx_pad, vt_pad, tw):
    b = kx_pad.shape[0]
    l = qt.shape[0] // b * Q_BLOCK
    lp = kx_pad.shape[1]
    tq = 2 * Q_BLOCK
    steps = l // tq
    return pl.pallas_call(
        _win_kernel,
        out_shape=jax.ShapeDtypeStruct((b, l, NSA_WIDTH), BF16),
        grid=(b, steps),
        in_specs=[pl.BlockSpec((2, NSA_KV_HEADS, LANE, ROWS), lambda bi, i: (bi * steps + i, 0, 0, 0)),
                  pl.BlockSpec((1, lp, 2 * LANE), lambda bi, i: (bi, 0, 0)),
                  pl.BlockSpec((1, NSA_KV_HEADS, HEAD_DIM + ONES_ROWS, lp), lambda bi, i: (bi, 0, 0, 0)),
                  pl.BlockSpec(tw.shape, lambda bi, i: (0, 0, 0))],
        out_specs=pl.BlockSpec((1, tq, NSA_WIDTH), lambda bi, i: (bi, i, 0)),
        scratch_shapes=[pltpu.VMEM((NSA_KV_HEADS, WIN_W, 2 * ROWS), F32)],
        compiler_params=pltpu.CompilerParams(vmem_limit_bytes=VMEM_LIMIT),
        name="nsa_win_prompt",
    )(qt, kx_pad, vt_pad, tw)


def _dense1_kernel(q_ref, k_ref, v_ref, bm_ref, o_ref, ps_ref):
    k = k_ref[0].astype(BF16)
    v = v_ref[0].astype(BF16)
    for g in range(NSA_KV_HEADS):
        rows = slice(g * NSA_REP, (g + 1) * NSA_REP)
        s = _dot_nt(q_ref[0, rows, :], k) + bm_ref[rows, :]
        p = jnp.exp(s - jnp.max(s, axis=-1, keepdims=True))
        pc = p / jnp.sum(p, axis=-1, keepdims=True)
        o_ref[0, rows, :] = jnp.dot(pc.astype(BF16), v, preferred_element_type=F32)
        ps_ref[0, g:g + 1, :] = jnp.sum(pc, axis=0, keepdims=True)


def _dense1(q_pad, kv, biasmask):
    nb, n, _ = kv.shape
    return pl.pallas_call(
        _dense1_kernel,
        out_shape=(jax.ShapeDtypeStruct((nb, NSA_HEADS, LANE), F32),
                   jax.ShapeDtypeStruct((nb, NSA_KV_HEADS, n), F32)),
        grid=(nb,),
        in_specs=[pl.BlockSpec((1, NSA_HEADS, LANE), lambda b: (b, 0, 0)),
                  pl.BlockSpec((1, n, LANE), lambda b: (b, 0, 0)),
                  pl.BlockSpec((1, n, LANE), lambda b: (b, 0, 1)),
                  pl.BlockSpec((NSA_HEADS, n), lambda b: (0, 0))],
        out_specs=(pl.BlockSpec((1, NSA_HEADS, LANE), lambda b: (b, 0, 0)),
                   pl.BlockSpec((1, NSA_KV_HEADS, n), lambda b: (b, 0, 0))),
        name="nsa_dense_sample",
    )(q_pad, kv, kv, biasmask)


def _take_group_half(o_pad):
    nb = o_pad.shape[0]
    o = o_pad.reshape(nb, NSA_KV_HEADS, NSA_REP, NSA_KV_HEADS, HEAD_DIM)
    o = jnp.stack([o[:, g, :, g, :] for g in range(NSA_KV_HEADS)], axis=1)
    return o.reshape(nb, NSA_WIDTH)


def _topk_kernel(ps_ref, ov_ref, idx_ref, imp_sc, *, n_blocks, cur):
    n_rows = ps_ref.shape[0]
    n_cols = ov_ref.shape[1]
    ps = jnp.concatenate([ps_ref[...], jnp.zeros((LANE - n_rows, ps_ref.shape[1]), F32)], axis=0)
    imp = _dot32(ps, ov_ref[...])
    jj = lax.broadcasted_iota(jnp.int32, (LANE, n_cols), 1)
    forced = (jj == 0) | (jj == cur) | (jj == cur - 1)
    imp = jnp.where(forced, 1e6, jnp.where(jj <= cur, imp, -1e6))
    imp = jnp.where(jj < n_blocks, imp, -2e6)
    imp_t = imp.T
    imp_sc[...] = imp_t
    jrow = lax.broadcasted_iota(jnp.int32, (n_cols, LANE), 0)

    def body(k, rank):
        rk = imp_sc[pl.ds(k, 1), :]
        ahead = (rk > imp_t) | ((rk == imp_t) & (jrow > k))
        return rank + jnp.where(ahead, 1.0, 0.0)

    rank = lax.fori_loop(0, n_blocks, body, jnp.zeros((n_cols, LANE), F32))
    jf = jrow.astype(F32)
    rows = [jnp.sum(jnp.where(rank == float(r), jf, 0.0), axis=0, keepdims=True) for r in range(SLC_TOP)]
    idx_ref[...] = jnp.concatenate(rows, axis=0).astype(jnp.int32)


def _topk_sample(psum, ov, n_blocks, cur):
    n_rows, n_keys = psum.shape
    n_cols = ov.shape[1]
    return pl.pallas_call(
        functools.partial(_topk_kernel, n_blocks=n_blocks, cur=cur),
        out_shape=jax.ShapeDtypeStruct((SLC_TOP, LANE), jnp.int32),
        grid=(1,),
        in_specs=[pl.BlockSpec((n_rows, n_keys), lambda i: (0, 0)),
                  pl.BlockSpec(ov.shape, lambda i: (0, 0))],
        out_specs=pl.BlockSpec((SLC_TOP, LANE), lambda i: (0, 0)),
        scratch_shapes=[pltpu.VMEM((n_cols, LANE), F32)],
        name="nsa_topk_sample",
    )(psum, ov)


def _slc1_kernel(phys_ref, jsel_ref, q_ref, *refs, cur, past):
    pages = refs[:SLC_TOP]
    new_ref, bb_ref, o_ref = refs[SLC_TOP:]
    b = pl.program_id(0)
    g = pl.program_id(1)
    goff = pl.multiple_of(g * HEAD_DIM, HEAD_DIM)
    q = q_ref[0]
    lane = lax.broadcasted_iota(jnp.int32, (NSA_REP, PAGE_SIZE), 1)
    first = lax.broadcasted_iota(jnp.int32, (HEAD_DIM, PAGE_SIZE), 1) == 0
    new_k = jnp.where(first, new_ref[0, pl.ds(goff, HEAD_DIM), :], 0.0)
    new_v = jnp.where(first, new_ref[0, pl.ds(LANE + goff, HEAD_DIM), :], 0.0)
    scores, values = [], []
    for n in range(SLC_TOP):
        j = jsel_ref[(b * NSA_KV_HEADS + g) * SLC_TOP + n]
        kt = jnp.where(j == cur, new_k, pages[n][0, pl.ds(goff, HEAD_DIM), :])
        vt = jnp.where(j == cur, new_v, pages[n][0, pl.ds(LANE + goff, HEAD_DIM), :])
        s = jnp.dot(q, kt.astype(BF16), preferred_element_type=F32)
        ok = (lane // SLC_BLOCK == j % 2) & ((j // 2) * PAGE_SIZE + lane <= past)
        bias = bb_ref[jnp.clip(j - (cur - 3), 0, 3), pl.ds(pl.multiple_of(g * NSA_REP, NSA_REP), NSA_REP), :]
        scores.append(jnp.where(ok, s + bias, NEG))
        values.append(vt.astype(BF16))
    s_all = jnp.concatenate(scores, axis=1)
    p = jnp.exp(s_all - jnp.max(s_all, axis=-1, keepdims=True))
    acc = jnp.zeros((NSA_REP, HEAD_DIM), F32)
    for n in range(SLC_TOP):
        acc = acc + _dot_nt(p[:, n * PAGE_SIZE:(n + 1) * PAGE_SIZE].astype(BF16), values[n])
    o_ref[0] = acc / jnp.sum(p, axis=-1, keepdims=True)


def _slc_sample(q, cache_pages_t, phys, jsel, new_cols, biasblk, cur, past):
    nb = q.shape[0]
    idx = lambda b, g, n: (b * NSA_KV_HEADS + g) * SLC_TOP + n
    page_specs = [pl.BlockSpec((1, KV_COLS, PAGE_SIZE), lambda b, g, ph, js, n=n: (ph[idx(b, g, n)], 0, 0))
                  for n in range(SLC_TOP)]
    return pl.pallas_call(
        functools.partial(_slc1_kernel, cur=cur, past=past),
        out_shape=jax.ShapeDtypeStruct((nb, NSA_HEADS, HEAD_DIM), F32),
        grid_spec=pltpu.PrefetchScalarGridSpec(
            num_scalar_prefetch=2,
            grid=(nb, NSA_KV_HEADS),
            in_specs=[pl.BlockSpec((1, NSA_REP, HEAD_DIM), lambda b, g, ph, js: (b, g, 0))] + page_specs
            + [pl.BlockSpec((1, KV_COLS, 1), lambda b, g, ph, js: (b, 0, 0)),
               pl.BlockSpec(biasblk.shape, lambda b, g, ph, js: (0, 0, 0))],
            out_specs=pl.BlockSpec((1, NSA_REP, HEAD_DIM), lambda b, g, ph, js: (b, g, 0))),
        name="nsa_slc_sample",
    )(phys, jsel, q, *([cache_pages_t] * SLC_TOP), new_cols, biasblk)


def _out_kernel(x_ref, gate_ref, yssd_ref, zs_ref, oc_ref, os_ref, ow_ref, gl_ref, za_ref,
                nw1_ref, nw2_ref, w_ref, eg_ref, fg_ref, o_ref, *, per_row, final):
    gates = _sigmoid(gl_ref[...])
    g2 = jnp.concatenate(_split_bf16(gates, 2), axis=1)
    expand = lambda br: jnp.dot(g2, eg_ref[br], preferred_element_type=F32)
    y_nsa = expand(0) * oc_ref[...] + expand(1) * os_ref[...] + expand(2) * ow_ref[...]

    def gated_norm(y, z, w):
        u = y * _silu(z)
        half = u.shape[1] // 2
        parts = []
        for g in range(2):
            ug = u[:, g * half:(g + 1) * half]
            parts.append(ug * lax.rsqrt(jnp.mean(ug * ug, axis=-1, keepdims=True) + NORM_EPS))
        return (jnp.concatenate(parts, axis=1) * w).astype(BF16)

    m1 = gated_norm(yssd_ref[...], zs_ref[...], nw1_ref[...])
    m2 = gated_norm(y_nsa, za_ref[...], nw2_ref[...])
    proj = (jnp.dot(m1, w_ref[0:SSD_WIDTH, :], preferred_element_type=F32)
            + jnp.dot(m2, w_ref[SSD_WIDTH:, :], preferred_element_type=F32))
    gate = gate_ref[...] if per_row else gate_ref[0]
    out = x_ref[...] + gate * proj
    if final:
        out = out * lax.rsqrt(jnp.mean(out * out, axis=-1, keepdims=True) + NORM_EPS) * fg_ref[...]
    o_ref[...] = out


def _gate_expand():
    e = np.zeros((3, LANE, NSA_WIDTH), np.float32)
    for br in range(3):
        for h in range(NSA_HEADS):
            e[br, br * NSA_HEADS + h, h * HEAD_DIM:(h + 1) * HEAD_DIM] = 1.0
    return jnp.asarray(np.concatenate([e, e], axis=1), BF16)


def _layer_out(x2d, gate, y_ssd, z_s, o_cmp, o_slc, o_win, gl, z_a, nw1, nw2, w_out_bf, final_g,
               rows_per_batch, final):
    m = x2d.shape[0]
    per_row = rows_per_batch == 1
    tm = m if per_row else PROJ_ROWS
    if per_row:
        gate_spec = pl.BlockSpec((tm, D_MODEL), lambda i: (0, 0))
        gt = gate
    else:
        gate_spec = pl.BlockSpec((1, 1, D_MODEL), lambda i: ((i * tm) // rows_per_batch, 0, 0))
        gt = gate[:, None, :]
    row = lambda w: pl.BlockSpec((tm, w), lambda i: (i, 0))
    full = lambda shape: pl.BlockSpec(shape, lambda i: (0,) * len(shape))
    eg = _gate_expand()
    return pl.pallas_call(
        functools.partial(_out_kernel, per_row=per_row, final=final),
        out_shape=jax.ShapeDtypeStruct((m, D_MODEL), F32),
        grid=(m // tm,),
        in_specs=[row(D_MODEL), gate_spec, row(SSD_WIDTH), row(SSD_WIDTH), row(NSA_WIDTH), row(NSA_WIDTH),
                  row(NSA_WIDTH), row(LANE), row(NSA_WIDTH), full((1, SSD_WIDTH)), full((1, NSA_WIDTH)),
                  full(w_out_bf.shape), full(eg.shape), full((1, D_MODEL))],
        out_specs=row(D_MODEL),
        compiler_params=pltpu.CompilerParams(vmem_limit_bytes=VMEM_LIMIT),
        name="layer_out",
    )(x2d, gt, y_ssd, z_s, o_cmp, o_slc, o_win, gl, z_a, nw1.reshape(1, SSD_WIDTH), nw2.reshape(1, NSA_WIDTH),
      w_out_bf, eg, final_g.reshape(1, D_MODEL))


def _pad_in_weights(w_in):
    cols = []
    off = 0
    for size, width in zip(_SEG_SIZES, _SEG_PAD):
        seg = w_in[:, off:off + size]
        cols.append(jnp.pad(seg, ((0, 0), (0, width - size))))
        off += size
    return jnp.concatenate(cols, axis=1).astype(BF16)


def _values_t(v_pad):
    b, rows, _ = v_pad.shape
    vt = jnp.transpose(v_pad.reshape(b, rows, NSA_KV_HEADS, HEAD_DIM), (0, 2, 3, 1))
    return jnp.concatenate([vt, jnp.ones((b, NSA_KV_HEADS, ONES_ROWS, rows), BF16)], axis=2)


def _front_pad_bf16(kv, rows):
    return jnp.pad(kv, ((0, 0), (rows, 0), (0, 0))).astype(BF16)


def kernel(x_prompt, x_sample, cache_cmp_kv, cache_slc_kv, state_win_kv, state_conv, state_ssm, page_table,
           c_prompt, c_sample, norm_g, ada_w, ada_b, w_in, conv_w, conv_b, dt_bias, a_log, d_skip,
           ssd_norm_w, cmp_pe, cmp_w1, cmp_w2, nsa_norm_w, w_out, rel_bias, final_norm_g):
    nbp, lp, _ = x_prompt.shape
    nbs = x_sample.shape[0]
    depth = w_in.shape[0]
    n_pool = cache_cmp_kv.shape[1]
    n_pages = page_table.shape[1]
    past = n_pages * PAGE_SIZE
    w_buf = state_win_kv.shape[2]
    kv_row = (2, NSA_KV_HEADS, HEAD_DIM)

    n_ck = lp // CMP_STRIDE
    n_sb = lp // SLC_BLOCK
    qi = np.arange(Q_BLOCK)[:, None]
    c0 = n_ck - 4
    pattern = _bias_of_dist(rel_bias, qi - CMP_STRIDE * (np.arange(n_ck)[None, :] - c0) - (CMP_STRIDE - 1)) * LOG2E
    far = rel_bias.astype(F32)[REL_BUCKETS - 1][:, None, None]
    par = np.arange(2)[:, None, None]
    qq = np.arange(Q_BLOCK)[None, None, :]
    dist_n = qq - np.arange(SLC_NEARW)[None, :, None] + Q_BLOCK * par + (SLC_NEAR - Q_BLOCK)
    c_max = Q_BLOCK + WIN_PAD
    master = _toeplitz_bias(rel_bias, c_max - (SLC_NEAR - Q_BLOCK) + SLC_NEARW, Q_BLOCK, c_max)
    window = lambda c0, rows: master[:, c_max - c0:c_max - c0 + rows]
    tnt = jnp.stack([window(Q_BLOCK * p + SLC_NEAR - Q_BLOCK, SLC_NEARW) for p in range(2)], axis=1)
    tnt = jnp.where(jnp.asarray((dist_n >= 0) & (dist_n <= qq + SLC_NEAR - Q_BLOCK))[None],
                    (tnt - far[..., None]) * LOG2E, NEG)
    tnt = jnp.transpose(tnt.reshape(NSA_KV_HEADS, NSA_REP, 2, SLC_NEARW, Q_BLOCK), (2, 0, 3, 1, 4))
    tnt = tnt.reshape(2, NSA_KV_HEADS, SLC_NEARW, ROWS)
    tile = _group_tile()
    key_row = np.arange(SLC_PAD + lp + SLC_BACK) - SLC_PAD
    assert (key_row[-1] // SLC_BLOCK) < PAD_FLAG_COL
    cols = np.arange(LANE)[None, :]
    blk_onehot = jnp.asarray(((key_row[:, None] >= 0) & (key_row[:, None] // SLC_BLOCK == cols))
                             | ((key_row[:, None] < 0) & (cols == PAD_FLAG_COL)), BF16)
    dist_w = qq - np.arange(WIN_W)[None, :, None] + Q_BLOCK * par + WIN_PAD
    tw = jnp.stack([window(Q_BLOCK * p + WIN_PAD, WIN_W) for p in range(2)], axis=1)
    tw = jnp.where(jnp.asarray((dist_w >= 0) & (dist_w < WINDOW))[None], tw * LOG2E, NEG)
    tw = jnp.transpose(tw.reshape(NSA_KV_HEADS, NSA_REP, 2, WIN_W, Q_BLOCK), (2, 0, 3, 1, 4))
    tw = tw.reshape(2, NSA_KV_HEADS, WIN_W, ROWS)
    tw = jnp.concatenate([tw[0], tw[1]], axis=-1)
    win_flag = jnp.asarray((np.arange(WIN_PAD + lp)[:, None] < WIN_PAD) & (np.arange(LANE)[None, :] == 0), BF16)
    ov_p = _overlap_matrix(n_ck, n_sb, LANE)

    n_cs = past // CMP_STRIDE
    cur = past // SLC_BLOCK
    n_blk_s = cur + 1
    n_cols_s = -(-n_blk_s // LANE) * LANE
    ov_s = _overlap_matrix(n_cs, n_blk_s, n_cols_s)
    m_s = np.arange(n_cs)
    bm_c = jnp.where(jnp.asarray(m_s >= 1)[None, :],
                     _bias_of_dist(rel_bias, past - (CMP_STRIDE * m_s + CMP_STRIDE - 1)), NEG)
    n_w = -(-(w_buf + 1) // LANE) * LANE
    iw = np.arange(n_w)
    dw = w_buf - iw
    ok_w = (iw <= w_buf) & (dw >= 0) & (dw < WINDOW) & (past - w_buf + iw >= 0)
    bm_w = jnp.where(jnp.asarray(ok_w)[None, :], _bias_of_dist(rel_bias, dw), NEG)
    jb = (cur - 3 + np.arange(4))[:, None]
    biasblk = _bias_of_dist(rel_bias, past - SLC_BLOCK * jb - np.arange(SLC_BLOCK)[None, :])
    biasblk = jnp.transpose(biasblk, (1, 0, 2))
    biasblk = jnp.concatenate([biasblk, biasblk], axis=-1)

    pages_t = lambda c: jnp.transpose(c, (0, 1, 3, 4, 5, 2)).reshape(depth * n_pool, KV_COLS, PAGE_SIZE)
    cmp_pages_t = pages_t(cache_cmp_kv)
    slc_pages_t = pages_t(cache_slc_kv)
    prompt_pages = jnp.arange(nbp * (lp // PAGE_SIZE), dtype=jnp.int32).reshape(nbp, lp // PAGE_SIZE)

    c_all = jnp.concatenate([c_prompt, c_sample], axis=0)
    xp = x_prompt.reshape(nbp * lp, D_MODEL)
    xs = x_sample.reshape(nbs, D_MODEL)
    outs = {k: [] for k in ("pc", "ps", "pw", "pconv", "pssm", "sc", "ss", "sw", "sconv", "sssm")}

    for l in range(depth):
        final = l == depth - 1
        w_pad = _pad_in_weights(w_in[l])
        w_out_bf = w_out[l].astype(BF16)
        cweights = _compress_weights(cmp_pe[l], cmp_w1[l], cmp_w2[l])
        mod = _modulation(c_all, ada_w[l], ada_b[l])
        shift, scale, gate = mod[:, :D_MODEL], mod[:, D_MODEL:2 * D_MODEL], mod[:, 2 * D_MODEL:]

        z_s, xbc, dt, q, kvc, kvs, kvw, gl, z_a, qt = _in_projection(xp, norm_g[l], scale[:nbp], shift[:nbp], w_pad,
                                                                     lp)
        xbc3 = xbc.reshape(nbp, lp, CONV_DIM)
        y_ssd, h_fin = _ssd_prompt(xbc3, dt.reshape(nbp, lp, LANE), conv_w[l], conv_b[l], dt_bias[l], a_log[l],
                                   d_skip[l])
        q3 = q.reshape(nbp, lp, NSA_WIDTH)
        kvc3, kvs3, kvw3 = (t.reshape(nbp, lp, KV_COLS) for t in (kvc, kvs, kvw))
        kc = _compress(kvc.reshape(nbp * (lp // PAGE_SIZE), PAGE_SIZE, KV_COLS), prompt_pages, cweights, False)
        o_cmp, sel = _cmp_prompt(q3, kc, pattern, ov_p)
        ks_pad = jnp.pad(kvs3, ((0, 0), (SLC_PAD, SLC_BACK), (0, 0))).astype(BF16)
        kx_pad = jnp.concatenate([ks_pad[:, :, :LANE], jnp.broadcast_to(blk_onehot, (nbp,) + blk_onehot.shape)],
                                 axis=-1)
        o_slc = _slc_prompt(qt, sel, kx_pad, _values_t(ks_pad[:, :, LANE:]), tnt, tile)
        kw_pad = _front_pad_bf16(kvw3, WIN_PAD)
        kwx_pad = jnp.concatenate([kw_pad[:, :, :LANE], jnp.broadcast_to(win_flag, (nbp,) + win_flag.shape)], axis=-1)
        o_win = _win_prompt(qt, kwx_pad, _values_t(kw_pad[:, :, LANE:]), tw)
        xp = _layer_out(xp, gate[:nbp], y_ssd.reshape(nbp * lp, SSD_WIDTH), z_s,
                        o_cmp.reshape(nbp * lp, NSA_WIDTH), o_slc.reshape(nbp * lp, NSA_WIDTH),
                        o_win.reshape(nbp * lp, NSA_WIDTH), gl, z_a, ssd_norm_w[l], nsa_norm_w[l], w_out_bf,
                        final_norm_g, lp, final)
        outs["pc"].append(kvc3.reshape((nbp, lp) + kv_row))
        outs["ps"].append(kvs3.reshape((nbp, lp) + kv_row))
        outs["pw"].append(kvw3[:, -min(WINDOW, lp):].reshape((nbp, min(WINDOW, lp)) + kv_row))
        outs["pconv"].append(xbc3[:, -(SSD_CONV - 1):])
        outs["pssm"].append(h_fin)

        z_s2, xbc2, dt2, q2, kvc2, kvs2, kvw2, gl2, z_a2 = _in_projection(
            xs, norm_g[l], scale[nbp:], shift[nbp:], w_pad, 1)
        y_ssd2, h2 = _ssd_step(xbc2, state_conv[l], dt2, state_ssm[l], conv_w[l], conv_b[l], dt_bias[l],
                               a_log[l], d_skip[l])
        qh = q2.reshape(nbs, NSA_KV_HEADS, NSA_REP, HEAD_DIM)
        zq = jnp.zeros((nbs, NSA_REP, HEAD_DIM), F32)
        q_pad = jnp.stack([jnp.concatenate([qh[:, 0], zq], axis=-1), jnp.concatenate([zq, qh[:, 1]], axis=-1)],
                          axis=1).reshape(nbs, NSA_HEADS, LANE).astype(BF16)
        kc2 = _compress(cmp_pages_t, page_table + l * n_pool, cweights, True)
        oc2, psum = _dense1(q_pad, kc2, bm_c)
        sel_idx = _topk_sample(psum.reshape(nbs * NSA_KV_HEADS, n_cs), ov_s, n_blk_s, cur)
        jsel = sel_idx[:, :nbs * NSA_KV_HEADS].T.reshape(nbs, NSA_KV_HEADS, SLC_TOP)
        jc = jnp.minimum(jsel, cur - 1)
        page = jnp.take_along_axis(page_table, (jc // 2).reshape(nbs, -1), axis=1).reshape(jsel.shape)
        phys = jnp.where(jsel < cur, page + l * n_pool, 0).astype(jnp.int32)
        os2 = _slc_sample(q2.reshape(nbs, NSA_HEADS, HEAD_DIM).astype(BF16), slc_pages_t, phys.reshape(-1),
                          jsel.reshape(-1), kvs2.reshape(nbs, KV_COLS, 1), biasblk, cur, past)
        kw_full = jnp.concatenate([state_win_kv[l].reshape(nbs, w_buf, KV_COLS), kvw2[:, None, :]], axis=1)
        kw_in = jnp.pad(kw_full, ((0, 0), (0, n_w - (w_buf + 1)), (0, 0)))
        ow2, _ = _dense1(q_pad, kw_in, bm_w)
        xs = _layer_out(xs, gate[nbp:], y_ssd2, z_s2, _take_group_half(oc2), os2.reshape(nbs, NSA_WIDTH),
                        _take_group_half(ow2), gl2, z_a2, ssd_norm_w[l], nsa_norm_w[l], w_out_bf,
                        final_norm_g, 1, final)
        outs["sc"].append(kvc2.reshape((nbs, 1) + kv_row))
        outs["ss"].append(kvs2.reshape((nbs, 1) + kv_row))
        outs["sw"].append(kw_full[:, -w_buf:].reshape((nbs, w_buf) + kv_row))
        outs["sconv"].append(jnp.concatenate([state_conv[l][:, 1:], xbc2[:, None, :]], axis=1))
        outs["sssm"].append(h2)

    st = lambda k: jnp.stack(outs[k])
    return (xp.reshape(nbp, lp, D_MODEL), xs.reshape(nbs, 1, D_MODEL),
            st("pc"), st("ps"), st("pw"), st("pconv"), st("pssm"),
            st("sc"), st("ss"), st("sw"), st("sconv"), st("sssm"))
```

```python
import functools
import math

import numpy as np
import jax
import jax.numpy as jnp
from jax import lax
from jax.experimental import pallas as pl
from jax.experimental.pallas import tpu as pltpu

F32 = jnp.float32
BF16 = jnp.bfloat16
HIGHEST = lax.Precision.HIGHEST

D_MODEL = 1024
HEAD_DIM = 64
SSD_WIDTH = 1024
SSD_HEADS = 16
SSD_GROUPS = 2
SSD_STATE = 128
SSD_CONV = 4
SSD_CHUNK = 256
CONV_DIM = SSD_WIDTH + 2 * SSD_GROUPS * SSD_STATE
NSA_WIDTH = 1024
NSA_HEADS = 16
NSA_KV_HEADS = 2
NSA_REP = NSA_HEADS // NSA_KV_HEADS
CMP_BLOCK = 32
CMP_STRIDE = 16
CMP_HID = 2 * HEAD_DIM
SLC_BLOCK = 64
SLC_TOP = 16
WINDOW = 512
Q_BLOCK = 64
REL_BUCKETS = 32
REL_MAX_DIST = 128
NORM_EPS = 1e-6
KV_COLS = 2 * NSA_KV_HEADS * HEAD_DIM
PAGE_SIZE = 128
NEG = -1e30
LOG2E = 1.4426950408889634

LANE = 128
HALF = LANE // 2
GROUP_W = NSA_REP * HEAD_DIM
ROWS = NSA_REP * Q_BLOCK
SLC_NEAR = 4 * SLC_BLOCK
SLC_PAD = SLC_NEAR
SLC_NEARW = SLC_NEAR + LANE
ONES_ROWS = 8
PAD_FLAG_COL = LANE - 1
SLC_BACK = 2 * SLC_NEAR
WIN_W = WINDOW + 2 * Q_BLOCK
WIN_PAD = WINDOW
MAX_PAGES_PER_STEP = 32
PROJ_ROWS = 256
MOD_COLS = 512
VMEM_LIMIT = 48 * 1024 * 1024

_SEG_NAMES = ("z_s", "xbc", "dt", "q", "kvc", "kvs", "kvw", "gl", "z_a")
_SEG_SIZES = (SSD_WIDTH, CONV_DIM, SSD_HEADS, NSA_WIDTH, KV_COLS, KV_COLS, KV_COLS, 3 * NSA_HEADS, NSA_WIDTH)
_SEG_PAD = tuple(-(-s // LANE) * LANE for s in _SEG_SIZES)
_SEG_OFF = tuple(int(o) for o in np.cumsum((0,) + _SEG_PAD[:-1]))
IN_PAD = int(sum(_SEG_PAD))


def _sigmoid(x):
    return 1.0 / (1.0 + jnp.exp(-x))


def _silu(x):
    return x * _sigmoid(x)


def _dot32(a, b):
    return jnp.dot(a, b, precision=HIGHEST, preferred_element_type=F32)


def _split_bf16(x, terms):
    parts = []
    for _ in range(terms):
        p = x.astype(BF16)
        parts.append(p)
        x = x - p.astype(F32)
    return parts


def _expand2(x, sel2_bf16):
    hi, lo = _split_bf16(x, 2)
    return jnp.dot(jnp.concatenate([hi, lo], axis=1), sel2_bf16, preferred_element_type=F32)


def _sel_dot(sel_bf16, x, terms):
    return sum(jnp.dot(sel_bf16, p, preferred_element_type=F32) for p in _split_bf16(x, terms))


def _dot_nt(a, b):
    return lax.dot_general(a, b, (((1,), (1,)), ((), ())), preferred_element_type=F32)


def _bucket_table():
    n = np.arange(REL_MAX_DIST + 1)
    max_exact = REL_BUCKETS // 2
    nf = np.maximum(n, 1).astype(np.float32)
    large = max_exact + (np.log(nf / np.float32(max_exact)) / np.float32(math.log(REL_MAX_DIST / max_exact))
                         * np.float32(REL_BUCKETS - max_exact)).astype(np.int32)
    large = np.minimum(large, REL_BUCKETS - 1)
    return np.where(n < max_exact, n, large).astype(np.int32)


_BUCKETS = _bucket_table()


def _bias_of_dist(rel_bias, dist):
    idx = _BUCKETS[np.clip(dist, 0, REL_MAX_DIST)]
    out = jnp.take(rel_bias.astype(F32), jnp.asarray(idx.reshape(-1)), axis=0)
    return out.T.reshape((NSA_HEADS,) + dist.shape)


def _toeplitz_bias(rel_bias, rows, cols, c0):
    n = rows + cols
    d = np.arange(n) + c0 - (rows - 1)
    v = jnp.take(rel_bias.astype(F32), jnp.asarray(_BUCKETS[np.clip(d, 0, REL_MAX_DIST)]), axis=0).T
    flat = jnp.tile(v, (1, rows))[:, :rows * (n - 1)]
    return flat.reshape(NSA_HEADS, rows, n - 1)[:, :, rows - 1:rows - 1 + cols]


def _mod_kernel(c_ref, w_ref, b_ref, o_ref):
    o_ref[...] = _dot32(_silu(c_ref[...]), w_ref[...]) + b_ref[...]


def _modulation(c, w, b):
    m, d = c.shape
    n = w.shape[1]
    tn = MOD_COLS
    return pl.pallas_call(
        _mod_kernel,
        out_shape=jax.ShapeDtypeStruct((m, n), F32),
        grid=(n // tn,),
        in_specs=[pl.BlockSpec((m, d), lambda j: (0, 0)),
                  pl.BlockSpec((d, tn), lambda j: (0, j)),
                  pl.BlockSpec((1, tn), lambda j: (0, j))],
        out_specs=pl.BlockSpec((m, tn), lambda j: (0, j)),
        name="adaln_mod",
    )(c, w, b.reshape(1, n))


def _inproj_kernel(x_ref, g_ref, sc_ref, sh_ref, w_ref, *out_refs, per_row):
    x = x_ref[...]
    xn = x * lax.rsqrt(jnp.mean(x * x, axis=-1, keepdims=True) + NORM_EPS)
    sc = sc_ref[...] if per_row else sc_ref[0]
    sh = sh_ref[...] if per_row else sh_ref[0]
    h = ((xn * g_ref[...]) * (1.0 + sc) + sh).astype(BF16)
    for name, off, width, ref in zip(_SEG_NAMES, _SEG_OFF, _SEG_PAD, out_refs):
        r = jnp.dot(h, w_ref[:, off:off + width], preferred_element_type=F32)
        if name == "q":
            r = r * (HEAD_DIM ** -0.5)
            if not per_row:
                qt_ref = out_refs[-1]
                for t in range(r.shape[0] // Q_BLOCK):
                    for g in range(NSA_KV_HEADS):
                        qt_ref[t, g] = _stack_q_f32(r[t * Q_BLOCK:(t + 1) * Q_BLOCK] * LOG2E, g).T.astype(BF16)
        ref[...] = r.astype(ref.dtype)


def _in_projection(x2d, g, scale, shift, w_pad, rows_per_batch):
    m = x2d.shape[0]
    per_row = rows_per_batch == 1
    tm = m if per_row else PROJ_ROWS
    outs = tuple(jax.ShapeDtypeStruct((m, w), BF16 if name in ("z_s", "z_a") else F32)
                 for name, w in zip(_SEG_NAMES, _SEG_PAD))
    out_specs = tuple(pl.BlockSpec((tm, w), lambda i: (i, 0)) for w in _SEG_PAD)
    if per_row:
        mod_spec = pl.BlockSpec((tm, D_MODEL), lambda i: (0, 0))
        sc, sh = scale, shift
    else:
        mod_spec = pl.BlockSpec((1, 1, D_MODEL), lambda i: ((i * tm) // rows_per_batch, 0, 0))
        sc, sh = scale[:, None, :], shift[:, None, :]
        outs += (jax.ShapeDtypeStruct((m // Q_BLOCK, NSA_KV_HEADS, LANE, ROWS), BF16),)
        out_specs += (pl.BlockSpec((tm // Q_BLOCK, NSA_KV_HEADS, LANE, ROWS), lambda i: (i, 0, 0, 0)),)
    return pl.pallas_call(
        functools.partial(_inproj_kernel, per_row=per_row),
        out_shape=outs,
        grid=(m // tm,),
        in_specs=[pl.BlockSpec((tm, D_MODEL), lambda i: (i, 0)),
                  pl.BlockSpec((1, D_MODEL), lambda i: (0, 0)),
                  mod_spec, mod_spec,
                  pl.BlockSpec((D_MODEL, IN_PAD), lambda i: (0, 0))],
        out_specs=out_specs,
        compiler_params=pltpu.CompilerParams(vmem_limit_bytes=VMEM_LIMIT),
        name="in_projection",
    )(x2d, g.reshape(1, D_MODEL), sc, sh, w_pad)


def _softplus(x):
    return jnp.maximum(x, 0.0) + jnp.log(1.0 + jnp.exp(-jnp.abs(x)))


def _ssd_kernel(xbc_ref, dt_ref, cw_ref, cb_ref, dtb_ref, alog_ref, dsk_ref, e_ref, tril_ref,
                y_ref, hfin_ref, xe_sc, st_sc):
    c = pl.program_id(1)
    q = SSD_CHUNK
    n_pairs = SSD_HEADS // 2

    @pl.when(c == 0)
    def _():
        xe_sc[0:8, :] = jnp.zeros((8, CONV_DIM), F32)
        st_sc[...] = jnp.zeros(st_sc.shape, F32)

    xe_sc[8:8 + q, :] = xbc_ref[0]
    acc = cb_ref[...] + cw_ref[0:1, :] * xe_sc[5:5 + q, :]
    for k in range(1, SSD_CONV):
        acc = acc + cw_ref[k:k + 1, :] * xe_sc[5 + k:5 + k + q, :]
    u = _silu(acc)
    xe_sc[0:8, :] = xe_sc[q:q + 8, :]

    xs = u[:, :SSD_WIDTH]
    gn = SSD_GROUPS * SSD_STATE
    bm = u[:, SSD_WIDTH:SSD_WIDTH + gn]
    cm = u[:, SSD_WIDTH + gn:]

    dt = _softplus(dt_ref[0] + dtb_ref[...])
    a = dt * (-jnp.exp(alog_ref[...]))
    cs = _sel_dot(tril_ref[...], a, 3)
    cs_t = cs.T
    cs_last = cs[q - 1:q, :]
    e = e_ref[...]
    dt_e = _expand2(dt, e)
    w_e = _expand2(dt * jnp.exp(cs_last - cs), e)
    ecs_e = _expand2(jnp.exp(cs), e)
    tot_e = _expand2(jnp.broadcast_to(jnp.exp(cs_last), (8, LANE)), e)[0:1, :]
    xdt = (xs * dt_e).astype(BF16)
    xw = (xs * w_e).astype(BF16)

    li = lax.broadcasted_iota(jnp.int32, (q, q), 0)
    si = lax.broadcasted_iota(jnp.int32, (q, q), 1)
    tri = li >= si
    lane = lax.broadcasted_iota(jnp.int32, (q, LANE), 1)

    for g in range(SSD_GROUPS):
        cg = cm[:, g * SSD_STATE:(g + 1) * SSD_STATE].astype(BF16)
        bg = bm[:, g * SSD_STATE:(g + 1) * SSD_STATE]
        cb = _dot_nt(cg, bg.astype(BF16))
        bg_t = bg.T.astype(BF16)
        for jp in range(n_pairs // SSD_GROUPS):
            j = g * (n_pairs // SSD_GROUPS) + jp
            sl = slice(j * LANE, (j + 1) * LANE)
            xdt_p = xdt[:, sl]
            ys = []
            for hh in (2 * j, 2 * j + 1):
                diff = cs[:, hh:hh + 1] - cs_t[hh:hh + 1, :]
                lmat = jnp.exp(jnp.where(tri, diff, NEG))
                ys.append(jnp.dot((cb * lmat).astype(BF16), xdt_p, preferred_element_type=F32))
            y_diag = jnp.where(lane < HALF, ys[0], ys[1])
            st = st_sc[j]
            y_off = jnp.dot(cg, st.astype(BF16), preferred_element_type=F32) * ecs_e[:, sl]
            y_ref[0, :, sl] = y_diag + y_off + xs[:, sl] * dsk_ref[:, sl]
            new = jnp.dot(bg_t, xw[:, sl], preferred_element_type=F32)
            st_sc[j] = st * tot_e[:, sl] + new

    @pl.when(c == pl.num_programs(1) - 1)
    def _():
        for j in range(n_pairs):
            hfin_ref[0, j * LANE:(j + 1) * LANE, :] = st_sc[j].T


def _head_expand():
    e = np.zeros((LANE, SSD_WIDTH), np.float32)
    for h in range(SSD_HEADS):
        e[h, h * HEAD_DIM:(h + 1) * HEAD_DIM] = 1.0
    return jnp.asarray(e)


def _pad_lanes(v):
    return jnp.pad(v.astype(F32), (0, LANE - v.shape[0])).reshape(1, LANE)


def _ssd_prompt(xbc, dt, conv_w, conv_b, dt_bias, a_log, d_skip):
    b, l, _ = xbc.shape
    nc = l // SSD_CHUNK
    full = lambda shape: pl.BlockSpec(shape, lambda i, c: (0,) * len(shape))
    y, hfin = pl.pallas_call(
        _ssd_kernel,
        out_shape=(jax.ShapeDtypeStruct((b, l, SSD_WIDTH), F32),
                   jax.ShapeDtypeStruct((b, SSD_HEADS * HEAD_DIM, SSD_STATE), F32)),
        grid=(b, nc),
        in_specs=[pl.BlockSpec((1, SSD_CHUNK, CONV_DIM), lambda i, c: (i, c, 0)),
                  pl.BlockSpec((1, SSD_CHUNK, LANE), lambda i, c: (i, c, 0)),
                  full((SSD_CONV, CONV_DIM)), full((1, CONV_DIM)), full((1, LANE)), full((1, LANE)),
                  full((1, SSD_WIDTH)), full((2 * LANE, SSD_WIDTH)), full((SSD_CHUNK, SSD_CHUNK))],
        out_specs=(pl.BlockSpec((1, SSD_CHUNK, SSD_WIDTH), lambda i, c: (i, c, 0)),
                   pl.BlockSpec((1, SSD_HEADS * HEAD_DIM, SSD_STATE), lambda i, c: (i, 0, 0))),
        scratch_shapes=[pltpu.VMEM((SSD_CHUNK + 8, CONV_DIM), F32),
                        pltpu.VMEM((SSD_HEADS // 2, SSD_STATE, LANE), F32)],
        compiler_params=pltpu.CompilerParams(dimension_semantics=("arbitrary", "arbitrary"),
                                             vmem_limit_bytes=VMEM_LIMIT),
        name="ssd_prompt",
    )(xbc, dt, conv_w, conv_b.reshape(1, CONV_DIM), _pad_lanes(dt_bias), _pad_lanes(a_log),
      jnp.repeat(d_skip.astype(F32), HEAD_DIM).reshape(1, SSD_WIDTH),
      jnp.concatenate([_head_expand(), _head_expand()], axis=0).astype(BF16),
      jnp.asarray(np.tril(np.ones((SSD_CHUNK, SSD_CHUNK), np.float32)), BF16))
    return y, hfin.reshape(b, SSD_HEADS, HEAD_DIM, SSD_STATE)


def _ssd_step_kernel(xbc_ref, c0_ref, c1_ref, c2_ref, dt_ref, h0_ref, cw_ref, cb_ref, dtb_ref, alog_ref,
                     dsk_ref, e_ref, y_ref, hout_ref, xt_sc, dect_sc, bc_sc, yt_sc, xs_sc):
    b = pl.program_id(0)
    nb = xbc_ref.shape[0]
    rows = SSD_HEADS * HEAD_DIM
    gn = SSD_GROUPS * SSD_STATE

    @pl.when(b == 0)
    def _():
        acc = (cb_ref[...] + cw_ref[0:1, :] * c0_ref[...] + cw_ref[1:2, :] * c1_ref[...]
               + cw_ref[2:3, :] * c2_ref[...] + cw_ref[3:4, :] * xbc_ref[...])
        u = _silu(acc)
        xs = u[:, :SSD_WIDTH]
        dt = _softplus(dt_ref[...] + dtb_ref[...])
        dec = jnp.exp(dt * (-jnp.exp(alog_ref[...])))
        e = e_ref[...]
        xdt = xs * _dot32(dt, e)
        dec_e = _dot32(dec, e)
        pad = jnp.zeros((LANE - nb, SSD_WIDTH), F32)
        xt_sc[...] = jnp.concatenate([xdt, pad], axis=0).T
        dect_sc[...] = jnp.concatenate([dec_e, pad], axis=0).T
        bc_sc[...] = u[:, SSD_WIDTH:]
        xs_sc[...] = xs
        yt_sc[...] = jnp.zeros(yt_sc.shape, F32)

    ri = lax.broadcasted_iota(jnp.int32, (LANE, LANE), 0)
    onehot = jnp.where(ri == b, 1.0, 0.0)
    xcol = _dot32(xt_sc[...], onehot)
    dcol = _dot32(dect_sc[...], onehot)
    bc = bc_sc[pl.ds(b, 1), :]
    row = lax.broadcasted_iota(jnp.int32, (rows, SSD_STATE), 0)
    first = row < rows // SSD_GROUPS
    b_full = jnp.where(first, bc[:, 0:SSD_STATE], bc[:, SSD_STATE:gn])
    c_full = jnp.where(first, bc[:, gn:gn + SSD_STATE], bc[:, gn + SSD_STATE:])
    new = dcol * h0_ref[0] + xcol * b_full
    hout_ref[0] = new
    ycol = _dot32(new * c_full, jnp.ones((SSD_STATE, LANE), F32))
    lane = lax.broadcasted_iota(jnp.int32, (rows, LANE), 1)
    yt_sc[...] = jnp.where(lane == b, ycol, yt_sc[...])

    @pl.when(b == nb - 1)
    def _():
        y_ref[...] = yt_sc[...].T[0:nb, :] + xs_sc[...] * dsk_ref[...]


def _ssd_step(xbc, conv_state, dt, h0, conv_w, conv_b, dt_bias, a_log, d_skip):
    nb = xbc.shape[0]
    rows = SSD_HEADS * HEAD_DIM
    full = lambda shape: pl.BlockSpec(shape, lambda i: (0,) * len(shape))
    y, hout = pl.pallas_call(
        _ssd_step_kernel,
        out_shape=(jax.ShapeDtypeStruct((nb, SSD_WIDTH), F32),
                   jax.ShapeDtypeStruct((nb, rows, SSD_STATE), F32)),
        grid=(nb,),
        in_specs=[full((nb, CONV_DIM)), full((nb, CONV_DIM)), full((nb, CONV_DIM)), full((nb, CONV_DIM)),
                  full((nb, LANE)),
                  pl.BlockSpec((1, rows, SSD_STATE), lambda i: (i, 0, 0)),
                  full((SSD_CONV, CONV_DIM)), full((1, CONV_DIM)), full((1, LANE)), full((1, LANE)),
                  full((1, SSD_WIDTH)), full((LANE, SSD_WIDTH))],
        out_specs=(full((nb, SSD_WIDTH)),
                   pl.BlockSpec((1, rows, SSD_STATE), lambda i: (i, 0, 0))),
        scratch_shapes=[pltpu.VMEM((rows, LANE), F32), pltpu.VMEM((rows, LANE), F32),
                        pltpu.VMEM((nb, 2 * SSD_GROUPS * SSD_STATE), F32),
                        pltpu.VMEM((rows, LANE), F32), pltpu.VMEM((nb, SSD_WIDTH), F32)],
        compiler_params=pltpu.CompilerParams(dimension_semantics=("arbitrary",)),
        name="ssd_step",
    )(xbc, conv_state[:, 0], conv_state[:, 1], conv_state[:, 2], dt, h0.reshape(nb, rows, SSD_STATE),
      conv_w, conv_b.reshape(1, CONV_DIM), _pad_lanes(dt_bias), _pad_lanes(a_log),
      jnp.repeat(d_skip.astype(F32), HEAD_DIM).reshape(1, SSD_WIDTH), _head_expand())
    return y, hout.reshape(nb, SSD_HEADS, HEAD_DIM, SSD_STATE)


def _compress_kernel(pt_ref, *refs, transposed, n_pg):
    segs = PAGE_SIZE // CMP_STRIDE
    if transposed:
        perm_ref, w1_ref, pe_ref, w2_ref, out_ref, sh_sc, pe_sc, xs_sc = refs[n_pg:]
        for i in range(n_pg):
            xs_sc[i] = _dot_nt(perm_ref[...], refs[i][0].astype(BF16))

        def token_rows(k, o):
            return jnp.concatenate([xs_sc[i, o * segs:(o + 1) * segs, k * LANE:(k + 1) * LANE]
                                    for i in range(n_pg)], axis=0)
    else:
        w1_ref, pe_ref, w2_ref, out_ref, sh_sc, pe_sc = refs[2 * n_pg:]

        def token_rows(k, o):
            return jnp.concatenate([refs[k * n_pg + i][0, pl.ds(o, segs, stride=CMP_STRIDE), :]
                                    for i in range(n_pg)], axis=0)
    s = pl.program_id(1)
    rows = n_pg * segs
    hid2 = NSA_KV_HEADS * CMP_HID

    @pl.when(s == 0)
    def _():
        sh_sc[:, 0:8, :] = jnp.zeros((2, 8, hid2), F32)
        for k in range(2):
            t = jnp.zeros((8, 2 * hid2), F32)
            for o in range(0, CMP_STRIDE, 2):
                pe2 = jnp.concatenate([pe_ref[o, k], pe_ref[o + 1, k]], axis=1)
                t = t + jnp.dot(pe2.astype(BF16), w1_ref[o // 2, k], preferred_element_type=F32)
            pe_sc[k] = jnp.broadcast_to(t[0:1, 0:hid2] + t[1:2, hid2:], (8, hid2))

    for k in range(2):
        acc = jnp.zeros((rows, 2 * hid2), F32)
        for o in range(0, CMP_STRIDE, 2):
            xo = jnp.concatenate([token_rows(k, o), token_rows(k, o + 1)], axis=1)
            acc = acc + jnp.dot(xo.astype(BF16), w1_ref[o // 2, k], preferred_element_type=F32)
        sh_sc[k, 8:8 + rows, :] = acc[:, 0:hid2]
        pre = acc[:, hid2:] + sh_sc[k, 7:7 + rows, :] + pe_sc[k, 0:1, :]
        sh_sc[k, 0:8, :] = sh_sc[k, rows:rows + 8, :]
        out_ref[0, :, k * LANE:(k + 1) * LANE] = jnp.dot(_silu(pre).astype(BF16), w2_ref[k],
                                                         preferred_element_type=F32)


def _compress_weights(cmp_pe, cmp_w1, cmp_w2):
    span = CMP_BLOCK // CMP_STRIDE
    w1s = cmp_w1.astype(F32).reshape(2, span, CMP_STRIDE, HEAD_DIM, CMP_HID)
    z = jnp.zeros((2, span, CMP_STRIDE, HEAD_DIM, CMP_HID), F32)
    top = jnp.concatenate([w1s, z], axis=-1)
    bot = jnp.concatenate([z, w1s], axis=-1)
    bd = jnp.concatenate([top, bot], axis=-2)
    w1 = jnp.transpose(bd, (2, 0, 3, 1, 4)).reshape(CMP_STRIDE, 2, LANE, span * 2 * CMP_HID).astype(BF16)
    w1 = jnp.transpose(w1.reshape(CMP_STRIDE // 2, 2, 2, LANE, span * 2 * CMP_HID), (0, 2, 1, 3, 4))
    w1 = w1.reshape(CMP_STRIDE // 2, 2, 2 * LANE, span * 2 * CMP_HID)
    pe = cmp_pe.astype(F32).reshape(2, span, CMP_STRIDE, HEAD_DIM)
    pe = jnp.transpose(pe, (2, 0, 1, 3))
    pe = jnp.concatenate([pe, pe], axis=-1)
    pe = jnp.pad(pe, ((0, 0), (0, 0), (0, 8 - span), (0, 0)))
    w2 = cmp_w2.astype(F32)
    z2 = jnp.zeros_like(w2)
    w2bd = jnp.concatenate([jnp.concatenate([w2, z2], axis=-1), jnp.concatenate([z2, w2], axis=-1)],
                           axis=-2).astype(BF16)
    return w1, pe, w2bd


def _compress(pages_arr, page_ids, cweights, transposed):
    nb, n_pages = page_ids.shape
    n_pg = math.gcd(n_pages, MAX_PAGES_PER_STEP)
    steps = n_pages // n_pg
    segs = PAGE_SIZE // CMP_STRIDE
    rows = n_pg * segs
    w1, pe, w2bd = cweights
    hid2 = NSA_KV_HEADS * CMP_HID
    page_of = lambda b, s, pt, i: pt[(b * steps + s) * n_pg + i]
    scratch = [pltpu.VMEM((2, rows + 8, hid2), F32), pltpu.VMEM((2, 8, hid2), F32)]
    if transposed:
        page_specs = [pl.BlockSpec((1, KV_COLS, PAGE_SIZE), lambda b, s, pt, i=i: (page_of(b, s, pt, i), 0, 0))
                      for i in range(n_pg)]
        scratch.append(pltpu.VMEM((n_pg, PAGE_SIZE, KV_COLS), F32))
        perm = np.zeros((PAGE_SIZE, PAGE_SIZE), np.float32)
        for o in range(CMP_STRIDE):
            for sg in range(segs):
                perm[o * segs + sg, sg * CMP_STRIDE + o] = 1.0
        extra, extra_specs = [jnp.asarray(perm, BF16)], [pl.BlockSpec((PAGE_SIZE, PAGE_SIZE), lambda b, s, pt: (0, 0))]
    else:
        extra, extra_specs = [], []
        page_specs = [pl.BlockSpec((1, PAGE_SIZE, LANE), lambda b, s, pt, i=i, k=k: (page_of(b, s, pt, i), 0, k))
                      for k in range(2) for i in range(n_pg)]
    full = lambda shape: pl.BlockSpec(shape, lambda b, s, pt: (0,) * len(shape))
    return pl.pallas_call(
        functools.partial(_compress_kernel, transposed=transposed, n_pg=n_pg),
        out_shape=jax.ShapeDtypeStruct((nb, n_pages * segs, KV_COLS), F32),
        grid_spec=pltpu.PrefetchScalarGridSpec(
            num_scalar_prefetch=1,
            grid=(nb, steps),
            in_specs=page_specs + extra_specs + [full(w1.shape), full(pe.shape), full(w2bd.shape)],
            out_specs=pl.BlockSpec((1, rows, KV_COLS), lambda b, s, pt: (b, s, 0)),
            scratch_shapes=scratch),
        compiler_params=pltpu.CompilerParams(dimension_semantics=("arbitrary", "arbitrary"),
                                             vmem_limit_bytes=VMEM_LIMIT),
        name="nsa_compress",
    )(page_ids.reshape(-1), *([pages_arr] * len(page_specs)), *extra, w1, pe, w2bd)


def _overlap_matrix(n_rows, n_blocks, n_cols):
    m = np.arange(n_rows)[:, None]
    j = np.arange(n_cols)[None, :]
    cs = (m - 1) * CMP_STRIDE
    ov = (m >= 1) & (j < n_blocks) & (cs < j * SLC_BLOCK + SLC_BLOCK) & (cs + CMP_BLOCK > j * SLC_BLOCK)
    return jnp.asarray(ov.astype(np.float32))


def _stack_q(q, g):
    return _stack_q_f32(q, g).astype(BF16)


def _stack_q_f32(q, g):
    lane = lax.broadcasted_iota(jnp.int32, (Q_BLOCK, LANE), 1)
    keep = (lane < HALF) if g == 0 else (lane >= HALF)
    parts = []
    for jp in range(NSA_REP // 2):
        j = g * (NSA_REP // 2) + jp
        slab = q[:, j * LANE:(j + 1) * LANE]
        rolled = pltpu.roll(slab, HALF, 1)
        first, second = (slab, rolled) if g == 0 else (rolled, slab)
        parts.append(jnp.where(keep, first, 0.0))
        parts.append(jnp.where(keep, second, 0.0))
    return jnp.concatenate(parts, axis=0)


def _unstack_o(acc, g):
    lane = lax.broadcasted_iota(jnp.int32, (Q_BLOCK, LANE), 1)
    outs = []
    for jp in range(NSA_REP // 2):
        a = acc[(2 * jp) * Q_BLOCK:(2 * jp + 1) * Q_BLOCK]
        b = acc[(2 * jp + 1) * Q_BLOCK:(2 * jp + 2) * Q_BLOCK]
        if g == 0:
            outs.append(jnp.where(lane < HALF, a, pltpu.roll(b, HALF, 1)))
        else:
            outs.append(jnp.where(lane < HALF, pltpu.roll(a, HALF, 1), b))
    return jnp.concatenate(outs, axis=1)


def _tile8(x):
    return jnp.concatenate([x] * NSA_REP, axis=0)


def _rank_rows(imp, n_valid):
    sub = 8
    n_rank = -(-n_valid // sub) * sub
    chunks = [imp[c:c + sub] for c in range(0, n_rank, sub)]
    ranks = [jnp.zeros(ch.shape, F32) for ch in chunks]
    jrow = lax.broadcasted_iota(jnp.int32, chunks[0].shape, 0)
    for k in range(n_valid):
        rk = imp[k:k + 1, :]
        for c, ch in enumerate(chunks):
            if c * sub > k:
                ahead = rk >= ch
            elif c * sub + sub - 1 < k:
                ahead = rk > ch
            else:
                ahead = (rk > ch) | ((rk == ch) & (jrow + c * sub > k))
            ranks[c] = ranks[c] + jnp.where(ahead, 1.0, 0.0)
    rest = jnp.full((imp.shape[0] - n_rank, imp.shape[1]), float(n_valid), F32)
    return jnp.concatenate(ranks + [rest], axis=0)


def _cmp_kernel(q_ref, kv_ref, pb_ref, ov_ref, o_ref, sel_ref, bias_sc, *, n_keys, n_blocks):
    i = pl.program_id(0)
    qi = lax.broadcasted_iota(jnp.int32, (Q_BLOCK, n_keys), 0)
    mi = lax.broadcasted_iota(jnp.int32, (Q_BLOCK, n_keys), 1)
    qpos = Q_BLOCK * i + qi

    @pl.when(pl.program_id(1) == 0)
    def _():
        valid = (mi >= 1) & (CMP_STRIDE * mi + CMP_STRIDE - 1 <= qpos)
        for h in range(NSA_HEADS):
            bias_sc[h] = jnp.where(valid, pltpu.roll(pb_ref[h], (4 * i + 4) % n_keys, 1), NEG)

    rowvalid8 = _tile8(jnp.where(qpos[:, 0:1] >= CMP_BLOCK - 1, 1.0, 0.0))
    jj = lax.broadcasted_iota(jnp.int32, (LANE, LANE), 1)
    ov = ov_ref[...].astype(BF16)
    forced = (jj == 0) | (jj == i) | (jj == i - 1)
    for bb in range(q_ref.shape[0]):
        q = q_ref[bb] * LOG2E
        kv = kv_ref[bb]
        kc = kv[:, 0:LANE].astype(BF16)
        vc = kv[:, LANE:].astype(BF16)
        psum = []
        for g in range(NSA_KV_HEADS):
            s = _dot_nt(_stack_q(q, g), kc) + bias_sc[g * NSA_REP:(g + 1) * NSA_REP].reshape(ROWS, n_keys)
            p = jnp.exp2(s - jnp.max(s, axis=-1, keepdims=True))
            pc = p / jnp.sum(p, axis=-1, keepdims=True) * rowvalid8
            o_ref[bb, :, g * GROUP_W:(g + 1) * GROUP_W] = _unstack_o(
                jnp.dot(pc.astype(BF16), vc, preferred_element_type=F32), g).astype(BF16)
            ps = pc[0:Q_BLOCK]
            for r in range(1, NSA_REP):
                ps = ps + pc[r * Q_BLOCK:(r + 1) * Q_BLOCK]
            psum.append(ps)
        ps = jnp.concatenate(psum, axis=0)
        hi = ps.astype(BF16)
        lo = (ps - hi.astype(F32)).astype(BF16)
        imp = (jnp.dot(hi, ov, preferred_element_type=F32)
               + jnp.dot(lo, ov, preferred_element_type=F32))
        imp = jnp.where(forced, 1e6, jnp.where(jj <= i, imp, -1e6))
        imp = jnp.where(jj < n_blocks, imp, -2e6)
        rank = _rank_rows(imp.T, n_blocks)
        sel_ref[bb, 0] = jnp.where(rank < SLC_TOP, 1.0, 0.0).astype(BF16)


def _cmp_prompt(q, kvc_cmp, pattern, ov):
    b, l, _ = q.shape
    n_keys = kvc_cmp.shape[1]
    n_blocks = l // SLC_BLOCK
    nb = 2 if b % 2 == 0 else 1
    return pl.pallas_call(
        functools.partial(_cmp_kernel, n_keys=n_keys, n_blocks=n_blocks),
        out_shape=(jax.ShapeDtypeStruct((b, l, NSA_WIDTH), BF16),
                   jax.ShapeDtypeStruct((b, l // Q_BLOCK, LANE, LANE), BF16)),
        grid=(l // Q_BLOCK, b // nb),
        in_specs=[pl.BlockSpec((nb, Q_BLOCK, NSA_WIDTH), lambda i, bi: (bi, i, 0)),
                  pl.BlockSpec((nb, n_keys, KV_COLS), lambda i, bi: (bi, 0, 0)),
                  pl.BlockSpec(pattern.shape, lambda i, bi: (0, 0, 0)),
                  pl.BlockSpec(ov.shape, lambda i, bi: (0, 0))],
        out_specs=(pl.BlockSpec((nb, Q_BLOCK, NSA_WIDTH), lambda i, bi: (bi, i, 0)),
                   pl.BlockSpec((nb, 1, LANE, LANE), lambda i, bi: (bi, i, 0, 0))),
        scratch_shapes=[pltpu.VMEM(pattern.shape, F32)],
        compiler_params=pltpu.CompilerParams(dimension_semantics=("arbitrary", "arbitrary"),
                                             vmem_limit_bytes=VMEM_LIMIT),
        name="nsa_cmp_prompt",
    )(q, kvc_cmp, pattern, ov)


def _flash_update_t(st, vt, m, acc):
    m_new = jnp.maximum(m, jnp.max(st, axis=0, keepdims=True))
    p = jnp.exp2(st - m_new)
    acc = jnp.exp2(m - m_new) * acc + jnp.dot(vt, p.astype(BF16), preferred_element_type=F32)
    return m_new, acc


def _finish_t(acc):
    o = (acc[0:HEAD_DIM] / acc[HEAD_DIM:HEAD_DIM + 1]).T
    return jnp.concatenate([o[r * Q_BLOCK:(r + 1) * Q_BLOCK] for r in range(NSA_REP)], axis=1).astype(BF16)


def _slc_kernel(qt_ref, sel_ref, kx_ref, vt_ref, tnt_ref, tile_ref, o_ref, far_sc, near_sc, sta_sc, stb_sc, stn_sc):
    i = pl.program_id(1)
    n_far = (jnp.maximum(i - 3, 0) + 3) // 4
    sel = sel_ref[0, 0]
    jrow = lax.broadcasted_iota(jnp.int32, (LANE, ROWS), 0)
    a = ((i + 1) // 2) * LANE
    delta = Q_BLOCK * (i + 1) - a
    groups = range(NSA_KV_HEADS)
    for g in groups:
        qs_t = qt_ref[0, g]
        hit = jnp.dot(sel, tile_ref[g], preferred_element_type=F32) > 0.5
        near_sc[g] = jnp.concatenate([qs_t, jnp.where(hit & (jrow != PAD_FLAG_COL), 0.0, NEG).astype(BF16)], axis=0)
        far_sc[g] = jnp.concatenate([qs_t, jnp.where(hit & (jrow < i - 3), 0.0, NEG).astype(BF16)], axis=0)

    def far_scores(t, dst):
        start = pl.multiple_of(SLC_PAD + SLC_NEAR * t, SLC_NEAR)
        kx = kx_ref[0, pl.ds(start, SLC_NEAR), :]
        for g in groups:
            dst[g] = jnp.dot(kx, far_sc[g], preferred_element_type=F32)

    def far_softmax(t, src, carry):
        start = pl.multiple_of(SLC_PAD + SLC_NEAR * t, SLC_NEAR)
        return tuple(_flash_update_t(src[g], vt_ref[0, g, :, pl.ds(start, SLC_NEAR)], *carry[g]) for g in groups)

    def far_pair(u, carry):
        far_scores(2 * u + 1, stb_sc)
        carry = far_softmax(2 * u, sta_sc, carry)
        far_scores(2 * u + 2, sta_sc)
        return far_softmax(2 * u + 1, stb_sc, carry)

    far_scores(0, sta_sc)
    start = pl.multiple_of(a, LANE)
    kx = kx_ref[0, pl.ds(start, SLC_NEARW), :]
    for g in groups:
        stn_sc[g] = jnp.dot(kx, near_sc[g], preferred_element_type=F32)
    init = (jnp.full((1, ROWS), NEG, F32), jnp.zeros((HEAD_DIM + ONES_ROWS, ROWS), F32))
    carry = lax.fori_loop(0, (n_far + 1) // 2, far_pair, (init, init))

    for g in groups:
        st = stn_sc[g] + tnt_ref[delta // Q_BLOCK, g]
        _, acc = _flash_update_t(st, vt_ref[0, g, :, pl.ds(start, SLC_NEARW)], *carry[g])
        o_ref[0, :, g * GROUP_W:(g + 1) * GROUP_W] = _finish_t(acc)


def _slc_prompt(qt, sel, kx_pad, vt_pad, tnt, tile):
    b, n_tiles = sel.shape[:2]
    l = n_tiles * Q_BLOCK
    lp = kx_pad.shape[1]
    tq = Q_BLOCK
    n_chains = NSA_KV_HEADS
    return pl.pallas_call(
        _slc_kernel,
        out_shape=jax.ShapeDtypeStruct((b, l, NSA_WIDTH), BF16),
        grid=(b, n_tiles),
        in_specs=[pl.BlockSpec((1, NSA_KV_HEADS, LANE, ROWS), lambda bi, i: (bi * n_tiles + i, 0, 0, 0)),
                  pl.BlockSpec((1, 1, LANE, LANE), lambda bi, i: (bi, i, 0, 0)),
                  pl.BlockSpec((1, lp, 2 * LANE), lambda bi, i: (bi, 0, 0)),
                  pl.BlockSpec((1, NSA_KV_HEADS, HEAD_DIM + ONES_ROWS, lp), lambda bi, i: (bi, 0, 0, 0)),
                  pl.BlockSpec(tnt.shape, lambda bi, i: (0, 0, 0, 0)),
                  pl.BlockSpec(tile.shape, lambda bi, i: (0, 0, 0))],
        out_specs=pl.BlockSpec((1, tq, NSA_WIDTH), lambda bi, i: (bi, i, 0)),
        scratch_shapes=[pltpu.VMEM((n_chains, 2 * LANE, ROWS), BF16),
                        pltpu.VMEM((n_chains, 2 * LANE, ROWS), BF16),
                        pltpu.VMEM((n_chains, SLC_NEAR, ROWS), F32),
                        pltpu.VMEM((n_chains, SLC_NEAR, ROWS), F32),
                        pltpu.VMEM((n_chains, SLC_NEARW, ROWS), F32)],
        compiler_params=pltpu.CompilerParams(vmem_limit_bytes=VMEM_LIMIT),
        name="nsa_slc_prompt",
    )(qt, sel, kx_pad, vt_pad, tnt, tile)


def _group_tile():
    t = np.zeros((NSA_KV_HEADS, LANE, ROWS), np.float32)
    for g in range(NSA_KV_HEADS):
        for r in range(NSA_REP):
            for qq in range(Q_BLOCK):
                t[g, g * Q_BLOCK + qq, r * Q_BLOCK + qq] = 1.0
    return jnp.asarray(t, BF16)


def _win_kernel(qt_ref, kx_ref, vt_ref, tw_ref, o_ref, st_sc):
    i2 = pl.program_id(1)
    start = pl.multiple_of(i2 * LANE, LANE)
    kx = kx_ref[0, pl.ds(start, WIN_W), :]
    groups = range(NSA_KV_HEADS)
    row = lax.broadcasted_iota(jnp.int32, (LANE, 2 * ROWS), 0)
    pad_rows = jnp.where(row == 0, NEG, 0.0).astype(BF16)
    for g in groups:
        qs_t = jnp.concatenate([qt_ref[h, g] for h in range(2)], axis=1)
        qx = jnp.concatenate([qs_t, pad_rows], axis=0)
        st_sc[g] = jnp.dot(kx, qx, preferred_element_type=F32)
    for g in groups:
        st = st_sc[g] + tw_ref[g]
        p = jnp.exp2(st - jnp.max(st, axis=0, keepdims=True))
        acc = jnp.dot(vt_ref[0, g, :, pl.ds(start, WIN_W)], p.astype(BF16), preferred_element_type=F32)
        for h in range(2):
            o_ref[0, h * Q_BLOCK:(h + 1) * Q_BLOCK, g * GROUP_W:(g + 1) * GROUP_W] = _finish_t(
                acc[:, h * ROWS:(h + 1) * ROWS])


def _win_prompt(qt, kx_pad, vt_pad, tw):
    b = kx_pad.shape[0]
    l = qt.shape[0] // b * Q_BLOCK
    lp = kx_pad.shape[1]
    tq = 2 * Q_BLOCK
    steps = l // tq
    return pl.pallas_call(
        _win_kernel,
        out_shape=jax.ShapeDtypeStruct((b, l, NSA_WIDTH), BF16),
        grid=(b, steps),
        in_specs=[pl.BlockSpec((2, NSA_KV_HEADS, LANE, ROWS), lambda bi, i: (bi * steps + i, 0, 0, 0)),
                  pl.BlockSpec((1, lp, 2 * LANE), lambda bi, i: (bi, 0, 0)),
                  pl.BlockSpec((1, NSA_KV_HEADS, HEAD_DIM + ONES_ROWS, lp), lambda bi, i: (bi, 0, 0, 0)),
                  pl.BlockSpec(tw.shape, lambda bi, i: (0, 0, 0))],
        out_specs=pl.BlockSpec((1, tq, NSA_WIDTH), lambda bi, i: (bi, i, 0)),
        scratch_shapes=[pltpu.VMEM((NSA_KV_HEADS, WIN_W, 2 * ROWS), F32)],
        compiler_params=pltpu.CompilerParams(vmem_limit_bytes=VMEM_LIMIT),
        name="nsa_win_prompt",
    )(qt, kx_pad, vt_pad, tw)


def _dense1_kernel(q_ref, k_ref, v_ref, bm_ref, o_ref, ps_ref):
    k = k_ref[0].astype(BF16)
    v = v_ref[0].astype(BF16)
    for g in range(NSA_KV_HEADS):
        rows = slice(g * NSA_REP, (g + 1) * NSA_REP)
        s = _dot_nt(q_ref[0, rows, :], k) + bm_ref[rows, :]
        p = jnp.exp(s - jnp.max(s, axis=-1, keepdims=True))
        pc = p / jnp.sum(p, axis=-1, keepdims=True)
        o_ref[0, rows, :] = jnp.dot(pc.astype(BF16), v, preferred_element_type=F32)
        ps_ref[0, g:g + 1, :] = jnp.sum(pc, axis=0, keepdims=True)


def _dense1(q_pad, kv, biasmask):
    nb, n, _ = kv.shape
    return pl.pallas_call(
        _dense1_kernel,
        out_shape=(jax.ShapeDtypeStruct((nb, NSA_HEADS, LANE), F32),
                   jax.ShapeDtypeStruct((nb, NSA_KV_HEADS, n), F32)),
        grid=(nb,),
        in_specs=[pl.BlockSpec((1, NSA_HEADS, LANE), lambda b: (b, 0, 0)),
                  pl.BlockSpec((1, n, LANE), lambda b: (b, 0, 0)),
                  pl.BlockSpec((1, n, LANE), lambda b: (b, 0, 1)),
                  pl.BlockSpec((NSA_HEADS, n), lambda b: (0, 0))],
        out_specs=(pl.BlockSpec((1, NSA_HEADS, LANE), lambda b: (b, 0, 0)),
                   pl.BlockSpec((1, NSA_KV_HEADS, n), lambda b: (b, 0, 0))),
        name="nsa_dense_sample",
    )(q_pad, kv, kv, biasmask)


def _take_group_half(o_pad):
    nb = o_pad.shape[0]
    o = o_pad.reshape(nb, NSA_KV_HEADS, NSA_REP, NSA_KV_HEADS, HEAD_DIM)
    o = jnp.stack([o[:, g, :, g, :] for g in range(NSA_KV_HEADS)], axis=1)
    return o.reshape(nb, NSA_WIDTH)


def _topk_kernel(ps_ref, ov_ref, idx_ref, imp_sc, *, n_blocks, cur):
    n_rows = ps_ref.shape[0]
    n_cols = ov_ref.shape[1]
    ps = jnp.concatenate([ps_ref[...], jnp.zeros((LANE - n_rows, ps_ref.shape[1]), F32)], axis=0)
    imp = _dot32(ps, ov_ref[...])
    jj = lax.broadcasted_iota(jnp.int32, (LANE, n_cols), 1)
    forced = (jj == 0) | (jj == cur) | (jj == cur - 1)
    imp = jnp.where(forced, 1e6, jnp.where(jj <= cur, imp, -1e6))
    imp = jnp.where(jj < n_blocks, imp, -2e6)
    imp_t = imp.T
    imp_sc[...] = imp_t
    jrow = lax.broadcasted_iota(jnp.int32, (n_cols, LANE), 0)

    def body(k, rank):
        rk = imp_sc[pl.ds(k, 1), :]
        ahead = (rk > imp_t) | ((rk == imp_t) & (jrow > k))
        return rank + jnp.where(ahead, 1.0, 0.0)

    rank = lax.fori_loop(0, n_blocks, body, jnp.zeros((n_cols, LANE), F32))
    jf = jrow.astype(F32)
    rows = [jnp.sum(jnp.where(rank == float(r), jf, 0.0), axis=0, keepdims=True) for r in range(SLC_TOP)]
    idx_ref[...] = jnp.concatenate(rows, axis=0).astype(jnp.int32)


def _topk_sample(psum, ov, n_blocks, cur):
    n_rows, n_keys = psum.shape
    n_cols = ov.shape[1]
    return pl.pallas_call(
        functools.partial(_topk_kernel, n_blocks=n_blocks, cur=cur),
        out_shape=jax.ShapeDtypeStruct((SLC_TOP, LANE), jnp.int32),
        grid=(1,),
        in_specs=[pl.BlockSpec((n_rows, n_keys), lambda i: (0, 0)),
                  pl.BlockSpec(ov.shape, lambda i: (0, 0))],
        out_specs=pl.BlockSpec((SLC_TOP, LANE), lambda i: (0, 0)),
        scratch_shapes=[pltpu.VMEM((n_cols, LANE), F32)],
        name="nsa_topk_sample",
    )(psum, ov)


def _slc1_kernel(phys_ref, jsel_ref, q_ref, *refs, cur, past):
    pages = refs[:SLC_TOP]
    new_ref, bb_ref, o_ref = refs[SLC_TOP:]
    b = pl.program_id(0)
    g = pl.program_id(1)
    goff = pl.multiple_of(g * HEAD_DIM, HEAD_DIM)
    q = q_ref[0]
    lane = lax.broadcasted_iota(jnp.int32, (NSA_REP, PAGE_SIZE), 1)
    first = lax.broadcasted_iota(jnp.int32, (HEAD_DIM, PAGE_SIZE), 1) == 0
    new_k = jnp.where(first, new_ref[0, pl.ds(goff, HEAD_DIM), :], 0.0)
    new_v = jnp.where(first, new_ref[0, pl.ds(LANE + goff, HEAD_DIM), :], 0.0)
    scores, values = [], []
    for n in range(SLC_TOP):
        j = jsel_ref[(b * NSA_KV_HEADS + g) * SLC_TOP + n]
        kt = jnp.where(j == cur, new_k, pages[n][0, pl.ds(goff, HEAD_DIM), :])
        vt = jnp.where(j == cur, new_v, pages[n][0, pl.ds(LANE + goff, HEAD_DIM), :])
        s = jnp.dot(q, kt.astype(BF16), preferred_element_type=F32)
        ok = (lane // SLC_BLOCK == j % 2) & ((j // 2) * PAGE_SIZE + lane <= past)
        bias = bb_ref[jnp.clip(j - (cur - 3), 0, 3), pl.ds(pl.multiple_of(g * NSA_REP, NSA_REP), NSA_REP), :]
        scores.append(jnp.where(ok, s + bias, NEG))
        values.append(vt.astype(BF16))
    s_all = jnp.concatenate(scores, axis=1)
    p = jnp.exp(s_all - jnp.max(s_all, axis=-1, keepdims=True))
    acc = jnp.zeros((NSA_REP, HEAD_DIM), F32)
    for n in range(SLC_TOP):
        acc = acc + _dot_nt(p[:, n * PAGE_SIZE:(n + 1) * PAGE_SIZE].astype(BF16), values[n])
    o_ref[0] = acc / jnp.sum(p, axis=-1, keepdims=True)


def _slc_sample(q, cache_pages_t, phys, jsel, new_cols, biasblk, cur, past):
    nb = q.shape[0]
    idx = lambda b, g, n: (b * NSA_KV_HEADS + g) * SLC_TOP + n
    page_specs = [pl.BlockSpec((1, KV_COLS, PAGE_SIZE), lambda b, g, ph, js, n=n: (ph[idx(b, g, n)], 0, 0))
                  for n in range(SLC_TOP)]
    return pl.pallas_call(
        functools.partial(_slc1_kernel, cur=cur, past=past),
        out_shape=jax.ShapeDtypeStruct((nb, NSA_HEADS, HEAD_DIM), F32),
        grid_spec=pltpu.PrefetchScalarGridSpec(
            num_scalar_prefetch=2,
            grid=(nb, NSA_KV_HEADS),
            in_specs=[pl.BlockSpec((1, NSA_REP, HEAD_DIM), lambda b, g, ph, js: (b, g, 0))] + page_specs
            + [pl.BlockSpec((1, KV_COLS, 1), lambda b, g, ph, js: (b, 0, 0)),
               pl.BlockSpec(biasblk.shape, lambda b, g, ph, js: (0, 0, 0))],
            out_specs=pl.BlockSpec((1, NSA_REP, HEAD_DIM), lambda b, g, ph, js: (b, g, 0))),
        name="nsa_slc_sample",
    )(phys, jsel, q, *([cache_pages_t] * SLC_TOP), new_cols, biasblk)


def _out_kernel(x_ref, gate_ref, yssd_ref, zs_ref, oc_ref, os_ref, ow_ref, gl_ref, za_ref,
                nw1_ref, nw2_ref, w_ref, eg_ref, fg_ref, o_ref, *, per_row, final):
    gates = _sigmoid(gl_ref[...])
    g2 = jnp.concatenate(_split_bf16(gates, 2), axis=1)
    expand = lambda br: jnp.dot(g2, eg_ref[br], preferred_element_type=F32)
    y_nsa = expand(0) * oc_ref[...] + expand(1) * os_ref[...] + expand(2) * ow_ref[...]

    def gated_norm(y, z, w):
        u = y * _silu(z.astype(F32))
        half = u.shape[1] // 2
        parts = []
        for g in range(2):
            ug = u[:, g * half:(g + 1) * half]
            parts.append(ug * lax.rsqrt(jnp.mean(ug * ug, axis=-1, keepdims=True) + NORM_EPS))
        return (jnp.concatenate(parts, axis=1) * w).astype(BF16)

    m1 = gated_norm(yssd_ref[...], zs_ref[...], nw1_ref[...])
    m2 = gated_norm(y_nsa, za_ref[...], nw2_ref[...])
    proj = (jnp.dot(m1, w_ref[0:SSD_WIDTH, :], preferred_element_type=F32)
            + jnp.dot(m2, w_ref[SSD_WIDTH:, :], preferred_element_type=F32))
    gate = gate_ref[...] if per_row else gate_ref[0]
    out = x_ref[...] + gate * proj
    if final:
        out = out * lax.rsqrt(jnp.mean(out * out, axis=-1, keepdims=True) + NORM_EPS) * fg_ref[...]
    o_ref[...] = out


def _gate_expand():
    e = np.zeros((3, LANE, NSA_WIDTH), np.float32)
    for br in range(3):
        for h in range(NSA_HEADS):
            e[br, br * NSA_HEADS + h, h * HEAD_DIM:(h + 1) * HEAD_DIM] = 1.0
    return jnp.asarray(np.concatenate([e, e], axis=1), BF16)


def _layer_out(x2d, gate, y_ssd, z_s, o_cmp, o_slc, o_win, gl, z_a, nw1, nw2, w_out_bf, final_g,
               rows_per_batch, final):
    m = x2d.shape[0]
    per_row = rows_per_batch == 1
    tm = m if per_row else PROJ_ROWS
    if per_row:
        gate_spec = pl.BlockSpec((tm, D_MODEL), lambda i: (0, 0))
        gt = gate
    else:
        gate_spec = pl.BlockSpec((1, 1, D_MODEL), lambda i: ((i * tm) // rows_per_batch, 0, 0))
        gt = gate[:, None, :]
    row = lambda w: pl.BlockSpec((tm, w), lambda i: (i, 0))
    full = lambda shape: pl.BlockSpec(shape, lambda i: (0,) * len(shape))
    eg = _gate_expand()
    return pl.pallas_call(
        functools.partial(_out_kernel, per_row=per_row, final=final),
        out_shape=jax.ShapeDtypeStruct((m, D_MODEL), F32),
        grid=(m // tm,),
        in_specs=[row(D_MODEL), gate_spec, row(SSD_WIDTH), row(SSD_WIDTH), row(NSA_WIDTH), row(NSA_WIDTH),
                  row(NSA_WIDTH), row(LANE), row(NSA_WIDTH), full((1, SSD_WIDTH)), full((1, NSA_WIDTH)),
                  full(w_out_bf.shape), full(eg.shape), full((1, D_MODEL))],
        out_specs=row(D_MODEL),
        compiler_params=pltpu.CompilerParams(vmem_limit_bytes=VMEM_LIMIT),
        name="layer_out",
    )(x2d, gt, y_ssd, z_s, o_cmp, o_slc, o_win, gl, z_a, nw1.reshape(1, SSD_WIDTH), nw2.reshape(1, NSA_WIDTH),
      w_out_bf, eg, final_g.reshape(1, D_MODEL))


def _pad_in_weights(w_in):
    cols = []
    off = 0
    for size, width in zip(_SEG_SIZES, _SEG_PAD):
        seg = w_in[:, off:off + size]
        cols.append(jnp.pad(seg, ((0, 0), (0, width - size))))
        off += size
    return jnp.concatenate(cols, axis=1).astype(BF16)


def _values_t(v_pad):
    b, rows, _ = v_pad.shape
    vt = jnp.transpose(v_pad.reshape(b, rows, NSA_KV_HEADS, HEAD_DIM), (0, 2, 3, 1))
    return jnp.concatenate([vt, jnp.ones((b, NSA_KV_HEADS, ONES_ROWS, rows), BF16)], axis=2)


def _front_pad_bf16(kv, rows):
    return jnp.pad(kv, ((0, 0), (rows, 0), (0, 0))).astype(BF16)


def kernel(x_prompt, x_sample, cache_cmp_kv, cache_slc_kv, state_win_kv, state_conv, state_ssm, page_table,
           c_prompt, c_sample, norm_g, ada_w, ada_b, w_in, conv_w, conv_b, dt_bias, a_log, d_skip,
           ssd_norm_w, cmp_pe, cmp_w1, cmp_w2, nsa_norm_w, w_out, rel_bias, final_norm_g):
    nbp, lp, _ = x_prompt.shape
    nbs = x_sample.shape[0]
    depth = w_in.shape[0]
    n_pool = cache_cmp_kv.shape[1]
    n_pages = page_table.shape[1]
    past = n_pages * PAGE_SIZE
    w_buf = state_win_kv.shape[2]
    kv_row = (2, NSA_KV_HEADS, HEAD_DIM)

    n_ck = lp // CMP_STRIDE
    n_sb = lp // SLC_BLOCK
    qi = np.arange(Q_BLOCK)[:, None]
    c0 = n_ck - 4
    pattern = _bias_of_dist(rel_bias, qi - CMP_STRIDE * (np.arange(n_ck)[None, :] - c0) - (CMP_STRIDE - 1)) * LOG2E
    far = rel_bias.astype(F32)[REL_BUCKETS - 1][:, None, None]
    par = np.arange(2)[:, None, None]
    qq = np.arange(Q_BLOCK)[None, None, :]
    dist_n = qq - np.arange(SLC_NEARW)[None, :, None] + Q_BLOCK * par + (SLC_NEAR - Q_BLOCK)
    c_max = Q_BLOCK + WIN_PAD
    master = _toeplitz_bias(rel_bias, c_max - (SLC_NEAR - Q_BLOCK) + SLC_NEARW, Q_BLOCK, c_max)
    window = lambda c0, rows: master[:, c_max - c0:c_max - c0 + rows]
    tnt = jnp.stack([window(Q_BLOCK * p + SLC_NEAR - Q_BLOCK, SLC_NEARW) for p in range(2)], axis=1)
    tnt = jnp.where(jnp.asarray((dist_n >= 0) & (dist_n <= qq + SLC_NEAR - Q_BLOCK))[None],
                    (tnt - far[..., None]) * LOG2E, NEG)
    tnt = jnp.transpose(tnt.reshape(NSA_KV_HEADS, NSA_REP, 2, SLC_NEARW, Q_BLOCK), (2, 0, 3, 1, 4))
    tnt = tnt.reshape(2, NSA_KV_HEADS, SLC_NEARW, ROWS)
    tile = _group_tile()
    key_row = np.arange(SLC_PAD + lp + SLC_BACK) - SLC_PAD
    assert (key_row[-1] // SLC_BLOCK) < PAD_FLAG_COL
    cols = np.arange(LANE)[None, :]
    blk_onehot = jnp.asarray(((key_row[:, None] >= 0) & (key_row[:, None] // SLC_BLOCK == cols))
                             | ((key_row[:, None] < 0) & (cols == PAD_FLAG_COL)), BF16)
    dist_w = qq - np.arange(WIN_W)[None, :, None] + Q_BLOCK * par + WIN_PAD
    tw = jnp.stack([window(Q_BLOCK * p + WIN_PAD, WIN_W) for p in range(2)], axis=1)
    tw = jnp.where(jnp.asarray((dist_w >= 0) & (dist_w < WINDOW))[None], tw * LOG2E, NEG)
    tw = jnp.transpose(tw.reshape(NSA_KV_HEADS, NSA_REP, 2, WIN_W, Q_BLOCK), (2, 0, 3, 1, 4))
    tw = tw.reshape(2, NSA_KV_HEADS, WIN_W, ROWS)
    tw = jnp.concatenate([tw[0], tw[1]], axis=-1)
    win_flag = jnp.asarray((np.arange(WIN_PAD + lp)[:, None] < WIN_PAD) & (np.arange(LANE)[None, :] == 0), BF16)
    ov_p = _overlap_matrix(n_ck, n_sb, LANE)

    n_cs = past // CMP_STRIDE
    cur = past // SLC_BLOCK
    n_blk_s = cur + 1
    n_cols_s = -(-n_blk_s // LANE) * LANE
    ov_s = _overlap_matrix(n_cs, n_blk_s, n_cols_s)
    m_s = np.arange(n_cs)
    bm_c = jnp.where(jnp.asarray(m_s >= 1)[None, :],
                     _bias_of_dist(rel_bias, past - (CMP_STRIDE * m_s + CMP_STRIDE - 1)), NEG)
    n_w = -(-(w_buf + 1) // LANE) * LANE
    iw = np.arange(n_w)
    dw = w_buf - iw
    ok_w = (iw <= w_buf) & (dw >= 0) & (dw < WINDOW) & (past - w_buf + iw >= 0)
    bm_w = jnp.where(jnp.asarray(ok_w)[None, :], _bias_of_dist(rel_bias, dw), NEG)
    jb = (cur - 3 + np.arange(4))[:, None]
    biasblk = _bias_of_dist(rel_bias, past - SLC_BLOCK * jb - np.arange(SLC_BLOCK)[None, :])
    biasblk = jnp.transpose(biasblk, (1, 0, 2))
    biasblk = jnp.concatenate([biasblk, biasblk], axis=-1)

    pages_t = lambda c: jnp.transpose(c, (0, 1, 3, 4, 5, 2)).reshape(depth * n_pool, KV_COLS, PAGE_SIZE)
    cmp_pages_t = pages_t(cache_cmp_kv)
    slc_pages_t = pages_t(cache_slc_kv)
    prompt_pages = jnp.arange(nbp * (lp // PAGE_SIZE), dtype=jnp.int32).reshape(nbp, lp // PAGE_SIZE)

    c_all = jnp.concatenate([c_prompt, c_sample], axis=0)
    xp = x_prompt.reshape(nbp * lp, D_MODEL)
    xs = x_sample.reshape(nbs, D_MODEL)
    outs = {k: [] for k in ("pc", "ps", "pw", "pconv", "pssm", "sc", "ss", "sw", "sconv", "sssm")}

    for l in range(depth):
        final = l == depth - 1
        w_pad = _pad_in_weights(w_in[l])
        w_out_bf = w_out[l].astype(BF16)
        cweights = _compress_weights(cmp_pe[l], cmp_w1[l], cmp_w2[l])
        mod = _modulation(c_all, ada_w[l], ada_b[l])
        shift, scale, gate = mod[:, :D_MODEL], mod[:, D_MODEL:2 * D_MODEL], mod[:, 2 * D_MODEL:]

        z_s, xbc, dt, q, kvc, kvs, kvw, gl, z_a, qt = _in_projection(xp, norm_g[l], scale[:nbp], shift[:nbp], w_pad,
                                                                     lp)
        xbc3 = xbc.reshape(nbp, lp, CONV_DIM)
        y_ssd, h_fin = _ssd_prompt(xbc3, dt.reshape(nbp, lp, LANE), conv_w[l], conv_b[l], dt_bias[l], a_log[l],
                                   d_skip[l])
        q3 = q.reshape(nbp, lp, NSA_WIDTH)
        kvc3, kvs3, kvw3 = (t.reshape(nbp, lp, KV_COLS) for t in (kvc, kvs, kvw))
        kc = _compress(kvc.reshape(nbp * (lp // PAGE_SIZE), PAGE_SIZE, KV_COLS), prompt_pages, cweights, False)
        o_cmp, sel = _cmp_prompt(q3, kc, pattern, ov_p)
        ks_pad = jnp.pad(kvs3, ((0, 0), (SLC_PAD, SLC_BACK), (0, 0))).astype(BF16)
        kx_pad = jnp.concatenate([ks_pad[:, :, :LANE], jnp.broadcast_to(blk_onehot, (nbp,) + blk_onehot.shape)],
                                 axis=-1)
        o_slc = _slc_prompt(qt, sel, kx_pad, _values_t(ks_pad[:, :, LANE:]), tnt, tile)
        kw_pad = _front_pad_bf16(kvw3, WIN_PAD)
        kwx_pad = jnp.concatenate([kw_pad[:, :, :LANE], jnp.broadcast_to(win_flag, (nbp,) + win_flag.shape)], axis=-1)
        o_win = _win_prompt(qt, kwx_pad, _values_t(kw_pad[:, :, LANE:]), tw)
        xp = _layer_out(xp, gate[:nbp], y_ssd.reshape(nbp * lp, SSD_WIDTH), z_s,
                        o_cmp.reshape(nbp * lp, NSA_WIDTH), o_slc.reshape(nbp * lp, NSA_WIDTH),
                        o_win.reshape(nbp * lp, NSA_WIDTH), gl, z_a, ssd_norm_w[l], nsa_norm_w[l], w_out_bf,
                        final_norm_g, lp, final)
        outs["pc"].append(kvc3.reshape((nbp, lp) + kv_row))
        outs["ps"].append(kvs3.reshape((nbp, lp) + kv_row))
        outs["pw"].append(kvw3[:, -min(WINDOW, lp):].reshape((nbp, min(WINDOW, lp)) + kv_row))
        outs["pconv"].append(xbc3[:, -(SSD_CONV - 1):])
        outs["pssm"].append(h_fin)

        z_s2, xbc2, dt2, q2, kvc2, kvs2, kvw2, gl2, z_a2 = _in_projection(
            xs, norm_g[l], scale[nbp:], shift[nbp:], w_pad, 1)
        y_ssd2, h2 = _ssd_step(xbc2, state_conv[l], dt2, state_ssm[l], conv_w[l], conv_b[l], dt_bias[l],
                               a_log[l], d_skip[l])
        qh = q2.reshape(nbs, NSA_KV_HEADS, NSA_REP, HEAD_DIM)
        zq = jnp.zeros((nbs, NSA_REP, HEAD_DIM), F32)
        q_pad = jnp.stack([jnp.concatenate([qh[:, 0], zq], axis=-1), jnp.concatenate([zq, qh[:, 1]], axis=-1)],
                          axis=1).reshape(nbs, NSA_HEADS, LANE).astype(BF16)
        kc2 = _compress(cmp_pages_t, page_table + l * n_pool, cweights, True)
        oc2, psum = _dense1(q_pad, kc2, bm_c)
        sel_idx = _topk_sample(psum.reshape(nbs * NSA_KV_HEADS, n_cs), ov_s, n_blk_s, cur)
        jsel = sel_idx[:, :nbs * NSA_KV_HEADS].T.reshape(nbs, NSA_KV_HEADS, SLC_TOP)
        jc = jnp.minimum(jsel, cur - 1)
        page = jnp.take_along_axis(page_table, (jc // 2).reshape(nbs, -1), axis=1).reshape(jsel.shape)
        phys = jnp.where(jsel < cur, page + l * n_pool, 0).astype(jnp.int32)
        os2 = _slc_sample(q2.reshape(nbs, NSA_HEADS, HEAD_DIM).astype(BF16), slc_pages_t, phys.reshape(-1),
                          jsel.reshape(-1), kvs2.reshape(nbs, KV_COLS, 1), biasblk, cur, past)
        kw_full = jnp.concatenate([state_win_kv[l].reshape(nbs, w_buf, KV_COLS), kvw2[:, None, :]], axis=1)
        kw_in = jnp.pad(kw_full, ((0, 0), (0, n_w - (w_buf + 1)), (0, 0)))
        ow2, _ = _dense1(q_pad, kw_in, bm_w)
        xs = _layer_out(xs, gate[nbp:], y_ssd2, z_s2, _take_group_half(oc2), os2.reshape(nbs, NSA_WIDTH),
                        _take_group_half(ow2), gl2, z_a2, ssd_norm_w[l], nsa_norm_w[l], w_out_bf,
                        final_norm_g, 1, final)
        outs["sc"].append(kvc2.reshape((nbs, 1) + kv_row))
        outs["ss"].append(kvs2.reshape((nbs, 1) + kv_row))
        outs["sw"].append(kw_full[:, -w_buf:].reshape((nbs, w_buf) + kv_row))
        outs["sconv"].append(jnp.concatenate([state_conv[l][:, 1:], xbc2[:, None, :]], axis=1))
        outs["sssm"].append(h2)

    st = lambda k: jnp.stack(outs[k])
    return (xp.reshape(nbp, lp, D_MODEL), xs.reshape(nbs, 1, D_MODEL),
            st("pc"), st("ps"), st("pw"), st("pconv"), st("pssm"),
            st("sc"), st("ss"), st("sw"), st("sconv"), st("sssm"))
```
